```python
import math
import jax, jax.numpy as jnp
from jax import lax
import numpy as np

D_MODEL = 1024
BATCH = 8
SEQ = 4096
DEPTH = 1

MIX_WIDTH = D_MODEL
FOURIER_WIDTH = MIX_WIDTH // 2
N_FOURIER_GROUPS = 4
FOURIER_GROUP = FOURIER_WIDTH // N_FOURIER_GROUPS
N_DIFF_HEADS = 4
DIFF_V_DIM = (MIX_WIDTH - FOURIER_WIDTH) // N_DIFF_HEADS
DIFF_QK_DIM = DIFF_V_DIM // 2
Q_WIDTH = N_DIFF_HEADS * 2 * DIFF_QK_DIM
V_WIDTH = N_DIFF_HEADS * DIFF_V_DIM
IN_COLS = FOURIER_WIDTH + 2 * Q_WIDTH + V_WIDTH
Q_BLOCK = 128
N_BUCKETS = 32
MAX_DISTANCE = 128
N_EXPERTS = 32
TOP_K = 4
D_FF = D_MODEL
SWIGLU_ALPHA = 1.702
SWIGLU_LIMIT = 7.0
MOE_BLOCK = 128
EPS = 1e-6

kernel_name = "hybrid_fourier_diffattn_moe_block"


def rmsnorm(x, g):
    xf = x.astype(jnp.float32)
    y = xf * lax.rsqrt(jnp.mean(xf * xf, axis=-1, keepdims=True) + EPS)
    return (y * g.astype(jnp.float32)).astype(x.dtype)


def modulate(h, shift, scale):
    return h * (1 + scale[:, None, :]) + shift[:, None, :]


def t5_bucket(rel):
    nb = N_BUCKETS // 2
    ret = jnp.where(rel > 0, nb, 0)
    n = jnp.abs(rel)
    max_exact = nb // 2
    nf = jnp.maximum(n, 1).astype(jnp.float32)
    large = max_exact + (jnp.log(nf / max_exact) / math.log(MAX_DISTANCE / max_exact)
                         * (nb - max_exact)).astype(jnp.int32)
    large = jnp.minimum(large, nb - 1)
    return ret + jnp.where(n < max_exact, n, large)


def fourier_mix(u, w_f):
    B, S, _ = u.shape
    ug = u.reshape(B, S, N_FOURIER_GROUPS, FOURIER_GROUP).astype(jnp.float32)
    f = jnp.fft.fftn(ug, axes=(1, 3), norm="ortho").real
    y = jnp.einsum("bsgc,gcd->bsgd", f.astype(u.dtype), w_f)
    return y.reshape(B, S, FOURIER_WIDTH)


def diff_attention(q, k, v, rel_bias, lam, lam_init, g_subln):
    B, S = q.shape[0], q.shape[1]
    H, dqk, dv = N_DIFF_HEADS, DIFF_QK_DIM, DIFF_V_DIM
    scale = dqk ** -0.5
    qh = q.transpose(3, 0, 2, 1, 4)
    kh = k.transpose(3, 0, 2, 1, 4)
    vh = v.transpose(0, 2, 1, 3)
    nqb = S // Q_BLOCK
    qb = qh.reshape(2, B, H, nqb, Q_BLOCK, dqk).transpose(3, 0, 1, 2, 4, 5)
    kpos = jnp.arange(S, dtype=jnp.int32)

    def block(args):
        qblk, start = args
        qpos = start + jnp.arange(Q_BLOCK, dtype=jnp.int32)
        bucket = t5_bucket(kpos[None, :] - qpos[:, None])
        bias = jnp.take(rel_bias, bucket, axis=0).astype(jnp.float32)
        bias = bias.transpose(2, 0, 1)
        logits = jnp.einsum("mbhqd,mbhkd->mbhqk", qblk, kh).astype(jnp.float32) * scale + bias
        p = jax.nn.softmax(logits, axis=-1)
        a = p[0] - lam * p[1]
        return jnp.einsum("bhqk,bhkd->bhqd", a.astype(vh.dtype), vh)

    starts = jnp.arange(nqb, dtype=jnp.int32) * Q_BLOCK
    o = lax.map(block, (qb, starts))
    o = o.transpose(1, 0, 3, 2, 4).reshape(B, S, H, dv)
    o = rmsnorm(o, g_subln) * (1.0 - lam_init)
    return o.reshape(B, S, H * dv)


def clamped_swiglu(z):
    z_glu = jnp.minimum(z[..., ::2], SWIGLU_LIMIT)
    z_lin = jnp.clip(z[..., 1::2], -SWIGLU_LIMIT, SWIGLU_LIMIT)
    return z_glu * jax.nn.sigmoid(SWIGLU_ALPHA * z_glu) * (z_lin + 1)


def moe(h, w_router, b_router, w1, b1, w2, b2):
    B, S, D = h.shape
    T = B * S
    ht = h.reshape(T, D)
    logits = (ht @ w_router + b_router).astype(jnp.float32)
    top_vals, top_idx = lax.top_k(logits, TOP_K)
    gates = jax.nn.softmax(top_vals, axis=-1)
    A = T * TOP_K
    e_flat = top_idx.reshape(A).astype(jnp.int32)
    g_flat = gates.reshape(A)
    tok_flat = jnp.repeat(jnp.arange(T, dtype=jnp.int32), TOP_K)
    order = jnp.argsort(e_flat)
    e_sorted = e_flat[order]
    tok_sorted = tok_flat[order]
    g_sorted = g_flat[order]
    counts = jnp.zeros((N_EXPERTS,), jnp.int32).at[e_flat].add(1)
    padded = (counts + MOE_BLOCK - 1) // MOE_BLOCK * MOE_BLOCK
    starts = jnp.cumsum(counts) - counts
    pstarts = jnp.cumsum(padded) - padded
    pends = jnp.cumsum(padded)
    rank = jnp.arange(A, dtype=jnp.int32) - starts[e_sorted]
    dest = pstarts[e_sorted] + rank
    P = A + N_EXPERTS * MOE_BLOCK
    nblk = P // MOE_BLOCK
    slot_tok = jnp.full((P,), T, jnp.int32).at[dest].set(tok_sorted)
    slot_gate = jnp.zeros((P,), jnp.float32).at[dest].set(g_sorted)
    blk_start = jnp.arange(nblk, dtype=jnp.int32) * MOE_BLOCK
    blk_expert = jnp.minimum(jnp.searchsorted(pends, blk_start, side="right"), N_EXPERTS - 1).astype(jnp.int32)
    h_pad = jnp.concatenate([ht, jnp.zeros((1, D), ht.dtype)], axis=0)

    def expert_block(args):
        tok, gate, e = args
        xb = h_pad[tok]
        z = xb @ w1[e] + b1[e]
        y = clamped_swiglu(z) @ w2[e] + b2[e]
        return y * gate[:, None].astype(y.dtype)

    y_slots = lax.map(expert_block, (slot_tok.reshape(nblk, MOE_BLOCK),
                                     slot_gate.reshape(nblk, MOE_BLOCK), blk_expert))
    out = jnp.zeros((T + 1, D), y_slots.dtype).at[slot_tok].add(y_slots.reshape(P, D))
    return out[:T].reshape(B, S, D).astype(h.dtype)


def setup_inputs(seed: int = 0) -> dict:
    key = jax.random.key(seed)
    ks = jax.random.split(key, 24)
    f32 = jnp.float32
    L, D = DEPTH, D_MODEL
    nrm = lambda k, shape, s: jax.random.normal(k, shape, f32) * s
    return {
        "x": nrm(ks[0], (BATCH, SEQ, D), 1.0),
        "c": nrm(ks[1], (BATCH, D), 1.0),
        "w_ada": nrm(ks[2], (L, D, 6 * D), D ** -0.5),
        "b_ada": nrm(ks[3], (L, 6 * D), 0.1),
        "g_mix": 1.0 + nrm(ks[4], (L, D), 0.05),
        "w_in": nrm(ks[5], (L, D, IN_COLS), D ** -0.5),
        "w_fourier": nrm(ks[6], (L, N_FOURIER_GROUPS, FOURIER_GROUP, FOURIER_GROUP), FOURIER_GROUP ** -0.5),
        "q_norm_g": 1.0 + nrm(ks[7], (L, DIFF_QK_DIM), 0.05),
        "k_norm_g": 1.0 + nrm(ks[8], (L, DIFF_QK_DIM), 0.05),
        "lambda_q1": nrm(ks[9], (L, DIFF_QK_DIM), 0.1),
        "lambda_k1": nrm(ks[10], (L, DIFF_QK_DIM), 0.1),
        "lambda_q2": nrm(ks[11], (L, DIFF_QK_DIM), 0.1),
        "lambda_k2": nrm(ks[12], (L, DIFF_QK_DIM), 0.1),
        "g_subln": 1.0 + nrm(ks[13], (L, DIFF_V_DIM), 0.05),
        "w_out": nrm(ks[14], (L, MIX_WIDTH, D), MIX_WIDTH ** -0.5),
        "rel_bias": nrm(ks[15], (N_BUCKETS, N_DIFF_HEADS), 0.5),
        "g_ffn": 1.0 + nrm(ks[16], (L, D), 0.05),
        "w_router": nrm(ks[17], (L, D, N_EXPERTS), D ** -0.5),
        "b_router": nrm(ks[18], (L, N_EXPERTS), 0.01),
        "w1": nrm(ks[19], (L, N_EXPERTS, D, 2 * D_FF), D ** -0.5),
        "b1": nrm(ks[20], (L, N_EXPERTS, 2 * D_FF), 0.02),
        "w2": nrm(ks[21], (L, N_EXPERTS, D_FF, D), D_FF ** -0.5),
        "b2": nrm(ks[22], (L, N_EXPERTS, D), 0.02),
    }


def reference(x, c, w_ada, b_ada, g_mix, w_in, w_fourier, q_norm_g, k_norm_g,
              lambda_q1, lambda_k1, lambda_q2, lambda_k2, g_subln, w_out, rel_bias,
              g_ffn, w_router, b_router, w1, b1, w2, b2):
    B, S, D = x.shape
    cond = jax.nn.silu(c)
    f0 = FOURIER_WIDTH
    f1 = f0 + Q_WIDTH
    f2 = f1 + Q_WIDTH
    for l in range(DEPTH):
        mod = cond @ w_ada[l] + b_ada[l]
        sh_a, sc_a, gt_a, sh_f, sc_f, gt_f = jnp.split(mod, 6, axis=-1)
        h = modulate(rmsnorm(x, g_mix[l]), sh_a, sc_a)
        proj = h @ w_in[l]
        u_f = proj[..., :f0]
        q = proj[..., f0:f1].reshape(B, S, N_DIFF_HEADS, 2, DIFF_QK_DIM)
        k = proj[..., f1:f2].reshape(B, S, N_DIFF_HEADS, 2, DIFF_QK_DIM)
        v = proj[..., f2:].reshape(B, S, N_DIFF_HEADS, DIFF_V_DIM)
        q = rmsnorm(q, q_norm_g[l])
        k = rmsnorm(k, k_norm_g[l])
        lam_init = 0.8 - 0.6 * math.exp(-0.3 * l)
        lam = (jnp.exp(jnp.sum(lambda_q1[l].astype(jnp.float32) * lambda_k1[l].astype(jnp.float32)))
               - jnp.exp(jnp.sum(lambda_q2[l].astype(jnp.float32) * lambda_k2[l].astype(jnp.float32)))
               + lam_init)
        y_f = fourier_mix(u_f, w_fourier[l])
        y_d = diff_attention(q, k, v, rel_bias, lam, lam_init, g_subln[l])
        mix = jnp.concatenate([y_f, y_d], axis=-1) @ w_out[l]
        x = x + gt_a[:, None, :] * mix
        h = modulate(rmsnorm(x, g_ffn[l]), sh_f, sc_f)
        y = moe(h, w_router[l], b_router[l], w1[l], b1[l], w2[l], b2[l])
        x = x + gt_f[:, None, :] * y
    return x
```

```python
import functools
import math

import numpy as np
import jax
import jax.numpy as jnp
from jax import lax
from jax.experimental import pallas as pl
from jax.experimental.pallas import tpu as pltpu

F32 = jnp.float32
BF16 = jnp.bfloat16

EPS = 1e-6
N_FOURIER_GROUPS = 4
N_DIFF_HEADS = 4
TOP_K = 4
N_BUCKETS = 32
MAX_DISTANCE = 128
SWIGLU_ALPHA = 1.702
SWIGLU_LIMIT = 7.0
LANES = 128
DFT_RADIX = 4
VMEM_LIMIT = 56 * 1024 * 1024


def _cparams(sem):
    return pltpu.CompilerParams(dimension_semantics=sem, vmem_limit_bytes=VMEM_LIMIT)


def _ada_kernel(c_ref, w_ref, b_ref, o_ref):
    c = c_ref[...]
    cond = c * jax.nn.sigmoid(c)
    o_ref[...] = jnp.dot(cond, w_ref[...], preferred_element_type=F32,
                         precision=lax.Precision.HIGHEST) + b_ref[...]


def _ada(c, w, b):
    B, D = c.shape
    N = w.shape[1]
    tn = 1536 if N % 1536 == 0 else N
    return pl.pallas_call(
        _ada_kernel,
        grid=(N // tn,),
        in_specs=[pl.BlockSpec((B, D), lambda j: (0, 0)),
                  pl.BlockSpec((D, tn), lambda j: (0, j)),
                  pl.BlockSpec((1, tn), lambda j: (0, j))],
        out_specs=pl.BlockSpec((B, tn), lambda j: (0, j)),
        out_shape=jax.ShapeDtypeStruct((B, N), F32),
        compiler_params=_cparams(("arbitrary",)),
        name="ada",
    )(c, w, b.reshape(1, N))


def _group_rms_inv(xh, gmat, group):
    sq = xh * xh
    hi = sq.astype(BF16)
    lo = (sq - hi.astype(F32)).astype(BF16)
    ss = jnp.dot(hi, gmat, preferred_element_type=F32) + jnp.dot(lo, gmat, preferred_element_type=F32)
    return lax.rsqrt(ss * (1.0 / group) + EPS)


def _in_proj_kernel(x_ref, mod_ref, g_ref, w_ref, gq_ref, gk_ref, gmat_ref,
                    uf_ref, q_ref, k_ref, v_ref, *, tiles_per_batch, tm, t_attn, n_chunks, fw, qw):
    i = pl.program_id(0)
    x = x_ref[...]
    ms = jnp.mean(x * x, axis=-1, keepdims=True)
    y = x * lax.rsqrt(ms + EPS) * g_ref[...]
    h = y * (1.0 + mod_ref[1:2, :]) + mod_ref[0:1, :]
    proj = jnp.dot(h.astype(BF16), w_ref[...], preferred_element_type=F32)
    uf_ref[...] = proj[:, :fw].astype(BF16)
    v_ref[...] = proj[:, fw + 2 * qw:].astype(BF16)

    lane = lax.broadcasted_iota(jnp.int32, (1, LANES), 1)
    chunk = ((i % tiles_per_batch) * tm) // t_attn
    gmat = gmat_ref[...]
    half = LANES // 2
    for hd in range(N_DIFF_HEADS):
        qh = proj[:, fw + hd * LANES: fw + (hd + 1) * LANES]
        kh = proj[:, fw + qw + hd * LANES: fw + qw + (hd + 1) * LANES]
        qn = (qh * _group_rms_inv(qh, gmat, half) * gq_ref[...]).astype(BF16)
        kn = (kh * _group_rms_inv(kh, gmat, half) * gk_ref[...]).astype(BF16)
        for m in range(2):
            fl = lane - half if m == 0 else lane
            is_data = (lane < half) if m == 0 else (lane >= half)
            f = jnp.where(fl < n_chunks, fl, fl - n_chunks)
            onehot = ((fl >= 0) & (fl < 2 * n_chunks) & (f == chunk)).astype(BF16)
            c0 = (2 * hd + m) * LANES
            q_ref[:, c0:c0 + LANES] = jnp.where(is_data, qn, jnp.zeros_like(qn))
            k_ref[:, c0:c0 + LANES] = jnp.where(is_data, kn, onehot)


def _in_proj(x2, mod3, g_mix, w_in_b, gq, gk, S, tm, t_attn):
    T, D = x2.shape
    ncols = w_in_b.shape[1]
    fw = ncols // 4
    qw = fw
    tiles_per_batch = S // tm
    n_chunks = S // t_attn
    half = LANES // 2
    gmat = (np.arange(LANES)[:, None] // half == np.arange(LANES)[None, :] // half)
    gmat = jnp.asarray(gmat, BF16)
    kern = functools.partial(_in_proj_kernel, tiles_per_batch=tiles_per_batch, tm=tm, t_attn=t_attn,
                             n_chunks=n_chunks, fw=fw, qw=qw)
    ext = 2 * qw
    return pl.pallas_call(
        kern,
        grid=(T // tm,),
        in_specs=[pl.BlockSpec((tm, D), lambda i: (i, 0)),
                  pl.BlockSpec((None, 6, D), lambda i: (i // tiles_per_batch, 0, 0)),
                  pl.BlockSpec((1, D), lambda i: (0, 0)),
                  pl.BlockSpec((D, ncols), lambda i: (0, 0)),
                  pl.BlockSpec((1, LANES), lambda i: (0, 0)),
                  pl.BlockSpec((1, LANES), lambda i: (0, 0)),
                  pl.BlockSpec((LANES, LANES), lambda i: (0, 0))],
        out_specs=[pl.BlockSpec((tm, fw), lambda i: (i, 0)),
                   pl.BlockSpec((tm, ext), lambda i: (i, 0)),
                   pl.BlockSpec((tm, ext), lambda i: (i, 0)),
                   pl.BlockSpec((tm, fw), lambda i: (i, 0))],
        out_shape=[jax.ShapeDtypeStruct((T, fw), BF16),
                   jax.ShapeDtypeStruct((T, ext), BF16),
                   jax.ShapeDtypeStruct((T, ext), BF16),
                   jax.ShapeDtypeStruct((T, fw), BF16)],
        compiler_params=_cparams(("arbitrary",)),
        name="in_proj",
    )(x2, mod3, g_mix, w_in_b, gq, gk, gmat)


def _fourier_kernel(uf_ref, wcs_ref, tw_ref, dft_ref, o_ref, tt_sc, *, M, gw):
    c = pl.program_id(1)

    @pl.when(c == 0)
    def _():
        for g in range(N_FOURIER_GROUPS):
            zr, zi = [], []
            for j in range(DFT_RADIX):
                ab = jnp.dot(uf_ref[j * M:(j + 1) * M, g * gw:(g + 1) * gw], wcs_ref[g],
                             preferred_element_type=F32)
                zr.append(ab[:, :gw])
                zi.append(-ab[:, gw:])
            yr = [zr[0] + zr[1] + zr[2] + zr[3],
                  zr[0] + zi[1] - zr[2] - zi[3],
                  zr[0] - zr[1] + zr[2] - zr[3],
                  zr[0] - zi[1] - zr[2] + zi[3]]
            yi = [zi[0] + zi[1] + zi[2] + zi[3],
                  zi[0] - zr[1] - zi[2] + zr[3],
                  zi[0] - zi[1] + zi[2] - zi[3],
                  zi[0] + zr[1] - zi[2] - zr[3]]
            for k2 in range(DFT_RADIX):
                cs = tw_ref[:, k2:k2 + 1]
                sn = tw_ref[:, DFT_RADIX + k2:DFT_RADIX + k2 + 1]
                tr = cs * yr[k2] + sn * yi[k2]
                ti = cs * yi[k2] - sn * yr[k2]
                tt_sc[k2, 0:M, g * gw:(g + 1) * gw] = tr.astype(BF16)
                tt_sc[k2, M:2 * M, g * gw:(g + 1) * gw] = ti.astype(BF16)

    o_ref[...] = jnp.dot(dft_ref[...], tt_sc[c], preferred_element_type=F32).astype(BF16)


def _fourier(uf3, w_f):
    B, S, W = uf3.shape
    G = N_FOURIER_GROUPS
    gw = W // G
    M = S // DFT_RADIX
    norm = 1.0 / math.sqrt(S * gw)
    cidx = (np.arange(gw)[:, None] * np.arange(gw)[None, :]) % gw
    ang_c = jnp.asarray(cidx, F32) * (2.0 * math.pi / gw)
    cc, sc = jnp.cos(ang_c) * norm, jnp.sin(ang_c) * norm
    hp = lax.Precision.HIGHEST
    wcs = jnp.concatenate([jnp.einsum("ab,gbd->gad", cc, w_f, precision=hp),
                           jnp.einsum("ab,gbd->gad", sc, w_f, precision=hp)], axis=-1).astype(BF16)
    n1 = np.arange(M)
    tw_idx = (np.arange(DFT_RADIX)[None, :] * n1[:, None]) % S
    ang_t = jnp.asarray(tw_idx, F32) * (2.0 * math.pi / S)
    tw = jnp.concatenate([jnp.cos(ang_t), jnp.sin(ang_t),
                          jnp.zeros((M, LANES - 2 * DFT_RADIX), F32)], axis=1)
    pidx = (n1[:, None] * n1[None, :]) % M
    ang_p = jnp.asarray(pidx, F32) * (2.0 * math.pi / M)
    dft = jnp.concatenate([jnp.cos(ang_p), jnp.sin(ang_p)], axis=1).astype(BF16)
    kern = functools.partial(_fourier_kernel, M=M, gw=gw)
    return pl.pallas_call(
        kern,
        grid=(B, DFT_RADIX),
        in_specs=[pl.BlockSpec((None, S, W), lambda b, c: (b, 0, 0)),
                  pl.BlockSpec((G, gw, 2 * gw), lambda b, c: (0, 0, 0)),
                  pl.BlockSpec((M, LANES), lambda b, c: (0, 0)),
                  pl.BlockSpec((M, 2 * M), lambda b, c: (0, 0))],
        out_specs=pl.BlockSpec((None, M, W), lambda b, c: (b, 0, c)),
        out_shape=jax.ShapeDtypeStruct((B, M, DFT_RADIX * W), BF16),
        scratch_shapes=[pltpu.VMEM((DFT_RADIX, 2 * M, W), BF16)],
        compiler_params=_cparams(("arbitrary", "arbitrary")),
        name="fourier",
    )(uf3, wcs, tw, dft)


def _bias_kernel(rb_ref, o_ref, *, t):
    h = pl.program_id(0)
    d = pl.program_id(1) - 1
    row = lax.broadcasted_iota(jnp.int32, (t, t), 0)
    col = lax.broadcasted_iota(jnp.int32, (t, t), 1)
    rel = d * t + col - row
    nb = N_BUCKETS // 2
    max_exact = nb // 2
    ret = jnp.where(rel > 0, nb, 0)
    n = jnp.abs(rel)
    nf = jnp.maximum(n, 1).astype(F32)
    large = max_exact + (jnp.log(nf / max_exact) / math.log(MAX_DISTANCE / max_exact)
                         * (nb - max_exact)).astype(jnp.int32)
    large = jnp.minimum(large, nb - 1)
    bucket = ret + jnp.where(n < max_exact, n, large)
    val = jnp.zeros((t, t), F32)
    for j in range(N_BUCKETS):
        val = jnp.where(bucket == j, rb_ref[j * N_DIFF_HEADS + h], val)
    o_ref[...] = val


def _bias_tiles(rel_bias, t):
    H = N_DIFF_HEADS
    return pl.pallas_call(
        functools.partial(_bias_kernel, t=t),
        grid=(H, 3),
        in_specs=[pl.BlockSpec(memory_space=pltpu.SMEM)],
        out_specs=pl.BlockSpec((None, None, t, t), lambda h, d: (h, d, 0, 0)),
        out_shape=jax.ShapeDtypeStruct((H, 3, t, t), F32),
        compiler_params=_cparams(("arbitrary", "arbitrary")),
        name="t5_bias",
    )(rel_bias.reshape(-1))


def _far_bucket_consts():
    nb = N_BUCKETS // 2
    return nb - 1, 2 * nb - 1


def _attn_kernel(rb_ref, q_ref, k_ref, v_ref, b_ref, lam_ref, gs_ref, o_ref,
                 s_sc, p_sc, vext_sc, *, t, n, lam_init):
    h = pl.program_id(1)
    i = pl.program_id(2)
    half = LANES // 2

    @pl.when(i == 0)
    def _():
        vext_sc[:, :LANES] = v_ref[...]
        vext_sc[:, LANES:] = jnp.ones(v_ref.shape, BF16)

    left_b, right_b = _far_bucket_consts()
    c_left = rb_ref[left_b * N_DIFF_HEADS + h]
    c_right = rb_ref[right_b * N_DIFF_HEADS + h]
    lane = lax.broadcasted_iota(jnp.int32, (1, LANES), 1)

    def one_map(m):
        fl = lane - half if m == 0 else lane
        is_data = (lane < half) if m == 0 else (lane >= half)
        f = jnp.where(fl < n, fl, fl - n)
        cfar = jnp.where(f < i - 1, c_left, jnp.where(f > i + 1, c_right, 0.0))
        c_hi = cfar.astype(BF16)
        c_lo = (cfar - c_hi.astype(F32)).astype(BF16)
        feat = jnp.where(fl < n, c_hi, c_lo)
        feat = jnp.where((fl >= 0) & (fl < 2 * n), feat, jnp.zeros_like(feat))
        qm = jnp.where(is_data, q_ref[:, m * LANES:(m + 1) * LANES], feat)

        def pass1(j, mx):
            kc = k_ref[pl.ds(pl.multiple_of(j * t, t), t), m * LANES:(m + 1) * LANES]
            s = lax.dot_general(qm, kc, (((1,), (1,)), ((), ())), preferred_element_type=F32)
            d = j - i
            s = lax.cond(jnp.abs(d) <= 1,
                         lambda: s + b_ref[jnp.clip(d + 1, 0, 2)],
                         lambda: s)
            s_sc[j] = s
            return jnp.maximum(mx, jnp.max(s, axis=-1, keepdims=True))

        mx = lax.fori_loop(0, n, pass1, jnp.full((t, 1), -jnp.inf, F32))
        for j in range(n):
            p_sc[:, j * t:(j + 1) * t] = jnp.exp(s_sc[j] - mx).astype(BF16)
        o = jnp.dot(p_sc[...], vext_sc[...], preferred_element_type=F32)
        return o[:, :LANES] / o[:, LANES:LANES + 1]

    lam = (jnp.exp(jnp.sum(lam_ref[0:1, :] * lam_ref[1:2, :], axis=-1, keepdims=True))
           - jnp.exp(jnp.sum(lam_ref[2:3, :] * lam_ref[3:4, :], axis=-1, keepdims=True)) + lam_init)
    o = one_map(0) - lam * one_map(1)
    ms = jnp.mean(o * o, axis=-1, keepdims=True)
    o_ref[...] = (o * lax.rsqrt(ms + EPS) * gs_ref[...] * (1.0 - lam_init)).astype(BF16)


def _attention(rel_bias, q3, k3, v3, btiles, lam4, g_subln, t, lam_init):
    B, S, _ = q3.shape
    H = N_DIFF_HEADS
    n = S // t
    kern = functools.partial(_attn_kernel, t=t, n=n, lam_init=lam_init)
    return pl.pallas_call(
        kern,
        grid=(B, H, n),
        in_specs=[pl.BlockSpec(memory_space=pltpu.SMEM),
                  pl.BlockSpec((None, t, 2 * LANES), lambda b, h, i: (b, i, h)),
                  pl.BlockSpec((None, S, 2 * LANES), lambda b, h, i: (b, 0, h)),
                  pl.BlockSpec((None, S, LANES), lambda b, h, i: (b, 0, h)),
                  pl.BlockSpec((None, 3, t, t), lambda b, h, i: (h, 0, 0, 0)),
                  pl.BlockSpec((4, LANES // 2), lambda b, h, i: (0, 0)),
                  pl.BlockSpec((1, LANES), lambda b, h, i: (0, 0))],
        out_specs=pl.BlockSpec((None, t, LANES), lambda b, h, i: (b, i, h)),
        out_shape=jax.ShapeDtypeStruct((B, S, H * LANES), BF16),
        scratch_shapes=[pltpu.VMEM((n, t, t), F32),
                        pltpu.VMEM((t, S), BF16),
                        pltpu.VMEM((S, 2 * LANES), BF16)],
        compiler_params=_cparams(("arbitrary", "arbitrary", "arbitrary")),
        name="diff_attn",
    )(rel_bias.reshape(-1), q3, k3, v3, btiles, lam4, g_subln)


def _out_proj_kernel(yf_ref, yd_ref, x_ref, mod_ref, wo_ref, g_ref, wr_ref, br_ref,
                     x1_ref, hp_ref, idx_ref, rank_ref, grow_ref, cnt_ref, run_sc, *, tm, fw, E):
    i = pl.program_id(0)

    @pl.when(i == 0)
    def _():
        run_sc[...] = jnp.zeros(run_sc.shape, F32)

    mix = (jnp.dot(yf_ref[...], wo_ref[:fw, :], preferred_element_type=F32)
           + jnp.dot(yd_ref[...], wo_ref[fw:, :], preferred_element_type=F32))
    x1 = x_ref[...] + mod_ref[2:3, :] * mix
    x1_ref[...] = x1
    ms = jnp.mean(x1 * x1, axis=-1, keepdims=True)
    h = x1 * lax.rsqrt(ms + EPS) * g_ref[...]
    h = h * (1.0 + mod_ref[4:5, :]) + mod_ref[3:4, :]
    hb = h.astype(BF16)
    bits = lax.bitcast_convert_type(hb.astype(F32), jnp.uint32)
    D = bits.shape[1]
    hp_ref[...] = (bits[:, :D // 2] >> 16) | (bits[:, D // 2:] & jnp.uint32(0xFFFF0000))

    logits = lax.dot_general(wr_ref[...], hb, (((1,), (1,)), ((), ())),
                             preferred_element_type=F32) + br_ref[...]
    rowid = lax.broadcasted_iota(jnp.int32, (E, tm), 0)
    vals, idxs = [], []
    comb = jnp.zeros((E, tm), F32)
    l = logits
    for _ in range(TOP_K):
        mv = jnp.max(l, axis=0, keepdims=True)
        ix = jnp.min(jnp.where(l == mv, rowid, E), axis=0, keepdims=True)
        sel = rowid == ix
        vals.append(mv)
        idxs.append(ix)
        comb = comb + sel.astype(F32)
        l = jnp.where(sel, -jnp.inf, l)
    es = [jnp.exp(v - vals[0]) for v in vals]
    den = es[0] + es[1] + es[2] + es[3]
    gates = [e / den for e in es]

    r_i = lax.broadcasted_iota(jnp.int32, (tm, tm), 0)
    c_i = lax.broadcasted_iota(jnp.int32, (tm, tm), 1)
    upper = (r_i < c_i).astype(BF16)
    before = jnp.dot(comb.astype(BF16), upper, preferred_element_type=F32) + run_sc[:, 0:1]
    ranks = [jnp.sum(jnp.where(rowid == ix, before, 0.0), axis=0, keepdims=True) for ix in idxs]
    run_sc[...] = run_sc[...] + jnp.sum(comb, axis=1, keepdims=True)
    cnt_ref[...] = run_sc[...].astype(jnp.int32)

    pad_i = jnp.zeros((8 - TOP_K, tm), jnp.int32)
    idx_ref[...] = jnp.concatenate(idxs + [pad_i], axis=0)
    rank_ref[...] = jnp.concatenate([r.astype(jnp.int32) for r in ranks] + [pad_i], axis=0)
    gpad = jnp.concatenate(gates + [jnp.zeros((LANES - TOP_K, tm), F32)], axis=0)
    grow_ref[...] = gpad.T


def _out_proj(yf2, yd2, x2, mod3, w_out_b, g_ffn, wr_t, br_t, S, tm):
    T, D = x2.shape
    fw = yf2.shape[1]
    E = wr_t.shape[0]
    tiles_per_batch = S // tm
    kern = functools.partial(_out_proj_kernel, tm=tm, fw=fw, E=E)
    return pl.pallas_call(
        kern,
        grid=(T // tm,),
        in_specs=[pl.BlockSpec((tm, fw), lambda i: (i, 0)),
                  pl.BlockSpec((tm, fw), lambda i: (i, 0)),
                  pl.BlockSpec((tm, D), lambda i: (i, 0)),
                  pl.BlockSpec((None, 6, D), lambda i: (i // tiles_per_batch, 0, 0)),
                  pl.BlockSpec((D, D), lambda i: (0, 0)),
                  pl.BlockSpec((1, D), lambda i: (0, 0)),
                  pl.BlockSpec((E, D), lambda i: (0, 0)),
                  pl.BlockSpec((E, 1), lambda i: (0, 0))],
        out_specs=[pl.BlockSpec((tm, D), lambda i: (i, 0)),
                   pl.BlockSpec((tm, D // 2), lambda i: (i, 0)),
                   pl.BlockSpec((8, tm), lambda i: (0, i)),
                   pl.BlockSpec((8, tm), lambda i: (0, i)),
                   pl.BlockSpec((tm, LANES), lambda i: (i, 0)),
                   pl.BlockSpec((E, LANES), lambda i: (0, 0))],
        out_shape=[jax.ShapeDtypeStruct((T, D), F32),
                   jax.ShapeDtypeStruct((T, D // 2), jnp.uint32),
                   jax.ShapeDtypeStruct((8, T), jnp.int32),
                   jax.ShapeDtypeStruct((8, T), jnp.int32),
                   jax.ShapeDtypeStruct((T, LANES), F32),
                   jax.ShapeDtypeStruct((E, LANES), jnp.int32)],
        scratch_shapes=[pltpu.VMEM((E, LANES), F32)],
        compiler_params=_cparams(("arbitrary",)),
        name="out_proj_router",
    )(yf2, yd2, x2, mod3, w_out_b, g_ffn, wr_t, br_t)


def _dispatch_kernel(dest_ref, zb_ref, hp_ref, xs_ref, zero_sc, sem, zsem, *, td, bm, E):
    i = pl.program_id(0)

    @pl.when(i == 0)
    def _():
        zero_sc[...] = jnp.zeros(zero_sc.shape, jnp.uint32)
        for e in range(2 * E):
            @pl.when(zb_ref[e] >= 0)
            def _():
                pltpu.make_async_copy(zero_sc, xs_ref.at[pl.ds(zb_ref[e] * bm, bm)], zsem).start()
        for e in range(2 * E):
            @pl.when(zb_ref[e] >= 0)
            def _():
                pltpu.make_async_copy(zero_sc, xs_ref.at[pl.ds(zb_ref[e] * bm, bm)], zsem).wait()

    def issue(r, carry):
        for k in range(TOP_K):
            pltpu.make_async_copy(hp_ref.at[pl.ds(r, 1)], xs_ref.at[pl.ds(dest_ref[0, k, r], 1)], sem).start()
        return carry

    lax.fori_loop(0, td, issue, 0)
    for k in range(TOP_K):
        pltpu.make_async_copy(hp_ref, xs_ref.at[pl.ds(0, td)], sem).wait()


def _dispatch(dest3, zero_blk, hp, P, bm):
    T, Wd = hp.shape
    nt, _, td = dest3.shape
    E = zero_blk.shape[0] // 2
    kern = functools.partial(_dispatch_kernel, td=td, bm=bm, E=E)
    return pl.pallas_call(
        kern,
        grid=(nt,),
        in_specs=[pl.BlockSpec((1, TOP_K, td), lambda i: (i, 0, 0), memory_space=pltpu.SMEM),
                  pl.BlockSpec(memory_space=pltpu.SMEM),
                  pl.BlockSpec((td, Wd), lambda i: (i, 0))],
        out_specs=pl.BlockSpec(memory_space=pl.ANY),
        out_shape=jax.ShapeDtypeStruct((P, Wd), jnp.uint32),
        scratch_shapes=[pltpu.VMEM((bm, Wd), jnp.uint32),
                        pltpu.SemaphoreType.DMA,
                        pltpu.SemaphoreType.DMA],
        compiler_params=_cparams(("arbitrary",)),
        name="moe_dispatch",
    )(dest3, zero_blk, hp)


def _expert_kernel(be_ref, nu_ref, x_ref, w1_ref, b1g_ref, b1l_ref, w2_ref, b2_ref, pm_ref, o_ref,
                   w1g_sc, w1l_sc, w2_sc):
    blk = pl.program_id(0)
    nused = nu_ref[0]
    e = be_ref[blk]
    prev = be_ref[jnp.maximum(blk - 1, 0)]
    active = blk < nused
    tile = 2 * LANES
    n_tiles = w1_ref.shape[1] // tile

    @pl.when(active & ((blk == 0) | (e != prev)))
    def _():
        for c in range(n_tiles):
            wt = w1_ref[:, c * tile:(c + 1) * tile].astype(BF16)
            pw = jnp.dot(wt, pm_ref[...], preferred_element_type=F32)
            w1g_sc[:, c * LANES:(c + 1) * LANES] = pw[:, :LANES].astype(BF16)
            w1l_sc[:, c * LANES:(c + 1) * LANES] = pw[:, LANES:].astype(BF16)
        w2_sc[...] = w2_ref[...].astype(BF16)

    @pl.when(jnp.logical_not(active))
    def _():
        o_ref[...] = jnp.zeros(o_ref.shape, F32)

    @pl.when(active)
    def _():
        xp = x_ref[...]
        hw = xp.shape[1]
        xlo = lax.bitcast_convert_type(xp << 16, F32).astype(BF16)
        xhi = lax.bitcast_convert_type(xp & jnp.uint32(0xFFFF0000), F32).astype(BF16)
        zg = (jnp.dot(xlo, w1g_sc[:hw, :], preferred_element_type=F32)
              + jnp.dot(xhi, w1g_sc[hw:, :], preferred_element_type=F32) + b1g_ref[...])
        zl = (jnp.dot(xlo, w1l_sc[:hw, :], preferred_element_type=F32)
              + jnp.dot(xhi, w1l_sc[hw:, :], preferred_element_type=F32) + b1l_ref[...])
        g = jnp.minimum(zg, SWIGLU_LIMIT)
        lin = jnp.clip(zl, -SWIGLU_LIMIT, SWIGLU_LIMIT)
        act = g * jax.nn.sigmoid(SWIGLU_ALPHA * g) * (lin + 1.0)
        o_ref[...] = jnp.dot(act.astype(BF16), w2_sc[...], preferred_element_type=F32) + b2_ref[...]


def _experts(blk_expert, nused, xs, w1, b1g, b1l, w2, b2, bm):
    P, Wd = xs.shape
    E, D, F2 = w1.shape
    F = F2 // 2
    nblk = P // bm
    tile = 2 * LANES
    pm = np.zeros((tile, tile), np.float32)
    pm[2 * np.arange(LANES), np.arange(LANES)] = 1.0
    pm[2 * np.arange(LANES) + 1, LANES + np.arange(LANES)] = 1.0
    pm = jnp.asarray(pm, BF16)

    def row_map(b, be, nu):
        return (jnp.minimum(b, nu[0] - 1), 0)

    def exp_map(b, be, nu):
        return (be[b], 0, 0)

    grid_spec = pltpu.PrefetchScalarGridSpec(
        num_scalar_prefetch=2,
        grid=(nblk,),
        in_specs=[pl.BlockSpec((bm, Wd), row_map),
                  pl.BlockSpec((None, D, F2), exp_map),
                  pl.BlockSpec((None, 1, F), exp_map),
                  pl.BlockSpec((None, 1, F), exp_map),
                  pl.BlockSpec((None, F, D), exp_map),
                  pl.BlockSpec((None, 1, D), exp_map),
                  pl.BlockSpec((tile, tile), lambda b, be, nu: (0, 0))],
        out_specs=pl.BlockSpec((bm, D), lambda b, be, nu: (b, 0)),
        scratch_shapes=[pltpu.VMEM((D, F), BF16),
                        pltpu.VMEM((D, F), BF16),
                        pltpu.VMEM((F, D), BF16)],
    )
    return pl.pallas_call(
        _expert_kernel,
        grid_spec=grid_spec,
        out_shape=jax.ShapeDtypeStruct((P, D), F32),
        compiler_params=_cparams(("arbitrary",)),
        name="moe_experts",
    )(blk_expert, nused, xs, w1, b1g, b1l, w2, b2, pm)


def _combine_kernel(dcur_ref, dnxt_ref, grow_ref, x1_ref, mod_ref, y_ref, o_ref, buf, sem, *, tc, nt):
    i = pl.program_id(0)
    slot = i % 2

    def issue(dref, sl):
        def body(r, carry):
            for k in range(TOP_K):
                pltpu.make_async_copy(y_ref.at[pl.ds(dref[0, k, r], 1)],
                                      buf.at[sl, k, pl.ds(r, 1)], sem.at[sl]).start()
            return carry
        lax.fori_loop(0, tc, body, 0)

    @pl.when(i == 0)
    def _():
        issue(dcur_ref, 0)

    @pl.when(i + 1 < nt)
    def _():
        issue(dnxt_ref, 1 - slot)

    for k in range(TOP_K):
        pltpu.make_async_copy(y_ref.at[pl.ds(0, tc)], buf.at[slot, k], sem.at[slot]).wait()
    gr = grow_ref[...]
    acc = gr[:, 0:1] * buf[slot, 0]
    for k in range(1, TOP_K):
        acc = acc + gr[:, k:k + 1] * buf[slot, k]
    o_ref[...] = x1_ref[...] + mod_ref[5:6, :] * acc


def _combine(dest3, grow, x1, mod3, y, S):
    T, D = x1.shape
    nt, _, tc = dest3.shape
    tiles_per_batch = S // tc
    kern = functools.partial(_combine_kernel, tc=tc, nt=nt)
    return pl.pallas_call(
        kern,
        grid=(nt,),
        in_specs=[pl.BlockSpec((1, TOP_K, tc), lambda i: (i, 0, 0), memory_space=pltpu.SMEM),
                  pl.BlockSpec((1, TOP_K, tc), lambda i: (jnp.minimum(i + 1, nt - 1), 0, 0),
                               memory_space=pltpu.SMEM),
                  pl.BlockSpec((tc, LANES), lambda i: (i, 0)),
                  pl.BlockSpec((tc, D), lambda i: (i, 0)),
                  pl.BlockSpec((None, 6, D), lambda i: (i // tiles_per_batch, 0, 0)),
                  pl.BlockSpec(memory_space=pl.ANY)],
        out_specs=pl.BlockSpec((tc, D), lambda i: (i, 0)),
        out_shape=jax.ShapeDtypeStruct((T, D), F32),
        scratch_shapes=[pltpu.VMEM((2, TOP_K, tc, D), F32),
                        pltpu.SemaphoreType.DMA((2,))],
        compiler_params=_cparams(("arbitrary",)),
        name="moe_combine",
    )(dest3, dest3, grow, x1, mod3, y)


def _tiles(S):
    t_attn = min(512, S // 4) if S >= 512 else S
    tm = min(512, t_attn)
    return dict(tm=tm, t_attn=t_attn, td=min(1024, S), tc=min(256, S), bm=512)


def kernel(x, c, w_ada, b_ada, g_mix, w_in, w_fourier, q_norm_g, k_norm_g, lambda_q1, lambda_k1,
           lambda_q2, lambda_k2, g_subln, w_out, rel_bias, g_ffn, w_router, b_router, w1, b1, w2, b2):
    B, S, D = x.shape
    T = B * S
    L = w_ada.shape[0]
    E = w_router.shape[-1]
    cfg = _tiles(S)
    tm, t_attn, td, tc, bm = cfg["tm"], cfg["t_attn"], cfg["td"], cfg["tc"], cfg["bm"]
    dqk = q_norm_g.shape[-1]
    scale = dqk ** -0.5
    fw = w_in.shape[-1] // 4

    btiles = _bias_tiles(rel_bias, t_attn)
    x2 = x.reshape(T, D)
    for l in range(L):
        lam_init = 0.8 - 0.6 * math.exp(-0.3 * l)
        mod3 = _ada(c, w_ada[l], b_ada[l]).reshape(B, 6, D)
        gq = (jnp.tile(q_norm_g[l], 2) * scale).reshape(1, LANES)
        gk = jnp.tile(k_norm_g[l], 2).reshape(1, LANES)
        uf, qx, kx, v = _in_proj(x2, mod3, g_mix[l].reshape(1, D), w_in[l].astype(BF16), gq, gk,
                                 S, tm, t_attn)
        yf = _fourier(uf.reshape(B, S, fw), w_fourier[l]).reshape(T, fw)
        lam4 = jnp.stack([lambda_q1[l], lambda_k1[l], lambda_q2[l], lambda_k2[l]]).astype(F32)
        yd = _attention(rel_bias, qx.reshape(B, S, -1), kx.reshape(B, S, -1), v.reshape(B, S, fw),
                        btiles, lam4, g_subln[l].reshape(1, LANES), t_attn, lam_init).reshape(T, fw)
        x1, hp, idx_t, rank_t, grow, cnt = _out_proj(
            yf, yd, x2, mod3, w_out[l].astype(BF16), g_ffn[l].reshape(1, D),
            w_router[l].T.astype(BF16), b_router[l].reshape(E, 1), S, tm)

        counts = cnt[:, 0]
        padded = (counts + bm - 1) // bm * bm
        pends = jnp.cumsum(padded)
        pstarts = pends - padded
        dest = pstarts[idx_t[:TOP_K]] + rank_t[:TOP_K]
        P = T * TOP_K + E * bm
        nblk = P // bm
        nused = (pends[-1] // bm).astype(jnp.int32).reshape(1)
        blk_start = jnp.arange(nblk, dtype=jnp.int32) * bm
        blk_expert = jnp.minimum(jnp.searchsorted(pends, blk_start, side="right"), E - 1).astype(jnp.int32)
        last_blk = jnp.where(padded > 0, pends // bm - 1, -1)
        tail_blk = nused[0] + jnp.arange(E, dtype=jnp.int32)
        tail_blk = jnp.where(tail_blk < nblk, tail_blk, -1)
        zero_blk = jnp.concatenate([last_blk, tail_blk]).astype(jnp.int32)

        def by_tile(a, tile):
            return a.reshape(TOP_K, T // tile, tile).transpose(1, 0, 2)

        xs = _dispatch(by_tile(dest, td), zero_blk, hp, P, bm)
        y = _experts(blk_expert, nused, xs, w1[l], b1[l][:, None, 0::2], b1[l][:, None, 1::2],
                     w2[l], b2[l][:, None, :], bm)
        x2 = _combine(by_tile(dest, tc), grow, x1, mod3, y, S)
    return x2.reshape(B, S, D)
```

```python
import functools
import math

import numpy as np
import jax
import jax.numpy as jnp
from jax import lax
from jax.experimental import pallas as pl
from jax.experimental.pallas import tpu as pltpu

F32 = jnp.float32
BF16 = jnp.bfloat16

EPS = 1e-6
N_FOURIER_GROUPS = 4
N_DIFF_HEADS = 4
TOP_K = 4
N_BUCKETS = 32
MAX_DISTANCE = 128
SWIGLU_ALPHA = 1.702
SWIGLU_LIMIT = 7.0
LANES = 128
DFT_RADIX = 4
VMEM_LIMIT = 56 * 1024 * 1024


def _cparams(sem):
    return pltpu.CompilerParams(dimension_semantics=sem, vmem_limit_bytes=VMEM_LIMIT)


def _ada_kernel(c_ref, w_ref, b_ref, o_ref):
    c = c_ref[...]
    cond = c * jax.nn.sigmoid(c)
    o_ref[...] = jnp.dot(cond, w_ref[...], preferred_element_type=F32,
                         precision=lax.Precision.HIGHEST) + b_ref[...]


def _ada(c, w, b):
    B, D = c.shape
    N = w.shape[1]
    tn = 1536 if N % 1536 == 0 else N
    return pl.pallas_call(
        _ada_kernel,
        grid=(N // tn,),
        in_specs=[pl.BlockSpec((B, D), lambda j: (0, 0)),
                  pl.BlockSpec((D, tn), lambda j: (0, j)),
                  pl.BlockSpec((1, tn), lambda j: (0, j))],
        out_specs=pl.BlockSpec((B, tn), lambda j: (0, j)),
        out_shape=jax.ShapeDtypeStruct((B, N), F32),
        compiler_params=_cparams(("arbitrary",)),
        name="ada",
    )(c, w, b.reshape(1, N))


def _group_rms_inv(xh, gmat, group):
    sq = xh * xh
    hi = sq.astype(BF16)
    lo = (sq - hi.astype(F32)).astype(BF16)
    ss = jnp.dot(hi, gmat, preferred_element_type=F32) + jnp.dot(lo, gmat, preferred_element_type=F32)
    return lax.rsqrt(ss * (1.0 / group) + EPS)


def _in_proj_kernel(x_ref, mod_ref, g_ref, w_ref, gq_ref, gk_ref, gmat_ref,
                    uf_ref, q_ref, k_ref, v_ref, *, tiles_per_batch, tm, t_attn, n_chunks, fw, qw):
    i = pl.program_id(0)
    x = x_ref[...]
    ms = jnp.mean(x * x, axis=-1, keepdims=True)
    y = x * lax.rsqrt(ms + EPS) * g_ref[...]
    h = y * (1.0 + mod_ref[1:2, :]) + mod_ref[0:1, :]
    proj = jnp.dot(h.astype(BF16), w_ref[...], preferred_element_type=F32)
    uf_ref[...] = proj[:, :fw].astype(BF16)
    v_ref[...] = proj[:, fw + 2 * qw:].astype(BF16)

    lane = lax.broadcasted_iota(jnp.int32, (1, LANES), 1)
    chunk = ((i % tiles_per_batch) * tm) // t_attn
    gmat = gmat_ref[...]
    half = LANES // 2
    for hd in range(N_DIFF_HEADS):
        qh = proj[:, fw + hd * LANES: fw + (hd + 1) * LANES]
        kh = proj[:, fw + qw + hd * LANES: fw + qw + (hd + 1) * LANES]
        qn = (qh * _group_rms_inv(qh, gmat, half) * gq_ref[...]).astype(BF16)
        kn = (kh * _group_rms_inv(kh, gmat, half) * gk_ref[...]).astype(BF16)
        for m in range(2):
            fl = lane - half if m == 0 else lane
            is_data = (lane < half) if m == 0 else (lane >= half)
            f = jnp.where(fl < n_chunks, fl, fl - n_chunks)
            onehot = ((fl >= 0) & (fl < 2 * n_chunks) & (f == chunk)).astype(BF16)
            c0 = (2 * hd + m) * LANES
            q_ref[:, c0:c0 + LANES] = jnp.where(is_data, qn, jnp.zeros_like(qn))
            k_ref[:, c0:c0 + LANES] = jnp.where(is_data, kn, onehot)


def _in_proj(x2, mod3, g_mix, w_in_b, gq, gk, S, tm, t_attn):
    T, D = x2.shape
    ncols = w_in_b.shape[1]
    fw = ncols // 4
    qw = fw
    tiles_per_batch = S // tm
    n_chunks = S // t_attn
    half = LANES // 2
    gmat = (np.arange(LANES)[:, None] // half == np.arange(LANES)[None, :] // half)
    gmat = jnp.asarray(gmat, BF16)
    kern = functools.partial(_in_proj_kernel, tiles_per_batch=tiles_per_batch, tm=tm, t_attn=t_attn,
                             n_chunks=n_chunks, fw=fw, qw=qw)
    ext = 2 * qw
    return pl.pallas_call(
        kern,
        grid=(T // tm,),
        in_specs=[pl.BlockSpec((tm, D), lambda i: (i, 0)),
                  pl.BlockSpec((None, 6, D), lambda i: (i // tiles_per_batch, 0, 0)),
                  pl.BlockSpec((1, D), lambda i: (0, 0)),
                  pl.BlockSpec((D, ncols), lambda i: (0, 0)),
                  pl.BlockSpec((1, LANES), lambda i: (0, 0)),
                  pl.BlockSpec((1, LANES), lambda i: (0, 0)),
                  pl.BlockSpec((LANES, LANES), lambda i: (0, 0))],
        out_specs=[pl.BlockSpec((tm, fw), lambda i: (i, 0)),
                   pl.BlockSpec((tm, ext), lambda i: (i, 0)),
                   pl.BlockSpec((tm, ext), lambda i: (i, 0)),
                   pl.BlockSpec((tm, fw), lambda i: (i, 0))],
        out_shape=[jax.ShapeDtypeStruct((T, fw), BF16),
                   jax.ShapeDtypeStruct((T, ext), BF16),
                   jax.ShapeDtypeStruct((T, ext), BF16),
                   jax.ShapeDtypeStruct((T, fw), BF16)],
        compiler_params=_cparams(("arbitrary",)),
        name="in_proj",
    )(x2, mod3, g_mix, w_in_b, gq, gk, gmat)


def _fourier_kernel(uf_ref, wcs_ref, tw_ref, dft_ref, o_ref, tt_sc, *, M, gw):
    c = pl.program_id(1)

    @pl.when(c == 0)
    def _():
        for g in range(N_FOURIER_GROUPS):
            zr, zi = [], []
            for j in range(DFT_RADIX):
                ab = jnp.dot(uf_ref[j * M:(j + 1) * M, g * gw:(g + 1) * gw], wcs_ref[g],
                             preferred_element_type=F32)
                zr.append(ab[:, :gw])
                zi.append(-ab[:, gw:])
            yr = [zr[0] + zr[1] + zr[2] + zr[3],
                  zr[0] + zi[1] - zr[2] - zi[3],
                  zr[0] - zr[1] + zr[2] - zr[3],
                  zr[0] - zi[1] - zr[2] + zi[3]]
            yi = [zi[0] + zi[1] + zi[2] + zi[3],
                  zi[0] - zr[1] - zi[2] + zr[3],
                  zi[0] - zi[1] + zi[2] - zi[3],
                  zi[0] + zr[1] - zi[2] - zr[3]]
            for k2 in range(DFT_RADIX):
                cs = tw_ref[:, k2:k2 + 1]
                sn = tw_ref[:, DFT_RADIX + k2:DFT_RADIX + k2 + 1]
                tr = cs * yr[k2] + sn * yi[k2]
                ti = cs * yi[k2] - sn * yr[k2]
                tt_sc[k2, 0:M, g * gw:(g + 1) * gw] = tr.astype(BF16)
                tt_sc[k2, M:2 * M, g * gw:(g + 1) * gw] = ti.astype(BF16)

    o_ref[...] = jnp.dot(dft_ref[...], tt_sc[c], preferred_element_type=F32).astype(BF16)


def _fourier(uf3, w_f):
    B, S, W = uf3.shape
    G = N_FOURIER_GROUPS
    gw = W // G
    M = S // DFT_RADIX
    norm = 1.0 / math.sqrt(S * gw)
    cidx = (np.arange(gw)[:, None] * np.arange(gw)[None, :]) % gw
    ang_c = jnp.asarray(cidx, F32) * (2.0 * math.pi / gw)
    cc, sc = jnp.cos(ang_c) * norm, jnp.sin(ang_c) * norm
    hp = lax.Precision.HIGHEST
    wcs = jnp.concatenate([jnp.einsum("ab,gbd->gad", cc, w_f, precision=hp),
                           jnp.einsum("ab,gbd->gad", sc, w_f, precision=hp)], axis=-1).astype(BF16)
    n1 = np.arange(M)
    tw_idx = (np.arange(DFT_RADIX)[None, :] * n1[:, None]) % S
    ang_t = jnp.asarray(tw_idx, F32) * (2.0 * math.pi / S)
    tw = jnp.concatenate([jnp.cos(ang_t), jnp.sin(ang_t),
                          jnp.zeros((M, LANES - 2 * DFT_RADIX), F32)], axis=1)
    pidx = (n1[:, None] * n1[None, :]) % M
    ang_p = jnp.asarray(pidx, F32) * (2.0 * math.pi / M)
    dft = jnp.concatenate([jnp.cos(ang_p), jnp.sin(ang_p)], axis=1).astype(BF16)
    kern = functools.partial(_fourier_kernel, M=M, gw=gw)
    return pl.pallas_call(
        kern,
        grid=(B, DFT_RADIX),
        in_specs=[pl.BlockSpec((None, S, W), lambda b, c: (b, 0, 0)),
                  pl.BlockSpec((G, gw, 2 * gw), lambda b, c: (0, 0, 0)),
                  pl.BlockSpec((M, LANES), lambda b, c: (0, 0)),
                  pl.BlockSpec((M, 2 * M), lambda b, c: (0, 0))],
        out_specs=pl.BlockSpec((None, M, W), lambda b, c: (b, 0, c)),
        out_shape=jax.ShapeDtypeStruct((B, M, DFT_RADIX * W), BF16),
        scratch_shapes=[pltpu.VMEM((DFT_RADIX, 2 * M, W), BF16)],
        compiler_params=_cparams(("arbitrary", "arbitrary")),
        name="fourier",
    )(uf3, wcs, tw, dft)


def _bias_kernel(rb_ref, o_ref, *, t):
    h = pl.program_id(0)
    d = pl.program_id(1) - 1
    row = lax.broadcasted_iota(jnp.int32, (t, t), 0)
    col = lax.broadcasted_iota(jnp.int32, (t, t), 1)
    rel = d * t + col - row
    nb = N_BUCKETS // 2
    max_exact = nb // 2
    ret = jnp.where(rel > 0, nb, 0)
    n = jnp.abs(rel)
    nf = jnp.maximum(n, 1).astype(F32)
    large = max_exact + (jnp.log(nf / max_exact) / math.log(MAX_DISTANCE / max_exact)
                         * (nb - max_exact)).astype(jnp.int32)
    large = jnp.minimum(large, nb - 1)
    bucket = ret + jnp.where(n < max_exact, n, large)
    val = jnp.zeros((t, t), F32)
    for j in range(N_BUCKETS):
        val = jnp.where(bucket == j, rb_ref[j * N_DIFF_HEADS + h], val)
    o_ref[...] = val


def _bias_tiles(rel_bias, t):
    H = N_DIFF_HEADS
    return pl.pallas_call(
        functools.partial(_bias_kernel, t=t),
        grid=(H, 3),
        in_specs=[pl.BlockSpec(memory_space=pltpu.SMEM)],
        out_specs=pl.BlockSpec((None, None, t, t), lambda h, d: (h, d, 0, 0)),
        out_shape=jax.ShapeDtypeStruct((H, 3, t, t), F32),
        compiler_params=_cparams(("arbitrary", "arbitrary")),
        name="t5_bias",
    )(rel_bias.reshape(-1))


def _far_bucket_consts():
    nb = N_BUCKETS // 2
    return nb - 1, 2 * nb - 1


def _attn_kernel(rb_ref, q_ref, k_ref, v_ref, b_ref, lam_ref, gs_ref, o_ref,
                 s_sc, p_sc, vext_sc, m_sc, *, t, n, lam_init):
    h = pl.program_id(1)
    i = pl.program_id(2)
    half = LANES // 2

    @pl.when(i == 0)
    def _():
        vext_sc[:, :LANES] = v_ref[...]
        vext_sc[:, LANES:] = jnp.ones(v_ref.shape, BF16)

    left_b, right_b = _far_bucket_consts()
    c_left = rb_ref[left_b * N_DIFF_HEADS + h]
    c_right = rb_ref[right_b * N_DIFF_HEADS + h]
    lane = lax.broadcasted_iota(jnp.int32, (1, LANES), 1)

    def one_map(m):
        fl = lane - half if m == 0 else lane
        is_data = (lane < half) if m == 0 else (lane >= half)
        f = jnp.where(fl < n, fl, fl - n)
        cfar = jnp.where(f < i - 1, c_left, jnp.where(f > i + 1, c_right, 0.0))
        c_hi = cfar.astype(BF16)
        c_lo = (cfar - c_hi.astype(F32)).astype(BF16)
        feat = jnp.where(fl < n, c_hi, c_lo)
        feat = jnp.where((fl >= 0) & (fl < 2 * n), feat, jnp.zeros_like(feat))
        qm = jnp.where(is_data, q_ref[:, m * LANES:(m + 1) * LANES], feat)

        def scores(j):
            kc = k_ref[pl.ds(pl.multiple_of(j * t, t), t), m * LANES:(m + 1) * LANES]
            return lax.dot_general(qm, kc, (((1,), (1,)), ((), ())), preferred_element_type=F32)

        def keep(j, s):
            s_sc[j] = s
            tile_max = s[:, 0:LANES]
            for c in range(1, t // LANES):
                tile_max = jnp.maximum(tile_max, s[:, c * LANES:(c + 1) * LANES])
            m_sc[...] = jnp.maximum(m_sc[...], tile_max)

        def far(j, carry):
            keep(j, scores(j))
            return carry

        m_sc[...] = jnp.full(m_sc.shape, -jnp.inf, F32)
        lax.fori_loop(0, jnp.maximum(i - 1, 0), far, 0)
        lax.fori_loop(jnp.minimum(i + 2, n), n, far, 0)
        for d in (-1, 0, 1):
            @pl.when((i + d >= 0) & (i + d < n))
            def _():
                keep(i + d, scores(i + d) + b_ref[d + 1])

        mx = jnp.max(m_sc[...], axis=-1, keepdims=True)
        for j in range(n):
            p_sc[:, j * t:(j + 1) * t] = jnp.exp(s_sc[j] - mx).astype(BF16)
        o = jnp.dot(p_sc[...], vext_sc[...], preferred_element_type=F32)
        return o[:, :LANES] / o[:, LANES:LANES + 1]

    lam = (jnp.exp(jnp.sum(lam_ref[0:1, :] * lam_ref[1:2, :], axis=-1, keepdims=True))
           - jnp.exp(jnp.sum(lam_ref[2:3, :] * lam_ref[3:4, :], axis=-1, keepdims=True)) + lam_init)
    o = one_map(0) - lam * one_map(1)
    ms = jnp.mean(o * o, axis=-1, keepdims=True)
    o_ref[...] = (o * lax.rsqrt(ms + EPS) * gs_ref[...] * (1.0 - lam_init)).astype(BF16)


def _attention(rel_bias, q3, k3, v3, btiles, lam4, g_subln, t, lam_init):
    B, S, _ = q3.shape
    H = N_DIFF_HEADS
    n = S // t
    kern = functools.partial(_attn_kernel, t=t, n=n, lam_init=lam_init)
    return pl.pallas_call(
        kern,
        grid=(B, H, n),
        in_specs=[pl.BlockSpec(memory_space=pltpu.SMEM),
                  pl.BlockSpec((None, t, 2 * LANES), lambda b, h, i: (b, i, h)),
                  pl.BlockSpec((None, S, 2 * LANES), lambda b, h, i: (b, 0, h)),
                  pl.BlockSpec((None, S, LANES), lambda b, h, i: (b, 0, h)),
                  pl.BlockSpec((None, 3, t, t), lambda b, h, i: (h, 0, 0, 0)),
                  pl.BlockSpec((4, LANES // 2), lambda b, h, i: (0, 0)),
                  pl.BlockSpec((1, LANES), lambda b, h, i: (0, 0))],
        out_specs=pl.BlockSpec((None, t, LANES), lambda b, h, i: (b, i, h)),
        out_shape=jax.ShapeDtypeStruct((B, S, H * LANES), BF16),
        scratch_shapes=[pltpu.VMEM((n, t, t), F32),
                        pltpu.VMEM((t, S), BF16),
                        pltpu.VMEM((S, 2 * LANES), BF16),
                        pltpu.VMEM((t, LANES), F32)],
        compiler_params=_cparams(("arbitrary", "arbitrary", "arbitrary")),
        name="diff_attn",
    )(rel_bias.reshape(-1), q3, k3, v3, btiles, lam4, g_subln)


def _out_proj_kernel(yf_ref, yd_ref, x_ref, mod_ref, wo_ref, g_ref, wr_ref, br_ref,
                     x1_ref, hp_ref, idx_ref, rank_ref, grow_ref, cnt_ref, run_sc, *, tm, fw, E):
    i = pl.program_id(0)

    @pl.when(i == 0)
    def _():
        run_sc[...] = jnp.zeros(run_sc.shape, F32)

    mix = (jnp.dot(yf_ref[...], wo_ref[:fw, :], preferred_element_type=F32)
           + jnp.dot(yd_ref[...], wo_ref[fw:, :], preferred_element_type=F32))
    x1 = x_ref[...] + mod_ref[2:3, :] * mix
    x1_ref[...] = x1
    ms = jnp.mean(x1 * x1, axis=-1, keepdims=True)
    h = x1 * lax.rsqrt(ms + EPS) * g_ref[...]
    h = h * (1.0 + mod_ref[4:5, :]) + mod_ref[3:4, :]
    hb = h.astype(BF16)
    bits = lax.bitcast_convert_type(hb.astype(F32), jnp.uint32)
    D = bits.shape[1]
    hp_ref[...] = (bits[:, :D // 2] >> 16) | (bits[:, D // 2:] & jnp.uint32(0xFFFF0000))

    logits = lax.dot_general(wr_ref[...], hb, (((1,), (1,)), ((), ())),
                             preferred_element_type=F32) + br_ref[...]
    rowid = lax.broadcasted_iota(jnp.int32, (E, tm), 0)
    vals, idxs = [], []
    comb = jnp.zeros((E, tm), F32)
    l = logits
    for _ in range(TOP_K):
        mv = jnp.max(l, axis=0, keepdims=True)
        ix = jnp.min(jnp.where(l == mv, rowid, E), axis=0, keepdims=True)
        sel = rowid == ix
        vals.append(mv)
        idxs.append(ix)
        comb = comb + sel.astype(F32)
        l = jnp.where(sel, -jnp.inf, l)
    es = [jnp.exp(v - vals[0]) for v in vals]
    den = es[0] + es[1] + es[2] + es[3]
    gates = [e / den for e in es]

    r_i = lax.broadcasted_iota(jnp.int32, (tm, tm), 0)
    c_i = lax.broadcasted_iota(jnp.int32, (tm, tm), 1)
    upper = (r_i < c_i).astype(BF16)
    before = jnp.dot(comb.astype(BF16), upper, preferred_element_type=F32) + run_sc[:, 0:1]
    ranks = [jnp.sum(jnp.where(rowid == ix, before, 0.0), axis=0, keepdims=True) for ix in idxs]
    run_sc[...] = run_sc[...] + jnp.sum(comb, axis=1, keepdims=True)
    cnt_ref[...] = run_sc[...].astype(jnp.int32)

    pad_i = jnp.zeros((8 - TOP_K, tm), jnp.int32)
    idx_ref[...] = jnp.concatenate(idxs + [pad_i], axis=0)
    rank_ref[...] = jnp.concatenate([r.astype(jnp.int32) for r in ranks] + [pad_i], axis=0)
    gpad = jnp.concatenate(gates + [jnp.zeros((LANES - TOP_K, tm), F32)], axis=0)
    grow_ref[...] = gpad.T


def _out_proj(yf2, yd2, x2, mod3, w_out_b, g_ffn, wr_t, br_t, S, tm):
    T, D = x2.shape
    fw = yf2.shape[1]
    E = wr_t.shape[0]
    tiles_per_batch = S // tm
    kern = functools.partial(_out_proj_kernel, tm=tm, fw=fw, E=E)
    return pl.pallas_call(
        kern,
        grid=(T // tm,),
        in_specs=[pl.BlockSpec((tm, fw), lambda i: (i, 0)),
                  pl.BlockSpec((tm, fw), lambda i: (i, 0)),
                  pl.BlockSpec((tm, D), lambda i: (i, 0)),
                  pl.BlockSpec((None, 6, D), lambda i: (i // tiles_per_batch, 0, 0)),
                  pl.BlockSpec((D, D), lambda i: (0, 0)),
                  pl.BlockSpec((1, D), lambda i: (0, 0)),
                  pl.BlockSpec((E, D), lambda i: (0, 0)),
                  pl.BlockSpec((E, 1), lambda i: (0, 0))],
        out_specs=[pl.BlockSpec((tm, D), lambda i: (i, 0)),
                   pl.BlockSpec((tm, D // 2), lambda i: (i, 0)),
                   pl.BlockSpec((8, tm), lambda i: (0, i)),
                   pl.BlockSpec((8, tm), lambda i: (0, i)),
                   pl.BlockSpec((tm, LANES), lambda i: (i, 0)),
                   pl.BlockSpec((E, LANES), lambda i: (0, 0))],
        out_shape=[jax.ShapeDtypeStruct((T, D), F32),
                   jax.ShapeDtypeStruct((T, D // 2), jnp.uint32),
                   jax.ShapeDtypeStruct((8, T), jnp.int32),
                   jax.ShapeDtypeStruct((8, T), jnp.int32),
                   jax.ShapeDtypeStruct((T, LANES), F32),
                   jax.ShapeDtypeStruct((E, LANES), jnp.int32)],
        scratch_shapes=[pltpu.VMEM((E, LANES), F32)],
        compiler_params=_cparams(("arbitrary",)),
        name="out_proj_router",
    )(yf2, yd2, x2, mod3, w_out_b, g_ffn, wr_t, br_t)


def _dispatch_kernel(dest_ref, zb_ref, hp_ref, xs_ref, zero_sc, sem, zsem, *, td, bm, E):
    i = pl.program_id(0)

    @pl.when(i == 0)
    def _():
        zero_sc[...] = jnp.zeros(zero_sc.shape, jnp.uint32)
        for e in range(2 * E):
            @pl.when(zb_ref[e] >= 0)
            def _():
                pltpu.make_async_copy(zero_sc, xs_ref.at[pl.ds(zb_ref[e] * bm, bm)], zsem).start()
        for e in range(2 * E):
            @pl.when(zb_ref[e] >= 0)
            def _():
                pltpu.make_async_copy(zero_sc, xs_ref.at[pl.ds(zb_ref[e] * bm, bm)], zsem).wait()

    def issue(r, carry):
        for k in range(TOP_K):
            pltpu.make_async_copy(hp_ref.at[pl.ds(r, 1)], xs_ref.at[pl.ds(dest_ref[0, k, r], 1)], sem).start()
        return carry

    lax.fori_loop(0, td, issue, 0)
    for k in range(TOP_K):
        pltpu.make_async_copy(hp_ref, xs_ref.at[pl.ds(0, td)], sem).wait()


def _dispatch(dest3, zero_blk, hp, P, bm):
    T, Wd = hp.shape
    nt, _, td = dest3.shape
    E = zero_blk.shape[0] // 2
    kern = functools.partial(_dispatch_kernel, td=td, bm=bm, E=E)
    return pl.pallas_call(
        kern,
        grid=(nt,),
        in_specs=[pl.BlockSpec((1, TOP_K, td), lambda i: (i, 0, 0), memory_space=pltpu.SMEM),
                  pl.BlockSpec(memory_space=pltpu.SMEM),
                  pl.BlockSpec((td, Wd), lambda i: (i, 0))],
        out_specs=pl.BlockSpec(memory_space=pl.ANY),
        out_shape=jax.ShapeDtypeStruct((P, Wd), jnp.uint32),
        scratch_shapes=[pltpu.VMEM((bm, Wd), jnp.uint32),
                        pltpu.SemaphoreType.DMA,
                        pltpu.SemaphoreType.DMA],
        compiler_params=_cparams(("arbitrary",)),
        name="moe_dispatch",
    )(dest3, zero_blk, hp)


def _expert_kernel(be_ref, nu_ref, x_ref, w1_ref, b1g_ref, b1l_ref, w2_ref, b2_ref, pm_ref, o_ref,
                   w1g_sc, w1l_sc, w2_sc):
    blk = pl.program_id(0)
    nused = nu_ref[0]
    e = be_ref[blk]
    prev = be_ref[jnp.maximum(blk - 1, 0)]
    active = blk < nused
    tile = 2 * LANES
    n_tiles = w1_ref.shape[1] // tile

    @pl.when(active & ((blk == 0) | (e != prev)))
    def _():
        for c in range(n_tiles):
            wt = w1_ref[:, c * tile:(c + 1) * tile].astype(BF16)
            pw = jnp.dot(wt, pm_ref[...], preferred_element_type=F32)
            w1g_sc[:, c * LANES:(c + 1) * LANES] = pw[:, :LANES].astype(BF16)
            w1l_sc[:, c * LANES:(c + 1) * LANES] = pw[:, LANES:].astype(BF16)
        w2_sc[...] = w2_ref[...].astype(BF16)

    @pl.when(jnp.logical_not(active))
    def _():
        o_ref[...] = jnp.zeros(o_ref.shape, F32)

    @pl.when(active)
    def _():
        xp = x_ref[...]
        hw = xp.shape[1]
        xlo = lax.bitcast_convert_type(xp << 16, F32).astype(BF16)
        xhi = lax.bitcast_convert_type(xp & jnp.uint32(0xFFFF0000), F32).astype(BF16)
        zg = (jnp.dot(xlo, w1g_sc[:hw, :], preferred_element_type=F32)
              + jnp.dot(xhi, w1g_sc[hw:, :], preferred_element_type=F32) + b1g_ref[...])
        zl = (jnp.dot(xlo, w1l_sc[:hw, :], preferred_element_type=F32)
              + jnp.dot(xhi, w1l_sc[hw:, :], preferred_element_type=F32) + b1l_ref[...])
        g = jnp.minimum(zg, SWIGLU_LIMIT)
        lin = jnp.clip(zl, -SWIGLU_LIMIT, SWIGLU_LIMIT)
        act = g * jax.nn.sigmoid(SWIGLU_ALPHA * g) * (lin + 1.0)
        o_ref[...] = jnp.dot(act.astype(BF16), w2_sc[...], preferred_element_type=F32) + b2_ref[...]


def _experts(blk_expert, nused, xs, w1, b1g, b1l, w2, b2, bm):
    P, Wd = xs.shape
    E, D, F2 = w1.shape
    F = F2 // 2
    nblk = P // bm
    tile = 2 * LANES
    pm = np.zeros((tile, tile), np.float32)
    pm[2 * np.arange(LANES), np.arange(LANES)] = 1.0
    pm[2 * np.arange(LANES) + 1, LANES + np.arange(LANES)] = 1.0
    pm = jnp.asarray(pm, BF16)

    def row_map(b, be, nu):
        return (jnp.minimum(b, nu[0] - 1), 0)

    def exp_map(b, be, nu):
        return (be[b], 0, 0)

    grid_spec = pltpu.PrefetchScalarGridSpec(
        num_scalar_prefetch=2,
        grid=(nblk,),
        in_specs=[pl.BlockSpec((bm, Wd), row_map),
                  pl.BlockSpec((None, D, F2), exp_map),
                  pl.BlockSpec((None, 1, F), exp_map),
                  pl.BlockSpec((None, 1, F), exp_map),
                  pl.BlockSpec((None, F, D), exp_map),
                  pl.BlockSpec((None, 1, D), exp_map),
                  pl.BlockSpec((tile, tile), lambda b, be, nu: (0, 0))],
        out_specs=pl.BlockSpec((bm, D), lambda b, be, nu: (b, 0)),
        scratch_shapes=[pltpu.VMEM((D, F), BF16),
                        pltpu.VMEM((D, F), BF16),
                        pltpu.VMEM((F, D), BF16)],
    )
    return pl.pallas_call(
        _expert_kernel,
        grid_spec=grid_spec,
        out_shape=jax.ShapeDtypeStruct((P, D), F32),
        compiler_params=_cparams(("arbitrary",)),
        name="moe_experts",
    )(blk_expert, nused, xs, w1, b1g, b1l, w2, b2, pm)


def _combine_kernel(dcur_ref, dnxt_ref, grow_ref, x1_ref, mod_ref, y_ref, o_ref, buf, sem, *, tc, nt):
    i = pl.program_id(0)
    slot = i % 2

    def issue(dref, sl):
        def body(r, carry):
            for k in range(TOP_K):
                pltpu.make_async_copy(y_ref.at[pl.ds(dref[0, k, r], 1)],
                                      buf.at[sl, k, pl.ds(r, 1)], sem.at[sl]).start()
            return carry
        lax.fori_loop(0, tc, body, 0)

    @pl.when(i == 0)
    def _():
        issue(dcur_ref, 0)

    @pl.when(i + 1 < nt)
    def _():
        issue(dnxt_ref, 1 - slot)

    for k in range(TOP_K):
        pltpu.make_async_copy(y_ref.at[pl.ds(0, tc)], buf.at[slot, k], sem.at[slot]).wait()
    gr = grow_ref[...]
    acc = gr[:, 0:1] * buf[slot, 0]
    for k in range(1, TOP_K):
        acc = acc + gr[:, k:k + 1] * buf[slot, k]
    o_ref[...] = x1_ref[...] + mod_ref[5:6, :] * acc


def _combine(dest3, grow, x1, mod3, y, S):
    T, D = x1.shape
    nt, _, tc = dest3.shape
    tiles_per_batch = S // tc
    kern = functools.partial(_combine_kernel, tc=tc, nt=nt)
    return pl.pallas_call(
        kern,
        grid=(nt,),
        in_specs=[pl.BlockSpec((1, TOP_K, tc), lambda i: (i, 0, 0), memory_space=pltpu.SMEM),
                  pl.BlockSpec((1, TOP_K, tc), lambda i: (jnp.minimum(i + 1, nt - 1), 0, 0),
                               memory_space=pltpu.SMEM),
                  pl.BlockSpec((tc, LANES), lambda i: (i, 0)),
                  pl.BlockSpec((tc, D), lambda i: (i, 0)),
                  pl.BlockSpec((None, 6, D), lambda i: (i // tiles_per_batch, 0, 0)),
                  pl.BlockSpec(memory_space=pl.ANY)],
        out_specs=pl.BlockSpec((tc, D), lambda i: (i, 0)),
        out_shape=jax.ShapeDtypeStruct((T, D), F32),
        scratch_shapes=[pltpu.VMEM((2, TOP_K, tc, D), F32),
                        pltpu.SemaphoreType.DMA((2,))],
        compiler_params=_cparams(("arbitrary",)),
        name="moe_combine",
    )(dest3, dest3, grow, x1, mod3, y)


def _tiles(S):
    t_attn = min(512, S // 4) if S >= 512 else S
    tm = min(512, t_attn)
    return dict(tm=tm, t_attn=t_attn, td=min(1024, S), tc=min(256, S), bm=512)


def kernel(x, c, w_ada, b_ada, g_mix, w_in, w_fourier, q_norm_g, k_norm_g, lambda_q1, lambda_k1,
           lambda_q2, lambda_k2, g_subln, w_out, rel_bias, g_ffn, w_router, b_router, w1, b1, w2, b2):
    B, S, D = x.shape
    T = B * S
    L = w_ada.shape[0]
    E = w_router.shape[-1]
    cfg = _tiles(S)
    tm, t_attn, td, tc, bm = cfg["tm"], cfg["t_attn"], cfg["td"], cfg["tc"], cfg["bm"]
    dqk = q_norm_g.shape[-1]
    scale = dqk ** -0.5
    fw = w_in.shape[-1] // 4

    btiles = _bias_tiles(rel_bias, t_attn)
    x2 = x.reshape(T, D)
    for l in range(L):
        lam_init = 0.8 - 0.6 * math.exp(-0.3 * l)
        mod3 = _ada(c, w_ada[l], b_ada[l]).reshape(B, 6, D)
        gq = (jnp.tile(q_norm_g[l], 2) * scale).reshape(1, LANES)
        gk = jnp.tile(k_norm_g[l], 2).reshape(1, LANES)
        uf, qx, kx, v = _in_proj(x2, mod3, g_mix[l].reshape(1, D), w_in[l].astype(BF16), gq, gk,
                                 S, tm, t_attn)
        yf = _fourier(uf.reshape(B, S, fw), w_fourier[l]).reshape(T, fw)
        lam4 = jnp.stack([lambda_q1[l], lambda_k1[l], lambda_q2[l], lambda_k2[l]]).astype(F32)
        yd = _attention(rel_bias, qx.reshape(B, S, -1), kx.reshape(B, S, -1), v.reshape(B, S, fw),
                        btiles, lam4, g_subln[l].reshape(1, LANES), t_attn, lam_init).reshape(T, fw)
        x1, hp, idx_t, rank_t, grow, cnt = _out_proj(
            yf, yd, x2, mod3, w_out[l].astype(BF16), g_ffn[l].reshape(1, D),
            w_router[l].T.astype(BF16), b_router[l].reshape(E, 1), S, tm)

        counts = cnt[:, 0]
        padded = (counts + bm - 1) // bm * bm
        pends = jnp.cumsum(padded)
        pstarts = pends - padded
        eids = jnp.arange(E, dtype=jnp.int32)[:, None, None]
        dest = jnp.sum(jnp.where(idx_t[None, :TOP_K] == eids, pstarts[:, None, None], 0), axis=0) \
            + rank_t[:TOP_K]
        P = T * TOP_K + E * bm
        nblk = P // bm
        nused = (pends[-1] // bm).astype(jnp.int32).reshape(1)
        blk_start = jnp.arange(nblk, dtype=jnp.int32) * bm
        blk_expert = jnp.minimum(jnp.sum(pends[None, :] <= blk_start[:, None], axis=1), E - 1).astype(jnp.int32)
        last_blk = jnp.where(padded > 0, pends // bm - 1, -1)
        tail_blk = nused[0] + jnp.arange(E, dtype=jnp.int32)
        tail_blk = jnp.where(tail_blk < nblk, tail_blk, -1)
        zero_blk = jnp.concatenate([last_blk, tail_blk]).astype(jnp.int32)

        def by_tile(a, tile):
            return a.reshape(TOP_K, T // tile, tile).transpose(1, 0, 2)

        xs = _dispatch(by_tile(dest, td), zero_blk, hp, P, bm)
        y = _experts(blk_expert, nused, xs, w1[l], b1[l][:, None, 0::2], b1[l][:, None, 1::2],
                     w2[l], b2[l][:, None, :], bm)
        x2 = _combine(by_tile(dest, tc), grow, x1, mod3, y, S)
    return x2.reshape(B, S, D)
```

```python
import functools
import math

import numpy as np
import jax
import jax.numpy as jnp
from jax import lax
from jax.experimental import pallas as pl
from jax.experimental.pallas import tpu as pltpu

F32 = jnp.float32
BF16 = jnp.bfloat16

EPS = 1e-6
N_FOURIER_GROUPS = 4
N_DIFF_HEADS = 4
TOP_K = 4
N_BUCKETS = 32
MAX_DISTANCE = 128
SWIGLU_ALPHA = 1.702
SWIGLU_LIMIT = 7.0
LANES = 128
DFT_RADIX = 4
VMEM_LIMIT = 56 * 1024 * 1024


def _cparams(sem):
    return pltpu.CompilerParams(dimension_semantics=sem, vmem_limit_bytes=VMEM_LIMIT)


def _ada_kernel(c_ref, w_ref, b_ref, o_ref):
    c = c_ref[...]
    cond = c * jax.nn.sigmoid(c)
    o_ref[...] = jnp.dot(cond, w_ref[...], preferred_element_type=F32,
                         precision=lax.Precision.HIGHEST) + b_ref[...]


def _ada(c, w, b):
    B, D = c.shape
    N = w.shape[1]
    tn = 1536 if N % 1536 == 0 else N
    return pl.pallas_call(
        _ada_kernel,
        grid=(N // tn,),
        in_specs=[pl.BlockSpec((B, D), lambda j: (0, 0)),
                  pl.BlockSpec((D, tn), lambda j: (0, j)),
                  pl.BlockSpec((1, tn), lambda j: (0, j))],
        out_specs=pl.BlockSpec((B, tn), lambda j: (0, j)),
        out_shape=jax.ShapeDtypeStruct((B, N), F32),
        compiler_params=_cparams(("arbitrary",)),
        name="ada",
    )(c, w, b.reshape(1, N))


def _group_rms_inv(xh, gmat, group):
    sq = xh * xh
    hi = sq.astype(BF16)
    lo = (sq - hi.astype(F32)).astype(BF16)
    ss = jnp.dot(hi, gmat, preferred_element_type=F32) + jnp.dot(lo, gmat, preferred_element_type=F32)
    return lax.rsqrt(ss * (1.0 / group) + EPS)


def _in_proj_kernel(x_ref, mod_ref, g_ref, w_ref, gq_ref, gk_ref, gmat_ref,
                    uf_ref, q_ref, k_ref, v_ref, *, tiles_per_batch, tm, t_attn, n_chunks, fw, qw):
    i = pl.program_id(0)
    x = x_ref[...]
    ms = jnp.mean(x * x, axis=-1, keepdims=True)
    y = x * lax.rsqrt(ms + EPS) * g_ref[...]
    h = y * (1.0 + mod_ref[1:2, :]) + mod_ref[0:1, :]
    proj = jnp.dot(h.astype(BF16), w_ref[...], preferred_element_type=F32)
    uf_ref[...] = proj[:, :fw].astype(BF16)
    v_ref[...] = proj[:, fw + 2 * qw:].astype(BF16)

    lane = lax.broadcasted_iota(jnp.int32, (1, LANES), 1)
    chunk = ((i % tiles_per_batch) * tm) // t_attn
    gmat = gmat_ref[...]
    half = LANES // 2
    for hd in range(N_DIFF_HEADS):
        qh = proj[:, fw + hd * LANES: fw + (hd + 1) * LANES]
        kh = proj[:, fw + qw + hd * LANES: fw + qw + (hd + 1) * LANES]
        qn = (qh * _group_rms_inv(qh, gmat, half) * gq_ref[...]).astype(BF16)
        kn = (kh * _group_rms_inv(kh, gmat, half) * gk_ref[...]).astype(BF16)
        for m in range(2):
            fl = lane - half if m == 0 else lane
            is_data = (lane < half) if m == 0 else (lane >= half)
            f = jnp.where(fl < n_chunks, fl, fl - n_chunks)
            onehot = ((fl >= 0) & (fl < 2 * n_chunks) & (f == chunk)).astype(BF16)
            c0 = (2 * hd + m) * LANES
            q_ref[:, c0:c0 + LANES] = jnp.where(is_data, qn, jnp.zeros_like(qn))
            k_ref[:, c0:c0 + LANES] = jnp.where(is_data, kn, onehot)


def _in_proj(x2, mod3, g_mix, w_in_b, gq, gk, S, tm, t_attn):
    T, D = x2.shape
    ncols = w_in_b.shape[1]
    fw = ncols // 4
    qw = fw
    tiles_per_batch = S // tm
    n_chunks = S // t_attn
    half = LANES // 2
    gmat = (np.arange(LANES)[:, None] // half == np.arange(LANES)[None, :] // half)
    gmat = jnp.asarray(gmat, BF16)
    kern = functools.partial(_in_proj_kernel, tiles_per_batch=tiles_per_batch, tm=tm, t_attn=t_attn,
                             n_chunks=n_chunks, fw=fw, qw=qw)
    ext = 2 * qw
    return pl.pallas_call(
        kern,
        grid=(T // tm,),
        in_specs=[pl.BlockSpec((tm, D), lambda i: (i, 0)),
                  pl.BlockSpec((None, 6, D), lambda i: (i // tiles_per_batch, 0, 0)),
                  pl.BlockSpec((1, D), lambda i: (0, 0)),
                  pl.BlockSpec((D, ncols), lambda i: (0, 0)),
                  pl.BlockSpec((1, LANES), lambda i: (0, 0)),
                  pl.BlockSpec((1, LANES), lambda i: (0, 0)),
                  pl.BlockSpec((LANES, LANES), lambda i: (0, 0))],
        out_specs=[pl.BlockSpec((tm, fw), lambda i: (i, 0)),
                   pl.BlockSpec((tm, ext), lambda i: (i, 0)),
                   pl.BlockSpec((tm, ext), lambda i: (i, 0)),
                   pl.BlockSpec((tm, fw), lambda i: (i, 0))],
        out_shape=[jax.ShapeDtypeStruct((T, fw), BF16),
                   jax.ShapeDtypeStruct((T, ext), BF16),
                   jax.ShapeDtypeStruct((T, ext), BF16),
                   jax.ShapeDtypeStruct((T, fw), BF16)],
        compiler_params=_cparams(("arbitrary",)),
        name="in_proj",
    )(x2, mod3, g_mix, w_in_b, gq, gk, gmat)


def _fourier_kernel(uf_ref, wcs_ref, tw_ref, dft_ref, o_ref, tt_sc, *, M, gw):
    c = pl.program_id(1)

    @pl.when(c == 0)
    def _():
        for g in range(N_FOURIER_GROUPS):
            zr, zi = [], []
            for j in range(DFT_RADIX):
                ab = jnp.dot(uf_ref[j * M:(j + 1) * M, g * gw:(g + 1) * gw], wcs_ref[g],
                             preferred_element_type=F32)
                zr.append(ab[:, :gw])
                zi.append(-ab[:, gw:])
            yr = [zr[0] + zr[1] + zr[2] + zr[3],
                  zr[0] + zi[1] - zr[2] - zi[3],
                  zr[0] - zr[1] + zr[2] - zr[3],
                  zr[0] - zi[1] - zr[2] + zi[3]]
            yi = [zi[0] + zi[1] + zi[2] + zi[3],
                  zi[0] - zr[1] - zi[2] + zr[3],
                  zi[0] - zi[1] + zi[2] - zi[3],
                  zi[0] + zr[1] - zi[2] - zr[3]]
            for k2 in range(DFT_RADIX):
                cs = tw_ref[:, k2:k2 + 1]
                sn = tw_ref[:, DFT_RADIX + k2:DFT_RADIX + k2 + 1]
                tr = cs * yr[k2] + sn * yi[k2]
                ti = cs * yi[k2] - sn * yr[k2]
                tt_sc[k2, 0:M, g * gw:(g + 1) * gw] = tr.astype(BF16)
                tt_sc[k2, M:2 * M, g * gw:(g + 1) * gw] = ti.astype(BF16)

    o_ref[...] = jnp.dot(dft_ref[...], tt_sc[c], preferred_element_type=F32).astype(BF16)


def _fourier(uf3, w_f):
    B, S, W = uf3.shape
    G = N_FOURIER_GROUPS
    gw = W // G
    M = S // DFT_RADIX
    norm = 1.0 / math.sqrt(S * gw)
    cidx = (np.arange(gw)[:, None] * np.arange(gw)[None, :]) % gw
    ang_c = jnp.asarray(cidx, F32) * (2.0 * math.pi / gw)
    cc, sc = jnp.cos(ang_c) * norm, jnp.sin(ang_c) * norm
    hp = lax.Precision.HIGHEST
    wcs = jnp.concatenate([jnp.einsum("ab,gbd->gad", cc, w_f, precision=hp),
                           jnp.einsum("ab,gbd->gad", sc, w_f, precision=hp)], axis=-1).astype(BF16)
    n1 = np.arange(M)
    tw_idx = (np.arange(DFT_RADIX)[None, :] * n1[:, None]) % S
    ang_t = jnp.asarray(tw_idx, F32) * (2.0 * math.pi / S)
    tw = jnp.concatenate([jnp.cos(ang_t), jnp.sin(ang_t),
                          jnp.zeros((M, LANES - 2 * DFT_RADIX), F32)], axis=1)
    pidx = (n1[:, None] * n1[None, :]) % M
    ang_p = jnp.asarray(pidx, F32) * (2.0 * math.pi / M)
    dft = jnp.concatenate([jnp.cos(ang_p), jnp.sin(ang_p)], axis=1).astype(BF16)
    kern = functools.partial(_fourier_kernel, M=M, gw=gw)
    return pl.pallas_call(
        kern,
        grid=(B, DFT_RADIX),
        in_specs=[pl.BlockSpec((None, S, W), lambda b, c: (b, 0, 0)),
                  pl.BlockSpec((G, gw, 2 * gw), lambda b, c: (0, 0, 0)),
                  pl.BlockSpec((M, LANES), lambda b, c: (0, 0)),
                  pl.BlockSpec((M, 2 * M), lambda b, c: (0, 0))],
        out_specs=pl.BlockSpec((None, M, W), lambda b, c: (b, 0, c)),
        out_shape=jax.ShapeDtypeStruct((B, M, DFT_RADIX * W), BF16),
        scratch_shapes=[pltpu.VMEM((DFT_RADIX, 2 * M, W), BF16)],
        compiler_params=_cparams(("arbitrary", "arbitrary")),
        name="fourier",
    )(uf3, wcs, tw, dft)


def _bias_kernel(rb_ref, o_ref, *, t):
    h = pl.program_id(0)
    d = pl.program_id(1) - 1
    row = lax.broadcasted_iota(jnp.int32, (t, t), 0)
    col = lax.broadcasted_iota(jnp.int32, (t, t), 1)
    rel = d * t + col - row
    nb = N_BUCKETS // 2
    max_exact = nb // 2
    ret = jnp.where(rel > 0, nb, 0)
    n = jnp.abs(rel)
    nf = jnp.maximum(n, 1).astype(F32)
    large = max_exact + (jnp.log(nf / max_exact) / math.log(MAX_DISTANCE / max_exact)
                         * (nb - max_exact)).astype(jnp.int32)
    large = jnp.minimum(large, nb - 1)
    bucket = ret + jnp.where(n < max_exact, n, large)
    val = jnp.zeros((t, t), F32)
    for j in range(N_BUCKETS):
        val = jnp.where(bucket == j, rb_ref[j * N_DIFF_HEADS + h], val)
    o_ref[...] = jnp.where(pl.program_id(1) == 3, 0.0, val)


def _bias_tiles(rel_bias, t):
    H = N_DIFF_HEADS
    assert t >= MAX_DISTANCE, "far key tiles must lie entirely in the saturated bucket"
    return pl.pallas_call(
        functools.partial(_bias_kernel, t=t),
        grid=(H, 4),
        in_specs=[pl.BlockSpec(memory_space=pltpu.SMEM)],
        out_specs=pl.BlockSpec((None, None, t, t), lambda h, d: (h, d, 0, 0)),
        out_shape=jax.ShapeDtypeStruct((H, 4, t, t), F32),
        compiler_params=_cparams(("arbitrary", "arbitrary")),
        name="t5_bias",
    )(rel_bias.reshape(-1))


def _far_bucket_consts():
    nb = N_BUCKETS // 2
    return nb - 1, 2 * nb - 1


def _attn_kernel(rb_ref, q_ref, k_ref, v_ref, b_ref, lam_ref, gs_ref, o_ref,
                 s_sc, p_sc, vext_sc, m_sc, *, t, n, lam_init):
    h = pl.program_id(1)
    i = pl.program_id(2)
    half = LANES // 2

    @pl.when(i == 0)
    def _():
        vext_sc[:, :LANES] = v_ref[...]
        vext_sc[:, LANES:] = jnp.ones(v_ref.shape, BF16)

    left_b, right_b = _far_bucket_consts()
    c_left = rb_ref[left_b * N_DIFF_HEADS + h]
    c_right = rb_ref[right_b * N_DIFF_HEADS + h]
    lane = lax.broadcasted_iota(jnp.int32, (1, LANES), 1)

    def all_scores(m):
        fl = lane - half if m == 0 else lane
        is_data = (lane < half) if m == 0 else (lane >= half)
        f = jnp.where(fl < n, fl, fl - n)
        cfar = jnp.where(f < i - 1, c_left, jnp.where(f > i + 1, c_right, 0.0))
        c_hi = cfar.astype(BF16)
        c_lo = (cfar - c_hi.astype(F32)).astype(BF16)
        feat = jnp.where(fl < n, c_hi, c_lo)
        feat = jnp.where((fl >= 0) & (fl < 2 * n), feat, jnp.zeros_like(feat))
        qm = jnp.where(is_data, q_ref[:, m * LANES:(m + 1) * LANES], feat)
        for j in range(n):
            kc = k_ref[j * t:(j + 1) * t, m * LANES:(m + 1) * LANES]
            s = lax.dot_general(qm, kc, (((1,), (1,)), ((), ())), preferred_element_type=F32)
            d = j - i
            s = s + b_ref[jnp.where(jnp.abs(d) <= 1, d + 1, 3)]
            s_sc[m, j] = s
            tile_max = s[:, 0:LANES]
            for c in range(1, t // LANES):
                tile_max = jnp.maximum(tile_max, s[:, c * LANES:(c + 1) * LANES])
            if j == 0:
                m_sc[m] = tile_max
            else:
                m_sc[m] = jnp.maximum(m_sc[m], tile_max)

    def weighted_values(m):
        mx = jnp.max(m_sc[m], axis=-1, keepdims=True)
        for j in range(n):
            p_sc[m, :, j * t:(j + 1) * t] = jnp.exp(s_sc[m, j] - mx).astype(BF16)
        o = jnp.dot(p_sc[m], vext_sc[...], preferred_element_type=F32)
        return o[:, :LANES] / o[:, LANES:LANES + 1]

    all_scores(0)
    all_scores(1)
    lam = (jnp.exp(jnp.sum(lam_ref[0:1, :] * lam_ref[1:2, :], axis=-1, keepdims=True))
           - jnp.exp(jnp.sum(lam_ref[2:3, :] * lam_ref[3:4, :], axis=-1, keepdims=True)) + lam_init)
    o = weighted_values(0) - lam * weighted_values(1)
    ms = jnp.mean(o * o, axis=-1, keepdims=True)
    o_ref[...] = (o * lax.rsqrt(ms + EPS) * gs_ref[...] * (1.0 - lam_init)).astype(BF16)


def _attention(rel_bias, q3, k3, v3, btiles, lam4, g_subln, t, lam_init):
    B, S, _ = q3.shape
    H = N_DIFF_HEADS
    n = S // t
    kern = functools.partial(_attn_kernel, t=t, n=n, lam_init=lam_init)
    return pl.pallas_call(
        kern,
        grid=(B, H, n),
        in_specs=[pl.BlockSpec(memory_space=pltpu.SMEM),
                  pl.BlockSpec((None, t, 2 * LANES), lambda b, h, i: (b, i, h)),
                  pl.BlockSpec((None, S, 2 * LANES), lambda b, h, i: (b, 0, h)),
                  pl.BlockSpec((None, S, LANES), lambda b, h, i: (b, 0, h)),
                  pl.BlockSpec((None, 4, t, t), lambda b, h, i: (h, 0, 0, 0)),
                  pl.BlockSpec((4, LANES // 2), lambda b, h, i: (0, 0)),
                  pl.BlockSpec((1, LANES), lambda b, h, i: (0, 0))],
        out_specs=pl.BlockSpec((None, t, LANES), lambda b, h, i: (b, i, h)),
        out_shape=jax.ShapeDtypeStruct((B, S, H * LANES), BF16),
        scratch_shapes=[pltpu.VMEM((2, n, t, t), F32),
                        pltpu.VMEM((2, t, S), BF16),
                        pltpu.VMEM((S, 2 * LANES), BF16),
                        pltpu.VMEM((2, t, LANES), F32)],
        compiler_params=_cparams(("arbitrary", "arbitrary", "arbitrary")),
        name="diff_attn",
    )(rel_bias.reshape(-1), q3, k3, v3, btiles, lam4, g_subln)


def _out_proj_kernel(yf_ref, yd_ref, x_ref, mod_ref, wo_ref, g_ref, wr_ref, br_ref,
                     x1_ref, hp_ref, idx_ref, rank_ref, grow_ref, cnt_ref, run_sc, *, tm, fw, E):
    i = pl.program_id(0)

    @pl.when(i == 0)
    def _():
        run_sc[...] = jnp.zeros(run_sc.shape, F32)

    mix = (jnp.dot(yf_ref[...], wo_ref[:fw, :], preferred_element_type=F32)
           + jnp.dot(yd_ref[...], wo_ref[fw:, :], preferred_element_type=F32))
    x1 = x_ref[...] + mod_ref[2:3, :] * mix
    x1_ref[...] = x1
    ms = jnp.mean(x1 * x1, axis=-1, keepdims=True)
    h = x1 * lax.rsqrt(ms + EPS) * g_ref[...]
    h = h * (1.0 + mod_ref[4:5, :]) + mod_ref[3:4, :]
    hb = h.astype(BF16)
    bits = lax.bitcast_convert_type(hb.astype(F32), jnp.uint32)
    D = bits.shape[1]
    hp_ref[...] = (bits[:, :D // 2] >> 16) | (bits[:, D // 2:] & jnp.uint32(0xFFFF0000))

    logits = lax.dot_general(wr_ref[...], hb, (((1,), (1,)), ((), ())),
                             preferred_element_type=F32) + br_ref[...]
    rowid = lax.broadcasted_iota(jnp.int32, (E, tm), 0)
    vals, idxs = [], []
    comb = jnp.zeros((E, tm), F32)
    l = logits
    for _ in range(TOP_K):
        mv = jnp.max(l, axis=0, keepdims=True)
        ix = jnp.min(jnp.where(l == mv, rowid, E), axis=0, keepdims=True)
        sel = rowid == ix
        vals.append(mv)
        idxs.append(ix)
        comb = comb + sel.astype(F32)
        l = jnp.where(sel, -jnp.inf, l)
    es = [jnp.exp(v - vals[0]) for v in vals]
    den = es[0] + es[1] + es[2] + es[3]
    gates = [e / den for e in es]

    r_i = lax.broadcasted_iota(jnp.int32, (tm, tm), 0)
    c_i = lax.broadcasted_iota(jnp.int32, (tm, tm), 1)
    upper = (r_i < c_i).astype(BF16)
    before = jnp.dot(comb.astype(BF16), upper, preferred_element_type=F32) + run_sc[:, 0:1]
    ranks = [jnp.sum(jnp.where(rowid == ix, before, 0.0), axis=0, keepdims=True) for ix in idxs]
    run_sc[...] = run_sc[...] + jnp.sum(comb, axis=1, keepdims=True)
    cnt_ref[...] = run_sc[...].astype(jnp.int32)

    pad_i = jnp.zeros((8 - TOP_K, tm), jnp.int32)
    idx_ref[...] = jnp.concatenate(idxs + [pad_i], axis=0)
    rank_ref[...] = jnp.concatenate([r.astype(jnp.int32) for r in ranks] + [pad_i], axis=0)
    gpad = jnp.concatenate(gates + [jnp.zeros((LANES - TOP_K, tm), F32)], axis=0)
    grow_ref[...] = gpad.T


def _out_proj(yf2, yd2, x2, mod3, w_out_b, g_ffn, wr_t, br_t, S, tm):
    T, D = x2.shape
    fw = yf2.shape[1]
    E = wr_t.shape[0]
    tiles_per_batch = S // tm
    kern = functools.partial(_out_proj_kernel, tm=tm, fw=fw, E=E)
    return pl.pallas_call(
        kern,
        grid=(T // tm,),
        in_specs=[pl.BlockSpec((tm, fw), lambda i: (i, 0)),
                  pl.BlockSpec((tm, fw), lambda i: (i, 0)),
                  pl.BlockSpec((tm, D), lambda i: (i, 0)),
                  pl.BlockSpec((None, 6, D), lambda i: (i // tiles_per_batch, 0, 0)),
                  pl.BlockSpec((D, D), lambda i: (0, 0)),
                  pl.BlockSpec((1, D), lambda i: (0, 0)),
                  pl.BlockSpec((E, D), lambda i: (0, 0)),
                  pl.BlockSpec((E, 1), lambda i: (0, 0))],
        out_specs=[pl.BlockSpec((tm, D), lambda i: (i, 0)),
                   pl.BlockSpec((tm, D // 2), lambda i: (i, 0)),
                   pl.BlockSpec((8, tm), lambda i: (0, i)),
                   pl.BlockSpec((8, tm), lambda i: (0, i)),
                   pl.BlockSpec((tm, LANES), lambda i: (i, 0)),
                   pl.BlockSpec((E, LANES), lambda i: (0, 0))],
        out_shape=[jax.ShapeDtypeStruct((T, D), F32),
                   jax.ShapeDtypeStruct((T, D // 2), jnp.uint32),
                   jax.ShapeDtypeStruct((8, T), jnp.int32),
                   jax.ShapeDtypeStruct((8, T), jnp.int32),
                   jax.ShapeDtypeStruct((T, LANES), F32),
                   jax.ShapeDtypeStruct((E, LANES), jnp.int32)],
        scratch_shapes=[pltpu.VMEM((E, LANES), F32)],
        compiler_params=_cparams(("arbitrary",)),
        name="out_proj_router",
    )(yf2, yd2, x2, mod3, w_out_b, g_ffn, wr_t, br_t)


def _dispatch_kernel(dest_ref, zb_ref, hp_ref, xs_ref, zero_sc, sem, zsem, *, td, bm, E):
    i = pl.program_id(0)

    @pl.when(i == 0)
    def _():
        zero_sc[...] = jnp.zeros(zero_sc.shape, jnp.uint32)
        for e in range(2 * E):
            @pl.when(zb_ref[e] >= 0)
            def _():
                pltpu.make_async_copy(zero_sc, xs_ref.at[pl.ds(zb_ref[e] * bm, bm)], zsem).start()
        for e in range(2 * E):
            @pl.when(zb_ref[e] >= 0)
            def _():
                pltpu.make_async_copy(zero_sc, xs_ref.at[pl.ds(zb_ref[e] * bm, bm)], zsem).wait()

    def issue(r, carry):
        for k in range(TOP_K):
            pltpu.make_async_copy(hp_ref.at[pl.ds(r, 1)], xs_ref.at[pl.ds(dest_ref[0, k, r], 1)], sem).start()
        return carry

    lax.fori_loop(0, td, issue, 0)
    for k in range(TOP_K):
        pltpu.make_async_copy(hp_ref, xs_ref.at[pl.ds(0, td)], sem).wait()


def _dispatch(dest3, zero_blk, hp, P, bm):
    T, Wd = hp.shape
    nt, _, td = dest3.shape
    E = zero_blk.shape[0] // 2
    kern = functools.partial(_dispatch_kernel, td=td, bm=bm, E=E)
    return pl.pallas_call(
        kern,
        grid=(nt,),
        in_specs=[pl.BlockSpec((1, TOP_K, td), lambda i: (i, 0, 0), memory_space=pltpu.SMEM),
                  pl.BlockSpec(memory_space=pltpu.SMEM),
                  pl.BlockSpec((td, Wd), lambda i: (i, 0))],
        out_specs=pl.BlockSpec(memory_space=pl.ANY),
        out_shape=jax.ShapeDtypeStruct((P, Wd), jnp.uint32),
        scratch_shapes=[pltpu.VMEM((bm, Wd), jnp.uint32),
                        pltpu.SemaphoreType.DMA,
                        pltpu.SemaphoreType.DMA],
        compiler_params=_cparams(("arbitrary",)),
        name="moe_dispatch",
    )(dest3, zero_blk, hp)


def _expert_kernel(be_ref, nu_ref, x_ref, w1_ref, b1g_ref, b1l_ref, w2_ref, b2_ref, pm_ref, o_ref,
                   w1g_sc, w1l_sc, w2_sc):
    blk = pl.program_id(0)
    nused = nu_ref[0]
    e = be_ref[blk]
    prev = be_ref[jnp.maximum(blk - 1, 0)]
    active = blk < nused
    tile = 2 * LANES
    n_tiles = w1_ref.shape[1] // tile

    @pl.when(active & ((blk == 0) | (e != prev)))
    def _():
        for c in range(n_tiles):
            wt = w1_ref[:, c * tile:(c + 1) * tile].astype(BF16)
            pw = jnp.dot(wt, pm_ref[...], preferred_element_type=F32)
            w1g_sc[:, c * LANES:(c + 1) * LANES] = pw[:, :LANES].astype(BF16)
            w1l_sc[:, c * LANES:(c + 1) * LANES] = pw[:, LANES:].astype(BF16)
        w2_sc[...] = w2_ref[...].astype(BF16)

    @pl.when(jnp.logical_not(active))
    def _():
        o_ref[...] = jnp.zeros(o_ref.shape, F32)

    @pl.when(active)
    def _():
        xp = x_ref[...]
        hw = xp.shape[1]
        xlo = lax.bitcast_convert_type(xp << 16, F32).astype(BF16)
        xhi = lax.bitcast_convert_type(xp & jnp.uint32(0xFFFF0000), F32).astype(BF16)
        zg = (jnp.dot(xlo, w1g_sc[:hw, :], preferred_element_type=F32)
              + jnp.dot(xhi, w1g_sc[hw:, :], preferred_element_type=F32) + b1g_ref[...])
        zl = (jnp.dot(xlo, w1l_sc[:hw, :], preferred_element_type=F32)
              + jnp.dot(xhi, w1l_sc[hw:, :], preferred_element_type=F32) + b1l_ref[...])
        g = jnp.minimum(zg, SWIGLU_LIMIT)
        lin = jnp.clip(zl, -SWIGLU_LIMIT, SWIGLU_LIMIT)
        act = g * jax.nn.sigmoid(SWIGLU_ALPHA * g) * (lin + 1.0)
        o_ref[...] = jnp.dot(act.astype(BF16), w2_sc[...], preferred_element_type=F32) + b2_ref[...]


def _experts(blk_expert, nused, xs, w1, b1g, b1l, w2, b2, bm):
    P, Wd = xs.shape
    E, D, F2 = w1.shape
    F = F2 // 2
    nblk = P // bm
    tile = 2 * LANES
    pm = np.zeros((tile, tile), np.float32)
    pm[2 * np.arange(LANES), np.arange(LANES)] = 1.0
    pm[2 * np.arange(LANES) + 1, LANES + np.arange(LANES)] = 1.0
    pm = jnp.asarray(pm, BF16)

    def row_map(b, be, nu):
        return (jnp.minimum(b, nu[0] - 1), 0)

    def exp_map(b, be, nu):
        return (be[b], 0, 0)

    grid_spec = pltpu.PrefetchScalarGridSpec(
        num_scalar_prefetch=2,
        grid=(nblk,),
        in_specs=[pl.BlockSpec((bm, Wd), row_map),
                  pl.BlockSpec((None, D, F2), exp_map),
                  pl.BlockSpec((None, 1, F), exp_map),
                  pl.BlockSpec((None, 1, F), exp_map),
                  pl.BlockSpec((None, F, D), exp_map),
                  pl.BlockSpec((None, 1, D), exp_map),
                  pl.BlockSpec((tile, tile), lambda b, be, nu: (0, 0))],
        out_specs=pl.BlockSpec((bm, D), lambda b, be, nu: (b, 0)),
        scratch_shapes=[pltpu.VMEM((D, F), BF16),
                        pltpu.VMEM((D, F), BF16),
                        pltpu.VMEM((F, D), BF16)],
    )
    return pl.pallas_call(
        _expert_kernel,
        grid_spec=grid_spec,
        out_shape=jax.ShapeDtypeStruct((P, D), F32),
        compiler_params=_cparams(("arbitrary",)),
        name="moe_experts",
    )(blk_expert, nused, xs, w1, b1g, b1l, w2, b2, pm)


def _combine_kernel(dcur_ref, dnxt_ref, grow_ref, x1_ref, mod_ref, y_ref, o_ref, buf, sem, *, tc, nt):
    i = pl.program_id(0)
    slot = i % 2

    def issue(dref, sl):
        def body(r, carry):
            for k in range(TOP_K):
                pltpu.make_async_copy(y_ref.at[pl.ds(dref[0, k, r], 1)],
                                      buf.at[sl, k, pl.ds(r, 1)], sem.at[sl]).start()
            return carry
        lax.fori_loop(0, tc, body, 0)

    @pl.when(i == 0)
    def _():
        issue(dcur_ref, 0)

    @pl.when(i + 1 < nt)
    def _():
        issue(dnxt_ref, 1 - slot)

    for k in range(TOP_K):
        pltpu.make_async_copy(y_ref.at[pl.ds(0, tc)], buf.at[slot, k], sem.at[slot]).wait()
    gr = grow_ref[...]
    acc = gr[:, 0:1] * buf[slot, 0]
    for k in range(1, TOP_K):
        acc = acc + gr[:, k:k + 1] * buf[slot, k]
    o_ref[...] = x1_ref[...] + mod_ref[5:6, :] * acc


def _combine(dest3, grow, x1, mod3, y, S):
    T, D = x1.shape
    nt, _, tc = dest3.shape
    tiles_per_batch = S // tc
    kern = functools.partial(_combine_kernel, tc=tc, nt=nt)
    return pl.pallas_call(
        kern,
        grid=(nt,),
        in_specs=[pl.BlockSpec((1, TOP_K, tc), lambda i: (i, 0, 0), memory_space=pltpu.SMEM),
                  pl.BlockSpec((1, TOP_K, tc), lambda i: (jnp.minimum(i + 1, nt - 1), 0, 0),
                               memory_space=pltpu.SMEM),
                  pl.BlockSpec((tc, LANES), lambda i: (i, 0)),
                  pl.BlockSpec((tc, D), lambda i: (i, 0)),
                  pl.BlockSpec((None, 6, D), lambda i: (i // tiles_per_batch, 0, 0)),
                  pl.BlockSpec(memory_space=pl.ANY)],
        out_specs=pl.BlockSpec((tc, D), lambda i: (i, 0)),
        out_shape=jax.ShapeDtypeStruct((T, D), F32),
        scratch_shapes=[pltpu.VMEM((2, TOP_K, tc, D), F32),
                        pltpu.SemaphoreType.DMA((2,))],
        compiler_params=_cparams(("arbitrary",)),
        name="moe_combine",
    )(dest3, dest3, grow, x1, mod3, y)


def _tiles(S):
    t_attn = min(512, S // 4) if S >= 512 else S
    tm = min(512, t_attn)
    return dict(tm=tm, t_attn=t_attn, td=min(1024, S), tc=min(256, S), bm=512)


def kernel(x, c, w_ada, b_ada, g_mix, w_in, w_fourier, q_norm_g, k_norm_g, lambda_q1, lambda_k1,
           lambda_q2, lambda_k2, g_subln, w_out, rel_bias, g_ffn, w_router, b_router, w1, b1, w2, b2):
    B, S, D = x.shape
    T = B * S
    L = w_ada.shape[0]
    E = w_router.shape[-1]
    cfg = _tiles(S)
    tm, t_attn, td, tc, bm = cfg["tm"], cfg["t_attn"], cfg["td"], cfg["tc"], cfg["bm"]
    dqk = q_norm_g.shape[-1]
    scale = dqk ** -0.5
    fw = w_in.shape[-1] // 4

    btiles = _bias_tiles(rel_bias, t_attn)
    x2 = x.reshape(T, D)
    for l in range(L):
        lam_init = 0.8 - 0.6 * math.exp(-0.3 * l)
        mod3 = _ada(c, w_ada[l], b_ada[l]).reshape(B, 6, D)
        gq = (jnp.tile(q_norm_g[l], 2) * scale).reshape(1, LANES)
        gk = jnp.tile(k_norm_g[l], 2).reshape(1, LANES)
        uf, qx, kx, v = _in_proj(x2, mod3, g_mix[l].reshape(1, D), w_in[l].astype(BF16), gq, gk,
                                 S, tm, t_attn)
        yf = _fourier(uf.reshape(B, S, fw), w_fourier[l]).reshape(T, fw)
        lam4 = jnp.stack([lambda_q1[l], lambda_k1[l], lambda_q2[l], lambda_k2[l]]).astype(F32)
        yd = _attention(rel_bias, qx.reshape(B, S, -1), kx.reshape(B, S, -1), v.reshape(B, S, fw),
                        btiles, lam4, g_subln[l].reshape(1, LANES), t_attn, lam_init).reshape(T, fw)
        x1, hp, idx_t, rank_t, grow, cnt = _out_proj(
            yf, yd, x2, mod3, w_out[l].astype(BF16), g_ffn[l].reshape(1, D),
            w_router[l].T.astype(BF16), b_router[l].reshape(E, 1), S, tm)

        counts = cnt[:, 0]
        padded = (counts + bm - 1) // bm * bm
        pends = jnp.cumsum(padded)
        pstarts = pends - padded
        eids = jnp.arange(E, dtype=jnp.int32)[:, None, None]
        dest = jnp.sum(jnp.where(idx_t[None, :TOP_K] == eids, pstarts[:, None, None], 0), axis=0) \
            + rank_t[:TOP_K]
        P = T * TOP_K + E * bm
        nblk = P // bm
        nused = (pends[-1] // bm).astype(jnp.int32).reshape(1)
        blk_start = jnp.arange(nblk, dtype=jnp.int32) * bm
        blk_expert = jnp.minimum(jnp.sum(pends[None, :] <= blk_start[:, None], axis=1), E - 1).astype(jnp.int32)
        last_blk = jnp.where(padded > 0, pends // bm - 1, -1)
        tail_blk = nused[0] + jnp.arange(E, dtype=jnp.int32)
        tail_blk = jnp.where(tail_blk < nblk, tail_blk, -1)
        zero_blk = jnp.concatenate([last_blk, tail_blk]).astype(jnp.int32)

        def by_tile(a, tile):
            return a.reshape(TOP_K, T // tile, tile).transpose(1, 0, 2)

        xs = _dispatch(by_tile(dest, td), zero_blk, hp, P, bm)
        y = _experts(blk_expert, nused, xs, w1[l], b1[l][:, None, 0::2], b1[l][:, None, 1::2],
                     w2[l], b2[l][:, None, :], bm)
        x2 = _combine(by_tile(dest, tc), grow, x1, mod3, y, S)
    return x2.reshape(B, S, D)
```

```python
import functools
import math

import numpy as np
import jax
import jax.numpy as jnp
from jax import lax
from jax.experimental import pallas as pl
from jax.experimental.pallas import tpu as pltpu

F32 = jnp.float32
BF16 = jnp.bfloat16

EPS = 1e-6
N_FOURIER_GROUPS = 4
N_DIFF_HEADS = 4
TOP_K = 4
N_BUCKETS = 32
MAX_DISTANCE = 128
SWIGLU_ALPHA = 1.702
SWIGLU_LIMIT = 7.0
LANES = 128
DFT_RADIX = 4
VMEM_LIMIT = 56 * 1024 * 1024


def _cparams(sem):
    return pltpu.CompilerParams(dimension_semantics=sem, vmem_limit_bytes=VMEM_LIMIT)


def _ada_kernel(c_ref, w_ref, b_ref, o_ref):
    c = c_ref[...]
    cond = c * jax.nn.sigmoid(c)
    o_ref[...] = jnp.dot(cond, w_ref[...], preferred_element_type=F32,
                         precision=lax.Precision.HIGHEST) + b_ref[...]


def _ada(c, w, b):
    B, D = c.shape
    N = w.shape[1]
    tn = 1536 if N % 1536 == 0 else N
    return pl.pallas_call(
        _ada_kernel,
        grid=(N // tn,),
        in_specs=[pl.BlockSpec((B, D), lambda j: (0, 0)),
                  pl.BlockSpec((D, tn), lambda j: (0, j)),
                  pl.BlockSpec((1, tn), lambda j: (0, j))],
        out_specs=pl.BlockSpec((B, tn), lambda j: (0, j)),
        out_shape=jax.ShapeDtypeStruct((B, N), F32),
        compiler_params=_cparams(("arbitrary",)),
        name="ada",
    )(c, w, b.reshape(1, N))


def _group_rms_inv(xh, gmat, group):
    sq = xh * xh
    hi = sq.astype(BF16)
    lo = (sq - hi.astype(F32)).astype(BF16)
    ss = jnp.dot(hi, gmat, preferred_element_type=F32) + jnp.dot(lo, gmat, preferred_element_type=F32)
    return lax.rsqrt(ss * (1.0 / group) + EPS)


def _in_proj_kernel(x_ref, mod_ref, g_ref, w_ref, gq_ref, gk_ref, gmat_ref,
                    uf_ref, q_ref, k_ref, v_ref, *, tiles_per_batch, tm, t_attn, n_chunks, fw, qw):
    i = pl.program_id(0)
    x = x_ref[...]
    ms = jnp.mean(x * x, axis=-1, keepdims=True)
    y = x * lax.rsqrt(ms + EPS) * g_ref[...]
    h = y * (1.0 + mod_ref[1:2, :]) + mod_ref[0:1, :]
    proj = jnp.dot(h.astype(BF16), w_ref[...], preferred_element_type=F32)
    uf_ref[...] = proj[:, :fw].astype(BF16)
    v_ref[...] = proj[:, fw + 2 * qw:].astype(BF16)

    lane = lax.broadcasted_iota(jnp.int32, (1, LANES), 1)
    chunk = ((i % tiles_per_batch) * tm) // t_attn
    gmat = gmat_ref[...]
    half = LANES // 2
    for hd in range(N_DIFF_HEADS):
        qh = proj[:, fw + hd * LANES: fw + (hd + 1) * LANES]
        kh = proj[:, fw + qw + hd * LANES: fw + qw + (hd + 1) * LANES]
        qn = (qh * _group_rms_inv(qh, gmat, half) * gq_ref[...]).astype(BF16)
        kn = (kh * _group_rms_inv(kh, gmat, half) * gk_ref[...]).astype(BF16)
        for m in range(2):
            fl = lane - half if m == 0 else lane
            is_data = (lane < half) if m == 0 else (lane >= half)
            f = jnp.where(fl < n_chunks, fl, fl - n_chunks)
            onehot = ((fl >= 0) & (fl < 2 * n_chunks) & (f == chunk)).astype(BF16)
            c0 = (2 * hd + m) * LANES
            q_ref[:, c0:c0 + LANES] = jnp.where(is_data, qn, jnp.zeros_like(qn))
            k_ref[:, c0:c0 + LANES] = jnp.where(is_data, kn, onehot)


def _in_proj(x2, mod3, g_mix, w_in_b, gq, gk, S, tm, t_attn):
    T, D = x2.shape
    ncols = w_in_b.shape[1]
    fw = ncols // 4
    qw = fw
    tiles_per_batch = S // tm
    n_chunks = S // t_attn
    half = LANES // 2
    gmat = (np.arange(LANES)[:, None] // half == np.arange(LANES)[None, :] // half)
    gmat = jnp.asarray(gmat, BF16)
    kern = functools.partial(_in_proj_kernel, tiles_per_batch=tiles_per_batch, tm=tm, t_attn=t_attn,
                             n_chunks=n_chunks, fw=fw, qw=qw)
    ext = 2 * qw
    return pl.pallas_call(
        kern,
        grid=(T // tm,),
        in_specs=[pl.BlockSpec((tm, D), lambda i: (i, 0)),
                  pl.BlockSpec((None, 6, D), lambda i: (i // tiles_per_batch, 0, 0)),
                  pl.BlockSpec((1, D), lambda i: (0, 0)),
                  pl.BlockSpec((D, ncols), lambda i: (0, 0)),
                  pl.BlockSpec((1, LANES), lambda i: (0, 0)),
                  pl.BlockSpec((1, LANES), lambda i: (0, 0)),
                  pl.BlockSpec((LANES, LANES), lambda i: (0, 0))],
        out_specs=[pl.BlockSpec((tm, fw), lambda i: (i, 0)),
                   pl.BlockSpec((tm, ext), lambda i: (i, 0)),
                   pl.BlockSpec((tm, ext), lambda i: (i, 0)),
                   pl.BlockSpec((tm, fw), lambda i: (i, 0))],
        out_shape=[jax.ShapeDtypeStruct((T, fw), BF16),
                   jax.ShapeDtypeStruct((T, ext), BF16),
                   jax.ShapeDtypeStruct((T, ext), BF16),
                   jax.ShapeDtypeStruct((T, fw), BF16)],
        compiler_params=_cparams(("arbitrary",)),
        name="in_proj",
    )(x2, mod3, g_mix, w_in_b, gq, gk, gmat)


def _fourier_kernel(uf_ref, wcs_ref, tw_ref, dft_ref, o_ref, tt_sc, *, M, gw):
    c = pl.program_id(1)

    @pl.when(c == 0)
    def _():
        for g in range(N_FOURIER_GROUPS):
            zr, zi = [], []
            for j in range(DFT_RADIX):
                ab = jnp.dot(uf_ref[j * M:(j + 1) * M, g * gw:(g + 1) * gw], wcs_ref[g],
                             preferred_element_type=F32)
                zr.append(ab[:, :gw])
                zi.append(-ab[:, gw:])
            yr = [zr[0] + zr[1] + zr[2] + zr[3],
                  zr[0] + zi[1] - zr[2] - zi[3],
                  zr[0] - zr[1] + zr[2] - zr[3],
                  zr[0] - zi[1] - zr[2] + zi[3]]
            yi = [zi[0] + zi[1] + zi[2] + zi[3],
                  zi[0] - zr[1] - zi[2] + zr[3],
                  zi[0] - zi[1] + zi[2] - zi[3],
                  zi[0] + zr[1] - zi[2] - zr[3]]
            for k2 in range(DFT_RADIX):
                cs = tw_ref[:, k2:k2 + 1]
                sn = tw_ref[:, DFT_RADIX + k2:DFT_RADIX + k2 + 1]
                tr = cs * yr[k2] + sn * yi[k2]
                ti = cs * yi[k2] - sn * yr[k2]
                tt_sc[k2, 0:M, g * gw:(g + 1) * gw] = tr.astype(BF16)
                tt_sc[k2, M:2 * M, g * gw:(g + 1) * gw] = ti.astype(BF16)

    o_ref[...] = jnp.dot(dft_ref[...], tt_sc[c], preferred_element_type=F32).astype(BF16)


def _fourier(uf3, w_f):
    B, S, W = uf3.shape
    G = N_FOURIER_GROUPS
    gw = W // G
    M = S // DFT_RADIX
    norm = 1.0 / math.sqrt(S * gw)
    cidx = (np.arange(gw)[:, None] * np.arange(gw)[None, :]) % gw
    ang_c = jnp.asarray(cidx, F32) * (2.0 * math.pi / gw)
    cc, sc = jnp.cos(ang_c) * norm, jnp.sin(ang_c) * norm
    hp = lax.Precision.HIGHEST
    wcs = jnp.concatenate([jnp.einsum("ab,gbd->gad", cc, w_f, precision=hp),
                           jnp.einsum("ab,gbd->gad", sc, w_f, precision=hp)], axis=-1).astype(BF16)
    n1 = np.arange(M)
    tw_idx = (np.arange(DFT_RADIX)[None, :] * n1[:, None]) % S
    ang_t = jnp.asarray(tw_idx, F32) * (2.0 * math.pi / S)
    tw = jnp.concatenate([jnp.cos(ang_t), jnp.sin(ang_t),
                          jnp.zeros((M, LANES - 2 * DFT_RADIX), F32)], axis=1)
    pidx = (n1[:, None] * n1[None, :]) % M
    ang_p = jnp.asarray(pidx, F32) * (2.0 * math.pi / M)
    dft = jnp.concatenate([jnp.cos(ang_p), jnp.sin(ang_p)], axis=1).astype(BF16)
    kern = functools.partial(_fourier_kernel, M=M, gw=gw)
    return pl.pallas_call(
        kern,
        grid=(B, DFT_RADIX),
        in_specs=[pl.BlockSpec((None, S, W), lambda b, c: (b, 0, 0)),
                  pl.BlockSpec((G, gw, 2 * gw), lambda b, c: (0, 0, 0)),
                  pl.BlockSpec((M, LANES), lambda b, c: (0, 0)),
                  pl.BlockSpec((M, 2 * M), lambda b, c: (0, 0))],
        out_specs=pl.BlockSpec((None, M, W), lambda b, c: (b, 0, c)),
        out_shape=jax.ShapeDtypeStruct((B, M, DFT_RADIX * W), BF16),
        scratch_shapes=[pltpu.VMEM((DFT_RADIX, 2 * M, W), BF16)],
        compiler_params=_cparams(("arbitrary", "arbitrary")),
        name="fourier",
    )(uf3, wcs, tw, dft)


def _bias_kernel(rb_ref, o_ref, *, t):
    h = pl.program_id(0)
    d = pl.program_id(1) - 1
    row = lax.broadcasted_iota(jnp.int32, (t, t), 0)
    col = lax.broadcasted_iota(jnp.int32, (t, t), 1)
    rel = d * t + col - row
    nb = N_BUCKETS // 2
    max_exact = nb // 2
    ret = jnp.where(rel > 0, nb, 0)
    n = jnp.abs(rel)
    nf = jnp.maximum(n, 1).astype(F32)
    large = max_exact + (jnp.log(nf / max_exact) / math.log(MAX_DISTANCE / max_exact)
                         * (nb - max_exact)).astype(jnp.int32)
    large = jnp.minimum(large, nb - 1)
    bucket = ret + jnp.where(n < max_exact, n, large)
    val = jnp.zeros((t, t), F32)
    for j in range(N_BUCKETS):
        val = jnp.where(bucket == j, rb_ref[j * N_DIFF_HEADS + h], val)
    o_ref[...] = jnp.where(pl.program_id(1) == 3, 0.0, val)


def _bias_tiles(rel_bias, t):
    H = N_DIFF_HEADS
    assert t >= MAX_DISTANCE, "far key tiles must lie entirely in the saturated bucket"
    return pl.pallas_call(
        functools.partial(_bias_kernel, t=t),
        grid=(H, 4),
        in_specs=[pl.BlockSpec(memory_space=pltpu.SMEM)],
        out_specs=pl.BlockSpec((None, None, t, t), lambda h, d: (h, d, 0, 0)),
        out_shape=jax.ShapeDtypeStruct((H, 4, t, t), F32),
        compiler_params=_cparams(("arbitrary", "arbitrary")),
        name="t5_bias",
    )(rel_bias.reshape(-1))


def _far_bucket_consts():
    nb = N_BUCKETS // 2
    return nb - 1, 2 * nb - 1


def _attn_kernel(rb_ref, q_ref, k_ref, v_ref, b_ref, lam_ref, gs_ref, o_ref,
                 s_sc, p_sc, vext_sc, m_sc, *, t, n, lam_init):
    h = pl.program_id(1)
    i = pl.program_id(2)
    half = LANES // 2

    @pl.when(i == 0)
    def _():
        vext_sc[:, :LANES] = v_ref[...]
        vext_sc[:, LANES:] = jnp.ones(v_ref.shape, BF16)

    left_b, right_b = _far_bucket_consts()
    c_left = rb_ref[left_b * N_DIFF_HEADS + h]
    c_right = rb_ref[right_b * N_DIFF_HEADS + h]
    lane = lax.broadcasted_iota(jnp.int32, (1, LANES), 1)

    def all_scores(m):
        fl = lane - half if m == 0 else lane
        is_data = (lane < half) if m == 0 else (lane >= half)
        f = jnp.where(fl < n, fl, fl - n)
        cfar = jnp.where(f < i - 1, c_left, jnp.where(f > i + 1, c_right, 0.0))
        c_hi = cfar.astype(BF16)
        c_lo = (cfar - c_hi.astype(F32)).astype(BF16)
        feat = jnp.where(fl < n, c_hi, c_lo)
        feat = jnp.where((fl >= 0) & (fl < 2 * n), feat, jnp.zeros_like(feat))
        qm = jnp.where(is_data, q_ref[:, m * LANES:(m + 1) * LANES], feat)
        for j in range(n):
            kc = k_ref[j * t:(j + 1) * t, m * LANES:(m + 1) * LANES]
            s = lax.dot_general(qm, kc, (((1,), (1,)), ((), ())), preferred_element_type=F32)
            d = j - i
            s = s + b_ref[jnp.where(jnp.abs(d) <= 1, d + 1, 3)]
            s_sc[m, j] = s
            tile_max = s[:, 0:LANES]
            for c in range(1, t // LANES):
                tile_max = jnp.maximum(tile_max, s[:, c * LANES:(c + 1) * LANES])
            if j == 0:
                m_sc[m] = tile_max
            else:
                m_sc[m] = jnp.maximum(m_sc[m], tile_max)

    def weighted_values(m):
        mx = jnp.max(m_sc[m], axis=-1, keepdims=True)
        for j in range(n):
            p_sc[m, :, j * t:(j + 1) * t] = jnp.exp(s_sc[m, j] - mx).astype(BF16)
        o = jnp.dot(p_sc[m], vext_sc[...], preferred_element_type=F32)
        return o[:, :LANES] / o[:, LANES:LANES + 1]

    all_scores(0)
    all_scores(1)
    lam = (jnp.exp(jnp.sum(lam_ref[0:1, :] * lam_ref[1:2, :], axis=-1, keepdims=True))
           - jnp.exp(jnp.sum(lam_ref[2:3, :] * lam_ref[3:4, :], axis=-1, keepdims=True)) + lam_init)
    o = weighted_values(0) - lam * weighted_values(1)
    ms = jnp.mean(o * o, axis=-1, keepdims=True)
    o_ref[...] = (o * lax.rsqrt(ms + EPS) * gs_ref[...] * (1.0 - lam_init)).astype(BF16)


def _attention(rel_bias, q3, k3, v3, btiles, lam4, g_subln, t, lam_init):
    B, S, _ = q3.shape
    H = N_DIFF_HEADS
    n = S // t
    kern = functools.partial(_attn_kernel, t=t, n=n, lam_init=lam_init)
    return pl.pallas_call(
        kern,
        grid=(B, H, n),
        in_specs=[pl.BlockSpec(memory_space=pltpu.SMEM),
                  pl.BlockSpec((None, t, 2 * LANES), lambda b, h, i: (b, i, h)),
                  pl.BlockSpec((None, S, 2 * LANES), lambda b, h, i: (b, 0, h)),
                  pl.BlockSpec((None, S, LANES), lambda b, h, i: (b, 0, h)),
                  pl.BlockSpec((None, 4, t, t), lambda b, h, i: (h, 0, 0, 0)),
                  pl.BlockSpec((4, LANES // 2), lambda b, h, i: (0, 0)),
                  pl.BlockSpec((1, LANES), lambda b, h, i: (0, 0))],
        out_specs=pl.BlockSpec((None, t, LANES), lambda b, h, i: (b, i, h)),
        out_shape=jax.ShapeDtypeStruct((B, S, H * LANES), BF16),
        scratch_shapes=[pltpu.VMEM((2, n, t, t), F32),
                        pltpu.VMEM((2, t, S), BF16),
                        pltpu.VMEM((S, 2 * LANES), BF16),
                        pltpu.VMEM((2, t, LANES), F32)],
        compiler_params=_cparams(("arbitrary", "arbitrary", "arbitrary")),
        name="diff_attn",
    )(rel_bias.reshape(-1), q3, k3, v3, btiles, lam4, g_subln)


def _pack_bf16_pairs(v):
    bits = lax.bitcast_convert_type(v.astype(BF16).astype(F32), jnp.uint32)
    n = bits.shape[1] // 2
    return (bits[:, :n] >> 16) | (bits[:, n:] & jnp.uint32(0xFFFF0000))


def _unpack_bf16_pairs(p):
    lo = lax.bitcast_convert_type(p << 16, F32).astype(BF16)
    hi = lax.bitcast_convert_type(p & jnp.uint32(0xFFFF0000), F32).astype(BF16)
    return lo, hi


def _out_proj_kernel(yf_ref, yd_ref, x_ref, mod_ref, wo_ref, g_ref, wr_ref, br_ref,
                     x1_ref, hp_ref, idx_ref, rank_ref, grow_ref, cnt_ref, *, tm, fw, E):
    mix =(jnp.dot(yf_ref[...], wo_ref[:fw, :], preferred_element_type=F32)
           + jnp.dot(yd_ref[...], wo_ref[fw:, :], preferred_element_type=F32))
    x1 = x_ref[...] + mod_ref[2:3, :] * mix
    x1_ref[...] = x1
    ms = jnp.mean(x1 * x1, axis=-1, keepdims=True)
    h = x1 * lax.rsqrt(ms + EPS) * g_ref[...]
    h = h * (1.0 + mod_ref[4:5, :]) + mod_ref[3:4, :]
    hb = h.astype(BF16)
    hp_ref[...] = _pack_bf16_pairs(hb)

    logits = lax.dot_general(wr_ref[...], hb, (((1,), (1,)), ((), ())),
                             preferred_element_type=F32) + br_ref[...]
    rowid = lax.broadcasted_iota(jnp.int32, (E, tm), 0)
    vals, idxs = [], []
    comb = jnp.zeros((E, tm), F32)
    l = logits
    for _ in range(TOP_K):
        mv = jnp.max(l, axis=0, keepdims=True)
        ix = jnp.min(jnp.where(l == mv, rowid, E), axis=0, keepdims=True)
        sel = rowid == ix
        vals.append(mv)
        idxs.append(ix)
        comb = comb + sel.astype(F32)
        l = jnp.where(sel, -jnp.inf, l)
    es = [jnp.exp(v - vals[0]) for v in vals]
    den = es[0] + es[1] + es[2] + es[3]
    gates = [e / den for e in es]

    r_i = lax.broadcasted_iota(jnp.int32, (tm, tm), 0)
    c_i = lax.broadcasted_iota(jnp.int32, (tm, tm), 1)
    upper = (r_i < c_i).astype(BF16)
    before = jnp.dot(comb.astype(BF16), upper, preferred_element_type=F32)
    ranks = [jnp.sum(jnp.where(rowid == ix, before, 0.0), axis=0, keepdims=True) for ix in idxs]
    cnt = jnp.sum(comb, axis=1, keepdims=True)
    cnt_ref[...] = jnp.broadcast_to(cnt, cnt_ref.shape).astype(jnp.int32)

    pad_i = jnp.zeros((8 - TOP_K, tm), jnp.int32)
    idx_ref[...] = jnp.concatenate(idxs + [pad_i], axis=0)
    rank_ref[...] = jnp.concatenate([r.astype(jnp.int32) for r in ranks] + [pad_i], axis=0)
    gpad = jnp.concatenate(gates + [jnp.zeros((LANES - TOP_K, tm), F32)], axis=0)
    grow_ref[...] = gpad.T


def _out_proj(yf2, yd2, x2, mod3, w_out_b, g_ffn, wr_t, br_t, S, tm):
    T, D = x2.shape
    fw = yf2.shape[1]
    E = wr_t.shape[0]
    nt = T // tm
    tiles_per_batch = S // tm
    kern = functools.partial(_out_proj_kernel, tm=tm, fw=fw, E=E)
    return pl.pallas_call(
        kern,
        grid=(nt,),
        in_specs=[pl.BlockSpec((tm, fw), lambda i: (i, 0)),
                  pl.BlockSpec((tm, fw), lambda i: (i, 0)),
                  pl.BlockSpec((tm, D), lambda i: (i, 0)),
                  pl.BlockSpec((None, 6, D), lambda i: (i // tiles_per_batch, 0, 0)),
                  pl.BlockSpec((D, D), lambda i: (0, 0)),
                  pl.BlockSpec((1, D), lambda i: (0, 0)),
                  pl.BlockSpec((E, D), lambda i: (0, 0)),
                  pl.BlockSpec((E, 1), lambda i: (0, 0))],
        out_specs=[pl.BlockSpec((tm, D), lambda i: (i, 0)),
                   pl.BlockSpec((tm, D // 2), lambda i: (i, 0)),
                   pl.BlockSpec((8, tm), lambda i: (0, i)),
                   pl.BlockSpec((8, tm), lambda i: (0, i)),
                   pl.BlockSpec((tm, LANES), lambda i: (i, 0)),
                   pl.BlockSpec((None, E, LANES), lambda i: (i, 0, 0))],
        out_shape=[jax.ShapeDtypeStruct((T, D), F32),
                   jax.ShapeDtypeStruct((T, D // 2), jnp.uint32),
                   jax.ShapeDtypeStruct((8, T), jnp.int32),
                   jax.ShapeDtypeStruct((8, T), jnp.int32),
                   jax.ShapeDtypeStruct((T, LANES), F32),
                   jax.ShapeDtypeStruct((nt, E, LANES), jnp.int32)],
        compiler_params=_cparams(("arbitrary",)),
        name="out_proj_router",
    )(yf2, yd2, x2, mod3, w_out_b, g_ffn, wr_t, br_t)


ROW_CHUNK = 8


def _wait_chunks(n, make_copy, max_rows):
    bit = 0
    while (ROW_CHUNK << bit) <= max_rows:
        rows = ROW_CHUNK << bit

        @pl.when(((n >> bit) & 1) == 1)
        def _():
            make_copy(rows).wait()
        bit += 1


def _dispatch_kernel(tbl_ref, nch_ref, zb_ref, lp_ref, hp_ref, xs_ref, buf, zero_sc, sem, zsem,
                     *, tm, bm, NR, nz, nt):
    i = pl.program_id(0)
    s = i % 2

    @pl.when(i == 0)
    def _():
        zero_sc[...] = jnp.zeros(zero_sc.shape, jnp.uint32)
        for e in range(nz):
            @pl.when(zb_ref[e] >= 0)
            def _():
                pltpu.make_async_copy(zero_sc, xs_ref.at[pl.ds(zb_ref[e] * bm, bm)], zsem).start()
        for e in range(nz):
            @pl.when(zb_ref[e] >= 0)
            def _():
                pltpu.make_async_copy(zero_sc, xs_ref.at[pl.ds(zb_ref[e] * bm, bm)], zsem).wait()

    lp = lp_ref[...]
    rows = lax.broadcasted_iota(jnp.int32, (NR, tm), 0)
    hit = rows == lp[0:1, :]
    for k in range(1, TOP_K):
        hit = hit | (rows == lp[k:k + 1, :])
    sel = jnp.where(hit, 1.0, 0.0).astype(BF16)
    lo, hi = _unpack_bf16_pairs(hp_ref[...])
    slo = jnp.dot(sel, lo, preferred_element_type=F32)
    shi = jnp.dot(sel, hi, preferred_element_type=F32)
    buf[s] = ((lax.bitcast_convert_type(slo, jnp.uint32) >> 16)
              | (lax.bitcast_convert_type(shi, jnp.uint32) & jnp.uint32(0xFFFF0000)))

    n = nch_ref[i]

    def issue(c, carry):
        d = pl.multiple_of(tbl_ref[0, 0, c], ROW_CHUNK)
        r = pl.multiple_of(c * ROW_CHUNK, ROW_CHUNK)
        pltpu.make_async_copy(buf.at[s, pl.ds(r, ROW_CHUNK)], xs_ref.at[pl.ds(d, ROW_CHUNK)],
                              sem.at[s]).start()
        return carry

    lax.fori_loop(0, n, issue, 0)

    def waiter(slot):
        return lambda nrows: pltpu.make_async_copy(buf.at[slot, pl.ds(0, nrows)],
                                                   xs_ref.at[pl.ds(0, nrows)], sem.at[slot])

    @pl.when(i > 0)
    def _():
        _wait_chunks(nch_ref[jnp.maximum(i - 1, 0)], waiter(1 - s), NR)

    @pl.when(i == nt - 1)
    def _():
        _wait_chunks(n, waiter(s), NR)


def _dispatch(tbl3, nch, zero_blk, lp8, hp, P, bm, tm, NR):
    T, Wd = hp.shape
    nt = T // tm
    nz = zero_blk.shape[0]
    nch_tbl = tbl3.shape[2]
    kern = functools.partial(_dispatch_kernel, tm=tm, bm=bm, NR=NR, nz=nz, nt=nt)
    return pl.pallas_call(
        kern,
        grid=(nt,),
        in_specs=[pl.BlockSpec((1, 1, nch_tbl), lambda i: (i, 0, 0), memory_space=pltpu.SMEM),
                  pl.BlockSpec(memory_space=pltpu.SMEM),
                  pl.BlockSpec(memory_space=pltpu.SMEM),
                  pl.BlockSpec((8, tm), lambda i: (0, i)),
                  pl.BlockSpec((tm, Wd), lambda i: (i, 0))],
        out_specs=pl.BlockSpec(memory_space=pl.ANY),
        out_shape=jax.ShapeDtypeStruct((P, Wd), jnp.uint32),
        scratch_shapes=[pltpu.VMEM((2, NR, Wd), jnp.uint32),
                        pltpu.VMEM((bm, Wd), jnp.uint32),
                        pltpu.SemaphoreType.DMA((2,)),
                        pltpu.SemaphoreType.DMA],
        compiler_params=_cparams(("arbitrary",)),
        name="moe_dispatch",
    )(tbl3, nch, zero_blk, lp8, hp)


def _expert_kernel(be_ref, nu_ref, x_ref, w1_ref, b1g_ref, b1l_ref, w2_ref, b2_ref, pm_ref, o_ref,
                   w1g_sc, w1l_sc, w2_sc):
    blk = pl.program_id(0)
    nused = nu_ref[0]
    e = be_ref[blk]
    prev = be_ref[jnp.maximum(blk - 1, 0)]
    active = blk < nused
    tile = 2 * LANES
    n_tiles = w1_ref.shape[1] // tile

    @pl.when(active & ((blk == 0) | (e != prev)))
    def _():
        for c in range(n_tiles):
            wt = w1_ref[:, c * tile:(c + 1) * tile].astype(BF16)
            pw = jnp.dot(wt, pm_ref[...], preferred_element_type=F32)
            w1g_sc[:, c * LANES:(c + 1) * LANES] = pw[:, :LANES].astype(BF16)
            w1l_sc[:, c * LANES:(c + 1) * LANES] = pw[:, LANES:].astype(BF16)
        w2_sc[...] = w2_ref[...].astype(BF16)

    @pl.when(jnp.logical_not(active))
    def _():
        o_ref[...] = jnp.zeros(o_ref.shape, jnp.uint32)

    @pl.when(active)
    def _():
        xp = x_ref[...]
        hw = xp.shape[1]
        xlo, xhi = _unpack_bf16_pairs(xp)
        zg =(jnp.dot(xlo, w1g_sc[:hw, :], preferred_element_type=F32)
              + jnp.dot(xhi, w1g_sc[hw:, :], preferred_element_type=F32) + b1g_ref[...])
        zl = (jnp.dot(xlo, w1l_sc[:hw, :], preferred_element_type=F32)
              + jnp.dot(xhi, w1l_sc[hw:, :], preferred_element_type=F32) + b1l_ref[...])
        g = jnp.minimum(zg, SWIGLU_LIMIT)
        lin = jnp.clip(zl, -SWIGLU_LIMIT, SWIGLU_LIMIT)
        act = g * jax.nn.sigmoid(SWIGLU_ALPHA * g) * (lin + 1.0)
        y = jnp.dot(act.astype(BF16), w2_sc[...], preferred_element_type=F32) + b2_ref[...]
        o_ref[...] = _pack_bf16_pairs(y)


def _experts(blk_expert, nused, xs, w1, b1g, b1l, w2, b2, bm):
    P, Wd = xs.shape
    E, D, F2 = w1.shape
    F = F2 // 2
    nblk = P // bm
    tile = 2 * LANES
    pm = np.zeros((tile, tile), np.float32)
    pm[2 * np.arange(LANES), np.arange(LANES)] = 1.0
    pm[2 * np.arange(LANES) + 1, LANES + np.arange(LANES)] = 1.0
    pm = jnp.asarray(pm, BF16)

    def row_map(b, be, nu):
        return (jnp.minimum(b, nu[0] - 1), 0)

    def exp_map(b, be, nu):
        return (be[b], 0, 0)

    grid_spec = pltpu.PrefetchScalarGridSpec(
        num_scalar_prefetch=2,
        grid=(nblk,),
        in_specs=[pl.BlockSpec((bm, Wd), row_map),
                  pl.BlockSpec((None, D, F2), exp_map),
                  pl.BlockSpec((None, 1, F), exp_map),
                  pl.BlockSpec((None, 1, F), exp_map),
                  pl.BlockSpec((None, F, D), exp_map),
                  pl.BlockSpec((None, 1, D), exp_map),
                  pl.BlockSpec((tile, tile), lambda b, be, nu: (0, 0))],
        out_specs=pl.BlockSpec((bm, D // 2), lambda b, be, nu: (b, 0)),
        scratch_shapes=[pltpu.VMEM((D, F), BF16),
                        pltpu.VMEM((D, F), BF16),
                        pltpu.VMEM((F, D), BF16)],
    )
    return pl.pallas_call(
        _expert_kernel,
        grid_spec=grid_spec,
        out_shape=jax.ShapeDtypeStruct((P, D // 2), jnp.uint32),
        compiler_params=_cparams(("arbitrary",)),
        name="moe_experts",
    )(blk_expert, nused, xs, w1, b1g, b1l, w2, b2, pm)


def _combine_kernel(tcur_ref, tnxt_ref, nch_ref, lp_ref, grow_ref, x1_ref, mod_ref, y_ref, o_ref,
                    buf, sem, *, tm, NR, nt):
    i = pl.program_id(0)
    s = i % 2

    def issue(tref, n, slot):
        def body(c, carry):
            d = pl.multiple_of(tref[0, 0, c], ROW_CHUNK)
            r = pl.multiple_of(c * ROW_CHUNK, ROW_CHUNK)
            pltpu.make_async_copy(y_ref.at[pl.ds(d, ROW_CHUNK)], buf.at[slot, pl.ds(r, ROW_CHUNK)],
                                  sem.at[slot]).start()
            return carry
        lax.fori_loop(0, n, body, 0)

    @pl.when(i == 0)
    def _():
        buf[...] = jnp.zeros(buf.shape, jnp.uint32)
        issue(tcur_ref, nch_ref[0], 0)

    @pl.when(i + 1 < nt)
    def _():
        issue(tnxt_ref, nch_ref[jnp.minimum(i + 1, nt - 1)], 1 - s)

    _wait_chunks(nch_ref[i],
                 lambda nrows: pltpu.make_async_copy(y_ref.at[pl.ds(0, nrows)],
                                                     buf.at[s, pl.ds(0, nrows)], sem.at[s]), NR)

    ylo, yhi = _unpack_bf16_pairs(buf[s])
    lpf = jnp.concatenate([lp_ref[...].astype(F32), jnp.zeros((LANES - 8, tm), F32)], axis=0)
    lpr = lpf.T.astype(jnp.int32)
    gr = grow_ref[...]
    cols = lax.broadcasted_iota(jnp.int32, (tm, NR), 1)
    gmat = jnp.where(cols == lpr[:, 0:1], gr[:, 0:1], 0.0)
    for k in range(1, TOP_K):
        gmat = gmat + jnp.where(cols == lpr[:, k:k + 1], gr[:, k:k + 1], 0.0)
    gmat = gmat.astype(BF16)
    hw = ylo.shape[1]
    o_ref[:, :hw] = x1_ref[:, :hw] + mod_ref[5:6, :hw] * jnp.dot(gmat, ylo, preferred_element_type=F32)
    o_ref[:, hw:] = x1_ref[:, hw:] + mod_ref[5:6, hw:] * jnp.dot(gmat, yhi, preferred_element_type=F32)


def _combine(tbl3, nch, lp8, grow, x1, mod3, y, S, tm, NR):
    T, D = x1.shape
    nt = T // tm
    tiles_per_batch = S // tm
    nch_tbl = tbl3.shape[2]
    kern = functools.partial(_combine_kernel, tm=tm, NR=NR, nt=nt)
    return pl.pallas_call(
        kern,
        grid=(nt,),
        in_specs=[pl.BlockSpec((1, 1, nch_tbl), lambda i: (i, 0, 0), memory_space=pltpu.SMEM),
                  pl.BlockSpec((1, 1, nch_tbl), lambda i: (jnp.minimum(i + 1, nt - 1), 0, 0),
                               memory_space=pltpu.SMEM),
                  pl.BlockSpec(memory_space=pltpu.SMEM),
                  pl.BlockSpec((8, tm), lambda i: (0, i)),
                  pl.BlockSpec((tm, LANES), lambda i: (i, 0)),
                  pl.BlockSpec((tm, D), lambda i: (i, 0)),
                  pl.BlockSpec((None, 6, D), lambda i: (i // tiles_per_batch, 0, 0)),
                  pl.BlockSpec(memory_space=pl.ANY)],
        out_specs=pl.BlockSpec((tm, D), lambda i: (i, 0)),
        out_shape=jax.ShapeDtypeStruct((T, D), F32),
        scratch_shapes=[pltpu.VMEM((2, NR, D // 2), jnp.uint32),
                        pltpu.SemaphoreType.DMA((2,))],
        compiler_params=_cparams(("arbitrary",)),
        name="moe_combine",
    )(tbl3, tbl3, nch, lp8, grow, x1, mod3, y)


def _tiles(S):
    t_attn = min(512, S // 4) if S >= 512 else S
    tm = min(512, t_attn)
    return dict(tm=tm, t_attn=t_attn, bm=512)


def _moe_layout(cnt, idx_t, lrank_t, tm, bm):
    nt, E = cnt.shape
    T = idx_t.shape[1]
    c8 = (cnt + ROW_CHUNK - 1) // ROW_CHUNK * ROW_CHUNK
    tile_end = jnp.cumsum(c8, axis=1)
    tile_off = tile_end - c8
    used = tile_end[:, -1]
    off = jnp.cumsum(c8, axis=0) - c8
    tot = jnp.sum(c8, axis=0)
    padded = (tot + bm - 1) // bm * bm
    pends = jnp.cumsum(padded)
    pstarts = pends - padded
    NR = TOP_K * tm + ROW_CHUNK * E
    nch_tbl = NR // ROW_CHUNK
    P = (T * TOP_K + nt * E * (ROW_CHUNK - 1) + E * bm + bm - 1) // bm * bm
    nblk = P // bm

    eids = jnp.arange(E, dtype=jnp.int32)
    tok_off = jnp.repeat(tile_off.T, tm, axis=1)
    lp = jnp.sum(jnp.where(idx_t[None, :TOP_K] == eids[:, None, None], tok_off[:, None, :], 0), axis=0) \
        + lrank_t[:TOP_K]
    lp8 = jnp.concatenate([lp, jnp.full((8 - TOP_K, T), -1, jnp.int32)], axis=0).astype(jnp.int32)

    c_row = jnp.arange(nch_tbl, dtype=jnp.int32) * ROW_CHUNK
    inrun = (tile_off[:, :, None] <= c_row) & (c_row < tile_end[:, :, None])
    base = pstarts[None, :] + off - tile_off
    tbl = jnp.sum(jnp.where(inrun, base[:, :, None], 0), axis=1) + c_row[None, :]
    tbl = jnp.where(c_row[None, :] < used[:, None], tbl, 0).astype(jnp.int32)
    nch = (used // ROW_CHUNK).astype(jnp.int32)

    nused = (pends[-1] // bm).astype(jnp.int32).reshape(1)
    blk_start = jnp.arange(nblk, dtype=jnp.int32) * bm
    blk_expert = jnp.minimum(jnp.sum(pends[None, :] <= blk_start[:, None], axis=1), E - 1).astype(jnp.int32)
    last_blk = jnp.where(padded > 0, pends // bm - 1, -1)
    n_tail = nblk - (T * TOP_K) // bm
    tail_blk = nused[0] + jnp.arange(n_tail, dtype=jnp.int32)
    tail_blk = jnp.where(tail_blk < nblk, tail_blk, -1)
    zero_blk = jnp.concatenate([last_blk, tail_blk]).astype(jnp.int32)
    return dict(lp8=lp8, tbl3=tbl.reshape(nt, 1, nch_tbl), nch=nch, nused=nused, blk_expert=blk_expert,
                zero_blk=zero_blk, P=P, NR=NR)


def kernel(x, c, w_ada, b_ada, g_mix, w_in, w_fourier, q_norm_g, k_norm_g, lambda_q1, lambda_k1,
           lambda_q2, lambda_k2, g_subln, w_out, rel_bias, g_ffn, w_router, b_router, w1, b1, w2, b2):
    B, S, D = x.shape
    T = B * S
    L = w_ada.shape[0]
    E = w_router.shape[-1]
    cfg = _tiles(S)
    tm, t_attn, bm = cfg["tm"], cfg["t_attn"], cfg["bm"]
    dqk = q_norm_g.shape[-1]
    scale = dqk ** -0.5
    fw = w_in.shape[-1] // 4

    btiles = _bias_tiles(rel_bias, t_attn)
    x2 = x.reshape(T, D)
    for l in range(L):
        lam_init = 0.8 - 0.6 * math.exp(-0.3 * l)
        mod3 = _ada(c, w_ada[l], b_ada[l]).reshape(B, 6, D)
        gq = (jnp.tile(q_norm_g[l], 2) * scale).reshape(1, LANES)
        gk = jnp.tile(k_norm_g[l], 2).reshape(1, LANES)
        uf, qx, kx, v = _in_proj(x2, mod3, g_mix[l].reshape(1, D), w_in[l].astype(BF16), gq, gk,
                                 S, tm, t_attn)
        yf = _fourier(uf.reshape(B, S, fw), w_fourier[l]).reshape(T, fw)
        lam4 = jnp.stack([lambda_q1[l], lambda_k1[l], lambda_q2[l], lambda_k2[l]]).astype(F32)
        yd = _attention(rel_bias, qx.reshape(B, S, -1), kx.reshape(B, S, -1), v.reshape(B, S, fw),
                        btiles, lam4, g_subln[l].reshape(1, LANES), t_attn, lam_init).reshape(T, fw)
        x1, hp, idx_t, lrank_t, grow, cnt3 = _out_proj(
            yf, yd, x2, mod3, w_out[l].astype(BF16), g_ffn[l].reshape(1, D),
            w_router[l].T.astype(BF16), b_router[l].reshape(E, 1), S, tm)
        lay = _moe_layout(cnt3[:, :, 0], idx_t, lrank_t, tm, bm)
        xs = _dispatch(lay["tbl3"], lay["nch"], lay["zero_blk"], lay["lp8"], hp, lay["P"], bm, tm, lay["NR"])
        y = _experts(lay["blk_expert"], lay["nused"], xs, w1[l], b1[l][:, None, 0::2], b1[l][:, None, 1::2],
                     w2[l], b2[l][:, None, :], bm)
        x2 = _combine(lay["tbl3"], lay["nch"], lay["lp8"], grow, x1, mod3, y, S, tm, lay["NR"])
    return x2.reshape(B, S, D)
```

```python
import functools
import math

import numpy as np
import jax
import jax.numpy as jnp
from jax import lax
from jax.experimental import pallas as pl
from jax.experimental.pallas import tpu as pltpu

F32 = jnp.float32
BF16 = jnp.bfloat16

EPS = 1e-6
N_FOURIER_GROUPS = 4
N_DIFF_HEADS = 4
TOP_K = 4
N_BUCKETS = 32
MAX_DISTANCE = 128
SWIGLU_ALPHA = 1.702
SWIGLU_LIMIT = 7.0
LANES = 128
DFT_RADIX = 4
LOG2E = 1.4426950408889634
VMEM_LIMIT = 56 * 1024 * 1024


def _cparams(sem):
    return pltpu.CompilerParams(dimension_semantics=sem, vmem_limit_bytes=VMEM_LIMIT)


def _ada_kernel(c_ref, w_ref, b_ref, o_ref):
    c = c_ref[...]
    cond = c * jax.nn.sigmoid(c)
    o_ref[...] = jnp.dot(cond, w_ref[...], preferred_element_type=F32,
                         precision=lax.Precision.HIGHEST) + b_ref[...]


def _ada(c, w, b):
    B, D = c.shape
    N = w.shape[1]
    tn = 1536 if N % 1536 == 0 else N
    return pl.pallas_call(
        _ada_kernel,
        grid=(N // tn,),
        in_specs=[pl.BlockSpec((B, D), lambda j: (0, 0)),
                  pl.BlockSpec((D, tn), lambda j: (0, j)),
                  pl.BlockSpec((1, tn), lambda j: (0, j))],
        out_specs=pl.BlockSpec((B, tn), lambda j: (0, j)),
        out_shape=jax.ShapeDtypeStruct((B, N), F32),
        compiler_params=_cparams(("arbitrary",)),
        name="ada",
    )(c, w, b.reshape(1, N))


def _group_rms_inv(xh, gmat, group):
    sq = xh * xh
    hi = sq.astype(BF16)
    lo = (sq - hi.astype(F32)).astype(BF16)
    ss = jnp.dot(hi, gmat, preferred_element_type=F32) + jnp.dot(lo, gmat, preferred_element_type=F32)
    return lax.rsqrt(ss * (1.0 / group) + EPS)


def _in_proj_kernel(x_ref, mod_ref, g_ref, w_ref, gq_ref, gk_ref, gmat_ref,
                    uf_ref, q_ref, k_ref, v_ref, *, tiles_per_batch, tm, t_attn, n_chunks, fw, qw):
    i = pl.program_id(0)
    x = x_ref[...]
    ms = jnp.mean(x * x, axis=-1, keepdims=True)
    y = x * lax.rsqrt(ms + EPS) * g_ref[...]
    h = y * (1.0 + mod_ref[1:2, :]) + mod_ref[0:1, :]
    proj = jnp.dot(h.astype(BF16), w_ref[...], preferred_element_type=F32)
    uf_ref[...] = proj[:, :fw].astype(BF16)
    v_ref[...] = proj[:, fw + 2 * qw:].astype(BF16)

    lane = lax.broadcasted_iota(jnp.int32, (1, LANES), 1)
    chunk = ((i % tiles_per_batch) * tm) // t_attn
    gmat = gmat_ref[...]
    half = LANES // 2
    for hd in range(N_DIFF_HEADS):
        qh = proj[:, fw + hd * LANES: fw + (hd + 1) * LANES]
        kh = proj[:, fw + qw + hd * LANES: fw + qw + (hd + 1) * LANES]
        qn = (qh * _group_rms_inv(qh, gmat, half) * gq_ref[...]).astype(BF16)
        kn = (kh * _group_rms_inv(kh, gmat, half) * gk_ref[...]).astype(BF16)
        for m in range(2):
            fl = lane - half if m == 0 else lane
            is_data = (lane < half) if m == 0 else (lane >= half)
            f = jnp.where(fl < n_chunks, fl, fl - n_chunks)
            onehot = ((fl >= 0) & (fl < 2 * n_chunks) & (f == chunk)).astype(BF16)
            c0 = (2 * hd + m) * LANES
            q_ref[:, c0:c0 + LANES] = jnp.where(is_data, qn, jnp.zeros_like(qn))
            k_ref[:, c0:c0 + LANES] = jnp.where(is_data, kn, onehot)


def _in_proj(x2, mod3, g_mix, w_in_b, gq, gk, S, tm, t_attn):
    T, D = x2.shape
    ncols = w_in_b.shape[1]
    fw = ncols // 4
    qw = fw
    tiles_per_batch = S // tm
    n_chunks = S // t_attn
    half = LANES // 2
    gmat = (np.arange(LANES)[:, None] // half == np.arange(LANES)[None, :] // half)
    gmat = jnp.asarray(gmat, BF16)
    kern = functools.partial(_in_proj_kernel, tiles_per_batch=tiles_per_batch, tm=tm, t_attn=t_attn,
                             n_chunks=n_chunks, fw=fw, qw=qw)
    ext = 2 * qw
    return pl.pallas_call(
        kern,
        grid=(T // tm,),
        in_specs=[pl.BlockSpec((tm, D), lambda i: (i, 0)),
                  pl.BlockSpec((None, 6, D), lambda i: (i // tiles_per_batch, 0, 0)),
                  pl.BlockSpec((1, D), lambda i: (0, 0)),
                  pl.BlockSpec((D, ncols), lambda i: (0, 0)),
                  pl.BlockSpec((1, LANES), lambda i: (0, 0)),
                  pl.BlockSpec((1, LANES), lambda i: (0, 0)),
                  pl.BlockSpec((LANES, LANES), lambda i: (0, 0))],
        out_specs=[pl.BlockSpec((tm, fw), lambda i: (i, 0)),
                   pl.BlockSpec((tm, ext), lambda i: (i, 0)),
                   pl.BlockSpec((tm, ext), lambda i: (i, 0)),
                   pl.BlockSpec((tm, fw), lambda i: (i, 0))],
        out_shape=[jax.ShapeDtypeStruct((T, fw), BF16),
                   jax.ShapeDtypeStruct((T, ext), BF16),
                   jax.ShapeDtypeStruct((T, ext), BF16),
                   jax.ShapeDtypeStruct((T, fw), BF16)],
        compiler_params=_cparams(("arbitrary",)),
        name="in_proj",
    )(x2, mod3, g_mix, w_in_b, gq, gk, gmat)


def _fourier_kernel(uf_ref, wcs_ref, tw_ref, dft_ref, o_ref, tt_sc, *, M, gw):
    c = pl.program_id(1)

    @pl.when(c == 0)
    def _():
        for g in range(N_FOURIER_GROUPS):
            zr, zi = [], []
            for j in range(DFT_RADIX):
                ab = jnp.dot(uf_ref[j * M:(j + 1) * M, g * gw:(g + 1) * gw], wcs_ref[g],
                             preferred_element_type=F32)
                zr.append(ab[:, :gw])
                zi.append(-ab[:, gw:])
            yr = [zr[0] + zr[1] + zr[2] + zr[3],
                  zr[0] + zi[1] - zr[2] - zi[3],
                  zr[0] - zr[1] + zr[2] - zr[3],
                  zr[0] - zi[1] - zr[2] + zi[3]]
            yi = [zi[0] + zi[1] + zi[2] + zi[3],
                  zi[0] - zr[1] - zi[2] + zr[3],
                  zi[0] - zi[1] + zi[2] - zi[3],
                  zi[0] + zr[1] - zi[2] - zr[3]]
            for k2 in range(DFT_RADIX):
                cs = tw_ref[:, k2:k2 + 1]
                sn = tw_ref[:, DFT_RADIX + k2:DFT_RADIX + k2 + 1]
                tr = cs * yr[k2] + sn * yi[k2]
                ti = cs * yi[k2] - sn * yr[k2]
                tt_sc[k2, 0:M, g * gw:(g + 1) * gw] = tr.astype(BF16)
                tt_sc[k2, M:2 * M, g * gw:(g + 1) * gw] = ti.astype(BF16)

    o_ref[...] = jnp.dot(dft_ref[...], tt_sc[c], preferred_element_type=F32).astype(BF16)


def _fourier(uf3, w_f):
    B, S, W = uf3.shape
    G = N_FOURIER_GROUPS
    gw = W // G
    M = S // DFT_RADIX
    norm = 1.0 / math.sqrt(S * gw)
    cidx = (np.arange(gw)[:, None] * np.arange(gw)[None, :]) % gw
    ang_c = jnp.asarray(cidx, F32) * (2.0 * math.pi / gw)
    cc, sc = jnp.cos(ang_c) * norm, jnp.sin(ang_c) * norm
    hp = lax.Precision.HIGHEST
    wcs = jnp.concatenate([jnp.einsum("ab,gbd->gad", cc, w_f, precision=hp),
                           jnp.einsum("ab,gbd->gad", sc, w_f, precision=hp)], axis=-1).astype(BF16)
    n1 = np.arange(M)
    tw_idx = (np.arange(DFT_RADIX)[None, :] * n1[:, None]) % S
    ang_t = jnp.asarray(tw_idx, F32) * (2.0 * math.pi / S)
    tw = jnp.concatenate([jnp.cos(ang_t), jnp.sin(ang_t),
                          jnp.zeros((M, LANES - 2 * DFT_RADIX), F32)], axis=1)
    pidx = (n1[:, None] * n1[None, :]) % M
    ang_p = jnp.asarray(pidx, F32) * (2.0 * math.pi / M)
    dft = jnp.concatenate([jnp.cos(ang_p), jnp.sin(ang_p)], axis=1).astype(BF16)
    kern = functools.partial(_fourier_kernel, M=M, gw=gw)
    return pl.pallas_call(
        kern,
        grid=(B, DFT_RADIX),
        in_specs=[pl.BlockSpec((None, S, W), lambda b, c: (b, 0, 0)),
                  pl.BlockSpec((G, gw, 2 * gw), lambda b, c: (0, 0, 0)),
                  pl.BlockSpec((M, LANES), lambda b, c: (0, 0)),
                  pl.BlockSpec((M, 2 * M), lambda b, c: (0, 0))],
        out_specs=pl.BlockSpec((None, M, W), lambda b, c: (b, 0, c)),
        out_shape=jax.ShapeDtypeStruct((B, M, DFT_RADIX * W), BF16),
        scratch_shapes=[pltpu.VMEM((DFT_RADIX, 2 * M, W), BF16)],
        compiler_params=_cparams(("arbitrary", "arbitrary")),
        name="fourier",
    )(uf3, wcs, tw, dft)


def _bias_kernel(rb_ref, o_ref, *, t):
    h = pl.program_id(0)
    d = pl.program_id(1) - 1
    row = lax.broadcasted_iota(jnp.int32, (t, t), 0)
    col = lax.broadcasted_iota(jnp.int32, (t, t), 1)
    rel = d * t + col - row
    nb = N_BUCKETS // 2
    max_exact = nb // 2
    ret = jnp.where(rel > 0, nb, 0)
    n = jnp.abs(rel)
    nf = jnp.maximum(n, 1).astype(F32)
    large = max_exact + (jnp.log(nf / max_exact) / math.log(MAX_DISTANCE / max_exact)
                         * (nb - max_exact)).astype(jnp.int32)
    large = jnp.minimum(large, nb - 1)
    bucket = ret + jnp.where(n < max_exact, n, large)
    val = jnp.zeros((t, t), F32)
    for j in range(N_BUCKETS):
        val = jnp.where(bucket == j, rb_ref[j * N_DIFF_HEADS + h], val)
    o_ref[...] = jnp.where(pl.program_id(1) == 3, 0.0, val * LOG2E)


def _bias_tiles(rel_bias, t):
    H = N_DIFF_HEADS
    assert t >= MAX_DISTANCE, "far key tiles must lie entirely in the saturated bucket"
    return pl.pallas_call(
        functools.partial(_bias_kernel, t=t),
        grid=(H, 4),
        in_specs=[pl.BlockSpec(memory_space=pltpu.SMEM)],
        out_specs=pl.BlockSpec((None, None, t, t), lambda h, d: (h, d, 0, 0)),
        out_shape=jax.ShapeDtypeStruct((H, 4, t, t), F32),
        compiler_params=_cparams(("arbitrary", "arbitrary")),
        name="t5_bias",
    )(rel_bias.reshape(-1))


def _far_bucket_consts():
    nb = N_BUCKETS // 2
    return nb - 1, 2 * nb - 1


def _attn_kernel(rb_ref, q_ref, k_ref, v_ref, b_ref, lam_ref, gs_ref, o_ref,
                 s_sc, p_sc, vext_sc, m_sc, o_sc, *, t, n, lam_init):
    h = pl.program_id(1)
    i = pl.program_id(2)
    half = LANES // 2

    @pl.when(i == 0)
    def _():
        vext_sc[:, :LANES] = v_ref[...]
        vext_sc[:, LANES:] = jnp.ones(v_ref.shape, BF16)

    left_b, right_b = _far_bucket_consts()
    c_left = rb_ref[left_b * N_DIFF_HEADS + h] * LOG2E
    c_right = rb_ref[right_b * N_DIFF_HEADS + h] * LOG2E
    lane = lax.broadcasted_iota(jnp.int32, (1, LANES), 1)

    def all_scores(m):
        fl = lane - half if m == 0 else lane
        is_data = (lane < half) if m == 0 else (lane >= half)
        f = jnp.where(fl < n, fl, fl - n)
        cfar = jnp.where(f < i - 1, c_left, jnp.where(f > i + 1, c_right, 0.0))
        c_hi = cfar.astype(BF16)
        c_lo = (cfar - c_hi.astype(F32)).astype(BF16)
        feat = jnp.where(fl < n, c_hi, c_lo)
        feat = jnp.where((fl >= 0) & (fl < 2 * n), feat, jnp.zeros_like(feat))
        qm = jnp.where(is_data, q_ref[:, m * LANES:(m + 1) * LANES], feat)
        for j in range(n):
            kc = k_ref[j * t:(j + 1) * t, m * LANES:(m + 1) * LANES]
            s = lax.dot_general(qm, kc, (((1,), (1,)), ((), ())), preferred_element_type=F32)
            d = j - i
            s = s + b_ref[jnp.where(jnp.abs(d) <= 1, d + 1, 3)]
            s_sc[m, j] = s
            tile_max = s[:, 0:LANES]
            for c in range(1, t // LANES):
                tile_max = jnp.maximum(tile_max, s[:, c * LANES:(c + 1) * LANES])
            if j == 0:
                m_sc[m] = tile_max
            else:
                m_sc[m] = jnp.maximum(m_sc[m], tile_max)

    def exponentials(m):
        mx = jnp.max(m_sc[m], axis=-1, keepdims=True)
        for j in range(n):
            p_sc[m, :, j * t:(j + 1) * t] = jnp.exp2(s_sc[m, j] - mx).astype(BF16)

    def weighted_values(m):
        o = jnp.dot(p_sc[m], vext_sc[...], preferred_element_type=F32)
        return o[:, :LANES] / o[:, LANES:LANES + 1]

    @pl.when(i < n)
    def _():
        all_scores(0)

    @pl.when(h < N_DIFF_HEADS)
    def _():
        all_scores(1)
        exponentials(0)
        o0 = weighted_values(0)
        exponentials(1)
        lam = (jnp.exp(jnp.sum(lam_ref[0:1, :] * lam_ref[1:2, :], axis=-1, keepdims=True))
               - jnp.exp(jnp.sum(lam_ref[2:3, :] * lam_ref[3:4, :], axis=-1, keepdims=True)) + lam_init)
        o = o0 - lam * weighted_values(1)
        ms = jnp.mean(o * o, axis=-1, keepdims=True)
        o_ref[...] = (o * lax.rsqrt(ms + EPS) * gs_ref[...] * (1.0 - lam_init)).astype(BF16)


def _attention(rel_bias, q3, k3, v3, btiles, lam4, g_subln, t, lam_init):
    B, S, _ = q3.shape
    H = N_DIFF_HEADS
    n = S // t
    kern = functools.partial(_attn_kernel, t=t, n=n, lam_init=lam_init)
    return pl.pallas_call(
        kern,
        grid=(B, H, n),
        in_specs=[pl.BlockSpec(memory_space=pltpu.SMEM),
                  pl.BlockSpec((None, t, 2 * LANES), lambda b, h, i: (b, i, h)),
                  pl.BlockSpec((None, S, 2 * LANES), lambda b, h, i: (b, 0, h)),
                  pl.BlockSpec((None, S, LANES), lambda b, h, i: (b, 0, h)),
                  pl.BlockSpec((None, 4, t, t), lambda b, h, i: (h, 0, 0, 0)),
                  pl.BlockSpec((4, LANES // 2), lambda b, h, i: (0, 0)),
                  pl.BlockSpec((1, LANES), lambda b, h, i: (0, 0))],
        out_specs=pl.BlockSpec((None, t, LANES), lambda b, h, i: (b, i, h)),
        out_shape=jax.ShapeDtypeStruct((B, S, H * LANES), BF16),
        scratch_shapes=[pltpu.VMEM((2, n, t, t), F32),
                        pltpu.VMEM((2, t, S), BF16),
                        pltpu.VMEM((S, 2 * LANES), BF16),
                        pltpu.VMEM((2, t, LANES), F32),
                        pltpu.VMEM((t, LANES), F32)],
        compiler_params=_cparams(("arbitrary", "arbitrary", "arbitrary")),
        name="diff_attn",
    )(rel_bias.reshape(-1), q3, k3, v3, btiles, lam4, g_subln)


def _pack_bf16_pairs(v):
    bits = lax.bitcast_convert_type(v.astype(BF16).astype(F32), jnp.uint32)
    n = bits.shape[1] // 2
    return (bits[:, :n] >> 16) | (bits[:, n:] & jnp.uint32(0xFFFF0000))


def _unpack_bf16_pairs(p):
    lo = lax.bitcast_convert_type(p << 16, F32).astype(BF16)
    hi = lax.bitcast_convert_type(p & jnp.uint32(0xFFFF0000), F32).astype(BF16)
    return lo, hi


def _out_proj_kernel(yf_ref, yd_ref, x_ref, mod_ref, wo_ref, g_ref, wr_ref, br_ref,
                     x1_ref, hp_ref, idx_ref, rank_ref, grow_ref, cnt_ref, *, tm, fw, E):
    mix =(jnp.dot(yf_ref[...], wo_ref[:fw, :], preferred_element_type=F32)
           + jnp.dot(yd_ref[...], wo_ref[fw:, :], preferred_element_type=F32))
    x1 = x_ref[...] + mod_ref[2:3, :] * mix
    x1_ref[...] = x1
    ms = jnp.mean(x1 * x1, axis=-1, keepdims=True)
    h = x1 * lax.rsqrt(ms + EPS) * g_ref[...]
    h = h * (1.0 + mod_ref[4:5, :]) + mod_ref[3:4, :]
    hb = h.astype(BF16)
    hp_ref[...] = _pack_bf16_pairs(hb)

    logits = lax.dot_general(wr_ref[...], hb, (((1,), (1,)), ((), ())),
                             preferred_element_type=F32) + br_ref[...]
    rowid = lax.broadcasted_iota(jnp.int32, (E, tm), 0)
    vals, idxs = [], []
    comb = jnp.zeros((E, tm), F32)
    l = logits
    for _ in range(TOP_K):
        mv = jnp.max(l, axis=0, keepdims=True)
        ix = jnp.min(jnp.where(l == mv, rowid, E), axis=0, keepdims=True)
        sel = rowid == ix
        vals.append(mv)
        idxs.append(ix)
        comb = comb + sel.astype(F32)
        l = jnp.where(sel, -jnp.inf, l)
    es = [jnp.exp(v - vals[0]) for v in vals]
    den = es[0] + es[1] + es[2] + es[3]
    gates = [e / den for e in es]

    r_i = lax.broadcasted_iota(jnp.int32, (tm, tm), 0)
    c_i = lax.broadcasted_iota(jnp.int32, (tm, tm), 1)
    upper = (r_i < c_i).astype(BF16)
    before = jnp.dot(comb.astype(BF16), upper, preferred_element_type=F32)
    ranks = [jnp.sum(jnp.where(rowid == ix, before, 0.0), axis=0, keepdims=True) for ix in idxs]
    cnt = jnp.sum(comb, axis=1, keepdims=True)
    cnt_ref[...] = jnp.broadcast_to(cnt, cnt_ref.shape).astype(jnp.int32)

    pad_i = jnp.zeros((8 - TOP_K, tm), jnp.int32)
    idx_ref[...] = jnp.concatenate(idxs + [pad_i], axis=0)
    rank_ref[...] = jnp.concatenate([r.astype(jnp.int32) for r in ranks] + [pad_i], axis=0)
    gpad = jnp.concatenate(gates + [jnp.zeros((LANES - TOP_K, tm), F32)], axis=0)
    grow_ref[...] = gpad.T


def _out_proj(yf2, yd2, x2, mod3, w_out_b, g_ffn, wr_t, br_t, S, tm):
    T, D = x2.shape
    fw = yf2.shape[1]
    E = wr_t.shape[0]
    nt = T // tm
    tiles_per_batch = S // tm
    kern = functools.partial(_out_proj_kernel, tm=tm, fw=fw, E=E)
    return pl.pallas_call(
        kern,
        grid=(nt,),
        in_specs=[pl.BlockSpec((tm, fw), lambda i: (i, 0)),
                  pl.BlockSpec((tm, fw), lambda i: (i, 0)),
                  pl.BlockSpec((tm, D), lambda i: (i, 0)),
                  pl.BlockSpec((None, 6, D), lambda i: (i // tiles_per_batch, 0, 0)),
                  pl.BlockSpec((D, D), lambda i: (0, 0)),
                  pl.BlockSpec((1, D), lambda i: (0, 0)),
                  pl.BlockSpec((E, D), lambda i: (0, 0)),
                  pl.BlockSpec((E, 1), lambda i: (0, 0))],
        out_specs=[pl.BlockSpec((tm, D), lambda i: (i, 0)),
                   pl.BlockSpec((tm, D // 2), lambda i: (i, 0)),
                   pl.BlockSpec((8, tm), lambda i: (0, i)),
                   pl.BlockSpec((8, tm), lambda i: (0, i)),
                   pl.BlockSpec((tm, LANES), lambda i: (i, 0)),
                   pl.BlockSpec((None, E, LANES), lambda i: (i, 0, 0))],
        out_shape=[jax.ShapeDtypeStruct((T, D), F32),
                   jax.ShapeDtypeStruct((T, D // 2), jnp.uint32),
                   jax.ShapeDtypeStruct((8, T), jnp.int32),
                   jax.ShapeDtypeStruct((8, T), jnp.int32),
                   jax.ShapeDtypeStruct((T, LANES), F32),
                   jax.ShapeDtypeStruct((nt, E, LANES), jnp.int32)],
        compiler_params=_cparams(("arbitrary",)),
        name="out_proj_router",
    )(yf2, yd2, x2, mod3, w_out_b, g_ffn, wr_t, br_t)


ROW_CHUNK = 8


def _wait_chunks(n, make_copy, max_rows):
    bit = 0
    while (ROW_CHUNK << bit) <= max_rows:
        rows = ROW_CHUNK << bit

        @pl.when(((n >> bit) & 1) == 1)
        def _():
            make_copy(rows).wait()
        bit += 1


def _dispatch_kernel(tbl_ref, nch_ref, zb_ref, lp_ref, hp_ref, xs_ref, buf, zero_sc, sem, zsem,
                     *, tm, bm, NR, nz, nt):
    i = pl.program_id(0)
    s = i % 2

    @pl.when(i == 0)
    def _():
        zero_sc[...] = jnp.zeros(zero_sc.shape, jnp.uint32)
        for e in range(nz):
            @pl.when(zb_ref[e] >= 0)
            def _():
                pltpu.make_async_copy(zero_sc, xs_ref.at[pl.ds(zb_ref[e] * bm, bm)], zsem).start()
        for e in range(nz):
            @pl.when(zb_ref[e] >= 0)
            def _():
                pltpu.make_async_copy(zero_sc, xs_ref.at[pl.ds(zb_ref[e] * bm, bm)], zsem).wait()

    lp = lp_ref[...]
    rows = lax.broadcasted_iota(jnp.int32, (NR, tm), 0)
    hit = rows == lp[0:1, :]
    for k in range(1, TOP_K):
        hit = hit | (rows == lp[k:k + 1, :])
    sel = jnp.where(hit, 1.0, 0.0).astype(BF16)
    lo, hi = _unpack_bf16_pairs(hp_ref[...])
    slo = jnp.dot(sel, lo, preferred_element_type=F32)
    shi = jnp.dot(sel, hi, preferred_element_type=F32)
    buf[s] = ((lax.bitcast_convert_type(slo, jnp.uint32) >> 16)
              | (lax.bitcast_convert_type(shi, jnp.uint32) & jnp.uint32(0xFFFF0000)))

    n = nch_ref[i]

    def issue(c, carry):
        d = pl.multiple_of(tbl_ref[0, 0, c], ROW_CHUNK)
        r = pl.multiple_of(c * ROW_CHUNK, ROW_CHUNK)
        pltpu.make_async_copy(buf.at[s, pl.ds(r, ROW_CHUNK)], xs_ref.at[pl.ds(d, ROW_CHUNK)],
                              sem.at[s]).start()
        return carry

    lax.fori_loop(0, n, issue, 0)

    def waiter(slot):
        return lambda nrows: pltpu.make_async_copy(buf.at[slot, pl.ds(0, nrows)],
                                                   xs_ref.at[pl.ds(0, nrows)], sem.at[slot])

    @pl.when(i > 0)
    def _():
        _wait_chunks(nch_ref[jnp.maximum(i - 1, 0)], waiter(1 - s), NR)

    @pl.when(i == nt - 1)
    def _():
        _wait_chunks(n, waiter(s), NR)


def _dispatch(tbl3, nch, zero_blk, lp8, hp, P, bm, tm, NR):
    T, Wd = hp.shape
    nt = T // tm
    nz = zero_blk.shape[0]
    nch_tbl = tbl3.shape[2]
    kern = functools.partial(_dispatch_kernel, tm=tm, bm=bm, NR=NR, nz=nz, nt=nt)
    return pl.pallas_call(
        kern,
        grid=(nt,),
        in_specs=[pl.BlockSpec((1, 1, nch_tbl), lambda i: (i, 0, 0), memory_space=pltpu.SMEM),
                  pl.BlockSpec(memory_space=pltpu.SMEM),
                  pl.BlockSpec(memory_space=pltpu.SMEM),
                  pl.BlockSpec((8, tm), lambda i: (0, i)),
                  pl.BlockSpec((tm, Wd), lambda i: (i, 0))],
        out_specs=pl.BlockSpec(memory_space=pl.ANY),
        out_shape=jax.ShapeDtypeStruct((P, Wd), jnp.uint32),
        scratch_shapes=[pltpu.VMEM((2, NR, Wd), jnp.uint32),
                        pltpu.VMEM((bm, Wd), jnp.uint32),
                        pltpu.SemaphoreType.DMA((2,)),
                        pltpu.SemaphoreType.DMA],
        compiler_params=_cparams(("arbitrary",)),
        name="moe_dispatch",
    )(tbl3, nch, zero_blk, lp8, hp)


def _expert_kernel(be_ref, nu_ref, x_ref, w1_ref, b1g_ref, b1l_ref, w2_ref, b2_ref, pm_ref, o_ref,
                   w1g_sc, w1l_sc, w2_sc):
    blk = pl.program_id(0)
    nused = nu_ref[0]
    e = be_ref[blk]
    prev = be_ref[jnp.maximum(blk - 1, 0)]
    active = blk < nused
    tile = 2 * LANES
    n_tiles = w1_ref.shape[1] // tile

    @pl.when(active & ((blk == 0) | (e != prev)))
    def _():
        for c in range(n_tiles):
            wt = w1_ref[:, c * tile:(c + 1) * tile].astype(BF16)
            pw = jnp.dot(wt, pm_ref[...], preferred_element_type=F32)
            w1g_sc[:, c * LANES:(c + 1) * LANES] = pw[:, :LANES].astype(BF16)
            w1l_sc[:, c * LANES:(c + 1) * LANES] = pw[:, LANES:].astype(BF16)
        w2_sc[...] = w2_ref[...].astype(BF16)

    @pl.when(jnp.logical_not(active))
    def _():
        o_ref[...] = jnp.zeros(o_ref.shape, jnp.uint32)

    @pl.when(active)
    def _():
        xp = x_ref[...]
        hw = xp.shape[1]
        xlo, xhi = _unpack_bf16_pairs(xp)
        zg =(jnp.dot(xlo, w1g_sc[:hw, :], preferred_element_type=F32)
              + jnp.dot(xhi, w1g_sc[hw:, :], preferred_element_type=F32) + b1g_ref[...])
        zl = (jnp.dot(xlo, w1l_sc[:hw, :], preferred_element_type=F32)
              + jnp.dot(xhi, w1l_sc[hw:, :], preferred_element_type=F32) + b1l_ref[...])
        g = jnp.minimum(zg, SWIGLU_LIMIT)
        lin = jnp.clip(zl, -SWIGLU_LIMIT, SWIGLU_LIMIT)
        act = g * jax.nn.sigmoid(SWIGLU_ALPHA * g) * (lin + 1.0)
        y = jnp.dot(act.astype(BF16), w2_sc[...], preferred_element_type=F32) + b2_ref[...]
        o_ref[...] = _pack_bf16_pairs(y)


def _experts(blk_expert, nused, xs, w1, b1g, b1l, w2, b2, bm):
    P, Wd = xs.shape
    E, D, F2 = w1.shape
    F = F2 // 2
    nblk = P // bm
    tile = 2 * LANES
    pm = np.zeros((tile, tile), np.float32)
    pm[2 * np.arange(LANES), np.arange(LANES)] = 1.0
    pm[2 * np.arange(LANES) + 1, LANES + np.arange(LANES)] = 1.0
    pm = jnp.asarray(pm, BF16)

    def row_map(b, be, nu):
        return (jnp.minimum(b, nu[0] - 1), 0)

    def exp_map(b, be, nu):
        return (be[b], 0, 0)

    grid_spec = pltpu.PrefetchScalarGridSpec(
        num_scalar_prefetch=2,
        grid=(nblk,),
        in_specs=[pl.BlockSpec((bm, Wd), row_map),
                  pl.BlockSpec((None, D, F2), exp_map),
                  pl.BlockSpec((None, 1, F), exp_map),
                  pl.BlockSpec((None, 1, F), exp_map),
                  pl.BlockSpec((None, F, D), exp_map),
                  pl.BlockSpec((None, 1, D), exp_map),
                  pl.BlockSpec((tile, tile), lambda b, be, nu: (0, 0))],
        out_specs=pl.BlockSpec((bm, D // 2), lambda b, be, nu: (b, 0)),
        scratch_shapes=[pltpu.VMEM((D, F), BF16),
                        pltpu.VMEM((D, F), BF16),
                        pltpu.VMEM((F, D), BF16)],
    )
    return pl.pallas_call(
        _expert_kernel,
        grid_spec=grid_spec,
        out_shape=jax.ShapeDtypeStruct((P, D // 2), jnp.uint32),
        compiler_params=_cparams(("arbitrary",)),
        name="moe_experts",
    )(blk_expert, nused, xs, w1, b1g, b1l, w2, b2, pm)


def _combine_kernel(tcur_ref, tnxt_ref, nch_ref, lp_ref, grow_ref, x1_ref, mod_ref, y_ref, o_ref,
                    buf, sem, *, tm, NR, nt):
    i = pl.program_id(0)
    s = i % 2

    def issue(tref, n, slot):
        def body(c, carry):
            d = pl.multiple_of(tref[0, 0, c], ROW_CHUNK)
            r = pl.multiple_of(c * ROW_CHUNK, ROW_CHUNK)
            pltpu.make_async_copy(y_ref.at[pl.ds(d, ROW_CHUNK)], buf.at[slot, pl.ds(r, ROW_CHUNK)],
                                  sem.at[slot]).start()
            return carry
        lax.fori_loop(0, n, body, 0)

    @pl.when(i == 0)
    def _():
        buf[...] = jnp.zeros(buf.shape, jnp.uint32)
        issue(tcur_ref, nch_ref[0], 0)

    @pl.when(i + 1 < nt)
    def _():
        issue(tnxt_ref, nch_ref[jnp.minimum(i + 1, nt - 1)], 1 - s)

    _wait_chunks(nch_ref[i],
                 lambda nrows: pltpu.make_async_copy(y_ref.at[pl.ds(0, nrows)],
                                                     buf.at[s, pl.ds(0, nrows)], sem.at[s]), NR)

    ylo, yhi = _unpack_bf16_pairs(buf[s])
    lpf = jnp.concatenate([lp_ref[...].astype(F32), jnp.zeros((LANES - 8, tm), F32)], axis=0)
    lpr = lpf.T.astype(jnp.int32)
    gr = grow_ref[...]
    cols = lax.broadcasted_iota(jnp.int32, (tm, NR), 1)
    gmat = jnp.where(cols == lpr[:, 0:1], gr[:, 0:1], 0.0)
    for k in range(1, TOP_K):
        gmat = gmat + jnp.where(cols == lpr[:, k:k + 1], gr[:, k:k + 1], 0.0)
    gmat = gmat.astype(BF16)
    hw = ylo.shape[1]
    o_ref[:, :hw] = x1_ref[:, :hw] + mod_ref[5:6, :hw] * jnp.dot(gmat, ylo, preferred_element_type=F32)
    o_ref[:, hw:] = x1_ref[:, hw:] + mod_ref[5:6, hw:] * jnp.dot(gmat, yhi, preferred_element_type=F32)


def _combine(tbl3, nch, lp8, grow, x1, mod3, y, S, tm, NR):
    T, D = x1.shape
    nt = T // tm
    tiles_per_batch = S // tm
    nch_tbl = tbl3.shape[2]
    kern = functools.partial(_combine_kernel, tm=tm, NR=NR, nt=nt)
    return pl.pallas_call(
        kern,
        grid=(nt,),
        in_specs=[pl.BlockSpec((1, 1, nch_tbl), lambda i: (i, 0, 0), memory_space=pltpu.SMEM),
                  pl.BlockSpec((1, 1, nch_tbl), lambda i: (jnp.minimum(i + 1, nt - 1), 0, 0),
                               memory_space=pltpu.SMEM),
                  pl.BlockSpec(memory_space=pltpu.SMEM),
                  pl.BlockSpec((8, tm), lambda i: (0, i)),
                  pl.BlockSpec((tm, LANES), lambda i: (i, 0)),
                  pl.BlockSpec((tm, D), lambda i: (i, 0)),
                  pl.BlockSpec((None, 6, D), lambda i: (i // tiles_per_batch, 0, 0)),
                  pl.BlockSpec(memory_space=pl.ANY)],
        out_specs=pl.BlockSpec((tm, D), lambda i: (i, 0)),
        out_shape=jax.ShapeDtypeStruct((T, D), F32),
        scratch_shapes=[pltpu.VMEM((2, NR, D // 2), jnp.uint32),
                        pltpu.SemaphoreType.DMA((2,))],
        compiler_params=_cparams(("arbitrary",)),
        name="moe_combine",
    )(tbl3, tbl3, nch, lp8, grow, x1, mod3, y)


def _tiles(S):
    t_attn = min(512, S // 4) if S >= 512 else S
    tm = min(512, t_attn)
    return dict(tm=tm, t_attn=t_attn, bm=512)


def _moe_layout(cnt, idx_t, lrank_t, tm, bm):
    nt, E = cnt.shape
    T = idx_t.shape[1]
    c8 = (cnt + ROW_CHUNK - 1) // ROW_CHUNK * ROW_CHUNK
    tile_end = jnp.cumsum(c8, axis=1)
    tile_off = tile_end - c8
    used = tile_end[:, -1]
    off = jnp.cumsum(c8, axis=0) - c8
    tot = jnp.sum(c8, axis=0)
    padded = (tot + bm - 1) // bm * bm
    pends = jnp.cumsum(padded)
    pstarts = pends - padded
    NR = TOP_K * tm + ROW_CHUNK * E
    nch_tbl = NR // ROW_CHUNK
    P = (T * TOP_K + nt * E * (ROW_CHUNK - 1) + E * bm + bm - 1) // bm * bm
    nblk = P // bm

    eids = jnp.arange(E, dtype=jnp.int32)
    tok_off = jnp.repeat(tile_off.T, tm, axis=1)
    lp = jnp.sum(jnp.where(idx_t[None, :TOP_K] == eids[:, None, None], tok_off[:, None, :], 0), axis=0) \
        + lrank_t[:TOP_K]
    lp8 = jnp.concatenate([lp, jnp.full((8 - TOP_K, T), -1, jnp.int32)], axis=0).astype(jnp.int32)

    c_row = jnp.arange(nch_tbl, dtype=jnp.int32) * ROW_CHUNK
    inrun = (tile_off[:, :, None] <= c_row) & (c_row < tile_end[:, :, None])
    base = pstarts[None, :] + off - tile_off
    tbl = jnp.sum(jnp.where(inrun, base[:, :, None], 0), axis=1) + c_row[None, :]
    tbl = jnp.where(c_row[None, :] < used[:, None], tbl, 0).astype(jnp.int32)
    nch = (used // ROW_CHUNK).astype(jnp.int32)

    nused = (pends[-1] // bm).astype(jnp.int32).reshape(1)
    blk_start = jnp.arange(nblk, dtype=jnp.int32) * bm
    blk_expert = jnp.minimum(jnp.sum(pends[None, :] <= blk_start[:, None], axis=1), E - 1).astype(jnp.int32)
    last_blk = jnp.where(padded > 0, pends // bm - 1, -1)
    n_tail = nblk - (T * TOP_K) // bm
    tail_blk = nused[0] + jnp.arange(n_tail, dtype=jnp.int32)
    tail_blk = jnp.where(tail_blk < nblk, tail_blk, -1)
    zero_blk = jnp.concatenate([last_blk, tail_blk]).astype(jnp.int32)
    return dict(lp8=lp8, tbl3=tbl.reshape(nt, 1, nch_tbl), nch=nch, nused=nused, blk_expert=blk_expert,
                zero_blk=zero_blk, P=P, NR=NR)


def kernel(x, c, w_ada, b_ada, g_mix, w_in, w_fourier, q_norm_g, k_norm_g, lambda_q1, lambda_k1,
           lambda_q2, lambda_k2, g_subln, w_out, rel_bias, g_ffn, w_router, b_router, w1, b1, w2, b2):
    B, S, D = x.shape
    T = B * S
    L = w_ada.shape[0]
    E = w_router.shape[-1]
    cfg = _tiles(S)
    tm, t_attn, bm = cfg["tm"], cfg["t_attn"], cfg["bm"]
    dqk = q_norm_g.shape[-1]
    scale = dqk ** -0.5
    fw = w_in.shape[-1] // 4

    btiles = _bias_tiles(rel_bias, t_attn)
    x2 = x.reshape(T, D)
    for l in range(L):
        lam_init = 0.8 - 0.6 * math.exp(-0.3 * l)
        mod3 = _ada(c, w_ada[l], b_ada[l]).reshape(B, 6, D)
        gq = (jnp.tile(q_norm_g[l], 2) * (scale * LOG2E)).reshape(1, LANES)
        gk = jnp.tile(k_norm_g[l], 2).reshape(1, LANES)
        uf, qx, kx, v = _in_proj(x2, mod3, g_mix[l].reshape(1, D), w_in[l].astype(BF16), gq, gk,
                                 S, tm, t_attn)
        yf = _fourier(uf.reshape(B, S, fw), w_fourier[l]).reshape(T, fw)
        lam4 = jnp.stack([lambda_q1[l], lambda_k1[l], lambda_q2[l], lambda_k2[l]]).astype(F32)
        yd = _attention(rel_bias, qx.reshape(B, S, -1), kx.reshape(B, S, -1), v.reshape(B, S, fw),
                        btiles, lam4, g_subln[l].reshape(1, LANES), t_attn, lam_init).reshape(T, fw)
        x1, hp, idx_t, lrank_t, grow, cnt3 = _out_proj(
            yf, yd, x2, mod3, w_out[l].astype(BF16), g_ffn[l].reshape(1, D),
            w_router[l].T.astype(BF16), b_router[l].reshape(E, 1), S, tm)
        lay = _moe_layout(cnt3[:, :, 0], idx_t, lrank_t, tm, bm)
        xs = _dispatch(lay["tbl3"], lay["nch"], lay["zero_blk"], lay["lp8"], hp, lay["P"], bm, tm, lay["NR"])
        y = _experts(lay["blk_expert"], lay["nused"], xs, w1[l], b1[l][:, None, 0::2], b1[l][:, None, 1::2],
                     w2[l], b2[l][:, None, :], bm)
        x2 = _combine(lay["tbl3"], lay["nch"], lay["lp8"], grow, x1, mod3, y, S, tm, lay["NR"])
    return x2.reshape(B, S, D)
```

```python
import functools
import math

import numpy as np
import jax
import jax.numpy as jnp
from jax import lax
from jax.experimental import pallas as pl
from jax.experimental.pallas import tpu as pltpu

F32 = jnp.float32
BF16 = jnp.bfloat16

EPS = 1e-6
N_FOURIER_GROUPS = 4
N_DIFF_HEADS = 4
TOP_K = 4
N_BUCKETS = 32
MAX_DISTANCE = 128
SWIGLU_ALPHA = 1.702
SWIGLU_LIMIT = 7.0
LANES = 128
DFT_RADIX = 4
LOG2E = 1.4426950408889634
VMEM_LIMIT = 56 * 1024 * 1024


def _cparams(sem):
    return pltpu.CompilerParams(dimension_semantics=sem, vmem_limit_bytes=VMEM_LIMIT)


def _ada_kernel(c_ref, w_ref, b_ref, o_ref):
    c = c_ref[...]
    cond = c * jax.nn.sigmoid(c)
    o_ref[...] = jnp.dot(cond, w_ref[...], preferred_element_type=F32,
                         precision=lax.Precision.HIGHEST) + b_ref[...]


def _ada(c, w, b):
    B, D = c.shape
    N = w.shape[1]
    tn = 1536 if N % 1536 == 0 else N
    return pl.pallas_call(
        _ada_kernel,
        grid=(N // tn,),
        in_specs=[pl.BlockSpec((B, D), lambda j: (0, 0)),
                  pl.BlockSpec((D, tn), lambda j: (0, j)),
                  pl.BlockSpec((1, tn), lambda j: (0, j))],
        out_specs=pl.BlockSpec((B, tn), lambda j: (0, j)),
        out_shape=jax.ShapeDtypeStruct((B, N), F32),
        compiler_params=_cparams(("arbitrary",)),
        name="ada",
    )(c, w, b.reshape(1, N))


def _group_rms_inv(xh, gmat, group):
    ss = jnp.dot((xh * xh).astype(BF16), gmat, preferred_element_type=F32)
    return lax.rsqrt(ss * (1.0 / group) + EPS)


def _in_proj_kernel(x_ref, mod_ref, g_ref, w_ref, gq_ref, gk_ref, gmat_ref,
                    uf_ref, q_ref, k_ref, v_ref, *, tiles_per_batch, tm, t_attn, n_chunks, fw, qw):
    i = pl.program_id(0)
    x = x_ref[...]
    ms = jnp.mean(x * x, axis=-1, keepdims=True)
    y = x * lax.rsqrt(ms + EPS) * g_ref[...]
    h = y * (1.0 + mod_ref[1:2, :]) + mod_ref[0:1, :]
    proj = jnp.dot(h.astype(BF16), w_ref[...], preferred_element_type=F32)
    uf_ref[...] = proj[:, :fw].astype(BF16)
    v_ref[...] = proj[:, fw + 2 * qw:].astype(BF16)

    lane = lax.broadcasted_iota(jnp.int32, (1, LANES), 1)
    chunk = ((i % tiles_per_batch) * tm) // t_attn
    gmat = gmat_ref[...]
    half = LANES // 2
    pair = 2 * LANES
    for hd in range(N_DIFF_HEADS):
        if hd % 2 == 0:
            q2 = proj[:, fw + hd * LANES: fw + hd * LANES + pair]
            k2 = proj[:, fw + qw + hd * LANES: fw + qw + hd * LANES + pair]
            qn2 = (q2 * _group_rms_inv(q2, gmat, half) * gq_ref[...]).astype(BF16)
            kn2 = (k2 * _group_rms_inv(k2, gmat, half) * gk_ref[...]).astype(BF16)
        qn = qn2[:, (hd % 2) * LANES:(hd % 2 + 1) * LANES]
        kn = kn2[:, (hd % 2) * LANES:(hd % 2 + 1) * LANES]
        for m in range(2):
            fl = lane - half if m == 0 else lane
            is_data = (lane < half) if m == 0 else (lane >= half)
            f = jnp.where(fl < n_chunks, fl, fl - n_chunks)
            onehot = ((fl >= 0) & (fl < 2 * n_chunks) & (f == chunk)).astype(BF16)
            c0 = (2 * hd + m) * LANES
            q_ref[:, c0:c0 + LANES] = jnp.where(is_data, qn, jnp.zeros_like(qn))
            k_ref[:, c0:c0 + LANES] = jnp.where(is_data, kn, onehot)


def _in_proj(x2, mod3, g_mix, w_in_b, gq, gk, S, tm, t_attn):
    T, D = x2.shape
    ncols = w_in_b.shape[1]
    fw = ncols // 4
    qw = fw
    tiles_per_batch = S // tm
    n_chunks = S // t_attn
    half = LANES // 2
    pair = 2 * LANES
    gmat = (np.arange(pair)[:, None] // half == np.arange(pair)[None, :] // half)
    gmat = jnp.asarray(gmat, BF16)
    kern = functools.partial(_in_proj_kernel, tiles_per_batch=tiles_per_batch, tm=tm, t_attn=t_attn,
                             n_chunks=n_chunks, fw=fw, qw=qw)
    ext = 2 * qw
    return pl.pallas_call(
        kern,
        grid=(T // tm,),
        in_specs=[pl.BlockSpec((tm, D), lambda i: (i, 0)),
                  pl.BlockSpec((None, 6, D), lambda i: (i // tiles_per_batch, 0, 0)),
                  pl.BlockSpec((1, D), lambda i: (0, 0)),
                  pl.BlockSpec((D, ncols), lambda i: (0, 0)),
                  pl.BlockSpec((1, pair), lambda i: (0, 0)),
                  pl.BlockSpec((1, pair), lambda i: (0, 0)),
                  pl.BlockSpec((pair, pair), lambda i: (0, 0))],
        out_specs=[pl.BlockSpec((tm, fw), lambda i: (i, 0)),
                   pl.BlockSpec((tm, ext), lambda i: (i, 0)),
                   pl.BlockSpec((tm, ext), lambda i: (i, 0)),
                   pl.BlockSpec((tm, fw), lambda i: (i, 0))],
        out_shape=[jax.ShapeDtypeStruct((T, fw), BF16),
                   jax.ShapeDtypeStruct((T, ext), BF16),
                   jax.ShapeDtypeStruct((T, ext), BF16),
                   jax.ShapeDtypeStruct((T, fw), BF16)],
        compiler_params=_cparams(("arbitrary",)),
        name="in_proj",
    )(x2, mod3, g_mix, w_in_b, gq, gk, gmat)


def _fourier_kernel(uf_ref, wcs_ref, tw_ref, dft_ref, o_ref, tt_sc, *, M, gw):
    c = pl.program_id(1)

    @pl.when(c == 0)
    def _():
        for g in range(N_FOURIER_GROUPS):
            zr, zi = [], []
            for j in range(DFT_RADIX):
                ab = jnp.dot(uf_ref[j * M:(j + 1) * M, g * gw:(g + 1) * gw], wcs_ref[g],
                             preferred_element_type=F32)
                zr.append(ab[:, :gw])
                zi.append(-ab[:, gw:])
            yr = [zr[0] + zr[1] + zr[2] + zr[3],
                  zr[0] + zi[1] - zr[2] - zi[3],
                  zr[0] - zr[1] + zr[2] - zr[3],
                  zr[0] - zi[1] - zr[2] + zi[3]]
            yi = [zi[0] + zi[1] + zi[2] + zi[3],
                  zi[0] - zr[1] - zi[2] + zr[3],
                  zi[0] - zi[1] + zi[2] - zi[3],
                  zi[0] + zr[1] - zi[2] - zr[3]]
            for k2 in range(DFT_RADIX):
                cs = tw_ref[:, k2:k2 + 1]
                sn = tw_ref[:, DFT_RADIX + k2:DFT_RADIX + k2 + 1]
                tr = cs * yr[k2] + sn * yi[k2]
                ti = cs * yi[k2] - sn * yr[k2]
                tt_sc[k2, 0:M, g * gw:(g + 1) * gw] = tr.astype(BF16)
                tt_sc[k2, M:2 * M, g * gw:(g + 1) * gw] = ti.astype(BF16)

    o_ref[...] = jnp.dot(dft_ref[...], tt_sc[c], preferred_element_type=F32).astype(BF16)


def _fourier(uf3, w_f):
    B, S, W = uf3.shape
    G = N_FOURIER_GROUPS
    gw = W // G
    M = S // DFT_RADIX
    norm = 1.0 / math.sqrt(S * gw)
    cidx = (np.arange(gw)[:, None] * np.arange(gw)[None, :]) % gw
    ang_c = jnp.asarray(cidx, F32) * (2.0 * math.pi / gw)
    cc, sc = jnp.cos(ang_c) * norm, jnp.sin(ang_c) * norm
    hp = lax.Precision.HIGHEST
    wcs = jnp.concatenate([jnp.einsum("ab,gbd->gad", cc, w_f, precision=hp),
                           jnp.einsum("ab,gbd->gad", sc, w_f, precision=hp)], axis=-1).astype(BF16)
    n1 = np.arange(M)
    tw_idx = (np.arange(DFT_RADIX)[None, :] * n1[:, None]) % S
    ang_t = jnp.asarray(tw_idx, F32) * (2.0 * math.pi / S)
    tw = jnp.concatenate([jnp.cos(ang_t), jnp.sin(ang_t),
                          jnp.zeros((M, LANES - 2 * DFT_RADIX), F32)], axis=1)
    pidx = (n1[:, None] * n1[None, :]) % M
    ang_p = jnp.asarray(pidx, F32) * (2.0 * math.pi / M)
    dft = jnp.concatenate([jnp.cos(ang_p), jnp.sin(ang_p)], axis=1).astype(BF16)
    kern = functools.partial(_fourier_kernel, M=M, gw=gw)
    return pl.pallas_call(
        kern,
        grid=(B, DFT_RADIX),
        in_specs=[pl.BlockSpec((None, S, W), lambda b, c: (b, 0, 0)),
                  pl.BlockSpec((G, gw, 2 * gw), lambda b, c: (0, 0, 0)),
                  pl.BlockSpec((M, LANES), lambda b, c: (0, 0)),
                  pl.BlockSpec((M, 2 * M), lambda b, c: (0, 0))],
        out_specs=pl.BlockSpec((None, M, W), lambda b, c: (b, 0, c)),
        out_shape=jax.ShapeDtypeStruct((B, M, DFT_RADIX * W), BF16),
        scratch_shapes=[pltpu.VMEM((DFT_RADIX, 2 * M, W), BF16)],
        compiler_params=_cparams(("arbitrary", "arbitrary")),
        name="fourier",
    )(uf3, wcs, tw, dft)


def _bias_kernel(rb_ref, o_ref, *, t):
    h = pl.program_id(0)
    which = pl.program_id(1)
    nb = N_BUCKETS // 2

    def tile(d, buckets):
        row = lax.broadcasted_iota(jnp.int32, (t, t), 0)
        col = lax.broadcasted_iota(jnp.int32, (t, t), 1)
        rel = d * t + col - row
        max_exact = nb // 2
        ret = jnp.where(rel > 0, nb, 0)
        n = jnp.abs(rel)
        nf = jnp.maximum(n, 1).astype(F32)
        large = max_exact + (jnp.log(nf / max_exact) / math.log(MAX_DISTANCE / max_exact)
                             * (nb - max_exact)).astype(jnp.int32)
        large = jnp.minimum(large, nb - 1)
        bucket = ret + jnp.where(n < max_exact, n, large)
        val = jnp.zeros((t, t), F32)
        for j in buckets:
            val = jnp.where(bucket == j, rb_ref[j * N_DIFF_HEADS + h], val)
        return val * LOG2E

    @pl.when(which == 0)
    def _():
        o_ref[...] = tile(-1, range(0, nb))

    @pl.when(which == 1)
    def _():
        o_ref[...] = tile(0, range(0, 2 * nb))

    @pl.when(which == 2)
    def _():
        o_ref[...] = tile(1, range(nb, 2 * nb))

    @pl.when(which == 3)
    def _():
        o_ref[...] = jnp.zeros((t, t), F32)


def _bias_tiles(rel_bias, t):
    H = N_DIFF_HEADS
    assert t >= MAX_DISTANCE, "far key tiles must lie entirely in the saturated bucket"
    return pl.pallas_call(
        functools.partial(_bias_kernel, t=t),
        grid=(H, 4),
        in_specs=[pl.BlockSpec(memory_space=pltpu.SMEM)],
        out_specs=pl.BlockSpec((None, None, t, t), lambda h, d: (h, d, 0, 0)),
        out_shape=jax.ShapeDtypeStruct((H, 4, t, t), F32),
        compiler_params=_cparams(("arbitrary", "arbitrary")),
        name="t5_bias",
    )(rel_bias.reshape(-1))


def _far_bucket_consts():
    nb = N_BUCKETS // 2
    return nb - 1, 2 * nb - 1


def _attn_kernel(rb_ref, q_ref, k_ref, v_ref, b_ref, lam_ref, gs_ref, o_ref,
                 s_sc, p_sc, vext_sc, m_sc, o_sc, *, t, n, lam_init):
    h = pl.program_id(1)
    i = pl.program_id(2)
    half = LANES // 2

    @pl.when(i == 0)
    def _():
        vext_sc[:, :LANES] = v_ref[...]
        vext_sc[:, LANES:] = jnp.ones(v_ref.shape, BF16)

    left_b, right_b = _far_bucket_consts()
    c_left = rb_ref[left_b * N_DIFF_HEADS + h] * LOG2E
    c_right = rb_ref[right_b * N_DIFF_HEADS + h] * LOG2E
    lane = lax.broadcasted_iota(jnp.int32, (1, LANES), 1)

    def all_scores(m):
        fl = lane - half if m == 0 else lane
        is_data = (lane < half) if m == 0 else (lane >= half)
        f = jnp.where(fl < n, fl, fl - n)
        cfar = jnp.where(f < i - 1, c_left, jnp.where(f > i + 1, c_right, 0.0))
        c_hi = cfar.astype(BF16)
        c_lo = (cfar - c_hi.astype(F32)).astype(BF16)
        feat = jnp.where(fl < n, c_hi, c_lo)
        feat = jnp.where((fl >= 0) & (fl < 2 * n), feat, jnp.zeros_like(feat))
        qm = jnp.where(is_data, q_ref[:, m * LANES:(m + 1) * LANES], feat)
        for j in range(n):
            kc = k_ref[j * t:(j + 1) * t, m * LANES:(m + 1) * LANES]
            s = lax.dot_general(qm, kc, (((1,), (1,)), ((), ())), preferred_element_type=F32)
            d = j - i
            s = s + b_ref[jnp.where(jnp.abs(d) <= 1, d + 1, 3)]
            s_sc[m, j] = s
            tile_max = s[:, 0:LANES]
            for c in range(1, t // LANES):
                tile_max = jnp.maximum(tile_max, s[:, c * LANES:(c + 1) * LANES])
            if j == 0:
                m_sc[m] = tile_max
            else:
                m_sc[m] = jnp.maximum(m_sc[m], tile_max)

    def exponentials(m):
        mx = jnp.max(m_sc[m], axis=-1, keepdims=True)
        for j in range(n):
            p_sc[m, :, j * t:(j + 1) * t] = jnp.exp2(s_sc[m, j] - mx).astype(BF16)

    def weighted_values(m):
        o = jnp.dot(p_sc[m], vext_sc[...], preferred_element_type=F32)
        return o[:, :LANES] / o[:, LANES:LANES + 1]

    @pl.when(i < n)
    def _():
        all_scores(0)

    @pl.when(h < N_DIFF_HEADS)
    def _():
        all_scores(1)
        exponentials(0)
        o0 = weighted_values(0)
        exponentials(1)
        lam = (jnp.exp(jnp.sum(lam_ref[0:1, :] * lam_ref[1:2, :], axis=-1, keepdims=True))
               - jnp.exp(jnp.sum(lam_ref[2:3, :] * lam_ref[3:4, :], axis=-1, keepdims=True)) + lam_init)
        o = o0 - lam * weighted_values(1)
        ms = jnp.mean(o * o, axis=-1, keepdims=True)
        o_ref[...] = (o * lax.rsqrt(ms + EPS) * gs_ref[...] * (1.0 - lam_init)).astype(BF16)


def _attention(rel_bias, q3, k3, v3, btiles, lam4, g_subln, t, lam_init):
    B, S, _ = q3.shape
    H = N_DIFF_HEADS
    n = S // t
    kern = functools.partial(_attn_kernel, t=t, n=n, lam_init=lam_init)
    return pl.pallas_call(
        kern,
        grid=(B, H, n),
        in_specs=[pl.BlockSpec(memory_space=pltpu.SMEM),
                  pl.BlockSpec((None, t, 2 * LANES), lambda b, h, i: (b, i, h)),
                  pl.BlockSpec((None, S, 2 * LANES), lambda b, h, i: (b, 0, h)),
                  pl.BlockSpec((None, S, LANES), lambda b, h, i: (b, 0, h)),
                  pl.BlockSpec((None, 4, t, t), lambda b, h, i: (h, 0, 0, 0)),
                  pl.BlockSpec((4, LANES // 2), lambda b, h, i: (0, 0)),
                  pl.BlockSpec((1, LANES), lambda b, h, i: (0, 0))],
        out_specs=pl.BlockSpec((None, t, LANES), lambda b, h, i: (b, i, h)),
        out_shape=jax.ShapeDtypeStruct((B, S, H * LANES), BF16),
        scratch_shapes=[pltpu.VMEM((2, n, t, t), F32),
                        pltpu.VMEM((2, t, S), BF16),
                        pltpu.VMEM((S, 2 * LANES), BF16),
                        pltpu.VMEM((2, t, LANES), F32),
                        pltpu.VMEM((t, LANES), F32)],
        compiler_params=_cparams(("arbitrary", "arbitrary", "arbitrary")),
        name="diff_attn",
    )(rel_bias.reshape(-1), q3, k3, v3, btiles, lam4, g_subln)


def _pack_bf16_pairs(v):
    bits = lax.bitcast_convert_type(v.astype(BF16).astype(F32), jnp.uint32)
    n = bits.shape[1] // 2
    return (bits[:, :n] >> 16) | (bits[:, n:] & jnp.uint32(0xFFFF0000))


def _unpack_bf16_pairs(p):
    lo = lax.bitcast_convert_type(p << 16, F32).astype(BF16)
    hi = lax.bitcast_convert_type(p & jnp.uint32(0xFFFF0000), F32).astype(BF16)
    return lo, hi


def _out_proj_kernel(yf_ref, yd_ref, x_ref, mod_ref, wo_ref, g_ref, wr_ref, br_ref,
                     x1_ref, hp_ref, idx_ref, rank_ref, grow_ref, cnt_ref, *, tm, fw, E):
    mix =(jnp.dot(yf_ref[...], wo_ref[:fw, :], preferred_element_type=F32)
           + jnp.dot(yd_ref[...], wo_ref[fw:, :], preferred_element_type=F32))
    x1 = x_ref[...] + mod_ref[2:3, :] * mix
    x1_ref[...] = x1
    ms = jnp.mean(x1 * x1, axis=-1, keepdims=True)
    h = x1 * lax.rsqrt(ms + EPS) * g_ref[...]
    h = h * (1.0 + mod_ref[4:5, :]) + mod_ref[3:4, :]
    hb = h.astype(BF16)
    hp_ref[...] = _pack_bf16_pairs(hb)

    logits = lax.dot_general(wr_ref[...], hb, (((1,), (1,)), ((), ())),
                             preferred_element_type=F32) + br_ref[...]
    rowid = lax.broadcasted_iota(jnp.int32, (E, tm), 0)
    vals, idxs = [], []
    comb = jnp.zeros((E, tm), F32)
    l = logits
    for _ in range(TOP_K):
        mv = jnp.max(l, axis=0, keepdims=True)
        ix = jnp.min(jnp.where(l == mv, rowid, E), axis=0, keepdims=True)
        sel = rowid == ix
        vals.append(mv)
        idxs.append(ix)
        comb = comb + sel.astype(F32)
        l = jnp.where(sel, -jnp.inf, l)
    es = [jnp.exp(v - vals[0]) for v in vals]
    den = es[0] + es[1] + es[2] + es[3]
    gates = [e / den for e in es]

    r_i = lax.broadcasted_iota(jnp.int32, (tm, tm), 0)
    c_i = lax.broadcasted_iota(jnp.int32, (tm, tm), 1)
    upper = (r_i < c_i).astype(BF16)
    before = jnp.dot(comb.astype(BF16), upper, preferred_element_type=F32)
    ranks = [jnp.sum(jnp.where(rowid == ix, before, 0.0), axis=0, keepdims=True) for ix in idxs]
    cnt = jnp.sum(comb, axis=1, keepdims=True)
    cnt_ref[...] = jnp.broadcast_to(cnt, cnt_ref.shape).astype(jnp.int32)

    pad_i = jnp.zeros((8 - TOP_K, tm), jnp.int32)
    idx_ref[...] = jnp.concatenate(idxs + [pad_i], axis=0)
    rank_ref[...] = jnp.concatenate([r.astype(jnp.int32) for r in ranks] + [pad_i], axis=0)
    gpad = jnp.concatenate(gates + [jnp.zeros((LANES - TOP_K, tm), F32)], axis=0)
    grow_ref[...] = gpad.T


def _out_proj(yf2, yd2, x2, mod3, w_out_b, g_ffn, wr_t, br_t, S, tm):
    T, D = x2.shape
    fw = yf2.shape[1]
    E = wr_t.shape[0]
    nt = T // tm
    tiles_per_batch = S // tm
    kern = functools.partial(_out_proj_kernel, tm=tm, fw=fw, E=E)
    return pl.pallas_call(
        kern,
        grid=(nt,),
        in_specs=[pl.BlockSpec((tm, fw), lambda i: (i, 0)),
                  pl.BlockSpec((tm, fw), lambda i: (i, 0)),
                  pl.BlockSpec((tm, D), lambda i: (i, 0)),
                  pl.BlockSpec((None, 6, D), lambda i: (i // tiles_per_batch, 0, 0)),
                  pl.BlockSpec((D, D), lambda i: (0, 0)),
                  pl.BlockSpec((1, D), lambda i: (0, 0)),
                  pl.BlockSpec((E, D), lambda i: (0, 0)),
                  pl.BlockSpec((E, 1), lambda i: (0, 0))],
        out_specs=[pl.BlockSpec((tm, D), lambda i: (i, 0)),
                   pl.BlockSpec((tm, D // 2), lambda i: (i, 0)),
                   pl.BlockSpec((8, tm), lambda i: (0, i)),
                   pl.BlockSpec((8, tm), lambda i: (0, i)),
                   pl.BlockSpec((tm, LANES), lambda i: (i, 0)),
                   pl.BlockSpec((None, E, LANES), lambda i: (i, 0, 0))],
        out_shape=[jax.ShapeDtypeStruct((T, D), F32),
                   jax.ShapeDtypeStruct((T, D // 2), jnp.uint32),
                   jax.ShapeDtypeStruct((8, T), jnp.int32),
                   jax.ShapeDtypeStruct((8, T), jnp.int32),
                   jax.ShapeDtypeStruct((T, LANES), F32),
                   jax.ShapeDtypeStruct((nt, E, LANES), jnp.int32)],
        compiler_params=_cparams(("arbitrary",)),
        name="out_proj_router",
    )(yf2, yd2, x2, mod3, w_out_b, g_ffn, wr_t, br_t)


ROW_CHUNK = 8


def _wait_chunks(n, make_copy, max_rows):
    bit = 0
    while (ROW_CHUNK << bit) <= max_rows:
        rows = ROW_CHUNK << bit

        @pl.when(((n >> bit) & 1) == 1)
        def _():
            make_copy(rows).wait()
        bit += 1


def _dispatch_kernel(tbl_ref, nch_ref, zb_ref, lp_ref, hp_ref, xs_ref, buf, zero_sc, sem, zsem,
                     *, tm, bm, NR, nz, nt):
    i = pl.program_id(0)
    s = i % 2

    @pl.when(i == 0)
    def _():
        zero_sc[...] = jnp.zeros(zero_sc.shape, jnp.uint32)
        for e in range(nz):
            @pl.when(zb_ref[e] >= 0)
            def _():
                pltpu.make_async_copy(zero_sc, xs_ref.at[pl.ds(zb_ref[e] * bm, bm)], zsem).start()
        for e in range(nz):
            @pl.when(zb_ref[e] >= 0)
            def _():
                pltpu.make_async_copy(zero_sc, xs_ref.at[pl.ds(zb_ref[e] * bm, bm)], zsem).wait()

    lp = lp_ref[...]
    rows = lax.broadcasted_iota(jnp.int32, (NR, tm), 0)
    hit = rows == lp[0:1, :]
    for k in range(1, TOP_K):
        hit = hit | (rows == lp[k:k + 1, :])
    sel = jnp.where(hit, 1.0, 0.0).astype(BF16)
    lo, hi = _unpack_bf16_pairs(hp_ref[...])
    slo = jnp.dot(sel, lo, preferred_element_type=F32)
    shi = jnp.dot(sel, hi, preferred_element_type=F32)
    buf[s] = ((lax.bitcast_convert_type(slo, jnp.uint32) >> 16)
              | (lax.bitcast_convert_type(shi, jnp.uint32) & jnp.uint32(0xFFFF0000)))

    n = nch_ref[i]

    def issue(c, carry):
        d = pl.multiple_of(tbl_ref[0, 0, c], ROW_CHUNK)
        r = pl.multiple_of(c * ROW_CHUNK, ROW_CHUNK)
        pltpu.make_async_copy(buf.at[s, pl.ds(r, ROW_CHUNK)], xs_ref.at[pl.ds(d, ROW_CHUNK)],
                              sem.at[s]).start()
        return carry

    lax.fori_loop(0, n, issue, 0)

    def waiter(slot):
        return lambda nrows: pltpu.make_async_copy(buf.at[slot, pl.ds(0, nrows)],
                                                   xs_ref.at[pl.ds(0, nrows)], sem.at[slot])

    @pl.when(i > 0)
    def _():
        _wait_chunks(nch_ref[jnp.maximum(i - 1, 0)], waiter(1 - s), NR)

    @pl.when(i == nt - 1)
    def _():
        _wait_chunks(n, waiter(s), NR)


def _dispatch(tbl3, nch, zero_blk, lp8, hp, P, bm, tm, NR):
    T, Wd = hp.shape
    nt = T // tm
    nz = zero_blk.shape[0]
    nch_tbl = tbl3.shape[2]
    kern = functools.partial(_dispatch_kernel, tm=tm, bm=bm, NR=NR, nz=nz, nt=nt)
    return pl.pallas_call(
        kern,
        grid=(nt,),
        in_specs=[pl.BlockSpec((1, 1, nch_tbl), lambda i: (i, 0, 0), memory_space=pltpu.SMEM),
                  pl.BlockSpec(memory_space=pltpu.SMEM),
                  pl.BlockSpec(memory_space=pltpu.SMEM),
                  pl.BlockSpec((8, tm), lambda i: (0, i)),
                  pl.BlockSpec((tm, Wd), lambda i: (i, 0))],
        out_specs=pl.BlockSpec(memory_space=pl.ANY),
        out_shape=jax.ShapeDtypeStruct((P, Wd), jnp.uint32),
        scratch_shapes=[pltpu.VMEM((2, NR, Wd), jnp.uint32),
                        pltpu.VMEM((bm, Wd), jnp.uint32),
                        pltpu.SemaphoreType.DMA((2,)),
                        pltpu.SemaphoreType.DMA],
        compiler_params=_cparams(("arbitrary",)),
        name="moe_dispatch",
    )(tbl3, nch, zero_blk, lp8, hp)


def _expert_kernel(be_ref, nu_ref, h2_ref, x_ref, w1_ref, b1g_ref, b1l_ref, w2_ref, b2_ref, pm_ref, o_ref,
                   w1g_sc, w1l_sc, w2_sc):
    blk = pl.program_id(0)
    nused = nu_ref[0]
    e = be_ref[blk]
    prev = be_ref[jnp.maximum(blk - 1, 0)]
    active = blk < nused
    second = active & (h2_ref[blk] != 0)
    half = x_ref.shape[0] // 2
    tile = 2 * LANES
    n_tiles = w1_ref.shape[1] // tile

    @pl.when(active & ((blk == 0) | (e != prev)))
    def _():
        for c in range(n_tiles):
            wt = w1_ref[:, c * tile:(c + 1) * tile].astype(BF16)
            pw = jnp.dot(wt, pm_ref[...], preferred_element_type=F32)
            w1g_sc[:, c * LANES:(c + 1) * LANES] = pw[:, :LANES].astype(BF16)
            w1l_sc[:, c * LANES:(c + 1) * LANES] = pw[:, LANES:].astype(BF16)
        w2_sc[...] = w2_ref[...].astype(BF16)

    @pl.when(jnp.logical_not(active))
    def _():
        o_ref[...] = jnp.zeros(o_ref.shape, jnp.uint32)

    def mlp(r0):
        xp = x_ref[r0:r0 + half, :]
        hw = xp.shape[1]
        xlo, xhi = _unpack_bf16_pairs(xp)
        zg = (jnp.dot(xlo, w1g_sc[:hw, :], preferred_element_type=F32)
              + jnp.dot(xhi, w1g_sc[hw:, :], preferred_element_type=F32) + b1g_ref[...])
        zl = (jnp.dot(xlo, w1l_sc[:hw, :], preferred_element_type=F32)
              + jnp.dot(xhi, w1l_sc[hw:, :], preferred_element_type=F32) + b1l_ref[...])
        g = jnp.minimum(zg, SWIGLU_LIMIT)
        lin = jnp.clip(zl, -SWIGLU_LIMIT, SWIGLU_LIMIT)
        act = g * jax.nn.sigmoid(SWIGLU_ALPHA * g) * (lin + 1.0)
        y = jnp.dot(act.astype(BF16), w2_sc[...], preferred_element_type=F32) + b2_ref[...]
        o_ref[r0:r0 + half, :] = _pack_bf16_pairs(y)

    @pl.when(active)
    def _():
        mlp(0)

    @pl.when(second)
    def _():
        mlp(half)

    @pl.when(active & jnp.logical_not(second))
    def _():
        o_ref[half:, :] = jnp.zeros((half, o_ref.shape[1]), jnp.uint32)


def _experts(blk_expert, nused, half2, xs, w1, b1g, b1l, w2, b2, bm):
    P, Wd = xs.shape
    E, D, F2 = w1.shape
    F = F2 // 2
    nblk = P // bm
    tile = 2 * LANES
    pm = np.zeros((tile, tile), np.float32)
    pm[2 * np.arange(LANES), np.arange(LANES)] = 1.0
    pm[2 * np.arange(LANES) + 1, LANES + np.arange(LANES)] = 1.0
    pm = jnp.asarray(pm, BF16)

    def row_map(b, be, nu, h2):
        return (jnp.minimum(b, nu[0] - 1), 0)

    def exp_map(b, be, nu, h2):
        return (be[b], 0, 0)

    grid_spec = pltpu.PrefetchScalarGridSpec(
        num_scalar_prefetch=3,
        grid=(nblk,),
        in_specs=[pl.BlockSpec((bm, Wd), row_map),
                  pl.BlockSpec((None, D, F2), exp_map),
                  pl.BlockSpec((None, 1, F), exp_map),
                  pl.BlockSpec((None, 1, F), exp_map),
                  pl.BlockSpec((None, F, D), exp_map),
                  pl.BlockSpec((None, 1, D), exp_map),
                  pl.BlockSpec((tile, tile), lambda b, be, nu, h2: (0, 0))],
        out_specs=pl.BlockSpec((bm, D // 2), lambda b, be, nu, h2: (b, 0)),
        scratch_shapes=[pltpu.VMEM((D, F), BF16),
                        pltpu.VMEM((D, F), BF16),
                        pltpu.VMEM((F, D), BF16)],
    )
    return pl.pallas_call(
        _expert_kernel,
        grid_spec=grid_spec,
        out_shape=jax.ShapeDtypeStruct((P, D // 2), jnp.uint32),
        compiler_params=_cparams(("arbitrary",)),
        name="moe_experts",
    )(blk_expert, nused, half2, xs, w1, b1g, b1l, w2, b2, pm)


def _combine_kernel(tcur_ref, tnxt_ref, nch_ref, lp_ref, grow_ref, x1_ref, mod_ref, y_ref, o_ref,
                    buf, sem, *, tm, NR, nt):
    i = pl.program_id(0)
    s = i % 2

    def issue(tref, n, slot):
        def body(c, carry):
            d = pl.multiple_of(tref[0, 0, c], ROW_CHUNK)
            r = pl.multiple_of(c * ROW_CHUNK, ROW_CHUNK)
            pltpu.make_async_copy(y_ref.at[pl.ds(d, ROW_CHUNK)], buf.at[slot, pl.ds(r, ROW_CHUNK)],
                                  sem.at[slot]).start()
            return carry
        lax.fori_loop(0, n, body, 0)

    @pl.when(i == 0)
    def _():
        buf[...] = jnp.zeros(buf.shape, jnp.uint32)
        issue(tcur_ref, nch_ref[0], 0)

    @pl.when(i + 1 < nt)
    def _():
        issue(tnxt_ref, nch_ref[jnp.minimum(i + 1, nt - 1)], 1 - s)

    _wait_chunks(nch_ref[i],
                 lambda nrows: pltpu.make_async_copy(y_ref.at[pl.ds(0, nrows)],
                                                     buf.at[s, pl.ds(0, nrows)], sem.at[s]), NR)

    ylo, yhi = _unpack_bf16_pairs(buf[s])
    lpf = jnp.concatenate([lp_ref[...].astype(F32), jnp.zeros((LANES - 8, tm), F32)], axis=0)
    lpr = lpf.T.astype(jnp.int32)
    gr = grow_ref[...]
    cols = lax.broadcasted_iota(jnp.int32, (tm, NR), 1)
    gmat = jnp.where(cols == lpr[:, 0:1], gr[:, 0:1], 0.0)
    for k in range(1, TOP_K):
        gmat = gmat + jnp.where(cols == lpr[:, k:k + 1], gr[:, k:k + 1], 0.0)
    gmat = gmat.astype(BF16)
    hw = ylo.shape[1]
    o_ref[:, :hw] = x1_ref[:, :hw] + mod_ref[5:6, :hw] * jnp.dot(gmat, ylo, preferred_element_type=F32)
    o_ref[:, hw:] = x1_ref[:, hw:] + mod_ref[5:6, hw:] * jnp.dot(gmat, yhi, preferred_element_type=F32)


def _combine(tbl3, nch, lp8, grow, x1, mod3, y, S, tm, NR):
    T, D = x1.shape
    nt = T // tm
    tiles_per_batch = S // tm
    nch_tbl = tbl3.shape[2]
    kern = functools.partial(_combine_kernel, tm=tm, NR=NR, nt=nt)
    return pl.pallas_call(
        kern,
        grid=(nt,),
        in_specs=[pl.BlockSpec((1, 1, nch_tbl), lambda i: (i, 0, 0), memory_space=pltpu.SMEM),
                  pl.BlockSpec((1, 1, nch_tbl), lambda i: (jnp.minimum(i + 1, nt - 1), 0, 0),
                               memory_space=pltpu.SMEM),
                  pl.BlockSpec(memory_space=pltpu.SMEM),
                  pl.BlockSpec((8, tm), lambda i: (0, i)),
                  pl.BlockSpec((tm, LANES), lambda i: (i, 0)),
                  pl.BlockSpec((tm, D), lambda i: (i, 0)),
                  pl.BlockSpec((None, 6, D), lambda i: (i // tiles_per_batch, 0, 0)),
                  pl.BlockSpec(memory_space=pl.ANY)],
        out_specs=pl.BlockSpec((tm, D), lambda i: (i, 0)),
        out_shape=jax.ShapeDtypeStruct((T, D), F32),
        scratch_shapes=[pltpu.VMEM((2, NR, D // 2), jnp.uint32),
                        pltpu.SemaphoreType.DMA((2,))],
        compiler_params=_cparams(("arbitrary",)),
        name="moe_combine",
    )(tbl3, tbl3, nch, lp8, grow, x1, mod3, y)


def _tiles(S):
    t_attn = min(512, S // 4) if S >= 512 else S
    tm = min(512, t_attn)
    return dict(tm=tm, t_attn=t_attn, bm=1024)


def _moe_layout(cnt, idx_t, lrank_t, tm, bm):
    nt, E = cnt.shape
    T = idx_t.shape[1]
    c8 = (cnt + ROW_CHUNK - 1) // ROW_CHUNK * ROW_CHUNK
    tile_end = jnp.cumsum(c8, axis=1)
    tile_off = tile_end - c8
    used = tile_end[:, -1]
    off = jnp.cumsum(c8, axis=0) - c8
    tot = jnp.sum(c8, axis=0)
    padded = (tot + bm - 1) // bm * bm
    pends = jnp.cumsum(padded)
    pstarts = pends - padded
    NR = TOP_K * tm + ROW_CHUNK * E
    nch_tbl = NR // ROW_CHUNK
    P = (T * TOP_K + nt * E * (ROW_CHUNK - 1) + E * bm + bm - 1) // bm * bm
    nblk = P // bm

    eids = jnp.arange(E, dtype=jnp.int32)
    tok_off = jnp.repeat(tile_off.T, tm, axis=1)
    lp = jnp.sum(jnp.where(idx_t[None, :TOP_K] == eids[:, None, None], tok_off[:, None, :], 0), axis=0) \
        + lrank_t[:TOP_K]
    lp8 = jnp.concatenate([lp, jnp.full((8 - TOP_K, T), -1, jnp.int32)], axis=0).astype(jnp.int32)

    c_row = jnp.arange(nch_tbl, dtype=jnp.int32) * ROW_CHUNK
    inrun = (tile_off[:, :, None] <= c_row) & (c_row < tile_end[:, :, None])
    base = pstarts[None, :] + off - tile_off
    tbl = jnp.sum(jnp.where(inrun, base[:, :, None], 0), axis=1) + c_row[None, :]
    tbl = jnp.where(c_row[None, :] < used[:, None], tbl, 0).astype(jnp.int32)
    nch = (used // ROW_CHUNK).astype(jnp.int32)

    nused = (pends[-1] // bm).astype(jnp.int32).reshape(1)
    blk_start = jnp.arange(nblk, dtype=jnp.int32) * bm
    blk_expert = jnp.minimum(jnp.sum(pends[None, :] <= blk_start[:, None], axis=1), E - 1).astype(jnp.int32)
    half2 = (blk_start + bm // 2 < (pstarts + tot)[blk_expert]).astype(jnp.int32)
    last_blk = jnp.where(padded > 0, pends // bm - 1, -1)
    n_tail = nblk - (T * TOP_K) // bm
    tail_blk = nused[0] + jnp.arange(n_tail, dtype=jnp.int32)
    tail_blk = jnp.where(tail_blk < nblk, tail_blk, -1)
    zero_blk = jnp.concatenate([last_blk, tail_blk]).astype(jnp.int32)
    return dict(lp8=lp8, tbl3=tbl.reshape(nt, 1, nch_tbl), nch=nch, nused=nused, blk_expert=blk_expert,
                half2=half2, zero_blk=zero_blk, P=P, NR=NR)


def kernel(x, c, w_ada, b_ada, g_mix, w_in, w_fourier, q_norm_g, k_norm_g, lambda_q1, lambda_k1,
           lambda_q2, lambda_k2, g_subln, w_out, rel_bias, g_ffn, w_router, b_router, w1, b1, w2, b2):
    B, S, D = x.shape
    T = B * S
    L = w_ada.shape[0]
    E = w_router.shape[-1]
    cfg = _tiles(S)
    tm, t_attn, bm = cfg["tm"], cfg["t_attn"], cfg["bm"]
    dqk = q_norm_g.shape[-1]
    scale = dqk ** -0.5
    fw = w_in.shape[-1] // 4

    btiles = _bias_tiles(rel_bias, t_attn)
    x2 = x.reshape(T, D)
    for l in range(L):
        lam_init = 0.8 - 0.6 * math.exp(-0.3 * l)
        mod3 = _ada(c, w_ada[l], b_ada[l]).reshape(B, 6, D)
        gq = (jnp.tile(q_norm_g[l], 4) * (scale * LOG2E)).reshape(1, 2 * LANES)
        gk = jnp.tile(k_norm_g[l], 4).reshape(1, 2 * LANES)
        uf, qx, kx, v = _in_proj(x2, mod3, g_mix[l].reshape(1, D), w_in[l].astype(BF16), gq, gk,
                                 S, tm, t_attn)
        yf = _fourier(uf.reshape(B, S, fw), w_fourier[l]).reshape(T, fw)
        lam4 = jnp.stack([lambda_q1[l], lambda_k1[l], lambda_q2[l], lambda_k2[l]]).astype(F32)
        yd = _attention(rel_bias, qx.reshape(B, S, -1), kx.reshape(B, S, -1), v.reshape(B, S, fw),
                        btiles, lam4, g_subln[l].reshape(1, LANES), t_attn, lam_init).reshape(T, fw)
        x1, hp, idx_t, lrank_t, grow, cnt3 = _out_proj(
            yf, yd, x2, mod3, w_out[l].astype(BF16), g_ffn[l].reshape(1, D),
            w_router[l].T.astype(BF16), b_router[l].reshape(E, 1), S, tm)
        lay = _moe_layout(cnt3[:, :, 0], idx_t, lrank_t, tm, bm)
        xs = _dispatch(lay["tbl3"], lay["nch"], lay["zero_blk"], lay["lp8"], hp, lay["P"], bm, tm, lay["NR"])
        y = _experts(lay["blk_expert"], lay["nused"], lay["half2"], xs, w1[l], b1[l][:, None, 0::2],
                     b1[l][:, None, 1::2], w2[l], b2[l][:, None, :], bm)
        x2 = _combine(lay["tbl3"], lay["nch"], lay["lp8"], grow, x1, mod3, y, S, tm, lay["NR"])
    return x2.reshape(B, S, D)
```

```python
import functools
import math

import numpy as np
import jax
import jax.numpy as jnp
from jax import lax
from jax.experimental import pallas as pl
from jax.experimental.pallas import tpu as pltpu

F32 = jnp.float32
BF16 = jnp.bfloat16

EPS = 1e-6
N_FOURIER_GROUPS = 4
N_DIFF_HEADS = 4
TOP_K = 4
N_BUCKETS = 32
MAX_DISTANCE = 128
SWIGLU_ALPHA = 1.702
SWIGLU_LIMIT = 7.0
LANES = 128
DFT_RADIX = 4
LOG2E = 1.4426950408889634
VMEM_LIMIT = 56 * 1024 * 1024


def _cparams(sem):
    return pltpu.CompilerParams(dimension_semantics=sem, vmem_limit_bytes=VMEM_LIMIT)


def _ada_kernel(c_ref, w_ref, b_ref, o_ref):
    c = c_ref[...]
    cond = c * jax.nn.sigmoid(c)
    o_ref[...] = jnp.dot(cond, w_ref[...], preferred_element_type=F32,
                         precision=lax.Precision.HIGHEST) + b_ref[...]


def _ada(c, w, b):
    B, D = c.shape
    N = w.shape[1]
    tn = 1536 if N % 1536 == 0 else N
    return pl.pallas_call(
        _ada_kernel,
        grid=(N // tn,),
        in_specs=[pl.BlockSpec((B, D), lambda j: (0, 0)),
                  pl.BlockSpec((D, tn), lambda j: (0, j)),
                  pl.BlockSpec((1, tn), lambda j: (0, j))],
        out_specs=pl.BlockSpec((B, tn), lambda j: (0, j)),
        out_shape=jax.ShapeDtypeStruct((B, N), F32),
        compiler_params=_cparams(("arbitrary",)),
        name="ada",
    )(c, w, b.reshape(1, N))


def _group_rms_inv(xh, gmat, group):
    ss = jnp.dot((xh * xh).astype(BF16), gmat, preferred_element_type=F32)
    return lax.rsqrt(ss * (1.0 / group) + EPS)


def _in_proj_kernel(x_ref, mod_ref, g_ref, w_ref, gq_ref, gk_ref, gmat_ref,
                    uf_ref, q_ref, k_ref, v_ref, *, tiles_per_batch, tm, t_attn, n_chunks, fw, qw):
    i = pl.program_id(0)
    x = x_ref[...]
    ms = jnp.mean(x * x, axis=-1, keepdims=True)
    y = x * lax.rsqrt(ms + EPS) * g_ref[...]
    h = y * (1.0 + mod_ref[1:2, :]) + mod_ref[0:1, :]
    proj = jnp.dot(h.astype(BF16), w_ref[...], preferred_element_type=F32)
    uf_ref[...] = proj[:, :fw].astype(BF16)
    v_ref[...] = proj[:, fw + 2 * qw:].astype(BF16)

    lane = lax.broadcasted_iota(jnp.int32, (1, LANES), 1)
    chunk = ((i % tiles_per_batch) * tm) // t_attn
    gmat = gmat_ref[...]
    half = LANES // 2
    pair = 2 * LANES
    for hd in range(N_DIFF_HEADS):
        if hd % 2 == 0:
            q2 = proj[:, fw + hd * LANES: fw + hd * LANES + pair]
            k2 = proj[:, fw + qw + hd * LANES: fw + qw + hd * LANES + pair]
            qn2 = (q2 * _group_rms_inv(q2, gmat, half) * gq_ref[...]).astype(BF16)
            kn2 = (k2 * _group_rms_inv(k2, gmat, half) * gk_ref[...]).astype(BF16)
        qn = qn2[:, (hd % 2) * LANES:(hd % 2 + 1) * LANES]
        kn = kn2[:, (hd % 2) * LANES:(hd % 2 + 1) * LANES]
        for m in range(2):
            fl = lane - half if m == 0 else lane
            is_data = (lane < half) if m == 0 else (lane >= half)
            f = jnp.where(fl < n_chunks, fl, fl - n_chunks)
            onehot = ((fl >= 0) & (fl < 2 * n_chunks) & (f == chunk)).astype(BF16)
            c0 = (2 * hd + m) * LANES
            q_ref[:, c0:c0 + LANES] = jnp.where(is_data, qn, jnp.zeros_like(qn))
            k_ref[:, c0:c0 + LANES] = jnp.where(is_data, kn, onehot)


def _in_proj(x2, mod3, g_mix, w_in_b, gq, gk, S, tm, t_attn):
    T, D = x2.shape
    ncols = w_in_b.shape[1]
    fw = ncols // 4
    qw = fw
    tiles_per_batch = S // tm
    n_chunks = S // t_attn
    half = LANES // 2
    pair = 2 * LANES
    gmat = (np.arange(pair)[:, None] // half == np.arange(pair)[None, :] // half)
    gmat = jnp.asarray(gmat, BF16)
    kern = functools.partial(_in_proj_kernel, tiles_per_batch=tiles_per_batch, tm=tm, t_attn=t_attn,
                             n_chunks=n_chunks, fw=fw, qw=qw)
    ext = 2 * qw
    return pl.pallas_call(
        kern,
        grid=(T // tm,),
        in_specs=[pl.BlockSpec((tm, D), lambda i: (i, 0)),
                  pl.BlockSpec((None, 6, D), lambda i: (i // tiles_per_batch, 0, 0)),
                  pl.BlockSpec((1, D), lambda i: (0, 0)),
                  pl.BlockSpec((D, ncols), lambda i: (0, 0)),
                  pl.BlockSpec((1, pair), lambda i: (0, 0)),
                  pl.BlockSpec((1, pair), lambda i: (0, 0)),
                  pl.BlockSpec((pair, pair), lambda i: (0, 0))],
        out_specs=[pl.BlockSpec((tm, fw), lambda i: (i, 0)),
                   pl.BlockSpec((tm, ext), lambda i: (i, 0)),
                   pl.BlockSpec((tm, ext), lambda i: (i, 0)),
                   pl.BlockSpec((tm, fw), lambda i: (i, 0))],
        out_shape=[jax.ShapeDtypeStruct((T, fw), BF16),
                   jax.ShapeDtypeStruct((T, ext), BF16),
                   jax.ShapeDtypeStruct((T, ext), BF16),
                   jax.ShapeDtypeStruct((T, fw), BF16)],
        compiler_params=_cparams(("arbitrary",)),
        name="in_proj",
    )(x2, mod3, g_mix, w_in_b, gq, gk, gmat)


def _fourier_kernel(uf_ref, wcs_ref, tw_ref, dft_ref, o_ref, tt_sc, *, M, gw):
    c = pl.program_id(1)

    @pl.when(c == 0)
    def _():
        for g in range(N_FOURIER_GROUPS):
            zr, zi = [], []
            for j in range(DFT_RADIX):
                ab = jnp.dot(uf_ref[j * M:(j + 1) * M, g * gw:(g + 1) * gw], wcs_ref[g],
                             preferred_element_type=F32)
                zr.append(ab[:, :gw])
                zi.append(-ab[:, gw:])
            yr = [zr[0] + zr[1] + zr[2] + zr[3],
                  zr[0] + zi[1] - zr[2] - zi[3],
                  zr[0] - zr[1] + zr[2] - zr[3],
                  zr[0] - zi[1] - zr[2] + zi[3]]
            yi = [zi[0] + zi[1] + zi[2] + zi[3],
                  zi[0] - zr[1] - zi[2] + zr[3],
                  zi[0] - zi[1] + zi[2] - zi[3],
                  zi[0] + zr[1] - zi[2] - zr[3]]
            for k2 in range(DFT_RADIX):
                cs = tw_ref[:, k2:k2 + 1]
                sn = tw_ref[:, DFT_RADIX + k2:DFT_RADIX + k2 + 1]
                tr = cs * yr[k2] + sn * yi[k2]
                ti = cs * yi[k2] - sn * yr[k2]
                tt_sc[k2, 0:M, g * gw:(g + 1) * gw] = tr.astype(BF16)
                tt_sc[k2, M:2 * M, g * gw:(g + 1) * gw] = ti.astype(BF16)

    o_ref[...] = jnp.dot(dft_ref[...], tt_sc[c], preferred_element_type=F32).astype(BF16)


def _fourier(uf3, w_f):
    B, S, W = uf3.shape
    G = N_FOURIER_GROUPS
    gw = W // G
    M = S // DFT_RADIX
    norm = 1.0 / math.sqrt(S * gw)
    cidx = (np.arange(gw)[:, None] * np.arange(gw)[None, :]) % gw
    ang_c = jnp.asarray(cidx, F32) * (2.0 * math.pi / gw)
    cc, sc = jnp.cos(ang_c) * norm, jnp.sin(ang_c) * norm
    hp = lax.Precision.HIGHEST
    wcs = jnp.concatenate([jnp.einsum("ab,gbd->gad", cc, w_f, precision=hp),
                           jnp.einsum("ab,gbd->gad", sc, w_f, precision=hp)], axis=-1).astype(BF16)
    n1 = np.arange(M)
    tw_idx = (np.arange(DFT_RADIX)[None, :] * n1[:, None]) % S
    ang_t = jnp.asarray(tw_idx, F32) * (2.0 * math.pi / S)
    tw = jnp.concatenate([jnp.cos(ang_t), jnp.sin(ang_t),
                          jnp.zeros((M, LANES - 2 * DFT_RADIX), F32)], axis=1)
    pidx = (n1[:, None] * n1[None, :]) % M
    ang_p = jnp.asarray(pidx, F32) * (2.0 * math.pi / M)
    dft = jnp.concatenate([jnp.cos(ang_p), jnp.sin(ang_p)], axis=1).astype(BF16)
    kern = functools.partial(_fourier_kernel, M=M, gw=gw)
    return pl.pallas_call(
        kern,
        grid=(B, DFT_RADIX),
        in_specs=[pl.BlockSpec((None, S, W), lambda b, c: (b, 0, 0)),
                  pl.BlockSpec((G, gw, 2 * gw), lambda b, c: (0, 0, 0)),
                  pl.BlockSpec((M, LANES), lambda b, c: (0, 0)),
                  pl.BlockSpec((M, 2 * M), lambda b, c: (0, 0))],
        out_specs=pl.BlockSpec((None, M, W), lambda b, c: (b, 0, c)),
        out_shape=jax.ShapeDtypeStruct((B, M, DFT_RADIX * W), BF16),
        scratch_shapes=[pltpu.VMEM((DFT_RADIX, 2 * M, W), BF16)],
        compiler_params=_cparams(("arbitrary", "arbitrary")),
        name="fourier",
    )(uf3, wcs, tw, dft)


def _bias_kernel(rb_ref, o_ref, *, t):
    h = pl.program_id(0)
    which = pl.program_id(1)
    nb = N_BUCKETS // 2

    def tile(d, buckets):
        row = lax.broadcasted_iota(jnp.int32, (t, t), 0)
        col = lax.broadcasted_iota(jnp.int32, (t, t), 1)
        rel = d * t + col - row
        max_exact = nb // 2
        ret = jnp.where(rel > 0, nb, 0)
        n = jnp.abs(rel)
        nf = jnp.maximum(n, 1).astype(F32)
        large = max_exact + (jnp.log(nf / max_exact) / math.log(MAX_DISTANCE / max_exact)
                             * (nb - max_exact)).astype(jnp.int32)
        large = jnp.minimum(large, nb - 1)
        bucket = ret + jnp.where(n < max_exact, n, large)
        val = jnp.zeros((t, t), F32)
        for j in buckets:
            val = jnp.where(bucket == j, rb_ref[j * N_DIFF_HEADS + h], val)
        return val * LOG2E

    @pl.when(which == 0)
    def _():
        o_ref[...] = tile(-1, range(0, nb))

    @pl.when(which == 1)
    def _():
        o_ref[...] = tile(0, range(0, 2 * nb))

    @pl.when(which == 2)
    def _():
        o_ref[...] = tile(1, range(nb, 2 * nb))

    @pl.when(which == 3)
    def _():
        o_ref[...] = jnp.zeros((t, t), F32)


def _bias_tiles(rel_bias, t):
    H = N_DIFF_HEADS
    assert t >= MAX_DISTANCE, "far key tiles must lie entirely in the saturated bucket"
    return pl.pallas_call(
        functools.partial(_bias_kernel, t=t),
        grid=(H, 4),
        in_specs=[pl.BlockSpec(memory_space=pltpu.SMEM)],
        out_specs=pl.BlockSpec((None, None, t, t), lambda h, d: (h, d, 0, 0)),
        out_shape=jax.ShapeDtypeStruct((H, 4, t, t), F32),
        compiler_params=_cparams(("arbitrary", "arbitrary")),
        name="t5_bias",
    )(rel_bias.reshape(-1))


def _far_bucket_consts():
    nb = N_BUCKETS // 2
    return nb - 1, 2 * nb - 1


def _attn_kernel(rb_ref, q_ref, k_ref, v_ref, b_ref, lam_ref, gs_ref, o_ref,
                 s_sc, p_sc, vext_sc, m_sc, raw_sc, *, t, n, n_tiles, lam_init):
    g = pl.program_id(0)
    cur = jnp.minimum(g, n_tiles - 1)
    h = (cur // n) % N_DIFF_HEADS
    i = cur % n
    half = LANES // 2

    @pl.when(g == 0)
    def _():
        raw_sc[...] = jnp.ones(raw_sc.shape, F32)

    @pl.when((i == 0) & (g < n_tiles))
    def _():
        vext_sc[:, :LANES] = v_ref[...]
        vext_sc[:, LANES:] = jnp.ones(v_ref.shape, BF16)

    left_b, right_b = _far_bucket_consts()
    c_left = rb_ref[left_b * N_DIFF_HEADS + h] * LOG2E
    c_right = rb_ref[right_b * N_DIFF_HEADS + h] * LOG2E
    lane = lax.broadcasted_iota(jnp.int32, (1, LANES), 1)

    def all_scores(m):
        fl = lane - half if m == 0 else lane
        is_data = (lane < half) if m == 0 else (lane >= half)
        f = jnp.where(fl < n, fl, fl - n)
        cfar = jnp.where(f < i - 1, c_left, jnp.where(f > i + 1, c_right, 0.0))
        c_hi = cfar.astype(BF16)
        c_lo = (cfar - c_hi.astype(F32)).astype(BF16)
        feat = jnp.where(fl < n, c_hi, c_lo)
        feat = jnp.where((fl >= 0) & (fl < 2 * n), feat, jnp.zeros_like(feat))
        qm = jnp.where(is_data, q_ref[:, m * LANES:(m + 1) * LANES], feat)
        for j in range(n):
            kc = k_ref[j * t:(j + 1) * t, m * LANES:(m + 1) * LANES]
            s = lax.dot_general(qm, kc, (((1,), (1,)), ((), ())), preferred_element_type=F32)
            d = j - i
            s = s + b_ref[jnp.where(jnp.abs(d) <= 1, d + 1, 3)]
            s_sc[m, j] = s
            tile_max = s[:, 0:LANES]
            for c in range(1, t // LANES):
                tile_max = jnp.maximum(tile_max, s[:, c * LANES:(c + 1) * LANES])
            if j == 0:
                m_sc[m] = tile_max
            else:
                m_sc[m] = jnp.maximum(m_sc[m], tile_max)

    def exponentials(m):
        mx = jnp.max(m_sc[m], axis=-1, keepdims=True)
        for j in range(n):
            p_sc[m, :, j * t:(j + 1) * t] = jnp.exp2(s_sc[m, j] - mx).astype(BF16)

    def finish_previous():
        r0 = raw_sc[0]
        r1 = raw_sc[1]
        lam = (jnp.exp(jnp.sum(lam_ref[0:1, :] * lam_ref[1:2, :], axis=-1, keepdims=True))
               - jnp.exp(jnp.sum(lam_ref[2:3, :] * lam_ref[3:4, :], axis=-1, keepdims=True)) + lam_init)
        o = r0[:, :LANES] / r0[:, LANES:LANES + 1] - lam * (r1[:, :LANES] / r1[:, LANES:LANES + 1])
        ms = jnp.mean(o * o, axis=-1, keepdims=True)
        o_ref[...] = (o * lax.rsqrt(ms + EPS) * gs_ref[...] * (1.0 - lam_init)).astype(BF16)

    @pl.when(i < n)
    def _():
        finish_previous()
        all_scores(0)

    @pl.when(g < n_tiles)
    def _():
        all_scores(1)
        exponentials(0)
        raw_sc[0] = jnp.dot(p_sc[0], vext_sc[...], preferred_element_type=F32)
        exponentials(1)
        raw_sc[1] = jnp.dot(p_sc[1], vext_sc[...], preferred_element_type=F32)


def _attention(rel_bias, q3, k3, v3, btiles, lam4, g_subln, t, lam_init):
    B, S, _ = q3.shape
    H = N_DIFF_HEADS
    n = S // t
    n_tiles = B * H * n
    kern = functools.partial(_attn_kernel, t=t, n=n, n_tiles=n_tiles, lam_init=lam_init)

    def tile_of(g):
        return g // (H * n), (g // n) % H, g % n

    def cur_map(fn):
        def index_map(g):
            return fn(*tile_of(jnp.minimum(g, n_tiles - 1)))
        return index_map

    def prev_map(g):
        b, h, i = tile_of(jnp.maximum(g - 1, 0))
        return (b, i, h)

    return pl.pallas_call(
        kern,
        grid=(n_tiles + 1,),
        in_specs=[pl.BlockSpec(memory_space=pltpu.SMEM),
                  pl.BlockSpec((None, t, 2 * LANES), cur_map(lambda b, h, i: (b, i, h))),
                  pl.BlockSpec((None, S, 2 * LANES), cur_map(lambda b, h, i: (b, 0, h))),
                  pl.BlockSpec((None, S, LANES), cur_map(lambda b, h, i: (b, 0, h))),
                  pl.BlockSpec((None, 4, t, t), cur_map(lambda b, h, i: (h, 0, 0, 0))),
                  pl.BlockSpec((4, LANES // 2), lambda g: (0, 0)),
                  pl.BlockSpec((1, LANES), lambda g: (0, 0))],
        out_specs=pl.BlockSpec((None, t, LANES), prev_map),
        out_shape=jax.ShapeDtypeStruct((B, S, H * LANES), BF16),
        scratch_shapes=[pltpu.VMEM((2, n, t, t), F32),
                        pltpu.VMEM((2, t, S), BF16),
                        pltpu.VMEM((S, 2 * LANES), BF16),
                        pltpu.VMEM((2, t, LANES), F32),
                        pltpu.VMEM((2, t, 2 * LANES), F32)],
        compiler_params=_cparams(("arbitrary",)),
        name="diff_attn",
    )(rel_bias.reshape(-1), q3, k3, v3, btiles, lam4, g_subln)


def _pack_bf16_pairs(v):
    bits = lax.bitcast_convert_type(v.astype(BF16).astype(F32), jnp.uint32)
    n = bits.shape[1] // 2
    return (bits[:, :n] >> 16) | (bits[:, n:] & jnp.uint32(0xFFFF0000))


def _unpack_bf16_pairs(p):
    lo = lax.bitcast_convert_type(p << 16, F32).astype(BF16)
    hi = lax.bitcast_convert_type(p & jnp.uint32(0xFFFF0000), F32).astype(BF16)
    return lo, hi


def _out_proj_kernel(yf_ref, yd_ref, x_ref, mod_ref, wo_ref, g_ref, wr_ref, br_ref,
                     x1_ref, hp_ref, idx_ref, rank_ref, grow_ref, cnt_ref, *, tm, fw, E):
    mix =(jnp.dot(yf_ref[...], wo_ref[:fw, :], preferred_element_type=F32)
           + jnp.dot(yd_ref[...], wo_ref[fw:, :], preferred_element_type=F32))
    x1 = x_ref[...] + mod_ref[2:3, :] * mix
    x1_ref[...] = x1
    ms = jnp.mean(x1 * x1, axis=-1, keepdims=True)
    h = x1 * lax.rsqrt(ms + EPS) * g_ref[...]
    h = h * (1.0 + mod_ref[4:5, :]) + mod_ref[3:4, :]
    hb = h.astype(BF16)
    hp_ref[...] = _pack_bf16_pairs(hb)

    logits = lax.dot_general(wr_ref[...], hb, (((1,), (1,)), ((), ())),
                             preferred_element_type=F32) + br_ref[...]
    rowid = lax.broadcasted_iota(jnp.int32, (E, tm), 0)
    vals, idxs = [], []
    comb = jnp.zeros((E, tm), F32)
    l = logits
    for _ in range(TOP_K):
        mv = jnp.max(l, axis=0, keepdims=True)
        ix = jnp.min(jnp.where(l == mv, rowid, E), axis=0, keepdims=True)
        sel = rowid == ix
        vals.append(mv)
        idxs.append(ix)
        comb = comb + sel.astype(F32)
        l = jnp.where(sel, -jnp.inf, l)
    es = [jnp.exp(v - vals[0]) for v in vals]
    den = es[0] + es[1] + es[2] + es[3]
    gates = [e / den for e in es]

    r_i = lax.broadcasted_iota(jnp.int32, (tm, tm), 0)
    c_i = lax.broadcasted_iota(jnp.int32, (tm, tm), 1)
    upper = (r_i < c_i).astype(BF16)
    before = jnp.dot(comb.astype(BF16), upper, preferred_element_type=F32)
    ranks = [jnp.sum(jnp.where(rowid == ix, before, 0.0), axis=0, keepdims=True) for ix in idxs]
    cnt = jnp.sum(comb, axis=1, keepdims=True)
    cnt_ref[...] = jnp.broadcast_to(cnt, cnt_ref.shape).astype(jnp.int32)

    pad_i = jnp.zeros((8 - TOP_K, tm), jnp.int32)
    idx_ref[...] = jnp.concatenate(idxs + [pad_i], axis=0)
    rank_ref[...] = jnp.concatenate([r.astype(jnp.int32) for r in ranks] + [pad_i], axis=0)
    gpad = jnp.concatenate(gates + [jnp.zeros((LANES - TOP_K, tm), F32)], axis=0)
    grow_ref[...] = gpad.T


def _out_proj(yf2, yd2, x2, mod3, w_out_b, g_ffn, wr_t, br_t, S, tm):
    T, D = x2.shape
    fw = yf2.shape[1]
    E = wr_t.shape[0]
    nt = T // tm
    tiles_per_batch = S // tm
    kern = functools.partial(_out_proj_kernel, tm=tm, fw=fw, E=E)
    return pl.pallas_call(
        kern,
        grid=(nt,),
        in_specs=[pl.BlockSpec((tm, fw), lambda i: (i, 0)),
                  pl.BlockSpec((tm, fw), lambda i: (i, 0)),
                  pl.BlockSpec((tm, D), lambda i: (i, 0)),
                  pl.BlockSpec((None, 6, D), lambda i: (i // tiles_per_batch, 0, 0)),
                  pl.BlockSpec((D, D), lambda i: (0, 0)),
                  pl.BlockSpec((1, D), lambda i: (0, 0)),
                  pl.BlockSpec((E, D), lambda i: (0, 0)),
                  pl.BlockSpec((E, 1), lambda i: (0, 0))],
        out_specs=[pl.BlockSpec((tm, D), lambda i: (i, 0)),
                   pl.BlockSpec((tm, D // 2), lambda i: (i, 0)),
                   pl.BlockSpec((8, tm), lambda i: (0, i)),
                   pl.BlockSpec((8, tm), lambda i: (0, i)),
                   pl.BlockSpec((tm, LANES), lambda i: (i, 0)),
                   pl.BlockSpec((None, E, LANES), lambda i: (i, 0, 0))],
        out_shape=[jax.ShapeDtypeStruct((T, D), F32),
                   jax.ShapeDtypeStruct((T, D // 2), jnp.uint32),
                   jax.ShapeDtypeStruct((8, T), jnp.int32),
                   jax.ShapeDtypeStruct((8, T), jnp.int32),
                   jax.ShapeDtypeStruct((T, LANES), F32),
                   jax.ShapeDtypeStruct((nt, E, LANES), jnp.int32)],
        compiler_params=_cparams(("arbitrary",)),
        name="out_proj_router",
    )(yf2, yd2, x2, mod3, w_out_b, g_ffn, wr_t, br_t)


ROW_CHUNK = 8


def _wait_chunks(n, make_copy, max_rows):
    bit = 0
    while (ROW_CHUNK << bit) <= max_rows:
        rows = ROW_CHUNK << bit

        @pl.when(((n >> bit) & 1) == 1)
        def _():
            make_copy(rows).wait()
        bit += 1


def _dispatch_kernel(tbl_ref, nch_ref, zb_ref, lp_ref, hp_ref, xs_ref, buf, zero_sc, sem, zsem,
                     *, tm, bm, NR, nz, nt):
    i = pl.program_id(0)
    s = i % 2

    @pl.when(i == 0)
    def _():
        zero_sc[...] = jnp.zeros(zero_sc.shape, jnp.uint32)
        for e in range(nz):
            @pl.when(zb_ref[e] >= 0)
            def _():
                pltpu.make_async_copy(zero_sc, xs_ref.at[pl.ds(zb_ref[e] * bm, bm)], zsem).start()
        for e in range(nz):
            @pl.when(zb_ref[e] >= 0)
            def _():
                pltpu.make_async_copy(zero_sc, xs_ref.at[pl.ds(zb_ref[e] * bm, bm)], zsem).wait()

    lp = lp_ref[...]
    rows = lax.broadcasted_iota(jnp.int32, (NR, tm), 0)
    hit = rows == lp[0:1, :]
    for k in range(1, TOP_K):
        hit = hit | (rows == lp[k:k + 1, :])
    sel = jnp.where(hit, 1.0, 0.0).astype(BF16)
    lo, hi = _unpack_bf16_pairs(hp_ref[...])
    slo = jnp.dot(sel, lo, preferred_element_type=F32)
    shi = jnp.dot(sel, hi, preferred_element_type=F32)
    buf[s] = ((lax.bitcast_convert_type(slo, jnp.uint32) >> 16)
              | (lax.bitcast_convert_type(shi, jnp.uint32) & jnp.uint32(0xFFFF0000)))

    n = nch_ref[i]

    def issue(c, carry):
        d = pl.multiple_of(tbl_ref[0, 0, c], ROW_CHUNK)
        r = pl.multiple_of(c * ROW_CHUNK, ROW_CHUNK)
        pltpu.make_async_copy(buf.at[s, pl.ds(r, ROW_CHUNK)], xs_ref.at[pl.ds(d, ROW_CHUNK)],
                              sem.at[s]).start()
        return carry

    lax.fori_loop(0, n, issue, 0)

    def waiter(slot):
        return lambda nrows: pltpu.make_async_copy(buf.at[slot, pl.ds(0, nrows)],
                                                   xs_ref.at[pl.ds(0, nrows)], sem.at[slot])

    @pl.when(i > 0)
    def _():
        _wait_chunks(nch_ref[jnp.maximum(i - 1, 0)], waiter(1 - s), NR)

    @pl.when(i == nt - 1)
    def _():
        _wait_chunks(n, waiter(s), NR)


def _dispatch(tbl3, nch, zero_blk, lp8, hp, P, bm, tm, NR):
    T, Wd = hp.shape
    nt = T // tm
    nz = zero_blk.shape[0]
    nch_tbl = tbl3.shape[2]
    kern = functools.partial(_dispatch_kernel, tm=tm, bm=bm, NR=NR, nz=nz, nt=nt)
    return pl.pallas_call(
        kern,
        grid=(nt,),
        in_specs=[pl.BlockSpec((1, 1, nch_tbl), lambda i: (i, 0, 0), memory_space=pltpu.SMEM),
                  pl.BlockSpec(memory_space=pltpu.SMEM),
                  pl.BlockSpec(memory_space=pltpu.SMEM),
                  pl.BlockSpec((8, tm), lambda i: (0, i)),
                  pl.BlockSpec((tm, Wd), lambda i: (i, 0))],
        out_specs=pl.BlockSpec(memory_space=pl.ANY),
        out_shape=jax.ShapeDtypeStruct((P, Wd), jnp.uint32),
        scratch_shapes=[pltpu.VMEM((2, NR, Wd), jnp.uint32),
                        pltpu.VMEM((bm, Wd), jnp.uint32),
                        pltpu.SemaphoreType.DMA((2,)),
                        pltpu.SemaphoreType.DMA],
        compiler_params=_cparams(("arbitrary",)),
        name="moe_dispatch",
    )(tbl3, nch, zero_blk, lp8, hp)


def _expert_kernel(be_ref, nu_ref, h2_ref, x_ref, w1_ref, b1g_ref, b1l_ref, w2_ref, b2_ref, pm_ref, o_ref,
                   w1g_sc, w1l_sc, w2_sc):
    blk = pl.program_id(0)
    nused = nu_ref[0]
    e = be_ref[blk]
    prev = be_ref[jnp.maximum(blk - 1, 0)]
    active = blk < nused
    second = active & (h2_ref[blk] != 0)
    half = x_ref.shape[0] // 2
    tile = 2 * LANES
    n_tiles = w1_ref.shape[1] // tile

    @pl.when(active & ((blk == 0) | (e != prev)))
    def _():
        for c in range(n_tiles):
            wt = w1_ref[:, c * tile:(c + 1) * tile].astype(BF16)
            pw = jnp.dot(wt, pm_ref[...], preferred_element_type=F32)
            w1g_sc[:, c * LANES:(c + 1) * LANES] = pw[:, :LANES].astype(BF16)
            w1l_sc[:, c * LANES:(c + 1) * LANES] = pw[:, LANES:].astype(BF16)
        w2_sc[...] = w2_ref[...].astype(BF16)

    @pl.when(jnp.logical_not(active))
    def _():
        o_ref[...] = jnp.zeros(o_ref.shape, jnp.uint32)

    def mlp(r0):
        xp = x_ref[r0:r0 + half, :]
        hw = xp.shape[1]
        xlo, xhi = _unpack_bf16_pairs(xp)
        zg = (jnp.dot(xlo, w1g_sc[:hw, :], preferred_element_type=F32)
              + jnp.dot(xhi, w1g_sc[hw:, :], preferred_element_type=F32) + b1g_ref[...])
        zl = (jnp.dot(xlo, w1l_sc[:hw, :], preferred_element_type=F32)
              + jnp.dot(xhi, w1l_sc[hw:, :], preferred_element_type=F32) + b1l_ref[...])
        g = jnp.minimum(zg, SWIGLU_LIMIT)
        lin = jnp.clip(zl, -SWIGLU_LIMIT, SWIGLU_LIMIT)
        act = g * jax.nn.sigmoid(SWIGLU_ALPHA * g) * (lin + 1.0)
        y = jnp.dot(act.astype(BF16), w2_sc[...], preferred_element_type=F32) + b2_ref[...]
        o_ref[r0:r0 + half, :] = _pack_bf16_pairs(y)

    @pl.when(active)
    def _():
        mlp(0)

    @pl.when(second)
    def _():
        mlp(half)

    @pl.when(active & jnp.logical_not(second))
    def _():
        o_ref[half:, :] = jnp.zeros((half, o_ref.shape[1]), jnp.uint32)


def _experts(blk_expert, nused, half2, xs, w1, b1g, b1l, w2, b2, bm):
    P, Wd = xs.shape
    E, D, F2 = w1.shape
    F = F2 // 2
    nblk = P // bm
    tile = 2 * LANES
    pm = np.zeros((tile, tile), np.float32)
    pm[2 * np.arange(LANES), np.arange(LANES)] = 1.0
    pm[2 * np.arange(LANES) + 1, LANES + np.arange(LANES)] = 1.0
    pm = jnp.asarray(pm, BF16)

    def row_map(b, be, nu, h2):
        return (jnp.minimum(b, nu[0] - 1), 0)

    def exp_map(b, be, nu, h2):
        return (be[b], 0, 0)

    grid_spec = pltpu.PrefetchScalarGridSpec(
        num_scalar_prefetch=3,
        grid=(nblk,),
        in_specs=[pl.BlockSpec((bm, Wd), row_map),
                  pl.BlockSpec((None, D, F2), exp_map),
                  pl.BlockSpec((None, 1, F), exp_map),
                  pl.BlockSpec((None, 1, F), exp_map),
                  pl.BlockSpec((None, F, D), exp_map),
                  pl.BlockSpec((None, 1, D), exp_map),
                  pl.BlockSpec((tile, tile), lambda b, be, nu, h2: (0, 0))],
        out_specs=pl.BlockSpec((bm, D // 2), lambda b, be, nu, h2: (b, 0)),
        scratch_shapes=[pltpu.VMEM((D, F), BF16),
                        pltpu.VMEM((D, F), BF16),
                        pltpu.VMEM((F, D), BF16)],
    )
    return pl.pallas_call(
        _expert_kernel,
        grid_spec=grid_spec,
        out_shape=jax.ShapeDtypeStruct((P, D // 2), jnp.uint32),
        compiler_params=_cparams(("arbitrary",)),
        name="moe_experts",
    )(blk_expert, nused, half2, xs, w1, b1g, b1l, w2, b2, pm)


def _combine_kernel(tcur_ref, tnxt_ref, nch_ref, lp_ref, grow_ref, x1_ref, mod_ref, y_ref, o_ref,
                    buf, sem, *, tm, NR, nt):
    i = pl.program_id(0)
    s = i % 2

    def issue(tref, n, slot):
        def body(c, carry):
            d = pl.multiple_of(tref[0, 0, c], ROW_CHUNK)
            r = pl.multiple_of(c * ROW_CHUNK, ROW_CHUNK)
            pltpu.make_async_copy(y_ref.at[pl.ds(d, ROW_CHUNK)], buf.at[slot, pl.ds(r, ROW_CHUNK)],
                                  sem.at[slot]).start()
            return carry
        lax.fori_loop(0, n, body, 0)

    @pl.when(i == 0)
    def _():
        buf[...] = jnp.zeros(buf.shape, jnp.uint32)
        issue(tcur_ref, nch_ref[0], 0)

    @pl.when(i + 1 < nt)
    def _():
        issue(tnxt_ref, nch_ref[jnp.minimum(i + 1, nt - 1)], 1 - s)

    _wait_chunks(nch_ref[i],
                 lambda nrows: pltpu.make_async_copy(y_ref.at[pl.ds(0, nrows)],
                                                     buf.at[s, pl.ds(0, nrows)], sem.at[s]), NR)

    ylo, yhi = _unpack_bf16_pairs(buf[s])
    lpf = jnp.concatenate([lp_ref[...].astype(F32), jnp.zeros((LANES - 8, tm), F32)], axis=0)
    lpr = lpf.T.astype(jnp.int32)
    gr = grow_ref[...]
    cols = lax.broadcasted_iota(jnp.int32, (tm, NR), 1)
    gmat = jnp.where(cols == lpr[:, 0:1], gr[:, 0:1], 0.0)
    for k in range(1, TOP_K):
        gmat = gmat + jnp.where(cols == lpr[:, k:k + 1], gr[:, k:k + 1], 0.0)
    gmat = gmat.astype(BF16)
    hw = ylo.shape[1]
    o_ref[:, :hw] = x1_ref[:, :hw] + mod_ref[5:6, :hw] * jnp.dot(gmat, ylo, preferred_element_type=F32)
    o_ref[:, hw:] = x1_ref[:, hw:] + mod_ref[5:6, hw:] * jnp.dot(gmat, yhi, preferred_element_type=F32)


def _combine(tbl3, nch, lp8, grow, x1, mod3, y, S, tm, NR):
    T, D = x1.shape
    nt = T // tm
    tiles_per_batch = S // tm
    nch_tbl = tbl3.shape[2]
    kern = functools.partial(_combine_kernel, tm=tm, NR=NR, nt=nt)
    return pl.pallas_call(
        kern,
        grid=(nt,),
        in_specs=[pl.BlockSpec((1, 1, nch_tbl), lambda i: (i, 0, 0), memory_space=pltpu.SMEM),
                  pl.BlockSpec((1, 1, nch_tbl), lambda i: (jnp.minimum(i + 1, nt - 1), 0, 0),
                               memory_space=pltpu.SMEM),
                  pl.BlockSpec(memory_space=pltpu.SMEM),
                  pl.BlockSpec((8, tm), lambda i: (0, i)),
                  pl.BlockSpec((tm, LANES), lambda i: (i, 0)),
                  pl.BlockSpec((tm, D), lambda i: (i, 0)),
                  pl.BlockSpec((None, 6, D), lambda i: (i // tiles_per_batch, 0, 0)),
                  pl.BlockSpec(memory_space=pl.ANY)],
        out_specs=pl.BlockSpec((tm, D), lambda i: (i, 0)),
        out_shape=jax.ShapeDtypeStruct((T, D), F32),
        scratch_shapes=[pltpu.VMEM((2, NR, D // 2), jnp.uint32),
                        pltpu.SemaphoreType.DMA((2,))],
        compiler_params=_cparams(("arbitrary",)),
        name="moe_combine",
    )(tbl3, tbl3, nch, lp8, grow, x1, mod3, y)


def _tiles(S):
    t_attn = min(512, S // 4) if S >= 512 else S
    tm = min(512, t_attn)
    return dict(tm=tm, t_attn=t_attn, bm=1024)


def _moe_layout(cnt, idx_t, lrank_t, tm, bm):
    nt, E = cnt.shape
    T = idx_t.shape[1]
    c8 = (cnt + ROW_CHUNK - 1) // ROW_CHUNK * ROW_CHUNK
    tile_end = jnp.cumsum(c8, axis=1)
    tile_off = tile_end - c8
    used = tile_end[:, -1]
    off = jnp.cumsum(c8, axis=0) - c8
    tot = jnp.sum(c8, axis=0)
    padded = (tot + bm - 1) // bm * bm
    pends = jnp.cumsum(padded)
    pstarts = pends - padded
    NR = TOP_K * tm + ROW_CHUNK * E
    nch_tbl = NR // ROW_CHUNK
    P = (T * TOP_K + nt * E * (ROW_CHUNK - 1) + E * bm + bm - 1) // bm * bm
    nblk = P // bm

    eids = jnp.arange(E, dtype=jnp.int32)
    tok_off = jnp.repeat(tile_off.T, tm, axis=1)
    lp = jnp.sum(jnp.where(idx_t[None, :TOP_K] == eids[:, None, None], tok_off[:, None, :], 0), axis=0) \
        + lrank_t[:TOP_K]
    lp8 = jnp.concatenate([lp, jnp.full((8 - TOP_K, T), -1, jnp.int32)], axis=0).astype(jnp.int32)

    c_row = jnp.arange(nch_tbl, dtype=jnp.int32) * ROW_CHUNK
    inrun = (tile_off[:, :, None] <= c_row) & (c_row < tile_end[:, :, None])
    base = pstarts[None, :] + off - tile_off
    tbl = jnp.sum(jnp.where(inrun, base[:, :, None], 0), axis=1) + c_row[None, :]
    tbl = jnp.where(c_row[None, :] < used[:, None], tbl, 0).astype(jnp.int32)
    nch = (used // ROW_CHUNK).astype(jnp.int32)

    nused = (pends[-1] // bm).astype(jnp.int32).reshape(1)
    blk_start = jnp.arange(nblk, dtype=jnp.int32) * bm
    blk_expert = jnp.minimum(jnp.sum(pends[None, :] <= blk_start[:, None], axis=1), E - 1).astype(jnp.int32)
    half2 = (blk_start + bm // 2 < (pstarts + tot)[blk_expert]).astype(jnp.int32)
    last_blk = jnp.where(padded > 0, pends // bm - 1, -1)
    n_tail = nblk - (T * TOP_K) // bm
    tail_blk = nused[0] + jnp.arange(n_tail, dtype=jnp.int32)
    tail_blk = jnp.where(tail_blk < nblk, tail_blk, -1)
    zero_blk = jnp.concatenate([last_blk, tail_blk]).astype(jnp.int32)
    return dict(lp8=lp8, tbl3=tbl.reshape(nt, 1, nch_tbl), nch=nch, nused=nused, blk_expert=blk_expert,
                half2=half2, zero_blk=zero_blk, P=P, NR=NR)


def kernel(x, c, w_ada, b_ada, g_mix, w_in, w_fourier, q_norm_g, k_norm_g, lambda_q1, lambda_k1,
           lambda_q2, lambda_k2, g_subln, w_out, rel_bias, g_ffn, w_router, b_router, w1, b1, w2, b2):
    B, S, D = x.shape
    T = B * S
    L = w_ada.shape[0]
    E = w_router.shape[-1]
    cfg = _tiles(S)
    tm, t_attn, bm = cfg["tm"], cfg["t_attn"], cfg["bm"]
    dqk = q_norm_g.shape[-1]
    scale = dqk ** -0.5
    fw = w_in.shape[-1] // 4

    btiles = _bias_tiles(rel_bias, t_attn)
    x2 = x.reshape(T, D)
    for l in range(L):
        lam_init = 0.8 - 0.6 * math.exp(-0.3 * l)
        mod3 = _ada(c, w_ada[l], b_ada[l]).reshape(B, 6, D)
        gq = (jnp.tile(q_norm_g[l], 4) * (scale * LOG2E)).reshape(1, 2 * LANES)
        gk = jnp.tile(k_norm_g[l], 4).reshape(1, 2 * LANES)
        uf, qx, kx, v = _in_proj(x2, mod3, g_mix[l].reshape(1, D), w_in[l].astype(BF16), gq, gk,
                                 S, tm, t_attn)
        yf = _fourier(uf.reshape(B, S, fw), w_fourier[l]).reshape(T, fw)
        lam4 = jnp.stack([lambda_q1[l], lambda_k1[l], lambda_q2[l], lambda_k2[l]]).astype(F32)
        yd = _attention(rel_bias, qx.reshape(B, S, -1), kx.reshape(B, S, -1), v.reshape(B, S, fw),
                        btiles, lam4, g_subln[l].reshape(1, LANES), t_attn, lam_init).reshape(T, fw)
        x1, hp, idx_t, lrank_t, grow, cnt3 = _out_proj(
            yf, yd, x2, mod3, w_out[l].astype(BF16), g_ffn[l].reshape(1, D),
            w_router[l].T.astype(BF16), b_router[l].reshape(E, 1), S, tm)
        lay = _moe_layout(cnt3[:, :, 0], idx_t, lrank_t, tm, bm)
        xs = _dispatch(lay["tbl3"], lay["nch"], lay["zero_blk"], lay["lp8"], hp, lay["P"], bm, tm, lay["NR"])
        y = _experts(lay["blk_expert"], lay["nused"], lay["half2"], xs, w1[l], b1[l][:, None, 0::2],
                     b1[l][:, None, 1::2], w2[l], b2[l][:, None, :], bm)
        x2 = _combine(lay["tbl3"], lay["nch"], lay["lp8"], grow, x1, mod3, y, S, tm, lay["NR"])
    return x2.reshape(B, S, D)
```

```python
import functools
import math

import numpy as np
import jax
import jax.numpy as jnp
from jax import lax
from jax.experimental import pallas as pl
from jax.experimental.pallas import tpu as pltpu

F32 = jnp.float32
BF16 = jnp.bfloat16

EPS = 1e-6
N_FOURIER_GROUPS = 4
N_DIFF_HEADS = 4
TOP_K = 4
N_BUCKETS = 32
MAX_DISTANCE = 128
SWIGLU_ALPHA = 1.702
SWIGLU_LIMIT = 7.0
LANES = 128
DFT_RADIX = 4
LOG2E = 1.4426950408889634
VMEM_LIMIT = 56 * 1024 * 1024


def _cparams(sem):
    return pltpu.CompilerParams(dimension_semantics=sem, vmem_limit_bytes=VMEM_LIMIT)


def _ada_kernel(c_ref, w_ref, b_ref, o_ref):
    c = c_ref[...]
    cond = c * jax.nn.sigmoid(c)
    o_ref[...] = jnp.dot(cond, w_ref[...], preferred_element_type=F32,
                         precision=lax.Precision.HIGHEST) + b_ref[...]


def _ada(c, w, b):
    B, D = c.shape
    N = w.shape[1]
    tn = 1536 if N % 1536 == 0 else N
    return pl.pallas_call(
        _ada_kernel,
        grid=(N // tn,),
        in_specs=[pl.BlockSpec((B, D), lambda j: (0, 0)),
                  pl.BlockSpec((D, tn), lambda j: (0, j)),
                  pl.BlockSpec((1, tn), lambda j: (0, j))],
        out_specs=pl.BlockSpec((B, tn), lambda j: (0, j)),
        out_shape=jax.ShapeDtypeStruct((B, N), F32),
        compiler_params=_cparams(("arbitrary",)),
        name="ada",
    )(c, w, b.reshape(1, N))


def _group_rms_inv(xh, gmat, group):
    ss = jnp.dot((xh * xh).astype(BF16), gmat, preferred_element_type=F32)
    return lax.rsqrt(ss * (1.0 / group) + EPS)


def _in_proj_kernel(x_ref, mod_ref, g_ref, w_ref, gq_ref, gk_ref, gmat_ref,
                    uf_ref, q_ref, k_ref, v_ref, *, tiles_per_batch, tm, t_attn, n_chunks, fw, qw):
    i = pl.program_id(0)
    x = x_ref[...]
    ms = jnp.mean(x * x, axis=-1, keepdims=True)
    y = x * lax.rsqrt(ms + EPS) * g_ref[...]
    h = y * (1.0 + mod_ref[1:2, :]) + mod_ref[0:1, :]
    proj = jnp.dot(h.astype(BF16), w_ref[...], preferred_element_type=F32)
    uf_ref[...] = proj[:, :fw].astype(BF16)
    v_ref[...] = proj[:, fw + 2 * qw:].astype(BF16)

    lane = lax.broadcasted_iota(jnp.int32, (1, LANES), 1)
    chunk = ((i % tiles_per_batch) * tm) // t_attn
    gmat = gmat_ref[...]
    half = LANES // 2
    pair = 2 * LANES
    for hd in range(N_DIFF_HEADS):
        if hd % 2 == 0:
            q2 = proj[:, fw + hd * LANES: fw + hd * LANES + pair]
            k2 = proj[:, fw + qw + hd * LANES: fw + qw + hd * LANES + pair]
            qn2 = (q2 * _group_rms_inv(q2, gmat, half) * gq_ref[...]).astype(BF16)
            kn2 = (k2 * _group_rms_inv(k2, gmat, half) * gk_ref[...]).astype(BF16)
        qn = qn2[:, (hd % 2) * LANES:(hd % 2 + 1) * LANES]
        kn = kn2[:, (hd % 2) * LANES:(hd % 2 + 1) * LANES]
        for m in range(2):
            fl = lane - half if m == 0 else lane
            is_data = (lane < half) if m == 0 else (lane >= half)
            f = jnp.where(fl < n_chunks, fl, fl - n_chunks)
            onehot = ((fl >= 0) & (fl < 2 * n_chunks) & (f == chunk)).astype(BF16)
            c0 = (2 * hd + m) * LANES
            q_ref[:, c0:c0 + LANES] = jnp.where(is_data, qn, jnp.zeros_like(qn))
            k_ref[:, c0:c0 + LANES] = jnp.where(is_data, kn, onehot)


def _in_proj(x2, mod3, g_mix, w_in_b, gq, gk, S, tm, t_attn):
    T, D = x2.shape
    ncols = w_in_b.shape[1]
    fw = ncols // 4
    qw = fw
    tiles_per_batch = S // tm
    n_chunks = S // t_attn
    half = LANES // 2
    pair = 2 * LANES
    gmat = (np.arange(pair)[:, None] // half == np.arange(pair)[None, :] // half)
    gmat = jnp.asarray(gmat, BF16)
    kern = functools.partial(_in_proj_kernel, tiles_per_batch=tiles_per_batch, tm=tm, t_attn=t_attn,
                             n_chunks=n_chunks, fw=fw, qw=qw)
    ext = 2 * qw
    return pl.pallas_call(
        kern,
        grid=(T // tm,),
        in_specs=[pl.BlockSpec((tm, D), lambda i: (i, 0)),
                  pl.BlockSpec((None, 6, D), lambda i: (i // tiles_per_batch, 0, 0)),
                  pl.BlockSpec((1, D), lambda i: (0, 0)),
                  pl.BlockSpec((D, ncols), lambda i: (0, 0)),
                  pl.BlockSpec((1, pair), lambda i: (0, 0)),
                  pl.BlockSpec((1, pair), lambda i: (0, 0)),
                  pl.BlockSpec((pair, pair), lambda i: (0, 0))],
        out_specs=[pl.BlockSpec((tm, fw), lambda i: (i, 0)),
                   pl.BlockSpec((tm, ext), lambda i: (i, 0)),
                   pl.BlockSpec((tm, ext), lambda i: (i, 0)),
                   pl.BlockSpec((tm, fw), lambda i: (i, 0))],
        out_shape=[jax.ShapeDtypeStruct((T, fw), BF16),
                   jax.ShapeDtypeStruct((T, ext), BF16),
                   jax.ShapeDtypeStruct((T, ext), BF16),
                   jax.ShapeDtypeStruct((T, fw), BF16)],
        compiler_params=_cparams(("arbitrary",)),
        name="in_proj",
    )(x2, mod3, g_mix, w_in_b, gq, gk, gmat)


def _fourier_kernel(uf_ref, wcs_ref, tw_ref, dft_ref, o_ref, tt_sc, *, M, gw):
    c = pl.program_id(1)

    @pl.when(c == 0)
    def _():
        for g in range(N_FOURIER_GROUPS):
            zr, zi = [], []
            for j in range(DFT_RADIX):
                ab = jnp.dot(uf_ref[j * M:(j + 1) * M, g * gw:(g + 1) * gw], wcs_ref[g],
                             preferred_element_type=F32)
                zr.append(ab[:, :gw])
                zi.append(-ab[:, gw:])
            yr = [zr[0] + zr[1] + zr[2] + zr[3],
                  zr[0] + zi[1] - zr[2] - zi[3],
                  zr[0] - zr[1] + zr[2] - zr[3],
                  zr[0] - zi[1] - zr[2] + zi[3]]
            yi = [zi[0] + zi[1] + zi[2] + zi[3],
                  zi[0] - zr[1] - zi[2] + zr[3],
                  zi[0] - zi[1] + zi[2] - zi[3],
                  zi[0] + zr[1] - zi[2] - zr[3]]
            for k2 in range(DFT_RADIX):
                cs = tw_ref[:, k2:k2 + 1]
                sn = tw_ref[:, DFT_RADIX + k2:DFT_RADIX + k2 + 1]
                tr = cs * yr[k2] + sn * yi[k2]
                ti = cs * yi[k2] - sn * yr[k2]
                tt_sc[k2, 0:M, g * gw:(g + 1) * gw] = tr.astype(BF16)
                tt_sc[k2, M:2 * M, g * gw:(g + 1) * gw] = ti.astype(BF16)

    o_ref[...] = jnp.dot(dft_ref[...], tt_sc[c], preferred_element_type=F32).astype(BF16)


def _fourier(uf3, w_f):
    B, S, W = uf3.shape
    G = N_FOURIER_GROUPS
    gw = W // G
    M = S // DFT_RADIX
    norm = 1.0 / math.sqrt(S * gw)
    cidx = (np.arange(gw)[:, None] * np.arange(gw)[None, :]) % gw
    ang_c = jnp.asarray(cidx, F32) * (2.0 * math.pi / gw)
    cc, sc = jnp.cos(ang_c) * norm, jnp.sin(ang_c) * norm
    hp = lax.Precision.HIGHEST
    wcs = jnp.concatenate([jnp.einsum("ab,gbd->gad", cc, w_f, precision=hp),
                           jnp.einsum("ab,gbd->gad", sc, w_f, precision=hp)], axis=-1).astype(BF16)
    n1 = np.arange(M)
    tw_idx = (np.arange(DFT_RADIX)[None, :] * n1[:, None]) % S
    ang_t = jnp.asarray(tw_idx, F32) * (2.0 * math.pi / S)
    tw = jnp.concatenate([jnp.cos(ang_t), jnp.sin(ang_t),
                          jnp.zeros((M, LANES - 2 * DFT_RADIX), F32)], axis=1)
    pidx = (n1[:, None] * n1[None, :]) % M
    ang_p = jnp.asarray(pidx, F32) * (2.0 * math.pi / M)
    dft = jnp.concatenate([jnp.cos(ang_p), jnp.sin(ang_p)], axis=1).astype(BF16)
    kern = functools.partial(_fourier_kernel, M=M, gw=gw)
    return pl.pallas_call(
        kern,
        grid=(B, DFT_RADIX),
        in_specs=[pl.BlockSpec((None, S, W), lambda b, c: (b, 0, 0)),
                  pl.BlockSpec((G, gw, 2 * gw), lambda b, c: (0, 0, 0)),
                  pl.BlockSpec((M, LANES), lambda b, c: (0, 0)),
                  pl.BlockSpec((M, 2 * M), lambda b, c: (0, 0))],
        out_specs=pl.BlockSpec((None, M, W), lambda b, c: (b, 0, c)),
        out_shape=jax.ShapeDtypeStruct((B, M, DFT_RADIX * W), BF16),
        scratch_shapes=[pltpu.VMEM((DFT_RADIX, 2 * M, W), BF16)],
        compiler_params=_cparams(("arbitrary", "arbitrary")),
        name="fourier",
    )(uf3, wcs, tw, dft)


def _bias_kernel(rb_ref, o_ref, *, t):
    h = pl.program_id(0)
    which = pl.program_id(1)
    nb = N_BUCKETS // 2

    def tile(d, buckets):
        row = lax.broadcasted_iota(jnp.int32, (t, t), 0)
        col = lax.broadcasted_iota(jnp.int32, (t, t), 1)
        rel = d * t + col - row
        max_exact = nb // 2
        ret = jnp.where(rel > 0, nb, 0)
        n = jnp.abs(rel)
        nf = jnp.maximum(n, 1).astype(F32)
        large = max_exact + (jnp.log(nf / max_exact) / math.log(MAX_DISTANCE / max_exact)
                             * (nb - max_exact)).astype(jnp.int32)
        large = jnp.minimum(large, nb - 1)
        bucket = ret + jnp.where(n < max_exact, n, large)
        val = jnp.zeros((t, t), F32)
        for j in buckets:
            val = jnp.where(bucket == j, rb_ref[j * N_DIFF_HEADS + h], val)
        return val * LOG2E

    @pl.when(which == 0)
    def _():
        o_ref[...] = tile(-1, range(0, nb))

    @pl.when(which == 1)
    def _():
        o_ref[...] = tile(0, range(0, 2 * nb))

    @pl.when(which == 2)
    def _():
        o_ref[...] = tile(1, range(nb, 2 * nb))

    @pl.when(which == 3)
    def _():
        o_ref[...] = jnp.zeros((t, t), F32)


def _bias_tiles(rel_bias, t):
    H = N_DIFF_HEADS
    assert t >= MAX_DISTANCE, "far key tiles must lie entirely in the saturated bucket"
    return pl.pallas_call(
        functools.partial(_bias_kernel, t=t),
        grid=(H, 4),
        in_specs=[pl.BlockSpec(memory_space=pltpu.SMEM)],
        out_specs=pl.BlockSpec((None, None, t, t), lambda h, d: (h, d, 0, 0)),
        out_shape=jax.ShapeDtypeStruct((H, 4, t, t), F32),
        compiler_params=_cparams(("arbitrary", "arbitrary")),
        name="t5_bias",
    )(rel_bias.reshape(-1))


def _far_bucket_consts():
    nb = N_BUCKETS // 2
    return nb - 1, 2 * nb - 1


def _attn_kernel(rb_ref, q_ref, k_ref, v_ref, b_ref, lam_ref, gs_ref, o_ref,
                 s_sc, p_sc, vext_sc, m_sc, raw_sc, *, t, n, n_tiles, lam_init):
    g = pl.program_id(0)
    cur = jnp.minimum(g, n_tiles - 1)
    h = (cur // n) % N_DIFF_HEADS
    i = cur % n
    half = LANES // 2

    @pl.when(g == 0)
    def _():
        raw_sc[...] = jnp.ones(raw_sc.shape, F32)

    @pl.when((i == 0) & (g < n_tiles))
    def _():
        vext_sc[:, :LANES] = v_ref[...]
        vext_sc[:, LANES:] = jnp.ones(v_ref.shape, BF16)

    left_b, right_b = _far_bucket_consts()
    c_left = rb_ref[left_b * N_DIFF_HEADS + h] * LOG2E
    c_right = rb_ref[right_b * N_DIFF_HEADS + h] * LOG2E
    lane = lax.broadcasted_iota(jnp.int32, (1, LANES), 1)

    def all_scores(m):
        fl = lane - half if m == 0 else lane
        is_data = (lane < half) if m == 0 else (lane >= half)
        f = jnp.where(fl < n, fl, fl - n)
        cfar = jnp.where(f < i - 1, c_left, jnp.where(f > i + 1, c_right, 0.0))
        c_hi = cfar.astype(BF16)
        c_lo = (cfar - c_hi.astype(F32)).astype(BF16)
        feat = jnp.where(fl < n, c_hi, c_lo)
        feat = jnp.where((fl >= 0) & (fl < 2 * n), feat, jnp.zeros_like(feat))
        qm = jnp.where(is_data, q_ref[:, m * LANES:(m + 1) * LANES], feat)
        for j in range(n):
            kc = k_ref[j * t:(j + 1) * t, m * LANES:(m + 1) * LANES]
            s = lax.dot_general(qm, kc, (((1,), (1,)), ((), ())), preferred_element_type=F32)
            d = j - i
            s = s + b_ref[jnp.where(jnp.abs(d) <= 1, d + 1, 3)]
            s_sc[m, j] = s
            tile_max = s[:, 0:LANES]
            for c in range(1, t // LANES):
                tile_max = jnp.maximum(tile_max, s[:, c * LANES:(c + 1) * LANES])
            if j == 0:
                m_sc[m] = tile_max
            else:
                m_sc[m] = jnp.maximum(m_sc[m], tile_max)

    def exponentials(m):
        mx = jnp.max(m_sc[m], axis=-1, keepdims=True)
        for j in range(n):
            p_sc[m, :, j * t:(j + 1) * t] = jnp.exp2(s_sc[m, j] - mx).astype(BF16)

    def finish_previous():
        r0 = raw_sc[0]
        r1 = raw_sc[1]
        lam = (jnp.exp(jnp.sum(lam_ref[0:1, :] * lam_ref[1:2, :], axis=-1, keepdims=True))
               - jnp.exp(jnp.sum(lam_ref[2:3, :] * lam_ref[3:4, :], axis=-1, keepdims=True)) + lam_init)
        o = r0[:, :LANES] / r0[:, LANES:LANES + 1] - lam * (r1[:, :LANES] / r1[:, LANES:LANES + 1])
        ms = jnp.mean(o * o, axis=-1, keepdims=True)
        o_ref[...] = (o * lax.rsqrt(ms + EPS) * gs_ref[...] * (1.0 - lam_init)).astype(BF16)

    @pl.when(i < n)
    def _():
        finish_previous()
        all_scores(0)

    @pl.when(g < n_tiles)
    def _():
        all_scores(1)
        exponentials(0)
        raw_sc[0] = jnp.dot(p_sc[0], vext_sc[...], preferred_element_type=F32)
        exponentials(1)
        raw_sc[1] = jnp.dot(p_sc[1], vext_sc[...], preferred_element_type=F32)


def _attention(rel_bias, q3, k3, v3, btiles, lam4, g_subln, t, lam_init):
    B, S, _ = q3.shape
    H = N_DIFF_HEADS
    n = S // t
    n_tiles = B * H * n
    kern = functools.partial(_attn_kernel, t=t, n=n, n_tiles=n_tiles, lam_init=lam_init)

    def tile_of(g):
        return g // (H * n), (g // n) % H, g % n

    def cur_map(fn):
        def index_map(g):
            return fn(*tile_of(jnp.minimum(g, n_tiles - 1)))
        return index_map

    def prev_map(g):
        b, h, i = tile_of(jnp.maximum(g - 1, 0))
        return (b, i, h)

    return pl.pallas_call(
        kern,
        grid=(n_tiles + 1,),
        in_specs=[pl.BlockSpec(memory_space=pltpu.SMEM),
                  pl.BlockSpec((None, t, 2 * LANES), cur_map(lambda b, h, i: (b, i, h))),
                  pl.BlockSpec((None, S, 2 * LANES), cur_map(lambda b, h, i: (b, 0, h))),
                  pl.BlockSpec((None, S, LANES), cur_map(lambda b, h, i: (b, 0, h))),
                  pl.BlockSpec((None, 4, t, t), cur_map(lambda b, h, i: (h, 0, 0, 0))),
                  pl.BlockSpec((4, LANES // 2), lambda g: (0, 0)),
                  pl.BlockSpec((1, LANES), lambda g: (0, 0))],
        out_specs=pl.BlockSpec((None, t, LANES), prev_map),
        out_shape=jax.ShapeDtypeStruct((B, S, H * LANES), BF16),
        scratch_shapes=[pltpu.VMEM((2, n, t, t), F32),
                        pltpu.VMEM((2, t, S), BF16),
                        pltpu.VMEM((S, 2 * LANES), BF16),
                        pltpu.VMEM((2, t, LANES), F32),
                        pltpu.VMEM((2, t, 2 * LANES), F32)],
        compiler_params=_cparams(("arbitrary",)),
        name="diff_attn",
    )(rel_bias.reshape(-1), q3, k3, v3, btiles, lam4, g_subln)


def _pack_bf16_pairs(v):
    bits = lax.bitcast_convert_type(v.astype(BF16).astype(F32), jnp.uint32)
    n = bits.shape[1] // 2
    return (bits[:, :n] >> 16) | (bits[:, n:] & jnp.uint32(0xFFFF0000))


def _unpack_bf16_pairs(p):
    lo = lax.bitcast_convert_type(p << 16, F32).astype(BF16)
    hi = lax.bitcast_convert_type(p & jnp.uint32(0xFFFF0000), F32).astype(BF16)
    return lo, hi


def _out_proj_kernel(yf_ref, yd_ref, x_ref, mod_ref, wo_ref, g_ref, wr_ref, br_ref,
                     x1_ref, hp_ref, idx_ref, rank_ref, grow_ref, cnt_ref, *, tm, fw, E):
    mix =(jnp.dot(yf_ref[...], wo_ref[:fw, :], preferred_element_type=F32)
           + jnp.dot(yd_ref[...], wo_ref[fw:, :], preferred_element_type=F32))
    x1 = x_ref[...] + mod_ref[2:3, :] * mix
    x1_ref[...] = x1
    ms = jnp.mean(x1 * x1, axis=-1, keepdims=True)
    h = x1 * lax.rsqrt(ms + EPS) * g_ref[...]
    h = h * (1.0 + mod_ref[4:5, :]) + mod_ref[3:4, :]
    hb = h.astype(BF16)
    hp_ref[...] = _pack_bf16_pairs(hb)

    logits = lax.dot_general(wr_ref[...], hb, (((1,), (1,)), ((), ())),
                             preferred_element_type=F32) + br_ref[...]
    rowid = lax.broadcasted_iota(jnp.int32, (E, tm), 0)
    vals, idxs = [], []
    comb = jnp.zeros((E, tm), F32)
    l = logits
    for _ in range(TOP_K):
        mv = jnp.max(l, axis=0, keepdims=True)
        ix = jnp.min(jnp.where(l == mv, rowid, E), axis=0, keepdims=True)
        sel = rowid == ix
        vals.append(mv)
        idxs.append(ix)
        comb = comb + sel.astype(F32)
        l = jnp.where(sel, -jnp.inf, l)
    es = [jnp.exp(v - vals[0]) for v in vals]
    den = es[0] + es[1] + es[2] + es[3]
    gates = [e / den for e in es]

    r_i = lax.broadcasted_iota(jnp.int32, (tm, tm), 0)
    c_i = lax.broadcasted_iota(jnp.int32, (tm, tm), 1)
    upper = (r_i < c_i).astype(BF16)
    before = jnp.dot(comb.astype(BF16), upper, preferred_element_type=F32)
    ranks = [jnp.sum(jnp.where(rowid == ix, before, 0.0), axis=0, keepdims=True) for ix in idxs]
    cnt = jnp.sum(comb, axis=1, keepdims=True)
    cnt_ref[...] = jnp.broadcast_to(cnt, cnt_ref.shape).astype(jnp.int32)

    pad_i = jnp.zeros((8 - TOP_K, tm), jnp.int32)
    idx_ref[...] = jnp.concatenate(idxs + [pad_i], axis=0)
    rank_ref[...] = jnp.concatenate([r.astype(jnp.int32) for r in ranks] + [pad_i], axis=0)
    gpad = jnp.concatenate(gates + [jnp.zeros((LANES - TOP_K, tm), F32)], axis=0)
    grow_ref[...] = gpad.T


def _out_proj(yf2, yd2, x2, mod3, w_out_b, g_ffn, wr_t, br_t, S, tm):
    T, D = x2.shape
    fw = yf2.shape[1]
    E = wr_t.shape[0]
    nt = T // tm
    tiles_per_batch = S // tm
    kern = functools.partial(_out_proj_kernel, tm=tm, fw=fw, E=E)
    return pl.pallas_call(
        kern,
        grid=(nt,),
        in_specs=[pl.BlockSpec((tm, fw), lambda i: (i, 0)),
                  pl.BlockSpec((tm, fw), lambda i: (i, 0)),
                  pl.BlockSpec((tm, D), lambda i: (i, 0)),
                  pl.BlockSpec((None, 6, D), lambda i: (i // tiles_per_batch, 0, 0)),
                  pl.BlockSpec((D, D), lambda i: (0, 0)),
                  pl.BlockSpec((1, D), lambda i: (0, 0)),
                  pl.BlockSpec((E, D), lambda i: (0, 0)),
                  pl.BlockSpec((E, 1), lambda i: (0, 0))],
        out_specs=[pl.BlockSpec((tm, D), lambda i: (i, 0)),
                   pl.BlockSpec((tm, D // 2), lambda i: (i, 0)),
                   pl.BlockSpec((8, tm), lambda i: (0, i)),
                   pl.BlockSpec((8, tm), lambda i: (0, i)),
                   pl.BlockSpec((tm, LANES), lambda i: (i, 0)),
                   pl.BlockSpec((None, E, LANES), lambda i: (i, 0, 0))],
        out_shape=[jax.ShapeDtypeStruct((T, D), F32),
                   jax.ShapeDtypeStruct((T, D // 2), jnp.uint32),
                   jax.ShapeDtypeStruct((8, T), jnp.int32),
                   jax.ShapeDtypeStruct((8, T), jnp.int32),
                   jax.ShapeDtypeStruct((T, LANES), F32),
                   jax.ShapeDtypeStruct((nt, E, LANES), jnp.int32)],
        compiler_params=_cparams(("arbitrary",)),
        name="out_proj_router",
    )(yf2, yd2, x2, mod3, w_out_b, g_ffn, wr_t, br_t)


ROW_CHUNK = 8
CHUNK_UNROLL = 8


def _dispatch_kernel(tbl_ref, zb_ref, lp_ref, hp_ref, xs_ref, buf, zero_sc, sem, zsem,
                     *, tm, bm, NR, nz, nt):
    i = pl.program_id(0)
    s = i % 2

    @pl.when(i == 0)
    def _():
        zero_sc[...] = jnp.zeros(zero_sc.shape, jnp.uint32)
        for e in range(nz):
            @pl.when(zb_ref[e] >= 0)
            def _():
                pltpu.make_async_copy(zero_sc, xs_ref.at[pl.ds(zb_ref[e] * bm, bm)], zsem).start()
        for e in range(nz):
            @pl.when(zb_ref[e] >= 0)
            def _():
                pltpu.make_async_copy(zero_sc, xs_ref.at[pl.ds(zb_ref[e] * bm, bm)], zsem).wait()

    lp = lp_ref[...]
    rows = lax.broadcasted_iota(jnp.int32, (NR, tm), 0)
    hit = rows == lp[0:1, :]
    for k in range(1, TOP_K):
        hit = hit | (rows == lp[k:k + 1, :])
    sel = jnp.where(hit, 1.0, 0.0).astype(BF16)
    lo, hi = _unpack_bf16_pairs(hp_ref[...])
    slo = jnp.dot(sel, lo, preferred_element_type=F32)
    shi = jnp.dot(sel, hi, preferred_element_type=F32)
    buf[s] = ((lax.bitcast_convert_type(slo, jnp.uint32) >> 16)
              | (lax.bitcast_convert_type(shi, jnp.uint32) & jnp.uint32(0xFFFF0000)))

    for slot in range(2):
        def issue(c, carry, slot=slot):
            d = pl.multiple_of(tbl_ref[0, 0, c], ROW_CHUNK)
            r = pl.multiple_of(c * ROW_CHUNK, ROW_CHUNK)
            pltpu.make_async_copy(buf.at[slot, pl.ds(r, ROW_CHUNK)], xs_ref.at[pl.ds(d, ROW_CHUNK)],
                                  sem.at[slot]).start()
            return carry

        @pl.when(s == slot)
        def _():
            lax.fori_loop(0, NR // ROW_CHUNK, issue, 0, unroll=CHUNK_UNROLL)

    def wait_slot(slot):
        pltpu.make_async_copy(buf.at[slot], xs_ref.at[pl.ds(0, NR)], sem.at[slot]).wait()

    @pl.when(i > 0)
    def _():
        wait_slot(1 - s)

    @pl.when(i == nt - 1)
    def _():
        wait_slot(s)


def _dispatch(tbl3, zero_blk, lp8, hp, P, bm, tm, NR):
    T, Wd = hp.shape
    nt = T // tm
    nz = zero_blk.shape[0]
    nch_tbl = tbl3.shape[2]
    kern = functools.partial(_dispatch_kernel, tm=tm, bm=bm, NR=NR, nz=nz, nt=nt)
    return pl.pallas_call(
        kern,
        grid=(nt,),
        in_specs=[pl.BlockSpec((1, 1, nch_tbl), lambda i: (i, 0, 0), memory_space=pltpu.SMEM),
                  pl.BlockSpec(memory_space=pltpu.SMEM),
                  pl.BlockSpec((8, tm), lambda i: (0, i)),
                  pl.BlockSpec((tm, Wd), lambda i: (i, 0))],
        out_specs=pl.BlockSpec(memory_space=pl.ANY),
        out_shape=jax.ShapeDtypeStruct((P, Wd), jnp.uint32),
        scratch_shapes=[pltpu.VMEM((2, NR, Wd), jnp.uint32),
                        pltpu.VMEM((bm, Wd), jnp.uint32),
                        pltpu.SemaphoreType.DMA((2,)),
                        pltpu.SemaphoreType.DMA],
        compiler_params=_cparams(("arbitrary",)),
        name="moe_dispatch",
    )(tbl3, zero_blk, lp8, hp)


def _expert_kernel(be_ref, nu_ref, h2_ref, x_ref, w1_ref, b1g_ref, b1l_ref, w2_ref, b2_ref, pm_ref, o_ref,
                   w1g_sc, w1l_sc, w2_sc):
    blk = pl.program_id(0)
    nused = nu_ref[0]
    e = be_ref[blk]
    prev = be_ref[jnp.maximum(blk - 1, 0)]
    active = blk < nused
    second = active & (h2_ref[blk] != 0)
    half = x_ref.shape[0] // 2
    tile = 2 * LANES
    n_tiles = w1_ref.shape[1] // tile

    @pl.when(active & ((blk == 0) | (e != prev)))
    def _():
        for c in range(n_tiles):
            wt = w1_ref[:, c * tile:(c + 1) * tile].astype(BF16)
            pw = jnp.dot(wt, pm_ref[...], preferred_element_type=F32)
            w1g_sc[:, c * LANES:(c + 1) * LANES] = pw[:, :LANES].astype(BF16)
            w1l_sc[:, c * LANES:(c + 1) * LANES] = pw[:, LANES:].astype(BF16)
        w2_sc[...] = w2_ref[...].astype(BF16)

    @pl.when(jnp.logical_not(active))
    def _():
        o_ref[...] = jnp.zeros(o_ref.shape, jnp.uint32)

    def mlp(r0):
        xp = x_ref[r0:r0 + half, :]
        hw = xp.shape[1]
        xlo, xhi = _unpack_bf16_pairs(xp)
        zg = (jnp.dot(xlo, w1g_sc[:hw, :], preferred_element_type=F32)
              + jnp.dot(xhi, w1g_sc[hw:, :], preferred_element_type=F32) + b1g_ref[...])
        zl = (jnp.dot(xlo, w1l_sc[:hw, :], preferred_element_type=F32)
              + jnp.dot(xhi, w1l_sc[hw:, :], preferred_element_type=F32) + b1l_ref[...])
        g = jnp.minimum(zg, SWIGLU_LIMIT)
        lin = jnp.clip(zl, -SWIGLU_LIMIT, SWIGLU_LIMIT)
        act = g * jax.nn.sigmoid(SWIGLU_ALPHA * g) * (lin + 1.0)
        y = jnp.dot(act.astype(BF16), w2_sc[...], preferred_element_type=F32) + b2_ref[...]
        o_ref[r0:r0 + half, :] = _pack_bf16_pairs(y)

    @pl.when(active)
    def _():
        mlp(0)

    @pl.when(second)
    def _():
        mlp(half)

    @pl.when(active & jnp.logical_not(second))
    def _():
        o_ref[half:, :] = jnp.zeros((half, o_ref.shape[1]), jnp.uint32)


def _experts(blk_expert, nused, half2, xs, P, w1, b1g, b1l, w2, b2, bm):
    Wd = xs.shape[1]
    E, D, F2 = w1.shape
    F = F2 // 2
    nblk = P // bm
    tile = 2 * LANES
    pm = np.zeros((tile, tile), np.float32)
    pm[2 * np.arange(LANES), np.arange(LANES)] = 1.0
    pm[2 * np.arange(LANES) + 1, LANES + np.arange(LANES)] = 1.0
    pm = jnp.asarray(pm, BF16)

    def row_map(b, be, nu, h2):
        return (jnp.minimum(b, nu[0] - 1), 0)

    def exp_map(b, be, nu, h2):
        return (be[b], 0, 0)

    grid_spec = pltpu.PrefetchScalarGridSpec(
        num_scalar_prefetch=3,
        grid=(nblk,),
        in_specs=[pl.BlockSpec((bm, Wd), row_map),
                  pl.BlockSpec((None, D, F2), exp_map),
                  pl.BlockSpec((None, 1, F), exp_map),
                  pl.BlockSpec((None, 1, F), exp_map),
                  pl.BlockSpec((None, F, D), exp_map),
                  pl.BlockSpec((None, 1, D), exp_map),
                  pl.BlockSpec((tile, tile), lambda b, be, nu, h2: (0, 0))],
        out_specs=pl.BlockSpec((bm, D // 2), lambda b, be, nu, h2: (b, 0)),
        scratch_shapes=[pltpu.VMEM((D, F), BF16),
                        pltpu.VMEM((D, F), BF16),
                        pltpu.VMEM((F, D), BF16)],
    )
    return pl.pallas_call(
        _expert_kernel,
        grid_spec=grid_spec,
        out_shape=jax.ShapeDtypeStruct((P, D // 2), jnp.uint32),
        compiler_params=_cparams(("arbitrary",)),
        name="moe_experts",
    )(blk_expert, nused, half2, xs, w1, b1g, b1l, w2, b2, pm)


def _combine_kernel(tcur_ref, tnxt_ref, lp_ref, grow_ref, x1_ref, mod_ref, y_ref, o_ref,
                    buf, sem, *, tm, NR, nt):
    i = pl.program_id(0)
    s = i % 2

    def issue(tref, slot):
        def body(c, carry):
            d = pl.multiple_of(tref[0, 0, c], ROW_CHUNK)
            r = pl.multiple_of(c * ROW_CHUNK, ROW_CHUNK)
            pltpu.make_async_copy(y_ref.at[pl.ds(d, ROW_CHUNK)], buf.at[slot, pl.ds(r, ROW_CHUNK)],
                                  sem.at[slot]).start()
            return carry
        lax.fori_loop(0, NR // ROW_CHUNK, body, 0, unroll=CHUNK_UNROLL)

    @pl.when(i == 0)
    def _():
        issue(tcur_ref, 0)

    for slot in range(2):
        @pl.when((i + 1 < nt) & (s != slot))
        def _():
            issue(tnxt_ref, slot)

    pltpu.make_async_copy(y_ref.at[pl.ds(0, NR)], buf.at[s], sem.at[s]).wait()

    ylo, yhi = _unpack_bf16_pairs(buf[s])
    lpf = jnp.concatenate([lp_ref[...].astype(F32), jnp.zeros((LANES - 8, tm), F32)], axis=0)
    lpr = lpf.T.astype(jnp.int32)
    gr = grow_ref[...]
    cols = lax.broadcasted_iota(jnp.int32, (tm, NR), 1)
    gmat = jnp.where(cols == lpr[:, 0:1], gr[:, 0:1], 0.0)
    for k in range(1, TOP_K):
        gmat = gmat + jnp.where(cols == lpr[:, k:k + 1], gr[:, k:k + 1], 0.0)
    gmat = gmat.astype(BF16)
    hw = ylo.shape[1]
    o_ref[:, :hw] = x1_ref[:, :hw] + mod_ref[5:6, :hw] * jnp.dot(gmat, ylo, preferred_element_type=F32)
    o_ref[:, hw:] = x1_ref[:, hw:] + mod_ref[5:6, hw:] * jnp.dot(gmat, yhi, preferred_element_type=F32)


def _combine(tbl3, lp8, grow, x1, mod3, y, S, tm, NR):
    T, D = x1.shape
    nt = T // tm
    tiles_per_batch = S // tm
    nch_tbl = tbl3.shape[2]
    kern = functools.partial(_combine_kernel, tm=tm, NR=NR, nt=nt)
    return pl.pallas_call(
        kern,
        grid=(nt,),
        in_specs=[pl.BlockSpec((1, 1, nch_tbl), lambda i: (i, 0, 0), memory_space=pltpu.SMEM),
                  pl.BlockSpec((1, 1, nch_tbl), lambda i: (jnp.minimum(i + 1, nt - 1), 0, 0),
                               memory_space=pltpu.SMEM),
                  pl.BlockSpec((8, tm), lambda i: (0, i)),
                  pl.BlockSpec((tm, LANES), lambda i: (i, 0)),
                  pl.BlockSpec((tm, D), lambda i: (i, 0)),
                  pl.BlockSpec((None, 6, D), lambda i: (i // tiles_per_batch, 0, 0)),
                  pl.BlockSpec(memory_space=pl.ANY)],
        out_specs=pl.BlockSpec((tm, D), lambda i: (i, 0)),
        out_shape=jax.ShapeDtypeStruct((T, D), F32),
        scratch_shapes=[pltpu.VMEM((2, NR, D // 2), jnp.uint32),
                        pltpu.SemaphoreType.DMA((2,))],
        compiler_params=_cparams(("arbitrary",)),
        name="moe_combine",
    )(tbl3, tbl3, lp8, grow, x1, mod3, y)


def _tiles(S):
    t_attn = min(512, S // 4) if S >= 512 else S
    tm = min(512, t_attn)
    return dict(tm=tm, t_attn=t_attn, bm=1024)


def _moe_layout(cnt, idx_t, lrank_t, tm, bm):
    nt, E = cnt.shape
    T = idx_t.shape[1]
    c8 = (cnt + ROW_CHUNK - 1) // ROW_CHUNK * ROW_CHUNK
    tile_end = jnp.cumsum(c8, axis=1)
    tile_off = tile_end - c8
    used = tile_end[:, -1]
    off = jnp.cumsum(c8, axis=0) - c8
    tot = jnp.sum(c8, axis=0)
    padded = (tot + bm - 1) // bm * bm
    pends = jnp.cumsum(padded)
    pstarts = pends - padded
    NR = TOP_K * tm + ROW_CHUNK * E
    nch_tbl = NR // ROW_CHUNK
    P = (T * TOP_K + nt * E * (ROW_CHUNK - 1) + E * bm + bm - 1) // bm * bm
    nblk = P // bm

    eids = jnp.arange(E, dtype=jnp.int32)
    tok_off = jnp.repeat(tile_off.T, tm, axis=1)
    lp = jnp.sum(jnp.where(idx_t[None, :TOP_K] == eids[:, None, None], tok_off[:, None, :], 0), axis=0) \
        + lrank_t[:TOP_K]
    lp8 = jnp.concatenate([lp, jnp.full((8 - TOP_K, T), -1, jnp.int32)], axis=0).astype(jnp.int32)

    c_row = jnp.arange(nch_tbl, dtype=jnp.int32) * ROW_CHUNK
    inrun = (tile_off[:, :, None] <= c_row) & (c_row < tile_end[:, :, None])
    base = pstarts[None, :] + off - tile_off
    tbl = jnp.sum(jnp.where(inrun, base[:, :, None], 0), axis=1) + c_row[None, :]
    in_use = c_row[None, :] < used[:, None]
    spare_rows = NR - TOP_K * tm
    spare = P + jnp.arange(nt, dtype=jnp.int32)[:, None] * spare_rows + (c_row[None, :] - used[:, None])
    tbl_out = jnp.where(in_use, tbl, spare).astype(jnp.int32)
    tbl_in = jnp.where(in_use, tbl, 0).astype(jnp.int32)
    n_spare_blk = (nt * spare_rows + bm - 1) // bm

    nused = (pends[-1] // bm).astype(jnp.int32).reshape(1)
    blk_start = jnp.arange(nblk, dtype=jnp.int32) * bm
    blk_expert = jnp.minimum(jnp.sum(pends[None, :] <= blk_start[:, None], axis=1), E - 1).astype(jnp.int32)
    half2 = (blk_start + bm // 2 < (pstarts + tot)[blk_expert]).astype(jnp.int32)
    last_blk = jnp.where(padded > 0, pends // bm - 1, -1)
    n_tail = nblk - (T * TOP_K) // bm
    tail_blk = nused[0] + jnp.arange(n_tail, dtype=jnp.int32)
    tail_blk = jnp.where(tail_blk < nblk, tail_blk, -1)
    spare_blk = nblk + jnp.arange(n_spare_blk, dtype=jnp.int32)
    zero_blk = jnp.concatenate([last_blk, tail_blk, spare_blk]).astype(jnp.int32)
    return dict(lp8=lp8, tbl_out=tbl_out.reshape(nt, 1, nch_tbl), tbl_in=tbl_in.reshape(nt, 1, nch_tbl),
                nused=nused, blk_expert=blk_expert, half2=half2, zero_blk=zero_blk,
                P=P, P_alloc=P + n_spare_blk * bm, NR=NR)


def kernel(x, c, w_ada, b_ada, g_mix, w_in, w_fourier, q_norm_g, k_norm_g, lambda_q1, lambda_k1,
           lambda_q2, lambda_k2, g_subln, w_out, rel_bias, g_ffn, w_router, b_router, w1, b1, w2, b2):
    B, S, D = x.shape
    T = B * S
    L = w_ada.shape[0]
    E = w_router.shape[-1]
    cfg = _tiles(S)
    tm, t_attn, bm = cfg["tm"], cfg["t_attn"], cfg["bm"]
    dqk = q_norm_g.shape[-1]
    scale = dqk ** -0.5
    fw = w_in.shape[-1] // 4

    btiles = _bias_tiles(rel_bias, t_attn)
    x2 = x.reshape(T, D)
    for l in range(L):
        lam_init = 0.8 - 0.6 * math.exp(-0.3 * l)
        mod3 = _ada(c, w_ada[l], b_ada[l]).reshape(B, 6, D)
        gq = (jnp.tile(q_norm_g[l], 4) * (scale * LOG2E)).reshape(1, 2 * LANES)
        gk = jnp.tile(k_norm_g[l], 4).reshape(1, 2 * LANES)
        uf, qx, kx, v = _in_proj(x2, mod3, g_mix[l].reshape(1, D), w_in[l].astype(BF16), gq, gk,
                                 S, tm, t_attn)
        yf = _fourier(uf.reshape(B, S, fw), w_fourier[l]).reshape(T, fw)
        lam4 = jnp.stack([lambda_q1[l], lambda_k1[l], lambda_q2[l], lambda_k2[l]]).astype(F32)
        yd = _attention(rel_bias, qx.reshape(B, S, -1), kx.reshape(B, S, -1), v.reshape(B, S, fw),
                        btiles, lam4, g_subln[l].reshape(1, LANES), t_attn, lam_init).reshape(T, fw)
        x1, hp, idx_t, lrank_t, grow, cnt3 = _out_proj(
            yf, yd, x2, mod3, w_out[l].astype(BF16), g_ffn[l].reshape(1, D),
            w_router[l].T.astype(BF16), b_router[l].reshape(E, 1), S, tm)
        lay = _moe_layout(cnt3[:, :, 0], idx_t, lrank_t, tm, bm)
        xs = _dispatch(lay["tbl_out"], lay["zero_blk"], lay["lp8"], hp, lay["P_alloc"], bm, tm, lay["NR"])
        y = _experts(lay["blk_expert"], lay["nused"], lay["half2"], xs, lay["P"], w1[l], b1[l][:, None, 0::2],
                     b1[l][:, None, 1::2], w2[l], b2[l][:, None, :], bm)
        x2 = _combine(lay["tbl_in"], lay["lp8"], grow, x1, mod3, y, S, tm, lay["NR"])
    return x2.reshape(B, S, D)
```

```python
import functools
import math

import numpy as np
import jax
import jax.numpy as jnp
from jax import lax
from jax.experimental import pallas as pl
from jax.experimental.pallas import tpu as pltpu

F32 = jnp.float32
BF16 = jnp.bfloat16

EPS = 1e-6
N_FOURIER_GROUPS = 4
N_DIFF_HEADS = 4
TOP_K = 4
N_BUCKETS = 32
MAX_DISTANCE = 128
SWIGLU_ALPHA = 1.702
SWIGLU_LIMIT = 7.0
LANES = 128
DFT_RADIX = 4
LOG2E = 1.4426950408889634
VMEM_LIMIT = 56 * 1024 * 1024


def _cparams(sem):
    return pltpu.CompilerParams(dimension_semantics=sem, vmem_limit_bytes=VMEM_LIMIT)


def _ada_kernel(c_ref, w_ref, b_ref, o_ref):
    c = c_ref[...]
    cond = c * jax.nn.sigmoid(c)
    o_ref[...] = jnp.dot(cond, w_ref[...], preferred_element_type=F32,
                         precision=lax.Precision.HIGHEST) + b_ref[...]


def _ada(c, w, b):
    B, D = c.shape
    N = w.shape[1]
    tn = 1536 if N % 1536 == 0 else N
    return pl.pallas_call(
        _ada_kernel,
        grid=(N // tn,),
        in_specs=[pl.BlockSpec((B, D), lambda j: (0, 0)),
                  pl.BlockSpec((D, tn), lambda j: (0, j)),
                  pl.BlockSpec((1, tn), lambda j: (0, j))],
        out_specs=pl.BlockSpec((B, tn), lambda j: (0, j)),
        out_shape=jax.ShapeDtypeStruct((B, N), F32),
        compiler_params=_cparams(("arbitrary",)),
        name="ada",
    )(c, w, b.reshape(1, N))


def _group_rms_inv(xh, gmat, group):
    ss = jnp.dot((xh * xh).astype(BF16), gmat, preferred_element_type=F32)
    return lax.rsqrt(ss * (1.0 / group) + EPS)


def _in_proj_kernel(x_ref, mod_ref, g_ref, w_ref, gq_ref, gk_ref, gmat_ref,
                    uf_ref, q_ref, k_ref, v_ref, *, tiles_per_batch, tm, t_attn, n_chunks, fw, qw):
    i = pl.program_id(0)
    x = x_ref[...]
    ms = jnp.mean(x * x, axis=-1, keepdims=True)
    y = x * lax.rsqrt(ms + EPS) * g_ref[...]
    h = y * (1.0 + mod_ref[1:2, :]) + mod_ref[0:1, :]
    proj = jnp.dot(h.astype(BF16), w_ref[...], preferred_element_type=F32)
    uf_ref[...] = proj[:, :fw].astype(BF16)
    v_ref[...] = proj[:, fw + 2 * qw:].astype(BF16)

    lane =lax.broadcasted_iota(jnp.int32, (1, LANES), 1)
    chunk = ((i % tiles_per_batch) * tm) // t_attn
    gmat = gmat_ref[...]
    half = LANES // 2
    pair = 2 * LANES
    for hd in range(N_DIFF_HEADS):
        if hd % 2 == 0:
            q2 = proj[:, fw + hd * LANES: fw + hd * LANES + pair]
            k2 = proj[:, fw + qw + hd * LANES: fw + qw + hd * LANES + pair]
            qn2 = (q2 * _group_rms_inv(q2, gmat, half) * gq_ref[...]).astype(BF16)
            kn2 = (k2 * _group_rms_inv(k2, gmat, half) * gk_ref[...]).astype(BF16)
        qn = qn2[:, (hd % 2) * LANES:(hd % 2 + 1) * LANES]
        kn = kn2[:, (hd % 2) * LANES:(hd % 2 + 1) * LANES]
        for m in range(2):
            fl = lane - half if m == 0 else lane
            is_data = (lane < half) if m == 0 else (lane >= half)
            f = jnp.where(fl < n_chunks, fl, fl - n_chunks)
            onehot = ((fl >= 0) & (fl < 2 * n_chunks) & (f == chunk)).astype(BF16)
            c0 = (2 * hd + m) * LANES
            q_ref[:, c0:c0 + LANES] = jnp.where(is_data, qn, jnp.zeros_like(qn))
            k_ref[:, c0:c0 + LANES] = jnp.where(is_data, kn, onehot)


def _in_proj(x2, mod3, g_mix, w_in_b, gq, gk, S, tm, t_attn):
    T, D = x2.shape
    ncols = w_in_b.shape[1]
    fw = ncols // 4
    qw = fw
    tiles_per_batch = S // tm
    n_chunks = S // t_attn
    half = LANES // 2
    pair = 2 * LANES
    gmat = (np.arange(pair)[:, None] // half == np.arange(pair)[None, :] // half)
    gmat = jnp.asarray(gmat, BF16)
    kern = functools.partial(_in_proj_kernel, tiles_per_batch=tiles_per_batch, tm=tm, t_attn=t_attn,
                             n_chunks=n_chunks, fw=fw, qw=qw)
    ext = 2 * qw
    return pl.pallas_call(
        kern,
        grid=(T // tm,),
        in_specs=[pl.BlockSpec((tm, D), lambda i: (i, 0)),
                  pl.BlockSpec((None, 6, D), lambda i: (i // tiles_per_batch, 0, 0)),
                  pl.BlockSpec((1, D), lambda i: (0, 0)),
                  pl.BlockSpec((D, ncols), lambda i: (0, 0)),
                  pl.BlockSpec((1, pair), lambda i: (0, 0)),
                  pl.BlockSpec((1, pair), lambda i: (0, 0)),
                  pl.BlockSpec((pair, pair), lambda i: (0, 0))],
        out_specs=[pl.BlockSpec((tm, fw), lambda i: (i, 0)),
                   pl.BlockSpec((tm, ext), lambda i: (i, 0)),
                   pl.BlockSpec((tm, ext), lambda i: (i, 0)),
                   pl.BlockSpec((tm, fw), lambda i: (i, 0))],
        out_shape=[jax.ShapeDtypeStruct((T, fw), BF16),
                   jax.ShapeDtypeStruct((T, ext), BF16),
                   jax.ShapeDtypeStruct((T, ext), BF16),
                   jax.ShapeDtypeStruct((T, fw), BF16)],
        compiler_params=_cparams(("arbitrary",)),
        name="in_proj",
    )(x2, mod3, g_mix, w_in_b, gq, gk, gmat)


def _fourier_kernel(uf_ref, wcs_ref, tw_ref, dft_ref, o_ref, z_sc, tt_sc, *, M, gw):
    c = pl.program_id(1)

    def channel_stage():
        for g in range(N_FOURIER_GROUPS):
            for j in range(DFT_RADIX):
                ab = jnp.dot(uf_ref[j * M:(j + 1) * M, g * gw:(g + 1) * gw], wcs_ref[g],
                             preferred_element_type=F32)
                z_sc[0, j, :, g * gw:(g + 1) * gw] = ab[:, :gw]
                z_sc[1, j, :, g * gw:(g + 1) * gw] = -ab[:, gw:]

    def butterfly(k2):
        cs = tw_ref[:, k2:k2 + 1]
        sn = tw_ref[:, DFT_RADIX + k2:DFT_RADIX + k2 + 1]
        for g in range(N_FOURIER_GROUPS):
            cols = slice(g * gw, (g + 1) * gw)
            zr = [z_sc[0, j, :, cols] for j in range(DFT_RADIX)]
            zi = [z_sc[1, j, :, cols] for j in range(DFT_RADIX)]
            if k2 == 0:
                yr = zr[0] + zr[1] + zr[2] + zr[3]
                yi = zi[0] + zi[1] + zi[2] + zi[3]
            elif k2 == 1:
                yr = zr[0] + zi[1] - zr[2] - zi[3]
                yi = zi[0] - zr[1] - zi[2] + zr[3]
            elif k2 == 2:
                yr = zr[0] - zr[1] + zr[2] - zr[3]
                yi = zi[0] - zi[1] + zi[2] - zi[3]
            else:
                yr = zr[0] - zi[1] - zr[2] + zi[3]
                yi = zi[0] + zr[1] - zi[2] - zr[3]
            tt_sc[k2, 0:M, cols] = (cs * yr + sn * yi).astype(BF16)
            tt_sc[k2, M:2 * M, cols] = (cs * yi - sn * yr).astype(BF16)

    for cc in range(DFT_RADIX):
        @pl.when(c == cc)
        def _(cc=cc):
            if cc == 0:
                channel_stage()
                butterfly(0)
            o_ref[...] = jnp.dot(dft_ref[...], tt_sc[cc], preferred_element_type=F32).astype(BF16)
            if cc + 1 < DFT_RADIX:
                butterfly(cc + 1)


def _fourier(uf3, w_f):
    B, S, W = uf3.shape
    G = N_FOURIER_GROUPS
    gw = W // G
    M = S // DFT_RADIX
    norm = 1.0 / math.sqrt(S * gw)
    cidx = (np.arange(gw)[:, None] * np.arange(gw)[None, :]) % gw
    ang_c = jnp.asarray(cidx, F32) * (2.0 * math.pi / gw)
    cc, sc = jnp.cos(ang_c) * norm, jnp.sin(ang_c) * norm
    hp = lax.Precision.HIGHEST
    wcs = jnp.concatenate([jnp.einsum("ab,gbd->gad", cc, w_f, precision=hp),
                           jnp.einsum("ab,gbd->gad", sc, w_f, precision=hp)], axis=-1).astype(BF16)
    n1 = np.arange(M)
    tw_idx = (np.arange(DFT_RADIX)[None, :] * n1[:, None]) % S
    ang_t = jnp.asarray(tw_idx, F32) * (2.0 * math.pi / S)
    tw = jnp.concatenate([jnp.cos(ang_t), jnp.sin(ang_t),
                          jnp.zeros((M, LANES - 2 * DFT_RADIX), F32)], axis=1)
    pidx = (n1[:, None] * n1[None, :]) % M
    ang_p = jnp.asarray(pidx, F32) * (2.0 * math.pi / M)
    dft = jnp.concatenate([jnp.cos(ang_p), jnp.sin(ang_p)], axis=1).astype(BF16)
    kern = functools.partial(_fourier_kernel, M=M, gw=gw)
    return pl.pallas_call(
        kern,
        grid=(B, DFT_RADIX),
        in_specs=[pl.BlockSpec((None, S, W), lambda b, c: (b, 0, 0)),
                  pl.BlockSpec((G, gw, 2 * gw), lambda b, c: (0, 0, 0)),
                  pl.BlockSpec((M, LANES), lambda b, c: (0, 0)),
                  pl.BlockSpec((M, 2 * M), lambda b, c: (0, 0))],
        out_specs=pl.BlockSpec((None, M, W), lambda b, c: (b, 0, c)),
        out_shape=jax.ShapeDtypeStruct((B, M, DFT_RADIX * W), BF16),
        scratch_shapes=[pltpu.VMEM((2, DFT_RADIX, M, W), F32),
                        pltpu.VMEM((DFT_RADIX, 2 * M, W), BF16)],
        compiler_params=_cparams(("arbitrary", "arbitrary")),
        name="fourier",
    )(uf3, wcs, tw, dft)


def _bias_kernel(rb_ref, o_ref, *, t):
    h = pl.program_id(0)
    which = pl.program_id(1)
    nb = N_BUCKETS // 2

    def tile(d, buckets):
        row = lax.broadcasted_iota(jnp.int32, (t, t), 0)
        col = lax.broadcasted_iota(jnp.int32, (t, t), 1)
        rel = d * t + col - row
        max_exact = nb // 2
        ret = jnp.where(rel > 0, nb, 0)
        n = jnp.abs(rel)
        nf = jnp.maximum(n, 1).astype(F32)
        large = max_exact + (jnp.log(nf / max_exact) / math.log(MAX_DISTANCE / max_exact)
                             * (nb - max_exact)).astype(jnp.int32)
        large = jnp.minimum(large, nb - 1)
        bucket = ret + jnp.where(n < max_exact, n, large)
        val = jnp.zeros((t, t), F32)
        for j in buckets:
            val = jnp.where(bucket == j, rb_ref[j * N_DIFF_HEADS + h], val)
        return val * LOG2E

    @pl.when(which == 0)
    def _():
        o_ref[...] = tile(-1, range(0, nb))

    @pl.when(which == 1)
    def _():
        o_ref[...] = tile(0, range(0, 2 * nb))

    @pl.when(which == 2)
    def _():
        o_ref[...] = tile(1, range(nb, 2 * nb))

    @pl.when(which == 3)
    def _():
        o_ref[...] = jnp.zeros((t, t), F32)


def _bias_tiles(rel_bias, t):
    H = N_DIFF_HEADS
    assert t >= MAX_DISTANCE, "far key tiles must lie entirely in the saturated bucket"
    return pl.pallas_call(
        functools.partial(_bias_kernel, t=t),
        grid=(H, 4),
        in_specs=[pl.BlockSpec(memory_space=pltpu.SMEM)],
        out_specs=pl.BlockSpec((None, None, t, t), lambda h, d: (h, d, 0, 0)),
        out_shape=jax.ShapeDtypeStruct((H, 4, t, t), F32),
        compiler_params=_cparams(("arbitrary", "arbitrary")),
        name="t5_bias",
    )(rel_bias.reshape(-1))


def _far_bucket_consts():
    nb = N_BUCKETS // 2
    return nb - 1, 2 * nb - 1


def _attn_kernel(rb_ref, q_ref, k_ref, v_ref, b_ref, lam_ref, gs_ref, o_ref,
                 s_sc, p_sc, vext_sc, m_sc, raw_sc, *, t, n, n_tiles, lam_init):
    g = pl.program_id(0)
    cur = jnp.minimum(g, n_tiles - 1)
    h = (cur // n) % N_DIFF_HEADS
    i = cur % n
    half = LANES // 2

    @pl.when(g == 0)
    def _():
        raw_sc[...] = jnp.ones(raw_sc.shape, F32)

    @pl.when((i == 0) & (g < n_tiles))
    def _():
        vext_sc[:, :LANES] = v_ref[...]
        vext_sc[:, LANES:] = jnp.ones(v_ref.shape, BF16)

    left_b, right_b = _far_bucket_consts()
    c_left = rb_ref[left_b * N_DIFF_HEADS + h] * LOG2E
    c_right = rb_ref[right_b * N_DIFF_HEADS + h] * LOG2E
    lane = lax.broadcasted_iota(jnp.int32, (1, LANES), 1)

    def all_scores(m):
        fl = lane - half if m == 0 else lane
        is_data = (lane < half) if m == 0 else (lane >= half)
        f = jnp.where(fl < n, fl, fl - n)
        cfar = jnp.where(f < i - 1, c_left, jnp.where(f > i + 1, c_right, 0.0))
        c_hi = cfar.astype(BF16)
        c_lo = (cfar - c_hi.astype(F32)).astype(BF16)
        feat = jnp.where(fl < n, c_hi, c_lo)
        feat = jnp.where((fl >= 0) & (fl < 2 * n), feat, jnp.zeros_like(feat))
        qm = jnp.where(is_data, q_ref[:, m * LANES:(m + 1) * LANES], feat)
        for j in range(n):
            kc = k_ref[j * t:(j + 1) * t, m * LANES:(m + 1) * LANES]
            s = lax.dot_general(qm, kc, (((1,), (1,)), ((), ())), preferred_element_type=F32)
            d = j - i
            s = s + b_ref[jnp.where(jnp.abs(d) <= 1, d + 1, 3)]
            s_sc[m, j] = s
            tile_max = s[:, 0:LANES]
            for c in range(1, t // LANES):
                tile_max = jnp.maximum(tile_max, s[:, c * LANES:(c + 1) * LANES])
            if j == 0:
                m_sc[m] = tile_max
            else:
                m_sc[m] = jnp.maximum(m_sc[m], tile_max)

    def exponentials(m):
        mx = jnp.max(m_sc[m], axis=-1, keepdims=True)
        for j in range(n):
            p_sc[m, :, j * t:(j + 1) * t] = jnp.exp2(s_sc[m, j] - mx).astype(BF16)

    def finish_previous():
        r0 = raw_sc[0]
        r1 = raw_sc[1]
        lam = (jnp.exp(jnp.sum(lam_ref[0:1, :] * lam_ref[1:2, :], axis=-1, keepdims=True))
               - jnp.exp(jnp.sum(lam_ref[2:3, :] * lam_ref[3:4, :], axis=-1, keepdims=True)) + lam_init)
        o = r0[:, :LANES] / r0[:, LANES:LANES + 1] - lam * (r1[:, :LANES] / r1[:, LANES:LANES + 1])
        ms = jnp.mean(o * o, axis=-1, keepdims=True)
        o_ref[...] = (o * lax.rsqrt(ms + EPS) * gs_ref[...] * (1.0 - lam_init)).astype(BF16)

    @pl.when(i < n)
    def _():
        finish_previous()
        all_scores(0)

    @pl.when(g < n_tiles)
    def _():
        all_scores(1)
        exponentials(0)
        raw_sc[0] = jnp.dot(p_sc[0], vext_sc[...], preferred_element_type=F32)
        exponentials(1)
        raw_sc[1] = jnp.dot(p_sc[1], vext_sc[...], preferred_element_type=F32)


def _attention(rel_bias, q3, k3, v3, btiles, lam4, g_subln, t, lam_init):
    B, S, _ = q3.shape
    H = N_DIFF_HEADS
    n = S // t
    n_tiles = B * H * n
    kern = functools.partial(_attn_kernel, t=t, n=n, n_tiles=n_tiles, lam_init=lam_init)

    def tile_of(g):
        return g // (H * n), (g // n) % H, g % n

    def cur_map(fn):
        def index_map(g):
            return fn(*tile_of(jnp.minimum(g, n_tiles - 1)))
        return index_map

    def prev_map(g):
        b, h, i = tile_of(jnp.maximum(g - 1, 0))
        return (b, i, h)

    return pl.pallas_call(
        kern,
        grid=(n_tiles + 1,),
        in_specs=[pl.BlockSpec(memory_space=pltpu.SMEM),
                  pl.BlockSpec((None, t, 2 * LANES), cur_map(lambda b, h, i: (b, i, h))),
                  pl.BlockSpec((None, S, 2 * LANES), cur_map(lambda b, h, i: (b, 0, h))),
                  pl.BlockSpec((None, S, LANES), cur_map(lambda b, h, i: (b, 0, h))),
                  pl.BlockSpec((None, 4, t, t), cur_map(lambda b, h, i: (h, 0, 0, 0))),
                  pl.BlockSpec((4, LANES // 2), lambda g: (0, 0)),
                  pl.BlockSpec((1, LANES), lambda g: (0, 0))],
        out_specs=pl.BlockSpec((None, t, LANES), prev_map),
        out_shape=jax.ShapeDtypeStruct((B, S, H * LANES), BF16),
        scratch_shapes=[pltpu.VMEM((2, n, t, t), F32),
                        pltpu.VMEM((2, t, S), BF16),
                        pltpu.VMEM((S, 2 * LANES), BF16),
                        pltpu.VMEM((2, t, LANES), F32),
                        pltpu.VMEM((2, t, 2 * LANES), F32)],
        compiler_params=_cparams(("arbitrary",)),
        name="diff_attn",
    )(rel_bias.reshape(-1), q3, k3, v3, btiles, lam4, g_subln)


def _pack_bf16_pairs(v):
    bits = lax.bitcast_convert_type(v.astype(BF16).astype(F32), jnp.uint32)
    n = bits.shape[1] // 2
    return (bits[:, :n] >> 16) | (bits[:, n:] & jnp.uint32(0xFFFF0000))


def _unpack_bf16_pairs(p):
    lo = lax.bitcast_convert_type(p << 16, F32).astype(BF16)
    hi = lax.bitcast_convert_type(p & jnp.uint32(0xFFFF0000), F32).astype(BF16)
    return lo, hi


def _out_proj_kernel(yf_ref, perm_ref, yd_ref, x_ref, mod_ref, wo_ref, g_ref, wr_ref, br_ref,
                     x1_ref, hp_ref, idx_ref, rank_ref, grow_ref, cnt_ref, *, tm, fw, E):
    yf_stack = jnp.concatenate([yf_ref[:, r * fw:(r + 1) * fw] for r in range(DFT_RADIX)], axis=0)
    yf = jnp.dot(perm_ref[...], yf_stack, preferred_element_type=F32).astype(BF16)
    mix = (jnp.dot(yf, wo_ref[:fw, :], preferred_element_type=F32)
           + jnp.dot(yd_ref[...], wo_ref[fw:, :], preferred_element_type=F32))
    x1 = x_ref[...] + mod_ref[2:3, :] * mix
    x1_ref[...] = x1
    ms = jnp.mean(x1 * x1, axis=-1, keepdims=True)
    h = x1 * lax.rsqrt(ms + EPS) * g_ref[...]
    h = h * (1.0 + mod_ref[4:5, :]) + mod_ref[3:4, :]
    hb = h.astype(BF16)
    hp_ref[...] = _pack_bf16_pairs(hb)

    logits = lax.dot_general(wr_ref[...], hb, (((1,), (1,)), ((), ())),
                             preferred_element_type=F32) + br_ref[...]
    rowid = lax.broadcasted_iota(jnp.int32, (E, tm), 0)
    vals, idxs = [], []
    comb = jnp.zeros((E, tm), F32)
    l = logits
    for _ in range(TOP_K):
        mv = jnp.max(l, axis=0, keepdims=True)
        ix = jnp.min(jnp.where(l == mv, rowid, E), axis=0, keepdims=True)
        sel = rowid == ix
        vals.append(mv)
        idxs.append(ix)
        comb = comb + sel.astype(F32)
        l = jnp.where(sel, -jnp.inf, l)
    es = [jnp.exp(v - vals[0]) for v in vals]
    den = es[0] + es[1] + es[2] + es[3]
    gates = [e / den for e in es]

    r_i = lax.broadcasted_iota(jnp.int32, (tm, tm), 0)
    c_i = lax.broadcasted_iota(jnp.int32, (tm, tm), 1)
    upper = (r_i < c_i).astype(BF16)
    before = jnp.dot(comb.astype(BF16), upper, preferred_element_type=F32)
    ranks = [jnp.sum(jnp.where(rowid == ix, before, 0.0), axis=0, keepdims=True) for ix in idxs]
    cnt = jnp.sum(comb, axis=1, keepdims=True)
    cnt_ref[...] = jnp.broadcast_to(cnt, cnt_ref.shape).astype(jnp.int32)

    pad_i = jnp.zeros((8 - TOP_K, tm), jnp.int32)
    idx_ref[...] = jnp.concatenate(idxs + [pad_i], axis=0)
    rank_ref[...] = jnp.concatenate([r.astype(jnp.int32) for r in ranks] + [pad_i], axis=0)
    gpad = jnp.concatenate(gates + [jnp.zeros((LANES - TOP_K, tm), F32)], axis=0)
    grow_ref[...] = gpad.T


def _out_proj(yf3, yd2, x2, mod3, w_out_b, g_ffn, wr_t, br_t, S, tm):
    T, D = x2.shape
    fw = yd2.shape[1]
    E = wr_t.shape[0]
    nt = T // tm
    tiles_per_batch = S // tm
    q = tm // DFT_RADIX
    col = np.arange(tm)
    perm = jnp.asarray(np.arange(tm)[:, None] == DFT_RADIX * (col % q) + col // q, BF16)
    kern = functools.partial(_out_proj_kernel, tm=tm, fw=fw, E=E)
    return pl.pallas_call(
        kern,
        grid=(nt,),
        in_specs=[pl.BlockSpec((None, q, DFT_RADIX * fw), lambda i: (i // tiles_per_batch, i % tiles_per_batch, 0)),
                  pl.BlockSpec((tm, tm), lambda i: (0, 0)),
                  pl.BlockSpec((tm, fw), lambda i: (i, 0)),
                  pl.BlockSpec((tm, D), lambda i: (i, 0)),
                  pl.BlockSpec((None, 6, D), lambda i: (i // tiles_per_batch, 0, 0)),
                  pl.BlockSpec((D, D), lambda i: (0, 0)),
                  pl.BlockSpec((1, D), lambda i: (0, 0)),
                  pl.BlockSpec((E, D), lambda i: (0, 0)),
                  pl.BlockSpec((E, 1), lambda i: (0, 0))],
        out_specs=[pl.BlockSpec((tm, D), lambda i: (i, 0)),
                   pl.BlockSpec((tm, D // 2), lambda i: (i, 0)),
                   pl.BlockSpec((8, tm), lambda i: (0, i)),
                   pl.BlockSpec((8, tm), lambda i: (0, i)),
                   pl.BlockSpec((tm, LANES), lambda i: (i, 0)),
                   pl.BlockSpec((None, E, LANES), lambda i: (i, 0, 0))],
        out_shape=[jax.ShapeDtypeStruct((T, D), F32),
                   jax.ShapeDtypeStruct((T, D // 2), jnp.uint32),
                   jax.ShapeDtypeStruct((8, T), jnp.int32),
                   jax.ShapeDtypeStruct((8, T), jnp.int32),
                   jax.ShapeDtypeStruct((T, LANES), F32),
                   jax.ShapeDtypeStruct((nt, E, LANES), jnp.int32)],
        compiler_params=_cparams(("arbitrary",)),
        name="out_proj_router",
    )(yf3, perm, yd2, x2, mod3, w_out_b, g_ffn, wr_t, br_t)


ROW_CHUNK = 8
CHUNK_UNROLL = 8


def _dispatch_kernel(tbl_ref, zb_ref, lp_ref, hp_ref, xs_ref, buf, zero_sc, sem, zsem,
                     *, tm, bm, NR, nz, nt):
    i = pl.program_id(0)
    s = i % 2

    @pl.when(i == 0)
    def _():
        zero_sc[...] = jnp.zeros(zero_sc.shape, jnp.uint32)
        for e in range(nz):
            @pl.when(zb_ref[e] >= 0)
            def _():
                pltpu.make_async_copy(zero_sc, xs_ref.at[pl.ds(zb_ref[e] * bm, bm)], zsem).start()
        for e in range(nz):
            @pl.when(zb_ref[e] >= 0)
            def _():
                pltpu.make_async_copy(zero_sc, xs_ref.at[pl.ds(zb_ref[e] * bm, bm)], zsem).wait()

    lp = lp_ref[...]
    rows = lax.broadcasted_iota(jnp.int32, (NR, tm), 0)
    hit = rows == lp[0:1, :]
    for k in range(1, TOP_K):
        hit = hit | (rows == lp[k:k + 1, :])
    sel = jnp.where(hit, 1.0, 0.0).astype(BF16)
    lo, hi = _unpack_bf16_pairs(hp_ref[...])
    slo = jnp.dot(sel, lo, preferred_element_type=F32)
    shi = jnp.dot(sel, hi, preferred_element_type=F32)
    buf[s] = ((lax.bitcast_convert_type(slo, jnp.uint32) >> 16)
              | (lax.bitcast_convert_type(shi, jnp.uint32) & jnp.uint32(0xFFFF0000)))

    for slot in range(2):
        def issue(c, carry, slot=slot):
            d = pl.multiple_of(tbl_ref[0, 0, c], ROW_CHUNK)
            r = pl.multiple_of(c * ROW_CHUNK, ROW_CHUNK)
            pltpu.make_async_copy(buf.at[slot, pl.ds(r, ROW_CHUNK)], xs_ref.at[pl.ds(d, ROW_CHUNK)],
                                  sem.at[slot]).start()
            return carry

        @pl.when(s == slot)
        def _():
            lax.fori_loop(0, NR // ROW_CHUNK, issue, 0, unroll=CHUNK_UNROLL)

    def wait_slot(slot):
        pltpu.make_async_copy(buf.at[slot], xs_ref.at[pl.ds(0, NR)], sem.at[slot]).wait()

    @pl.when(i > 0)
    def _():
        wait_slot(1 - s)

    @pl.when(i == nt - 1)
    def _():
        wait_slot(s)


def _dispatch(tbl3, zero_blk, lp8, hp, P, bm, tm, NR):
    T, Wd = hp.shape
    nt = T // tm
    nz = zero_blk.shape[0]
    nch_tbl = tbl3.shape[2]
    kern = functools.partial(_dispatch_kernel, tm=tm, bm=bm, NR=NR, nz=nz, nt=nt)
    return pl.pallas_call(
        kern,
        grid=(nt,),
        in_specs=[pl.BlockSpec((1, 1, nch_tbl), lambda i: (i, 0, 0), memory_space=pltpu.SMEM),
                  pl.BlockSpec(memory_space=pltpu.SMEM),
                  pl.BlockSpec((8, tm), lambda i: (0, i)),
                  pl.BlockSpec((tm, Wd), lambda i: (i, 0))],
        out_specs=pl.BlockSpec(memory_space=pl.ANY),
        out_shape=jax.ShapeDtypeStruct((P, Wd), jnp.uint32),
        scratch_shapes=[pltpu.VMEM((2, NR, Wd), jnp.uint32),
                        pltpu.VMEM((bm, Wd), jnp.uint32),
                        pltpu.SemaphoreType.DMA((2,)),
                        pltpu.SemaphoreType.DMA],
        compiler_params=_cparams(("arbitrary",)),
        name="moe_dispatch",
    )(tbl3, zero_blk, lp8, hp)


def _expert_kernel(be_ref, nu_ref, h2_ref, x_ref, w1_ref, b1g_ref, b1l_ref, w2_ref, b2_ref, pm_ref, o_ref,
                   w1g_sc, w1l_sc, w2_sc):
    blk = pl.program_id(0)
    nused = nu_ref[0]
    e = be_ref[blk]
    prev = be_ref[jnp.maximum(blk - 1, 0)]
    active = blk < nused
    second = active & (h2_ref[blk] != 0)
    half = x_ref.shape[0] // 2
    tile = 2 * LANES
    n_tiles = w1_ref.shape[1] // tile

    @pl.when(active & ((blk == 0) | (e != prev)))
    def _():
        for c in range(n_tiles):
            wt = w1_ref[:, c * tile:(c + 1) * tile].astype(BF16)
            pw = jnp.dot(wt, pm_ref[...], preferred_element_type=F32)
            w1g_sc[:, c * LANES:(c + 1) * LANES] = pw[:, :LANES].astype(BF16)
            w1l_sc[:, c * LANES:(c + 1) * LANES] = pw[:, LANES:].astype(BF16)
        w2_sc[...] = w2_ref[...].astype(BF16)

    @pl.when(jnp.logical_not(active))
    def _():
        o_ref[...] = jnp.zeros(o_ref.shape, jnp.uint32)

    def mlp(r0):
        xp = x_ref[r0:r0 + half, :]
        hw = xp.shape[1]
        xlo, xhi = _unpack_bf16_pairs(xp)
        zg = (jnp.dot(xlo, w1g_sc[:hw, :], preferred_element_type=F32)
              + jnp.dot(xhi, w1g_sc[hw:, :], preferred_element_type=F32) + b1g_ref[...])
        zl = (jnp.dot(xlo, w1l_sc[:hw, :], preferred_element_type=F32)
              + jnp.dot(xhi, w1l_sc[hw:, :], preferred_element_type=F32) + b1l_ref[...])
        g = jnp.minimum(zg, SWIGLU_LIMIT)
        lin = jnp.clip(zl, -SWIGLU_LIMIT, SWIGLU_LIMIT)
        act = g * jax.nn.sigmoid(SWIGLU_ALPHA * g) * (lin + 1.0)
        y = jnp.dot(act.astype(BF16), w2_sc[...], preferred_element_type=F32) + b2_ref[...]
        o_ref[r0:r0 + half, :] = _pack_bf16_pairs(y)

    @pl.when(second)
    def _():
        mlp(0)
        mlp(half)

    @pl.when(active & jnp.logical_not(second))
    def _():
        mlp(0)
        o_ref[half:, :] = jnp.zeros((half, o_ref.shape[1]), jnp.uint32)


def _experts(blk_expert, nused, half2, xs, P, w1, b1g, b1l, w2, b2, bm):
    Wd = xs.shape[1]
    E, D, F2 = w1.shape
    F = F2 // 2
    nblk = P // bm
    tile = 2 * LANES
    pm = np.zeros((tile, tile), np.float32)
    pm[2 * np.arange(LANES), np.arange(LANES)] = 1.0
    pm[2 * np.arange(LANES) + 1, LANES + np.arange(LANES)] = 1.0
    pm = jnp.asarray(pm, BF16)

    def row_map(b, be, nu, h2):
        return (jnp.minimum(b, nu[0] - 1), 0)

    def exp_map(b, be, nu, h2):
        return (be[b], 0, 0)

    grid_spec = pltpu.PrefetchScalarGridSpec(
        num_scalar_prefetch=3,
        grid=(nblk,),
        in_specs=[pl.BlockSpec((bm, Wd), row_map),
                  pl.BlockSpec((None, D, F2), exp_map),
                  pl.BlockSpec((None, 1, F), exp_map),
                  pl.BlockSpec((None, 1, F), exp_map),
                  pl.BlockSpec((None, F, D), exp_map),
                  pl.BlockSpec((None, 1, D), exp_map),
                  pl.BlockSpec((tile, tile), lambda b, be, nu, h2: (0, 0))],
        out_specs=pl.BlockSpec((bm, D // 2), lambda b, be, nu, h2: (b, 0)),
        scratch_shapes=[pltpu.VMEM((D, F), BF16),
                        pltpu.VMEM((D, F), BF16),
                        pltpu.VMEM((F, D), BF16)],
    )
    return pl.pallas_call(
        _expert_kernel,
        grid_spec=grid_spec,
        out_shape=jax.ShapeDtypeStruct((P, D // 2), jnp.uint32),
        compiler_params=_cparams(("arbitrary",)),
        name="moe_experts",
    )(blk_expert, nused, half2, xs, w1, b1g, b1l, w2, b2, pm)


def _combine_kernel(tcur_ref, tnxt_ref, lp_ref, grow_ref, x1_ref, mod_ref, y_ref, o_ref,
                    buf, sem, *, tm, NR, nt):
    i = pl.program_id(0)
    s = i % 2

    def issue(tref, slot):
        def body(c, carry):
            d = pl.multiple_of(tref[0, 0, c], ROW_CHUNK)
            r = pl.multiple_of(c * ROW_CHUNK, ROW_CHUNK)
            pltpu.make_async_copy(y_ref.at[pl.ds(d, ROW_CHUNK)], buf.at[slot, pl.ds(r, ROW_CHUNK)],
                                  sem.at[slot]).start()
            return carry
        lax.fori_loop(0, NR // ROW_CHUNK, body, 0, unroll=CHUNK_UNROLL)

    @pl.when(i == 0)
    def _():
        issue(tcur_ref, 0)

    for slot in range(2):
        @pl.when((i + 1 < nt) & (s != slot))
        def _():
            issue(tnxt_ref, slot)

    pltpu.make_async_copy(y_ref.at[pl.ds(0, NR)], buf.at[s], sem.at[s]).wait()

    ylo, yhi = _unpack_bf16_pairs(buf[s])
    lpf = jnp.concatenate([lp_ref[...].astype(F32), jnp.zeros((LANES - 8, tm), F32)], axis=0)
    lpr = lpf.T.astype(jnp.int32)
    gr = grow_ref[...]
    cols = lax.broadcasted_iota(jnp.int32, (tm, NR), 1)
    gmat = jnp.where(cols == lpr[:, 0:1], gr[:, 0:1], 0.0)
    for k in range(1, TOP_K):
        gmat = gmat + jnp.where(cols == lpr[:, k:k + 1], gr[:, k:k + 1], 0.0)
    gmat = gmat.astype(BF16)
    hw = ylo.shape[1]
    o_ref[:, :hw] = x1_ref[:, :hw] + mod_ref[5:6, :hw] * jnp.dot(gmat, ylo, preferred_element_type=F32)
    o_ref[:, hw:] = x1_ref[:, hw:] + mod_ref[5:6, hw:] * jnp.dot(gmat, yhi, preferred_element_type=F32)


def _combine(tbl3, lp8, grow, x1, mod3, y, S, tm, NR):
    T, D = x1.shape
    nt = T // tm
    tiles_per_batch = S // tm
    nch_tbl = tbl3.shape[2]
    kern = functools.partial(_combine_kernel, tm=tm, NR=NR, nt=nt)
    return pl.pallas_call(
        kern,
        grid=(nt,),
        in_specs=[pl.BlockSpec((1, 1, nch_tbl), lambda i: (i, 0, 0), memory_space=pltpu.SMEM),
                  pl.BlockSpec((1, 1, nch_tbl), lambda i: (jnp.minimum(i + 1, nt - 1), 0, 0),
                               memory_space=pltpu.SMEM),
                  pl.BlockSpec((8, tm), lambda i: (0, i)),
                  pl.BlockSpec((tm, LANES), lambda i: (i, 0)),
                  pl.BlockSpec((tm, D), lambda i: (i, 0)),
                  pl.BlockSpec((None, 6, D), lambda i: (i // tiles_per_batch, 0, 0)),
                  pl.BlockSpec(memory_space=pl.ANY)],
        out_specs=pl.BlockSpec((tm, D), lambda i: (i, 0)),
        out_shape=jax.ShapeDtypeStruct((T, D), F32),
        scratch_shapes=[pltpu.VMEM((2, NR, D // 2), jnp.uint32),
                        pltpu.SemaphoreType.DMA((2,))],
        compiler_params=_cparams(("arbitrary",)),
        name="moe_combine",
    )(tbl3, tbl3, lp8, grow, x1, mod3, y)


def _tiles(S):
    t_attn = min(512, S // 4) if S >= 512 else S
    tm = min(512, t_attn)
    return dict(tm=tm, t_attn=t_attn, bm=1024)


def _moe_layout(cnt, idx_t, lrank_t, tm, bm):
    nt, E = cnt.shape
    T = idx_t.shape[1]
    c8 = (cnt + ROW_CHUNK - 1) // ROW_CHUNK * ROW_CHUNK
    tile_end = jnp.cumsum(c8, axis=1)
    tile_off = tile_end - c8
    used = tile_end[:, -1]
    off = jnp.cumsum(c8, axis=0) - c8
    tot = jnp.sum(c8, axis=0)
    padded = (tot + bm - 1) // bm * bm
    pends = jnp.cumsum(padded)
    pstarts = pends - padded
    NR = TOP_K * tm + ROW_CHUNK * E
    nch_tbl = NR // ROW_CHUNK
    P = (T * TOP_K + nt * E * (ROW_CHUNK - 1) + E * bm + bm - 1) // bm * bm
    nblk = P // bm

    eids = jnp.arange(E, dtype=jnp.int32)
    tok_off = jnp.repeat(tile_off.T, tm, axis=1)
    lp = jnp.sum(jnp.where(idx_t[None, :TOP_K] == eids[:, None, None], tok_off[:, None, :], 0), axis=0) \
        + lrank_t[:TOP_K]
    lp8 = jnp.concatenate([lp, jnp.full((8 - TOP_K, T), -1, jnp.int32)], axis=0).astype(jnp.int32)

    c_row = jnp.arange(nch_tbl, dtype=jnp.int32) * ROW_CHUNK
    inrun = (tile_off[:, :, None] <= c_row) & (c_row < tile_end[:, :, None])
    base = pstarts[None, :] + off - tile_off
    tbl = jnp.sum(jnp.where(inrun, base[:, :, None], 0), axis=1) + c_row[None, :]
    in_use = c_row[None, :] < used[:, None]
    spare_rows = NR - TOP_K * tm
    spare = P + jnp.arange(nt, dtype=jnp.int32)[:, None] * spare_rows + (c_row[None, :] - used[:, None])
    tbl_out = jnp.where(in_use, tbl, spare).astype(jnp.int32)
    tbl_in = jnp.where(in_use, tbl, 0).astype(jnp.int32)
    n_spare_blk = (nt * spare_rows + bm - 1) // bm

    nused = (pends[-1] // bm).astype(jnp.int32).reshape(1)
    blk_start = jnp.arange(nblk, dtype=jnp.int32) * bm
    blk_expert = jnp.minimum(jnp.sum(pends[None, :] <= blk_start[:, None], axis=1), E - 1).astype(jnp.int32)
    half2 = (blk_start + bm // 2 < (pstarts + tot)[blk_expert]).astype(jnp.int32)
    last_blk = jnp.where(padded > 0, pends // bm - 1, -1)
    n_tail = nblk - (T * TOP_K) // bm
    tail_blk = nused[0] + jnp.arange(n_tail, dtype=jnp.int32)
    tail_blk = jnp.where(tail_blk < nblk, tail_blk, -1)
    spare_blk = nblk + jnp.arange(n_spare_blk, dtype=jnp.int32)
    zero_blk = jnp.concatenate([last_blk, tail_blk, spare_blk]).astype(jnp.int32)
    return dict(lp8=lp8, tbl_out=tbl_out.reshape(nt, 1, nch_tbl), tbl_in=tbl_in.reshape(nt, 1, nch_tbl),
                nused=nused, blk_expert=blk_expert, half2=half2, zero_blk=zero_blk,
                P=P, P_alloc=P + n_spare_blk * bm, NR=NR)


def kernel(x, c, w_ada, b_ada, g_mix, w_in, w_fourier, q_norm_g, k_norm_g, lambda_q1, lambda_k1,
           lambda_q2, lambda_k2, g_subln, w_out, rel_bias, g_ffn, w_router, b_router, w1, b1, w2, b2):
    B, S, D = x.shape
    T = B * S
    L = w_ada.shape[0]
    E = w_router.shape[-1]
    cfg = _tiles(S)
    tm, t_attn, bm = cfg["tm"], cfg["t_attn"], cfg["bm"]
    dqk = q_norm_g.shape[-1]
    scale = dqk ** -0.5
    fw = w_in.shape[-1] // 4

    btiles = _bias_tiles(rel_bias, t_attn)
    x2 = x.reshape(T, D)
    for l in range(L):
        lam_init = 0.8 - 0.6 * math.exp(-0.3 * l)
        mod3 = _ada(c, w_ada[l], b_ada[l]).reshape(B, 6, D)
        gq = (jnp.tile(q_norm_g[l], 4) * (scale * LOG2E)).reshape(1, 2 * LANES)
        gk = jnp.tile(k_norm_g[l], 4).reshape(1, 2 * LANES)
        uf, qx, kx, v = _in_proj(x2, mod3, g_mix[l].reshape(1, D), w_in[l].astype(BF16), gq, gk,
                                 S, tm, t_attn)
        yf = _fourier(uf.reshape(B, S, fw), w_fourier[l])
        lam4 = jnp.stack([lambda_q1[l], lambda_k1[l], lambda_q2[l], lambda_k2[l]]).astype(F32)
        yd = _attention(rel_bias, qx.reshape(B, S, -1), kx.reshape(B, S, -1), v.reshape(B, S, fw),
                        btiles, lam4, g_subln[l].reshape(1, LANES), t_attn, lam_init).reshape(T, fw)
        x1, hp, idx_t, lrank_t, grow, cnt3 = _out_proj(
            yf, yd, x2, mod3, w_out[l].astype(BF16), g_ffn[l].reshape(1, D),
            w_router[l].T.astype(BF16), b_router[l].reshape(E, 1), S, tm)
        lay = _moe_layout(cnt3[:, :, 0], idx_t, lrank_t, tm, bm)
        xs = _dispatch(lay["tbl_out"], lay["zero_blk"], lay["lp8"], hp, lay["P_alloc"], bm, tm, lay["NR"])
        y = _experts(lay["blk_expert"], lay["nused"], lay["half2"], xs, lay["P"], w1[l], b1[l][:, None, 0::2],
                     b1[l][:, None, 1::2], w2[l], b2[l][:, None, :], bm)
        x2 = _combine(lay["tbl_in"], lay["lp8"], grow, x1, mod3, y, S, tm, lay["NR"])
    return x2.reshape(B, S, D)
```

```python
import functools
import math

import numpy as np
import jax
import jax.numpy as jnp
from jax import lax
from jax.experimental import pallas as pl
from jax.experimental.pallas import tpu as pltpu

F32 = jnp.float32
BF16 = jnp.bfloat16

EPS = 1e-6
N_FOURIER_GROUPS = 4
N_DIFF_HEADS = 4
TOP_K = 4
N_BUCKETS = 32
MAX_DISTANCE = 128
SWIGLU_ALPHA = 1.702
SWIGLU_LIMIT = 7.0
LANES = 128
DFT_RADIX = 4
LOG2E = 1.4426950408889634
VMEM_LIMIT = 56 * 1024 * 1024


def _cparams(sem):
    return pltpu.CompilerParams(dimension_semantics=sem, vmem_limit_bytes=VMEM_LIMIT)


def _ada_kernel(c_ref, w_ref, b_ref, o_ref):
    c = c_ref[...]
    cond = c * jax.nn.sigmoid(c)
    o_ref[...] = jnp.dot(cond.astype(BF16), w_ref[...].astype(BF16),
                         preferred_element_type=F32) + b_ref[...]


def _ada(c, w, b):
    B, D = c.shape
    N = w.shape[1]
    tn = 1536 if N % 1536 == 0 else N
    return pl.pallas_call(
        _ada_kernel,
        grid=(N // tn,),
        in_specs=[pl.BlockSpec((B, D), lambda j: (0, 0)),
                  pl.BlockSpec((D, tn), lambda j: (0, j)),
                  pl.BlockSpec((1, tn), lambda j: (0, j))],
        out_specs=pl.BlockSpec((B, tn), lambda j: (0, j)),
        out_shape=jax.ShapeDtypeStruct((B, N), F32),
        compiler_params=_cparams(("arbitrary",)),
        name="ada",
    )(c, w, b.reshape(1, N))


def _group_rms_inv(xh, gmat, group):
    ss = jnp.dot((xh * xh).astype(BF16), gmat, preferred_element_type=F32)
    return lax.rsqrt(ss * (1.0 / group) + EPS)


def _in_proj_kernel(x_ref, mod_ref, g_ref, w_ref, gq_ref, gk_ref, gmat_ref,
                    uf_ref, q_ref, k_ref, v_ref, *, tiles_per_batch, tm, t_attn, n_chunks, fw, qw):
    i = pl.program_id(0)
    x = x_ref[...]
    ms = jnp.mean(x * x, axis=-1, keepdims=True)
    y = x * lax.rsqrt(ms + EPS) * g_ref[...]
    h = y * (1.0 + mod_ref[1:2, :]) + mod_ref[0:1, :]
    proj = jnp.dot(h.astype(BF16), w_ref[...], preferred_element_type=F32)
    uf_ref[...] = proj[:, :fw].astype(BF16)
    v_ref[...] = proj[:, fw + 2 * qw:].astype(BF16)

    lane =lax.broadcasted_iota(jnp.int32, (1, LANES), 1)
    chunk = ((i % tiles_per_batch) * tm) // t_attn
    gmat = gmat_ref[...]
    half = LANES // 2
    pair = 2 * LANES
    for hd in range(N_DIFF_HEADS):
        if hd % 2 == 0:
            q2 = proj[:, fw + hd * LANES: fw + hd * LANES + pair]
            k2 = proj[:, fw + qw + hd * LANES: fw + qw + hd * LANES + pair]
            qn2 = (q2 * _group_rms_inv(q2, gmat, half) * gq_ref[...]).astype(BF16)
            kn2 = (k2 * _group_rms_inv(k2, gmat, half) * gk_ref[...]).astype(BF16)
        qn = qn2[:, (hd % 2) * LANES:(hd % 2 + 1) * LANES]
        kn = kn2[:, (hd % 2) * LANES:(hd % 2 + 1) * LANES]
        for m in range(2):
            fl = lane - half if m == 0 else lane
            is_data = (lane < half) if m == 0 else (lane >= half)
            f = jnp.where(fl < n_chunks, fl, fl - n_chunks)
            onehot = ((fl >= 0) & (fl < 2 * n_chunks) & (f == chunk)).astype(BF16)
            c0 = (2 * hd + m) * LANES
            q_ref[:, c0:c0 + LANES] = jnp.where(is_data, qn, jnp.zeros_like(qn))
            k_ref[:, c0:c0 + LANES] = jnp.where(is_data, kn, onehot)


def _in_proj(x2, mod3, g_mix, w_in_b, gq, gk, S, tm, t_attn):
    T, D = x2.shape
    ncols = w_in_b.shape[1]
    fw = ncols // 4
    qw = fw
    tiles_per_batch = S // tm
    n_chunks = S // t_attn
    half = LANES // 2
    pair = 2 * LANES
    gmat = (np.arange(pair)[:, None] // half == np.arange(pair)[None, :] // half)
    gmat = jnp.asarray(gmat, BF16)
    kern = functools.partial(_in_proj_kernel, tiles_per_batch=tiles_per_batch, tm=tm, t_attn=t_attn,
                             n_chunks=n_chunks, fw=fw, qw=qw)
    ext = 2 * qw
    return pl.pallas_call(
        kern,
        grid=(T // tm,),
        in_specs=[pl.BlockSpec((tm, D), lambda i: (i, 0)),
                  pl.BlockSpec((None, 6, D), lambda i: (i // tiles_per_batch, 0, 0)),
                  pl.BlockSpec((1, D), lambda i: (0, 0)),
                  pl.BlockSpec((D, ncols), lambda i: (0, 0)),
                  pl.BlockSpec((1, pair), lambda i: (0, 0)),
                  pl.BlockSpec((1, pair), lambda i: (0, 0)),
                  pl.BlockSpec((pair, pair), lambda i: (0, 0))],
        out_specs=[pl.BlockSpec((tm, fw), lambda i: (i, 0)),
                   pl.BlockSpec((tm, ext), lambda i: (i, 0)),
                   pl.BlockSpec((tm, ext), lambda i: (i, 0)),
                   pl.BlockSpec((tm, fw), lambda i: (i, 0))],
        out_shape=[jax.ShapeDtypeStruct((T, fw), BF16),
                   jax.ShapeDtypeStruct((T, ext), BF16),
                   jax.ShapeDtypeStruct((T, ext), BF16),
                   jax.ShapeDtypeStruct((T, fw), BF16)],
        compiler_params=_cparams(("arbitrary",)),
        name="in_proj",
    )(x2, mod3, g_mix, w_in_b, gq, gk, gmat)


def _fourier_kernel(uf_ref, wcs_ref, tw_ref, dft_ref, o_ref, z_sc, tt_sc, *, M, gw):
    c = pl.program_id(1)

    def channel_stage():
        for g in range(N_FOURIER_GROUPS):
            for j in range(DFT_RADIX):
                ab = jnp.dot(uf_ref[j * M:(j + 1) * M, g * gw:(g + 1) * gw], wcs_ref[g],
                             preferred_element_type=F32)
                z_sc[0, j, :, g * gw:(g + 1) * gw] = ab[:, :gw]
                z_sc[1, j, :, g * gw:(g + 1) * gw] = -ab[:, gw:]

    def butterfly(k2):
        cs = tw_ref[:, k2:k2 + 1]
        sn = tw_ref[:, DFT_RADIX + k2:DFT_RADIX + k2 + 1]
        for g in range(N_FOURIER_GROUPS):
            cols = slice(g * gw, (g + 1) * gw)
            zr = [z_sc[0, j, :, cols] for j in range(DFT_RADIX)]
            zi = [z_sc[1, j, :, cols] for j in range(DFT_RADIX)]
            if k2 == 0:
                yr = zr[0] + zr[1] + zr[2] + zr[3]
                yi = zi[0] + zi[1] + zi[2] + zi[3]
            elif k2 == 1:
                yr = zr[0] + zi[1] - zr[2] - zi[3]
                yi = zi[0] - zr[1] - zi[2] + zr[3]
            elif k2 == 2:
                yr = zr[0] - zr[1] + zr[2] - zr[3]
                yi = zi[0] - zi[1] + zi[2] - zi[3]
            else:
                yr = zr[0] - zi[1] - zr[2] + zi[3]
                yi = zi[0] + zr[1] - zi[2] - zr[3]
            tt_sc[k2, 0:M, cols] = (cs * yr + sn * yi).astype(BF16)
            tt_sc[k2, M:2 * M, cols] = (cs * yi - sn * yr).astype(BF16)

    for cc in range(DFT_RADIX):
        @pl.when(c == cc)
        def _(cc=cc):
            if cc == 0:
                channel_stage()
                butterfly(0)
            o_ref[...] = jnp.dot(dft_ref[...], tt_sc[cc], preferred_element_type=F32).astype(BF16)
            if cc + 1 < DFT_RADIX:
                butterfly(cc + 1)


def _fourier(uf3, w_f):
    B, S, W = uf3.shape
    G = N_FOURIER_GROUPS
    gw = W // G
    M = S // DFT_RADIX
    norm = 1.0 / math.sqrt(S * gw)
    cidx = (np.arange(gw)[:, None] * np.arange(gw)[None, :]) % gw
    ang_c = jnp.asarray(cidx, F32) * (2.0 * math.pi / gw)
    cc, sc = jnp.cos(ang_c) * norm, jnp.sin(ang_c) * norm
    hp = lax.Precision.HIGHEST
    wcs = jnp.concatenate([jnp.einsum("ab,gbd->gad", cc, w_f, precision=hp),
                           jnp.einsum("ab,gbd->gad", sc, w_f, precision=hp)], axis=-1).astype(BF16)
    n1 = np.arange(M)
    tw_idx = (np.arange(DFT_RADIX)[None, :] * n1[:, None]) % S
    ang_t = jnp.asarray(tw_idx, F32) * (2.0 * math.pi / S)
    tw = jnp.concatenate([jnp.cos(ang_t), jnp.sin(ang_t),
                          jnp.zeros((M, LANES - 2 * DFT_RADIX), F32)], axis=1)
    pidx = (n1[:, None] * n1[None, :]) % M
    ang_p = jnp.asarray(pidx, F32) * (2.0 * math.pi / M)
    dft = jnp.concatenate([jnp.cos(ang_p), jnp.sin(ang_p)], axis=1).astype(BF16)
    kern = functools.partial(_fourier_kernel, M=M, gw=gw)
    return pl.pallas_call(
        kern,
        grid=(B, DFT_RADIX),
        in_specs=[pl.BlockSpec((None, S, W), lambda b, c: (b, 0, 0)),
                  pl.BlockSpec((G, gw, 2 * gw), lambda b, c: (0, 0, 0)),
                  pl.BlockSpec((M, LANES), lambda b, c: (0, 0)),
                  pl.BlockSpec((M, 2 * M), lambda b, c: (0, 0))],
        out_specs=pl.BlockSpec((None, M, W), lambda b, c: (b, 0, c)),
        out_shape=jax.ShapeDtypeStruct((B, M, DFT_RADIX * W), BF16),
        scratch_shapes=[pltpu.VMEM((2, DFT_RADIX, M, W), F32),
                        pltpu.VMEM((DFT_RADIX, 2 * M, W), BF16)],
        compiler_params=_cparams(("arbitrary", "arbitrary")),
        name="fourier",
    )(uf3, wcs, tw, dft)


def _bias_kernel(rb_ref, o_ref, *, t):
    h = pl.program_id(0)
    which = pl.program_id(1)
    nb = N_BUCKETS // 2

    def tile(d, buckets):
        row = lax.broadcasted_iota(jnp.int32, (t, t), 0)
        col = lax.broadcasted_iota(jnp.int32, (t, t), 1)
        rel = d * t + col - row
        max_exact = nb // 2
        ret = jnp.where(rel > 0, nb, 0)
        n = jnp.abs(rel)
        nf = jnp.maximum(n, 1).astype(F32)
        large = max_exact + (jnp.log(nf / max_exact) / math.log(MAX_DISTANCE / max_exact)
                             * (nb - max_exact)).astype(jnp.int32)
        large = jnp.minimum(large, nb - 1)
        bucket = ret + jnp.where(n < max_exact, n, large)
        val = jnp.zeros((t, t), F32)
        for j in buckets:
            val = jnp.where(bucket == j, rb_ref[j * N_DIFF_HEADS + h], val)
        return val * LOG2E

    @pl.when(which == 0)
    def _():
        o_ref[...] = tile(-1, range(0, nb))

    @pl.when(which == 1)
    def _():
        o_ref[...] = tile(0, range(0, 2 * nb))

    @pl.when(which == 2)
    def _():
        o_ref[...] = tile(1, range(nb, 2 * nb))

    @pl.when(which == 3)
    def _():
        o_ref[...] = jnp.zeros((t, t), F32)


def _bias_tiles(rel_bias, t):
    H = N_DIFF_HEADS
    assert t >= MAX_DISTANCE, "far key tiles must lie entirely in the saturated bucket"
    return pl.pallas_call(
        functools.partial(_bias_kernel, t=t),
        grid=(H, 4),
        in_specs=[pl.BlockSpec(memory_space=pltpu.SMEM)],
        out_specs=pl.BlockSpec((None, None, t, t), lambda h, d: (h, d, 0, 0)),
        out_shape=jax.ShapeDtypeStruct((H, 4, t, t), F32),
        compiler_params=_cparams(("arbitrary", "arbitrary")),
        name="t5_bias",
    )(rel_bias.reshape(-1))


def _far_bucket_consts():
    nb = N_BUCKETS // 2
    return nb - 1, 2 * nb - 1


def _attn_kernel(rb_ref, q_ref, k_ref, v_ref, b_ref, lam_ref, gs_ref, o_ref,
                 s_sc, p_sc, vext_sc, m_sc, raw_sc, *, t, n, n_tiles, lam_init):
    g = pl.program_id(0)
    cur = jnp.minimum(g, n_tiles - 1)
    h = (cur // n) % N_DIFF_HEADS
    i = cur % n
    half = LANES // 2

    @pl.when(g == 0)
    def _():
        raw_sc[...] = jnp.ones(raw_sc.shape, F32)

    @pl.when((i == 0) & (g < n_tiles))
    def _():
        vext_sc[:, :LANES] = v_ref[...]
        vext_sc[:, LANES:] = jnp.ones(v_ref.shape, BF16)

    left_b, right_b = _far_bucket_consts()
    c_left = rb_ref[left_b * N_DIFF_HEADS + h] * LOG2E
    c_right = rb_ref[right_b * N_DIFF_HEADS + h] * LOG2E
    lane = lax.broadcasted_iota(jnp.int32, (1, LANES), 1)

    def all_scores(m):
        fl = lane - half if m == 0 else lane
        is_data = (lane < half) if m == 0 else (lane >= half)
        f = jnp.where(fl < n, fl, fl - n)
        cfar = jnp.where(f < i - 1, c_left, jnp.where(f > i + 1, c_right, 0.0))
        c_hi = cfar.astype(BF16)
        c_lo = (cfar - c_hi.astype(F32)).astype(BF16)
        feat = jnp.where(fl < n, c_hi, c_lo)
        feat = jnp.where((fl >= 0) & (fl < 2 * n), feat, jnp.zeros_like(feat))
        qm = jnp.where(is_data, q_ref[:, m * LANES:(m + 1) * LANES], feat)
        for j in range(n):
            kc = k_ref[j * t:(j + 1) * t, m * LANES:(m + 1) * LANES]
            s = lax.dot_general(qm, kc, (((1,), (1,)), ((), ())), preferred_element_type=F32)
            d = j - i
            s = s + b_ref[jnp.where(jnp.abs(d) <= 1, d + 1, 3)]
            s_sc[m, j] = s
            tile_max = s[:, 0:LANES]
            for c in range(1, t // LANES):
                tile_max = jnp.maximum(tile_max, s[:, c * LANES:(c + 1) * LANES])
            if j == 0:
                m_sc[m] = tile_max
            else:
                m_sc[m] = jnp.maximum(m_sc[m], tile_max)

    def exponentials(m):
        mx = jnp.max(m_sc[m], axis=-1, keepdims=True)
        for j in range(n):
            p_sc[m, :, j * t:(j + 1) * t] = jnp.exp2(s_sc[m, j] - mx).astype(BF16)

    def finish_previous():
        r0 = raw_sc[0]
        r1 = raw_sc[1]
        lam = (jnp.exp(jnp.sum(lam_ref[0:1, :] * lam_ref[1:2, :], axis=-1, keepdims=True))
               - jnp.exp(jnp.sum(lam_ref[2:3, :] * lam_ref[3:4, :], axis=-1, keepdims=True)) + lam_init)
        o = r0[:, :LANES] / r0[:, LANES:LANES + 1] - lam * (r1[:, :LANES] / r1[:, LANES:LANES + 1])
        ms = jnp.mean(o * o, axis=-1, keepdims=True)
        o_ref[...] = (o * lax.rsqrt(ms + EPS) * gs_ref[...] * (1.0 - lam_init)).astype(BF16)

    @pl.when(i < n)
    def _():
        finish_previous()
        all_scores(0)

    @pl.when(g < n_tiles)
    def _():
        all_scores(1)
        exponentials(0)
        raw_sc[0] = jnp.dot(p_sc[0], vext_sc[...], preferred_element_type=F32)
        exponentials(1)
        raw_sc[1] = jnp.dot(p_sc[1], vext_sc[...], preferred_element_type=F32)


def _attention(rel_bias, q3, k3, v3, btiles, lam4, g_subln, t, lam_init):
    B, S, _ = q3.shape
    H = N_DIFF_HEADS
    n = S // t
    n_tiles = B * H * n
    kern = functools.partial(_attn_kernel, t=t, n=n, n_tiles=n_tiles, lam_init=lam_init)

    def tile_of(g):
        return g // (H * n), (g // n) % H, g % n

    def cur_map(fn):
        def index_map(g):
            return fn(*tile_of(jnp.minimum(g, n_tiles - 1)))
        return index_map

    def prev_map(g):
        b, h, i = tile_of(jnp.maximum(g - 1, 0))
        return (b, i, h)

    return pl.pallas_call(
        kern,
        grid=(n_tiles + 1,),
        in_specs=[pl.BlockSpec(memory_space=pltpu.SMEM),
                  pl.BlockSpec((None, t, 2 * LANES), cur_map(lambda b, h, i: (b, i, h))),
                  pl.BlockSpec((None, S, 2 * LANES), cur_map(lambda b, h, i: (b, 0, h))),
                  pl.BlockSpec((None, S, LANES), cur_map(lambda b, h, i: (b, 0, h))),
                  pl.BlockSpec((None, 4, t, t), cur_map(lambda b, h, i: (h, 0, 0, 0))),
                  pl.BlockSpec((4, LANES // 2), lambda g: (0, 0)),
                  pl.BlockSpec((1, LANES), lambda g: (0, 0))],
        out_specs=pl.BlockSpec((None, t, LANES), prev_map),
        out_shape=jax.ShapeDtypeStruct((B, S, H * LANES), BF16),
        scratch_shapes=[pltpu.VMEM((2, n, t, t), F32),
                        pltpu.VMEM((2, t, S), BF16),
                        pltpu.VMEM((S, 2 * LANES), BF16),
                        pltpu.VMEM((2, t, LANES), F32),
                        pltpu.VMEM((2, t, 2 * LANES), F32)],
        compiler_params=_cparams(("arbitrary",)),
        name="diff_attn",
    )(rel_bias.reshape(-1), q3, k3, v3, btiles, lam4, g_subln)


def _pack_bf16_pairs(v):
    bits = lax.bitcast_convert_type(v.astype(BF16).astype(F32), jnp.uint32)
    n = bits.shape[1] // 2
    return (bits[:, :n] >> 16) | (bits[:, n:] & jnp.uint32(0xFFFF0000))


def _unpack_bf16_pairs(p):
    lo = lax.bitcast_convert_type(p << 16, F32).astype(BF16)
    hi = lax.bitcast_convert_type(p & jnp.uint32(0xFFFF0000), F32).astype(BF16)
    return lo, hi


def _out_proj_kernel(yf_ref, perm_ref, yd_ref, x_ref, mod_ref, wo_ref, g_ref, wr_ref, br_ref,
                     x1_ref, hp_ref, lp_ref, grow_ref, cnt_ref, *, tm, fw, E):
    yf_stack = jnp.concatenate([yf_ref[:, r * fw:(r + 1) * fw] for r in range(DFT_RADIX)], axis=0)
    yf = jnp.dot(perm_ref[...], yf_stack, preferred_element_type=F32).astype(BF16)
    mix = (jnp.dot(yf, wo_ref[:fw, :], preferred_element_type=F32)
           + jnp.dot(yd_ref[...], wo_ref[fw:, :], preferred_element_type=F32))
    x1 = x_ref[...] + mod_ref[2:3, :] * mix
    x1_ref[...] = x1
    ms = jnp.mean(x1 * x1, axis=-1, keepdims=True)
    h = x1 * lax.rsqrt(ms + EPS) * g_ref[...]
    h = h * (1.0 + mod_ref[4:5, :]) + mod_ref[3:4, :]
    hb = h.astype(BF16)
    hp_ref[...] = _pack_bf16_pairs(hb)

    logits = lax.dot_general(wr_ref[...], hb, (((1,), (1,)), ((), ())),
                             preferred_element_type=F32) + br_ref[...]
    rowid = lax.broadcasted_iota(jnp.int32, (E, tm), 0)
    vals, idxs = [], []
    comb = jnp.zeros((E, tm), F32)
    l = logits
    for _ in range(TOP_K):
        mv = jnp.max(l, axis=0, keepdims=True)
        ix = jnp.min(jnp.where(l == mv, rowid, E), axis=0, keepdims=True)
        sel = rowid == ix
        vals.append(mv)
        idxs.append(ix)
        comb = comb + sel.astype(F32)
        l = jnp.where(sel, -jnp.inf, l)
    es = [jnp.exp(v - vals[0]) for v in vals]
    den = es[0] + es[1] + es[2] + es[3]
    gates = [e / den for e in es]

    r_i = lax.broadcasted_iota(jnp.int32, (tm, tm), 0)
    c_i = lax.broadcasted_iota(jnp.int32, (tm, tm), 1)
    upper = (r_i < c_i).astype(BF16)
    before = jnp.dot(comb.astype(BF16), upper, preferred_element_type=F32)
    cnt = jnp.sum(comb, axis=1, keepdims=True)
    cnt_ref[...] = jnp.broadcast_to(cnt, cnt_ref.shape).astype(jnp.int32)
    chunks = jnp.floor((cnt + (ROW_CHUNK - 1)) * (1.0 / ROW_CHUNK))
    e_r = lax.broadcasted_iota(jnp.int32, (E, E), 0)
    e_c = lax.broadcasted_iota(jnp.int32, (E, E), 1)
    run_start = ROW_CHUNK * jnp.dot((e_c < e_r).astype(BF16),
                                    jnp.broadcast_to(chunks, (E, LANES)).astype(BF16),
                                    preferred_element_type=F32)[:, 0:1]
    ranks = [jnp.sum(jnp.where(rowid == ix, before + run_start, 0.0), axis=0, keepdims=True) for ix in idxs]

    pad_i = jnp.full((8 - TOP_K, tm), -1, jnp.int32)
    lp_ref[...] = jnp.concatenate([r.astype(jnp.int32) for r in ranks] + [pad_i], axis=0)
    gpad = jnp.concatenate(gates + [jnp.zeros((LANES - TOP_K, tm), F32)], axis=0)
    grow_ref[...] = gpad.T


def _out_proj(yf3, yd2, x2, mod3, w_out_b, g_ffn, wr_t, br_t, S, tm):
    T, D = x2.shape
    fw = yd2.shape[1]
    E = wr_t.shape[0]
    nt = T // tm
    tiles_per_batch = S // tm
    q = tm // DFT_RADIX
    col = np.arange(tm)
    perm = jnp.asarray(np.arange(tm)[:, None] == DFT_RADIX * (col % q) + col // q, BF16)
    kern = functools.partial(_out_proj_kernel, tm=tm, fw=fw, E=E)
    return pl.pallas_call(
        kern,
        grid=(nt,),
        in_specs=[pl.BlockSpec((None, q, DFT_RADIX * fw), lambda i: (i // tiles_per_batch, i % tiles_per_batch, 0)),
                  pl.BlockSpec((tm, tm), lambda i: (0, 0)),
                  pl.BlockSpec((tm, fw), lambda i: (i, 0)),
                  pl.BlockSpec((tm, D), lambda i: (i, 0)),
                  pl.BlockSpec((None, 6, D), lambda i: (i // tiles_per_batch, 0, 0)),
                  pl.BlockSpec((D, D), lambda i: (0, 0)),
                  pl.BlockSpec((1, D), lambda i: (0, 0)),
                  pl.BlockSpec((E, D), lambda i: (0, 0)),
                  pl.BlockSpec((E, 1), lambda i: (0, 0))],
        out_specs=[pl.BlockSpec((tm, D), lambda i: (i, 0)),
                   pl.BlockSpec((tm, D // 2), lambda i: (i, 0)),
                   pl.BlockSpec((8, tm), lambda i: (0, i)),
                   pl.BlockSpec((tm, LANES), lambda i: (i, 0)),
                   pl.BlockSpec((None, E, LANES), lambda i: (i, 0, 0))],
        out_shape=[jax.ShapeDtypeStruct((T, D), F32),
                   jax.ShapeDtypeStruct((T, D // 2), jnp.uint32),
                   jax.ShapeDtypeStruct((8, T), jnp.int32),
                   jax.ShapeDtypeStruct((T, LANES), F32),
                   jax.ShapeDtypeStruct((nt, E, LANES), jnp.int32)],
        compiler_params=_cparams(("arbitrary",)),
        name="out_proj_router",
    )(yf3, perm, yd2, x2, mod3, w_out_b, g_ffn, wr_t, br_t)


ROW_CHUNK = 8
CHUNK_UNROLL = 8


def _dispatch_kernel(tbl_ref, zb_ref, lp_ref, hp_ref, xs_ref, buf, zero_sc, sem, zsem, tsem,
                     *, tm, bm, NR, nz, nz_first, nt):
    i = pl.program_id(0)
    s = i % 2

    def zero_copy(e, zs):
        return pltpu.make_async_copy(zero_sc, xs_ref.at[pl.ds(zb_ref[e] * bm, bm)], zs)

    @pl.when(i == 0)
    def _():
        zero_sc[...] = jnp.zeros(zero_sc.shape, jnp.uint32)
        for e in range(nz):
            @pl.when(zb_ref[e] >= 0)
            def _():
                zero_copy(e, zsem if e < nz_first else tsem).start()
        for e in range(nz_first):
            @pl.when(zb_ref[e] >= 0)
            def _():
                zero_copy(e, zsem).wait()

    @pl.when(i == nt - 1)
    def _():
        for e in range(nz_first, nz):
            @pl.when(zb_ref[e] >= 0)
            def _():
                zero_copy(e, tsem).wait()

    lp = lp_ref[...]
    rows = lax.broadcasted_iota(jnp.int32, (NR, tm), 0)
    hit = rows == lp[0:1, :]
    for k in range(1, TOP_K):
        hit = hit | (rows == lp[k:k + 1, :])
    sel = jnp.where(hit, 1.0, 0.0).astype(BF16)
    lo, hi = _unpack_bf16_pairs(hp_ref[...])
    slo = jnp.dot(sel, lo, preferred_element_type=F32)
    shi = jnp.dot(sel, hi, preferred_element_type=F32)
    buf[s] = ((lax.bitcast_convert_type(slo, jnp.uint32) >> 16)
              | (lax.bitcast_convert_type(shi, jnp.uint32) & jnp.uint32(0xFFFF0000)))

    for slot in range(2):
        def issue(c, carry, slot=slot):
            d = pl.multiple_of(tbl_ref[0, 0, c], ROW_CHUNK)
            r = pl.multiple_of(c * ROW_CHUNK, ROW_CHUNK)
            pltpu.make_async_copy(buf.at[slot, pl.ds(r, ROW_CHUNK)], xs_ref.at[pl.ds(d, ROW_CHUNK)],
                                  sem.at[slot]).start()
            return carry

        @pl.when(s == slot)
        def _():
            lax.fori_loop(0, NR // ROW_CHUNK, issue, 0, unroll=CHUNK_UNROLL)

    def wait_slot(slot):
        pltpu.make_async_copy(buf.at[slot], xs_ref.at[pl.ds(0, NR)], sem.at[slot]).wait()

    @pl.when(i > 0)
    def _():
        wait_slot(1 - s)

    @pl.when(i == nt - 1)
    def _():
        wait_slot(s)


def _dispatch(tbl3, zero_blk, nz_first, lp8, hp, P, bm, tm, NR):
    T, Wd = hp.shape
    nt = T // tm
    nz = zero_blk.shape[0]
    nch_tbl = tbl3.shape[2]
    kern = functools.partial(_dispatch_kernel, tm=tm, bm=bm, NR=NR, nz=nz, nz_first=nz_first, nt=nt)
    return pl.pallas_call(
        kern,
        grid=(nt,),
        in_specs=[pl.BlockSpec((1, 1, nch_tbl), lambda i: (i, 0, 0), memory_space=pltpu.SMEM),
                  pl.BlockSpec(memory_space=pltpu.SMEM),
                  pl.BlockSpec((8, tm), lambda i: (0, i)),
                  pl.BlockSpec((tm, Wd), lambda i: (i, 0))],
        out_specs=pl.BlockSpec(memory_space=pl.ANY),
        out_shape=jax.ShapeDtypeStruct((P, Wd), jnp.uint32),
        scratch_shapes=[pltpu.VMEM((2, NR, Wd), jnp.uint32),
                        pltpu.VMEM((bm, Wd), jnp.uint32),
                        pltpu.SemaphoreType.DMA((2,)),
                        pltpu.SemaphoreType.DMA,
                        pltpu.SemaphoreType.DMA],
        compiler_params=_cparams(("arbitrary",)),
        name="moe_dispatch",
    )(tbl3, zero_blk, lp8, hp)


def _expert_kernel(be_ref, nu_ref, h2_ref, x_ref, w1_ref, b1g_ref, b1l_ref, w2_ref, b2_ref, pm_ref, o_ref,
                   w1g_sc, w1l_sc, w2_sc):
    blk = pl.program_id(0)
    nused = nu_ref[0]
    e = be_ref[blk]
    prev = be_ref[jnp.maximum(blk - 1, 0)]
    active = blk < nused
    second = active & (h2_ref[blk] != 0)
    half = x_ref.shape[0] // 2
    tile = 2 * LANES
    n_tiles = w1_ref.shape[1] // tile

    @pl.when(active & ((blk == 0) | (e != prev)))
    def _():
        for c in range(n_tiles):
            wt = w1_ref[:, c * tile:(c + 1) * tile].astype(BF16)
            pw = jnp.dot(wt, pm_ref[...], preferred_element_type=F32)
            w1g_sc[:, c * LANES:(c + 1) * LANES] = pw[:, :LANES].astype(BF16)
            w1l_sc[:, c * LANES:(c + 1) * LANES] = pw[:, LANES:].astype(BF16)
        w2_sc[...] = w2_ref[...].astype(BF16)

    @pl.when(jnp.logical_not(active))
    def _():
        o_ref[...] = jnp.zeros(o_ref.shape, jnp.uint32)

    def mlp(r0):
        xp = x_ref[r0:r0 + half, :]
        hw = xp.shape[1]
        xlo, xhi = _unpack_bf16_pairs(xp)
        zg = (jnp.dot(xlo, w1g_sc[:hw, :], preferred_element_type=F32)
              + jnp.dot(xhi, w1g_sc[hw:, :], preferred_element_type=F32) + b1g_ref[...])
        zl = (jnp.dot(xlo, w1l_sc[:hw, :], preferred_element_type=F32)
              + jnp.dot(xhi, w1l_sc[hw:, :], preferred_element_type=F32) + b1l_ref[...])
        g = jnp.minimum(zg, SWIGLU_LIMIT)
        lin = jnp.clip(zl, -SWIGLU_LIMIT, SWIGLU_LIMIT)
        act = g * jax.nn.sigmoid(SWIGLU_ALPHA * g) * (lin + 1.0)
        y = jnp.dot(act.astype(BF16), w2_sc[...], preferred_element_type=F32) + b2_ref[...]
        o_ref[r0:r0 + half, :] = _pack_bf16_pairs(y)

    @pl.when(second)
    def _():
        mlp(0)
        mlp(half)

    @pl.when(active & jnp.logical_not(second))
    def _():
        mlp(0)
        o_ref[half:, :] = jnp.zeros((half, o_ref.shape[1]), jnp.uint32)


def _experts(blk_expert, nused, half2, xs, P, w1, b1g, b1l, w2, b2, bm):
    Wd = xs.shape[1]
    E, D, F2 = w1.shape
    F = F2 // 2
    nblk = P // bm
    tile = 2 * LANES
    pm = np.zeros((tile, tile), np.float32)
    pm[2 * np.arange(LANES), np.arange(LANES)] = 1.0
    pm[2 * np.arange(LANES) + 1, LANES + np.arange(LANES)] = 1.0
    pm = jnp.asarray(pm, BF16)

    def row_map(b, be, nu, h2):
        return (jnp.minimum(b, nu[0] - 1), 0)

    def exp_map(b, be, nu, h2):
        return (be[b], 0, 0)

    grid_spec = pltpu.PrefetchScalarGridSpec(
        num_scalar_prefetch=3,
        grid=(nblk,),
        in_specs=[pl.BlockSpec((bm, Wd), row_map),
                  pl.BlockSpec((None, D, F2), exp_map),
                  pl.BlockSpec((None, 1, F), exp_map),
                  pl.BlockSpec((None, 1, F), exp_map),
                  pl.BlockSpec((None, F, D), exp_map),
                  pl.BlockSpec((None, 1, D), exp_map),
                  pl.BlockSpec((tile, tile), lambda b, be, nu, h2: (0, 0))],
        out_specs=pl.BlockSpec((bm, D // 2), lambda b, be, nu, h2: (b, 0)),
        scratch_shapes=[pltpu.VMEM((D, F), BF16),
                        pltpu.VMEM((D, F), BF16),
                        pltpu.VMEM((F, D), BF16)],
    )
    return pl.pallas_call(
        _expert_kernel,
        grid_spec=grid_spec,
        out_shape=jax.ShapeDtypeStruct((P, D // 2), jnp.uint32),
        compiler_params=_cparams(("arbitrary",)),
        name="moe_experts",
    )(blk_expert, nused, half2, xs, w1, b1g, b1l, w2, b2, pm)


def _combine_kernel(tcur_ref, tnxt_ref, lp_ref, grow_ref, x1_ref, mod_ref, y_ref, o_ref,
                    buf, sem, *, tm, NR, nt):
    i = pl.program_id(0)
    s = i % 2

    def issue(tref, slot):
        def body(c, carry):
            d = pl.multiple_of(tref[0, 0, c], ROW_CHUNK)
            r = pl.multiple_of(c * ROW_CHUNK, ROW_CHUNK)
            pltpu.make_async_copy(y_ref.at[pl.ds(d, ROW_CHUNK)], buf.at[slot, pl.ds(r, ROW_CHUNK)],
                                  sem.at[slot]).start()
            return carry
        lax.fori_loop(0, NR // ROW_CHUNK, body, 0, unroll=CHUNK_UNROLL)

    @pl.when(i == 0)
    def _():
        issue(tcur_ref, 0)

    for slot in range(2):
        @pl.when((i + 1 < nt) & (s != slot))
        def _():
            issue(tnxt_ref, slot)

    pltpu.make_async_copy(y_ref.at[pl.ds(0, NR)], buf.at[s], sem.at[s]).wait()

    ylo, yhi = _unpack_bf16_pairs(buf[s])
    lpf = jnp.concatenate([lp_ref[...].astype(F32), jnp.zeros((LANES - 8, tm), F32)], axis=0)
    lpr = lpf.T.astype(jnp.int32)
    gr = grow_ref[...]
    cols = lax.broadcasted_iota(jnp.int32, (tm, NR), 1)
    gmat = jnp.zeros((tm, NR), F32)
    for k in range(TOP_K):
        gmat = jnp.where(cols == lpr[:, k:k + 1], gr[:, k:k + 1], gmat)
    gmat = gmat.astype(BF16)
    hw = ylo.shape[1]
    o_ref[:, :hw] = x1_ref[:, :hw] + mod_ref[5:6, :hw] * jnp.dot(gmat, ylo, preferred_element_type=F32)
    o_ref[:, hw:] = x1_ref[:, hw:] + mod_ref[5:6, hw:] * jnp.dot(gmat, yhi, preferred_element_type=F32)


def _combine(tbl3, lp8, grow, x1, mod3, y, S, tm, NR):
    T, D = x1.shape
    nt = T // tm
    tiles_per_batch = S // tm
    nch_tbl = tbl3.shape[2]
    kern = functools.partial(_combine_kernel, tm=tm, NR=NR, nt=nt)
    return pl.pallas_call(
        kern,
        grid=(nt,),
        in_specs=[pl.BlockSpec((1, 1, nch_tbl), lambda i: (i, 0, 0), memory_space=pltpu.SMEM),
                  pl.BlockSpec((1, 1, nch_tbl), lambda i: (jnp.minimum(i + 1, nt - 1), 0, 0),
                               memory_space=pltpu.SMEM),
                  pl.BlockSpec((8, tm), lambda i: (0, i)),
                  pl.BlockSpec((tm, LANES), lambda i: (i, 0)),
                  pl.BlockSpec((tm, D), lambda i: (i, 0)),
                  pl.BlockSpec((None, 6, D), lambda i: (i // tiles_per_batch, 0, 0)),
                  pl.BlockSpec(memory_space=pl.ANY)],
        out_specs=pl.BlockSpec((tm, D), lambda i: (i, 0)),
        out_shape=jax.ShapeDtypeStruct((T, D), F32),
        scratch_shapes=[pltpu.VMEM((2, NR, D // 2), jnp.uint32),
                        pltpu.SemaphoreType.DMA((2,))],
        compiler_params=_cparams(("arbitrary",)),
        name="moe_combine",
    )(tbl3, tbl3, lp8, grow, x1, mod3, y)


def _tiles(S):
    t_attn = min(512, S // 4) if S >= 512 else S
    tm = min(512, t_attn)
    return dict(tm=tm, t_attn=t_attn, bm=1024)


def _moe_layout(cnt, T, tm, bm):
    nt, E = cnt.shape
    c8 = (cnt + ROW_CHUNK - 1) // ROW_CHUNK * ROW_CHUNK
    tile_end = jnp.cumsum(c8, axis=1)
    tile_off = tile_end - c8
    used = tile_end[:, -1]
    off = jnp.cumsum(c8, axis=0) - c8
    tot = jnp.sum(c8, axis=0)
    padded = (tot + bm - 1) // bm * bm
    pends = jnp.cumsum(padded)
    pstarts = pends - padded
    NR = TOP_K * tm + ROW_CHUNK * E
    nch_tbl = NR // ROW_CHUNK
    P = (T * TOP_K + nt * E * (ROW_CHUNK - 1) + E * bm + bm - 1) // bm * bm
    nblk = P // bm

    c_row = jnp.arange(nch_tbl, dtype=jnp.int32) * ROW_CHUNK
    inrun = (tile_off[:, :, None] <= c_row) & (c_row < tile_end[:, :, None])
    base = pstarts[None, :] + off - tile_off
    tbl = jnp.sum(jnp.where(inrun, base[:, :, None], 0), axis=1) + c_row[None, :]
    in_use = c_row[None, :] < used[:, None]
    spare_rows = NR - TOP_K * tm
    spare = P + jnp.arange(nt, dtype=jnp.int32)[:, None] * spare_rows + (c_row[None, :] - used[:, None])
    tbl_out = jnp.where(in_use, tbl, spare).astype(jnp.int32)
    tbl_in = jnp.where(in_use, tbl, 0).astype(jnp.int32)
    n_spare_blk = (nt * spare_rows + bm - 1) // bm

    nused = (pends[-1] // bm).astype(jnp.int32).reshape(1)
    blk_start = jnp.arange(nblk, dtype=jnp.int32) * bm
    blk_expert = jnp.minimum(jnp.sum(pends[None, :] <= blk_start[:, None], axis=1), E - 1).astype(jnp.int32)
    half2 = (blk_start + bm // 2 < (pstarts + tot)[blk_expert]).astype(jnp.int32)
    last_blk = jnp.where(padded > 0, pends // bm - 1, -1)
    n_tail = nblk - (T * TOP_K) // bm
    tail_blk = nused[0] + jnp.arange(n_tail, dtype=jnp.int32)
    tail_blk = jnp.where(tail_blk < nblk, tail_blk, -1)
    spare_blk = nblk + jnp.arange(n_spare_blk, dtype=jnp.int32)
    zero_blk = jnp.concatenate([last_blk, spare_blk, tail_blk]).astype(jnp.int32)
    return dict(tbl_out=tbl_out.reshape(nt, 1, nch_tbl), tbl_in=tbl_in.reshape(nt, 1, nch_tbl),
                nused=nused, blk_expert=blk_expert, half2=half2, zero_blk=zero_blk,
                nz_first=E + n_spare_blk,
                P=P, P_alloc=P + n_spare_blk * bm, NR=NR)


def kernel(x, c, w_ada, b_ada, g_mix, w_in, w_fourier, q_norm_g, k_norm_g, lambda_q1, lambda_k1,
           lambda_q2, lambda_k2, g_subln, w_out, rel_bias, g_ffn, w_router, b_router, w1, b1, w2, b2):
    B, S, D = x.shape
    T = B * S
    L = w_ada.shape[0]
    E = w_router.shape[-1]
    cfg = _tiles(S)
    tm, t_attn, bm = cfg["tm"], cfg["t_attn"], cfg["bm"]
    dqk = q_norm_g.shape[-1]
    scale = dqk ** -0.5
    fw = w_in.shape[-1] // 4

    btiles = _bias_tiles(rel_bias, t_attn)
    x2 = x.reshape(T, D)
    for l in range(L):
        lam_init = 0.8 - 0.6 * math.exp(-0.3 * l)
        mod3 = _ada(c, w_ada[l], b_ada[l]).reshape(B, 6, D)
        gq = (jnp.tile(q_norm_g[l], 4) * (scale * LOG2E)).reshape(1, 2 * LANES)
        gk = jnp.tile(k_norm_g[l], 4).reshape(1, 2 * LANES)
        uf, qx, kx, v = _in_proj(x2, mod3, g_mix[l].reshape(1, D), w_in[l].astype(BF16), gq, gk,
                                 S, tm, t_attn)
        yf = _fourier(uf.reshape(B, S, fw), w_fourier[l])
        lam4 = jnp.stack([lambda_q1[l], lambda_k1[l], lambda_q2[l], lambda_k2[l]]).astype(F32)
        yd = _attention(rel_bias, qx.reshape(B, S, -1), kx.reshape(B, S, -1), v.reshape(B, S, fw),
                        btiles, lam4, g_subln[l].reshape(1, LANES), t_attn, lam_init).reshape(T, fw)
        x1, hp, lp8, grow, cnt3 = _out_proj(
            yf, yd, x2, mod3, w_out[l].astype(BF16), g_ffn[l].reshape(1, D),
            w_router[l].T.astype(BF16), b_router[l].reshape(E, 1), S, tm)
        lay = _moe_layout(cnt3[:, :, 0], T, tm, bm)
        xs = _dispatch(lay["tbl_out"], lay["zero_blk"], lay["nz_first"], lp8, hp, lay["P_alloc"], bm, tm,
                       lay["NR"])
        y = _experts(lay["blk_expert"], lay["nused"], lay["half2"], xs, lay["P"], w1[l], b1[l][:, None, 0::2],
                     b1[l][:, None, 1::2], w2[l], b2[l][:, None, :], bm)
        x2 = _combine(lay["tbl_in"], lp8, grow, x1, mod3, y, S, tm, lay["NR"])
    return x2.reshape(B, S, D)
```

```python
import functools
import math

import numpy as np
import jax
import jax.numpy as jnp
from jax import lax
from jax.experimental import pallas as pl
from jax.experimental.pallas import tpu as pltpu

F32 = jnp.float32
BF16 = jnp.bfloat16

EPS = 1e-6
N_FOURIER_GROUPS = 4
N_DIFF_HEADS = 4
TOP_K = 4
N_BUCKETS = 32
MAX_DISTANCE = 128
SWIGLU_ALPHA = 1.702
SWIGLU_LIMIT = 7.0
LANES = 128
DFT_RADIX = 4
LOG2E = 1.4426950408889634
VMEM_LIMIT = 56 * 1024 * 1024


def _cparams(sem):
    return pltpu.CompilerParams(dimension_semantics=sem, vmem_limit_bytes=VMEM_LIMIT)


def _ada_kernel(c_ref, w_ref, b_ref, o_ref):
    c = c_ref[...]
    cond = c * jax.nn.sigmoid(c)
    o_ref[...] = jnp.dot(cond.astype(BF16), w_ref[...].astype(BF16),
                         preferred_element_type=F32) + b_ref[...]


def _ada(c, w, b):
    B, D = c.shape
    N = w.shape[1]
    tn = 1536 if N % 1536 == 0 else N
    return pl.pallas_call(
        _ada_kernel,
        grid=(N // tn,),
        in_specs=[pl.BlockSpec((B, D), lambda j: (0, 0)),
                  pl.BlockSpec((D, tn), lambda j: (0, j)),
                  pl.BlockSpec((1, tn), lambda j: (0, j))],
        out_specs=pl.BlockSpec((B, tn), lambda j: (0, j)),
        out_shape=jax.ShapeDtypeStruct((B, N), F32),
        compiler_params=_cparams(("arbitrary",)),
        name="ada",
    )(c, w, b.reshape(1, N))


def _group_rms_inv(xh, gmat, group):
    ss = jnp.dot((xh * xh).astype(BF16), gmat, preferred_element_type=F32)
    return lax.rsqrt(ss * (1.0 / group) + EPS)


def _in_proj_kernel(x_ref, mod_ref, g_ref, w_ref, gq_ref, gk_ref, gmat_ref,
                    uf_ref, q_ref, k_ref, v_ref, *, tiles_per_batch, tm, t_attn, n_chunks, fw, qw):
    i = pl.program_id(0)
    x = x_ref[...]
    ms = jnp.mean(x * x, axis=-1, keepdims=True)
    y = x * lax.rsqrt(ms + EPS) * g_ref[...]
    h = y * (1.0 + mod_ref[1:2, :]) + mod_ref[0:1, :]
    proj = jnp.dot(h.astype(BF16), w_ref[...], preferred_element_type=F32)
    uf_ref[...] = proj[:, :fw].astype(BF16)
    v_ref[...] = proj[:, fw + 2 * qw:].astype(BF16)

    lane = lax.broadcasted_iota(jnp.int32, (1, LANES), 1)
    chunk = ((i % tiles_per_batch) * tm) // t_attn
    gmat = gmat_ref[...]
    half = LANES // 2
    pair = 2 * LANES
    for hd in range(N_DIFF_HEADS):
        if hd % 2 == 0:
            q2 = proj[:, fw + hd * LANES: fw + hd * LANES + pair]
            k2 = proj[:, fw + qw + hd * LANES: fw + qw + hd * LANES + pair]
            qn2 = (q2 * _group_rms_inv(q2, gmat, half) * gq_ref[...]).astype(BF16)
            kn2 = (k2 * _group_rms_inv(k2, gmat, half) * gk_ref[...]).astype(BF16)
        qn = qn2[:, (hd % 2) * LANES:(hd % 2 + 1) * LANES]
        kn = kn2[:, (hd % 2) * LANES:(hd % 2 + 1) * LANES]
        for m in range(2):
            fl = lane - half if m == 0 else lane
            is_data = (lane < half) if m == 0 else (lane >= half)
            f = jnp.where(fl < n_chunks, fl, fl - n_chunks)
            onehot = ((fl >= 0) & (fl < 2 * n_chunks) & (f == chunk)).astype(BF16)
            c0 = (2 * hd + m) * LANES
            q_ref[:, c0:c0 + LANES] = jnp.where(is_data, qn, jnp.zeros_like(qn))
            k_ref[:, c0:c0 + LANES] = jnp.where(is_data, kn, onehot)


def _in_proj(x2, mod3, g_mix, w_in_b, gq, gk, S, tm, t_attn):
    T, D = x2.shape
    ncols = w_in_b.shape[1]
    fw = ncols // 4
    qw = fw
    tiles_per_batch = S // tm
    n_chunks = S // t_attn
    half = LANES // 2
    pair = 2 * LANES
    gmat = (np.arange(pair)[:, None] // half == np.arange(pair)[None, :] // half)
    gmat = jnp.asarray(gmat, BF16)
    kern = functools.partial(_in_proj_kernel, tiles_per_batch=tiles_per_batch, tm=tm, t_attn=t_attn,
                             n_chunks=n_chunks, fw=fw, qw=qw)
    ext = 2 * qw
    return pl.pallas_call(
        kern,
        grid=(T // tm,),
        in_specs=[pl.BlockSpec((tm, D), lambda i: (i, 0)),
                  pl.BlockSpec((None, 6, D), lambda i: (i // tiles_per_batch, 0, 0)),
                  pl.BlockSpec((1, D), lambda i: (0, 0)),
                  pl.BlockSpec((D, ncols), lambda i: (0, 0)),
                  pl.BlockSpec((1, pair), lambda i: (0, 0)),
                  pl.BlockSpec((1, pair), lambda i: (0, 0)),
                  pl.BlockSpec((pair, pair), lambda i: (0, 0))],
        out_specs=[pl.BlockSpec((tm, fw), lambda i: (i, 0)),
                   pl.BlockSpec((tm, ext), lambda i: (i, 0)),
                   pl.BlockSpec((tm, ext), lambda i: (i, 0)),
                   pl.BlockSpec((tm, fw), lambda i: (i, 0))],
        out_shape=[jax.ShapeDtypeStruct((T, fw), BF16),
                   jax.ShapeDtypeStruct((T, ext), BF16),
                   jax.ShapeDtypeStruct((T, ext), BF16),
                   jax.ShapeDtypeStruct((T, fw), BF16)],
        compiler_params=_cparams(("arbitrary",)),
        name="in_proj",
    )(x2, mod3, g_mix, w_in_b, gq, gk, gmat)


def _fourier_kernel(uf_ref, wcs_ref, tw_ref, dft_ref, o_ref, z_sc, tt_sc, *, M, gw):
    c = pl.program_id(1)

    def channel_stage():
        for g in range(N_FOURIER_GROUPS):
            for j in range(DFT_RADIX):
                ab = jnp.dot(uf_ref[j * M:(j + 1) * M, g * gw:(g + 1) * gw], wcs_ref[g],
                             preferred_element_type=F32)
                z_sc[0, j, :, g * gw:(g + 1) * gw] = ab[:, :gw]
                z_sc[1, j, :, g * gw:(g + 1) * gw] = -ab[:, gw:]

    def butterfly(k2):
        cs = tw_ref[:, k2:k2 + 1]
        sn = tw_ref[:, DFT_RADIX + k2:DFT_RADIX + k2 + 1]
        for g in range(N_FOURIER_GROUPS):
            cols = slice(g * gw, (g + 1) * gw)
            zr = [z_sc[0, j, :, cols] for j in range(DFT_RADIX)]
            zi = [z_sc[1, j, :, cols] for j in range(DFT_RADIX)]
            if k2 == 0:
                yr = zr[0] + zr[1] + zr[2] + zr[3]
                yi = zi[0] + zi[1] + zi[2] + zi[3]
            elif k2 == 1:
                yr = zr[0] + zi[1] - zr[2] - zi[3]
                yi = zi[0] - zr[1] - zi[2] + zr[3]
            elif k2 == 2:
                yr = zr[0] - zr[1] + zr[2] - zr[3]
                yi = zi[0] - zi[1] + zi[2] - zi[3]
            else:
                yr = zr[0] - zi[1] - zr[2] + zi[3]
                yi = zi[0] + zr[1] - zi[2] - zr[3]
            tt_sc[k2, 0:M, cols] = (cs * yr + sn * yi).astype(BF16)
            tt_sc[k2, M:2 * M, cols] = (cs * yi - sn * yr).astype(BF16)

    for cc in range(DFT_RADIX):
        @pl.when(c == cc)
        def _(cc=cc):
            if cc == 0:
                channel_stage()
                butterfly(0)
            o_ref[...] = jnp.dot(dft_ref[...], tt_sc[cc], preferred_element_type=F32).astype(BF16)
            if cc + 1 < DFT_RADIX:
                butterfly(cc + 1)


def _fourier(uf3, w_f):
    B, S, W = uf3.shape
    G = N_FOURIER_GROUPS
    gw = W // G
    M = S // DFT_RADIX
    norm = 1.0 / math.sqrt(S * gw)
    cidx = (np.arange(gw)[:, None] * np.arange(gw)[None, :]) % gw
    ang_c = cidx * (2.0 * math.pi / gw)
    cc = jnp.asarray(np.cos(ang_c) * norm, F32)
    sc = jnp.asarray(np.sin(ang_c) * norm, F32)
    hp = lax.Precision.HIGHEST
    wcs = jnp.concatenate([jnp.einsum("ab,gbd->gad", cc, w_f, precision=hp),
                           jnp.einsum("ab,gbd->gad", sc, w_f, precision=hp)], axis=-1).astype(BF16)
    n1 = np.arange(M)
    ang_t = ((np.arange(DFT_RADIX)[None, :] * n1[:, None]) % S) * (2.0 * math.pi / S)
    tw = jnp.asarray(np.concatenate([np.cos(ang_t), np.sin(ang_t),
                                     np.zeros((M, LANES - 2 * DFT_RADIX))], axis=1), F32)
    ang_p = ((n1[:, None] * n1[None, :]) % M) * (2.0 * math.pi / M)
    dft = jnp.asarray(np.concatenate([np.cos(ang_p), np.sin(ang_p)], axis=1).astype(np.float32)
                      .astype(jnp.bfloat16))
    kern = functools.partial(_fourier_kernel, M=M, gw=gw)
    return pl.pallas_call(
        kern,
        grid=(B, DFT_RADIX),
        in_specs=[pl.BlockSpec((None, S, W), lambda b, c: (b, 0, 0)),
                  pl.BlockSpec((G, gw, 2 * gw), lambda b, c: (0, 0, 0)),
                  pl.BlockSpec((M, LANES), lambda b, c: (0, 0)),
                  pl.BlockSpec((M, 2 * M), lambda b, c: (0, 0))],
        out_specs=pl.BlockSpec((None, M, W), lambda b, c: (b, 0, c)),
        out_shape=jax.ShapeDtypeStruct((B, M, DFT_RADIX * W), BF16),
        scratch_shapes=[pltpu.VMEM((2, DFT_RADIX, M, W), F32),
                        pltpu.VMEM((DFT_RADIX, 2 * M, W), BF16)],
        compiler_params=_cparams(("arbitrary", "arbitrary")),
        name="fourier",
    )(uf3, wcs, tw, dft)


def _bias_kernel(rb_ref, o_ref, *, t):
    h = pl.program_id(0)
    which = pl.program_id(1)
    nb = N_BUCKETS // 2

    def tile(d, buckets, r0=0, c0=0, nr=t, nc=t):
        row = r0 + lax.broadcasted_iota(jnp.int32, (nr, nc), 0)
        col = c0 + lax.broadcasted_iota(jnp.int32, (nr, nc), 1)
        rel = d * t + col - row
        max_exact = nb // 2
        ret = jnp.where(rel > 0, nb, 0)
        n = jnp.abs(rel)
        nf = jnp.maximum(n, 1).astype(F32)
        large = max_exact + (jnp.log(nf / max_exact) / math.log(MAX_DISTANCE / max_exact)
                             * (nb - max_exact)).astype(jnp.int32)
        large = jnp.minimum(large, nb - 1)
        bucket = ret + jnp.where(n < max_exact, n, large)
        val = jnp.zeros((nr, nc), F32)
        for j in buckets:
            val = jnp.where(bucket == j, rb_ref[j * N_DIFF_HEADS + h], val)
        return val * LOG2E

    left_b, right_b = _far_bucket_consts()
    corner = MAX_DISTANCE

    @pl.when(which == 0)
    def _():
        o_ref[...] = jnp.full((t, t), rb_ref[left_b * N_DIFF_HEADS + h] * LOG2E, F32)
        o_ref[0:corner, t - corner:t] = tile(-1, range(0, nb), 0, t - corner, corner, corner)

    @pl.when(which == 1)
    def _():
        o_ref[...] = tile(0, range(0, 2 * nb))

    @pl.when(which == 2)
    def _():
        o_ref[...] = jnp.full((t, t), rb_ref[right_b * N_DIFF_HEADS + h] * LOG2E, F32)
        o_ref[t - corner:t, 0:corner] = tile(1, range(nb, 2 * nb), t - corner, 0, corner, corner)

    @pl.when(which == 3)
    def _():
        o_ref[...] = jnp.zeros((t, t), F32)


def _bias_tiles(rel_bias, t):
    H = N_DIFF_HEADS
    assert t >= MAX_DISTANCE, "far key tiles must lie entirely in the saturated bucket"
    return pl.pallas_call(
        functools.partial(_bias_kernel, t=t),
        grid=(H, 4),
        in_specs=[pl.BlockSpec(memory_space=pltpu.SMEM)],
        out_specs=pl.BlockSpec((None, None, t, t), lambda h, d: (h, d, 0, 0)),
        out_shape=jax.ShapeDtypeStruct((H, 4, t, t), F32),
        compiler_params=_cparams(("arbitrary", "arbitrary")),
        name="t5_bias",
    )(rel_bias.reshape(-1))


def _far_bucket_consts():
    nb = N_BUCKETS // 2
    return nb - 1, 2 * nb - 1


def _attn_kernel(rb_ref, q_ref, k_ref, v_ref, b_ref, lam_ref, gs_ref, o_ref,
                 s_sc, p_sc, vext_sc, m_sc, raw_sc, *, t, n, n_tiles, lam_init):
    g = pl.program_id(0)
    cur = jnp.minimum(g, n_tiles - 1)
    h = (cur // n) % N_DIFF_HEADS
    i = cur % n
    half = LANES // 2

    @pl.when(g == 0)
    def _():
        raw_sc[...] = jnp.ones(raw_sc.shape, F32)

    @pl.when((i == 0) & (g < n_tiles))
    def _():
        vext_sc[:, :LANES] = v_ref[...]
        vext_sc[:, LANES:] = jnp.ones(v_ref.shape, BF16)

    left_b, right_b = _far_bucket_consts()
    c_left = rb_ref[left_b * N_DIFF_HEADS + h] * LOG2E
    c_right = rb_ref[right_b * N_DIFF_HEADS + h] * LOG2E
    lane = lax.broadcasted_iota(jnp.int32, (1, LANES), 1)

    def all_scores(m):
        fl = lane - half if m == 0 else lane
        is_data = (lane < half) if m == 0 else (lane >= half)
        f = jnp.where(fl < n, fl, fl - n)
        cfar = jnp.where(f < i - 1, c_left, jnp.where(f > i + 1, c_right, 0.0))
        c_hi = cfar.astype(BF16)
        c_lo = (cfar - c_hi.astype(F32)).astype(BF16)
        feat = jnp.where(fl < n, c_hi, c_lo)
        feat = jnp.where((fl >= 0) & (fl < 2 * n), feat, jnp.zeros_like(feat))
        qm = jnp.where(is_data, q_ref[:, m * LANES:(m + 1) * LANES], feat)
        for j in range(n):
            kc = k_ref[j * t:(j + 1) * t, m * LANES:(m + 1) * LANES]
            s = lax.dot_general(qm, kc, (((1,), (1,)), ((), ())), preferred_element_type=F32)
            d = j - i
            s = s + b_ref[jnp.where(jnp.abs(d) <= 1, d + 1, 3)]
            s_sc[m, j] = s
            tile_max = s[:, 0:LANES]
            for c in range(1, t // LANES):
                tile_max = jnp.maximum(tile_max, s[:, c * LANES:(c + 1) * LANES])
            if j == 0:
                m_sc[m] = tile_max
            else:
                m_sc[m] = jnp.maximum(m_sc[m], tile_max)

    def exponentials(m):
        mx = jnp.max(m_sc[m], axis=-1, keepdims=True)
        for j in range(n):
            p_sc[m, :, j * t:(j + 1) * t] = jnp.exp2(s_sc[m, j] - mx).astype(BF16)

    def finish_previous():
        r0 = raw_sc[0]
        r1 = raw_sc[1]
        lam = (jnp.exp(jnp.sum(lam_ref[0:1, :] * lam_ref[1:2, :], axis=-1, keepdims=True))
               - jnp.exp(jnp.sum(lam_ref[2:3, :] * lam_ref[3:4, :], axis=-1, keepdims=True)) + lam_init)
        o = r0[:, :LANES] / r0[:, LANES:LANES + 1] - lam * (r1[:, :LANES] / r1[:, LANES:LANES + 1])
        ms = jnp.mean(o * o, axis=-1, keepdims=True)
        o_ref[...] = (o * lax.rsqrt(ms + EPS) * gs_ref[...] * (1.0 - lam_init)).astype(BF16)

    @pl.when(i < n)
    def _():
        finish_previous()
        all_scores(0)

    @pl.when(g < n_tiles)
    def _():
        all_scores(1)
        exponentials(0)
        raw_sc[0] = jnp.dot(p_sc[0], vext_sc[...], preferred_element_type=F32)
        exponentials(1)
        raw_sc[1] = jnp.dot(p_sc[1], vext_sc[...], preferred_element_type=F32)


def _attention(rel_bias, q3, k3, v3, btiles, lam4, g_subln, t, lam_init):
    B, S, _ = q3.shape
    H = N_DIFF_HEADS
    n = S // t
    n_tiles = B * H * n
    kern = functools.partial(_attn_kernel, t=t, n=n, n_tiles=n_tiles, lam_init=lam_init)

    def tile_of(g):
        return g // (H * n), (g // n) % H, g % n

    def cur_map(fn):
        def index_map(g):
            return fn(*tile_of(jnp.minimum(g, n_tiles - 1)))
        return index_map

    def prev_map(g):
        b, h, i = tile_of(jnp.maximum(g - 1, 0))
        return (b, i, h)

    return pl.pallas_call(
        kern,
        grid=(n_tiles + 1,),
        in_specs=[pl.BlockSpec(memory_space=pltpu.SMEM),
                  pl.BlockSpec((None, t, 2 * LANES), cur_map(lambda b, h, i: (b, i, h))),
                  pl.BlockSpec((None, S, 2 * LANES), cur_map(lambda b, h, i: (b, 0, h))),
                  pl.BlockSpec((None, S, LANES), cur_map(lambda b, h, i: (b, 0, h))),
                  pl.BlockSpec((None, 4, t, t), cur_map(lambda b, h, i: (h, 0, 0, 0))),
                  pl.BlockSpec((4, LANES // 2), lambda g: (0, 0)),
                  pl.BlockSpec((1, LANES), lambda g: (0, 0))],
        out_specs=pl.BlockSpec((None, t, LANES), prev_map),
        out_shape=jax.ShapeDtypeStruct((B, S, H * LANES), BF16),
        scratch_shapes=[pltpu.VMEM((2, n, t, t), F32),
                        pltpu.VMEM((2, t, S), BF16),
                        pltpu.VMEM((S, 2 * LANES), BF16),
                        pltpu.VMEM((2, t, LANES), F32),
                        pltpu.VMEM((2, t, 2 * LANES), F32)],
        compiler_params=_cparams(("arbitrary",)),
        name="diff_attn",
    )(rel_bias.reshape(-1), q3, k3, v3, btiles, lam4, g_subln)


def _pack_bf16_pairs(v):
    bits = lax.bitcast_convert_type(v.astype(BF16).astype(F32), jnp.uint32)
    n = bits.shape[1] // 2
    return (bits[:, :n] >> 16) | (bits[:, n:] & jnp.uint32(0xFFFF0000))


def _unpack_bf16_pairs(p):
    lo = lax.bitcast_convert_type(p << 16, F32).astype(BF16)
    hi = lax.bitcast_convert_type(p & jnp.uint32(0xFFFF0000), F32).astype(BF16)
    return lo, hi


def _out_proj_kernel(yf_ref, perm_ref, yd_ref, x_ref, mod_ref, wo_ref, g_ref, wr_ref, br_ref,
                     x1_ref, hp_ref, lp_ref, grow_ref, cnt_ref, *, tm, fw, E):
    yf_stack = jnp.concatenate([yf_ref[:, r * fw:(r + 1) * fw] for r in range(DFT_RADIX)], axis=0)
    yf = jnp.dot(perm_ref[...], yf_stack, preferred_element_type=F32).astype(BF16)
    mix = (jnp.dot(yf, wo_ref[:fw, :], preferred_element_type=F32)
           + jnp.dot(yd_ref[...], wo_ref[fw:, :], preferred_element_type=F32))
    x1 = x_ref[...] + mod_ref[2:3, :] * mix
    x1_ref[...] = x1
    ms = jnp.mean(x1 * x1, axis=-1, keepdims=True)
    h = x1 * lax.rsqrt(ms + EPS) * g_ref[...]
    h = h * (1.0 + mod_ref[4:5, :]) + mod_ref[3:4, :]
    hb = h.astype(BF16)
    hp_ref[...] = _pack_bf16_pairs(hb)

    logits = lax.dot_general(wr_ref[...], hb, (((1,), (1,)), ((), ())),
                             preferred_element_type=F32) + br_ref[...]
    rowid = lax.broadcasted_iota(jnp.int32, (E, tm), 0)
    vals, idxs = [], []
    comb = jnp.zeros((E, tm), F32)
    l = logits
    for _ in range(TOP_K):
        mv = jnp.max(l, axis=0, keepdims=True)
        ix = jnp.min(jnp.where(l == mv, rowid, E), axis=0, keepdims=True)
        sel = rowid == ix
        vals.append(mv)
        idxs.append(ix)
        comb = comb + sel.astype(F32)
        l = jnp.where(sel, -jnp.inf, l)
    es = [jnp.exp(v - vals[0]) for v in vals]
    den = es[0] + es[1] + es[2] + es[3]
    gates = [e / den for e in es]

    r_i = lax.broadcasted_iota(jnp.int32, (tm, tm), 0)
    c_i = lax.broadcasted_iota(jnp.int32, (tm, tm), 1)
    upper = (r_i < c_i).astype(BF16)
    before = jnp.dot(comb.astype(BF16), upper, preferred_element_type=F32)
    cnt = jnp.sum(comb, axis=1, keepdims=True)
    cnt_ref[...] = jnp.broadcast_to(cnt, cnt_ref.shape).astype(jnp.int32)
    chunks = jnp.floor((cnt + (ROW_CHUNK - 1)) * (1.0 / ROW_CHUNK))
    e_r = lax.broadcasted_iota(jnp.int32, (E, E), 0)
    e_c = lax.broadcasted_iota(jnp.int32, (E, E), 1)
    run_start = ROW_CHUNK * jnp.dot((e_c < e_r).astype(BF16),
                                    jnp.broadcast_to(chunks, (E, LANES)).astype(BF16),
                                    preferred_element_type=F32)[:, 0:1]
    ranks = [jnp.sum(jnp.where(rowid == ix, before + run_start, 0.0), axis=0, keepdims=True) for ix in idxs]

    pad_i = jnp.full((8 - TOP_K, tm), -1, jnp.int32)
    lp_ref[...] = jnp.concatenate([r.astype(jnp.int32) for r in ranks] + [pad_i], axis=0)
    gpad = jnp.concatenate(gates + [jnp.zeros((LANES - TOP_K, tm), F32)], axis=0)
    grow_ref[...] = gpad.T


def _out_proj(yf3, yd2, x2, mod3, w_out_b, g_ffn, wr_t, br_t, S, tm):
    T, D = x2.shape
    fw = yd2.shape[1]
    E = wr_t.shape[0]
    nt = T // tm
    tiles_per_batch = S // tm
    q = tm // DFT_RADIX
    col = np.arange(tm)
    perm = jnp.asarray(np.arange(tm)[:, None] == DFT_RADIX * (col % q) + col // q, BF16)
    kern = functools.partial(_out_proj_kernel, tm=tm, fw=fw, E=E)
    return pl.pallas_call(
        kern,
        grid=(nt,),
        in_specs=[pl.BlockSpec((None, q, DFT_RADIX * fw), lambda i: (i // tiles_per_batch, i % tiles_per_batch, 0)),
                  pl.BlockSpec((tm, tm), lambda i: (0, 0)),
                  pl.BlockSpec((tm, fw), lambda i: (i, 0)),
                  pl.BlockSpec((tm, D), lambda i: (i, 0)),
                  pl.BlockSpec((None, 6, D), lambda i: (i // tiles_per_batch, 0, 0)),
                  pl.BlockSpec((D, D), lambda i: (0, 0)),
                  pl.BlockSpec((1, D), lambda i: (0, 0)),
                  pl.BlockSpec((E, D), lambda i: (0, 0)),
                  pl.BlockSpec((E, 1), lambda i: (0, 0))],
        out_specs=[pl.BlockSpec((tm, D), lambda i: (i, 0)),
                   pl.BlockSpec((tm, D // 2), lambda i: (i, 0)),
                   pl.BlockSpec((8, tm), lambda i: (0, i)),
                   pl.BlockSpec((tm, LANES), lambda i: (i, 0)),
                   pl.BlockSpec((None, E, LANES), lambda i: (i, 0, 0))],
        out_shape=[jax.ShapeDtypeStruct((T, D), F32),
                   jax.ShapeDtypeStruct((T, D // 2), jnp.uint32),
                   jax.ShapeDtypeStruct((8, T), jnp.int32),
                   jax.ShapeDtypeStruct((T, LANES), F32),
                   jax.ShapeDtypeStruct((nt, E, LANES), jnp.int32)],
        compiler_params=_cparams(("arbitrary",)),
        name="out_proj_router",
    )(yf3, perm, yd2, x2, mod3, w_out_b, g_ffn, wr_t, br_t)


ROW_CHUNK = 8
CHUNK_UNROLL = 8


def _dispatch_kernel(tbl_ref, zb_ref, lp_ref, hp_ref, xs_ref, buf, zero_sc, sem, zsem, tsem,
                     *, tm, bm, NR, nz, nz_first, nt):
    i = pl.program_id(0)
    s = i % 2

    def zero_copy(e, zs):
        return pltpu.make_async_copy(zero_sc, xs_ref.at[pl.ds(zb_ref[e] * bm, bm)], zs)

    @pl.when(i == 0)
    def _():
        zero_sc[...] = jnp.zeros(zero_sc.shape, jnp.uint32)
        for e in range(nz):
            @pl.when(zb_ref[e] >= 0)
            def _():
                zero_copy(e, zsem if e < nz_first else tsem).start()
        for e in range(nz_first):
            @pl.when(zb_ref[e] >= 0)
            def _():
                zero_copy(e, zsem).wait()

    @pl.when(i == nt - 1)
    def _():
        for e in range(nz_first, nz):
            @pl.when(zb_ref[e] >= 0)
            def _():
                zero_copy(e, tsem).wait()

    lp = lp_ref[...]
    rows = lax.broadcasted_iota(jnp.int32, (NR, tm), 0)
    hit = rows == lp[0:1, :]
    for k in range(1, TOP_K):
        hit = hit | (rows == lp[k:k + 1, :])
    sel = jnp.where(hit, 1.0, 0.0).astype(BF16)
    lo, hi = _unpack_bf16_pairs(hp_ref[...])
    slo = jnp.dot(sel, lo, preferred_element_type=F32)
    shi = jnp.dot(sel, hi, preferred_element_type=F32)
    buf[s] = ((lax.bitcast_convert_type(slo, jnp.uint32) >> 16)
              | (lax.bitcast_convert_type(shi, jnp.uint32) & jnp.uint32(0xFFFF0000)))

    for slot in range(2):
        def issue(c, carry, slot=slot):
            d = pl.multiple_of(tbl_ref[0, 0, c], ROW_CHUNK)
            r = pl.multiple_of(c * ROW_CHUNK, ROW_CHUNK)
            pltpu.make_async_copy(buf.at[slot, pl.ds(r, ROW_CHUNK)], xs_ref.at[pl.ds(d, ROW_CHUNK)],
                                  sem.at[slot]).start()
            return carry

        @pl.when(s == slot)
        def _():
            lax.fori_loop(0, NR // ROW_CHUNK, issue, 0, unroll=CHUNK_UNROLL)

    def wait_slot(slot):
        pltpu.make_async_copy(buf.at[slot], xs_ref.at[pl.ds(0, NR)], sem.at[slot]).wait()

    @pl.when(i > 0)
    def _():
        wait_slot(1 - s)

    @pl.when(i == nt - 1)
    def _():
        wait_slot(s)


def _dispatch(tbl3, zero_blk, nz_first, lp8, hp, P, bm, tm, NR):
    T, Wd = hp.shape
    nt = T // tm
    nz = zero_blk.shape[0]
    nch_tbl = tbl3.shape[2]
    kern = functools.partial(_dispatch_kernel, tm=tm, bm=bm, NR=NR, nz=nz, nz_first=nz_first, nt=nt)
    return pl.pallas_call(
        kern,
        grid=(nt,),
        in_specs=[pl.BlockSpec((1, 1, nch_tbl), lambda i: (i, 0, 0), memory_space=pltpu.SMEM),
                  pl.BlockSpec(memory_space=pltpu.SMEM),
                  pl.BlockSpec((8, tm), lambda i: (0, i)),
                  pl.BlockSpec((tm, Wd), lambda i: (i, 0))],
        out_specs=pl.BlockSpec(memory_space=pl.ANY),
        out_shape=jax.ShapeDtypeStruct((P, Wd), jnp.uint32),
        scratch_shapes=[pltpu.VMEM((2, NR, Wd), jnp.uint32),
                        pltpu.VMEM((bm, Wd), jnp.uint32),
                        pltpu.SemaphoreType.DMA((2,)),
                        pltpu.SemaphoreType.DMA,
                        pltpu.SemaphoreType.DMA],
        compiler_params=_cparams(("arbitrary",)),
        name="moe_dispatch",
    )(tbl3, zero_blk, lp8, hp)


def _expert_kernel(be_ref, nu_ref, h2_ref, x_ref, w1_ref, b1g_ref, b1l_ref, w2_ref, b2_ref, pm_ref, o_ref,
                   w1g_sc, w1l_sc, w2_sc):
    blk = pl.program_id(0)
    nused = nu_ref[0]
    e = be_ref[blk]
    prev = be_ref[jnp.maximum(blk - 1, 0)]
    active = blk < nused
    second = active & (h2_ref[blk] != 0)
    half = x_ref.shape[0] // 2
    tile = 2 * LANES
    n_tiles = w1_ref.shape[1] // tile

    @pl.when(active & ((blk == 0) | (e != prev)))
    def _():
        for c in range(n_tiles):
            wt = w1_ref[:, c * tile:(c + 1) * tile].astype(BF16)
            pw = jnp.dot(wt, pm_ref[...], preferred_element_type=F32)
            w1g_sc[:, c * LANES:(c + 1) * LANES] = pw[:, :LANES].astype(BF16)
            w1l_sc[:, c * LANES:(c + 1) * LANES] = pw[:, LANES:].astype(BF16)
        w2_sc[...] = w2_ref[...].astype(BF16)

    @pl.when(jnp.logical_not(active))
    def _():
        o_ref[...] = jnp.zeros(o_ref.shape, jnp.uint32)

    def mlp(r0):
        xp = x_ref[r0:r0 + half, :]
        hw = xp.shape[1]
        xlo, xhi = _unpack_bf16_pairs(xp)
        zg = (jnp.dot(xlo, w1g_sc[:hw, :], preferred_element_type=F32)
              + jnp.dot(xhi, w1g_sc[hw:, :], preferred_element_type=F32) + b1g_ref[...])
        zl = (jnp.dot(xlo, w1l_sc[:hw, :], preferred_element_type=F32)
              + jnp.dot(xhi, w1l_sc[hw:, :], preferred_element_type=F32) + b1l_ref[...])
        g = jnp.minimum(zg, SWIGLU_LIMIT)
        lin = jnp.clip(zl, -SWIGLU_LIMIT, SWIGLU_LIMIT)
        act = g * jax.nn.sigmoid(SWIGLU_ALPHA * g) * (lin + 1.0)
        y = jnp.dot(act.astype(BF16), w2_sc[...], preferred_element_type=F32) + b2_ref[...]
        o_ref[r0:r0 + half, :] = _pack_bf16_pairs(y)

    @pl.when(second)
    def _():
        mlp(0)
        mlp(half)

    @pl.when(active & jnp.logical_not(second))
    def _():
        mlp(0)
        o_ref[half:, :] = jnp.zeros((half, o_ref.shape[1]), jnp.uint32)


def _experts(blk_expert, nused, half2, xs, P, w1, b1g, b1l, w2, b2, bm):
    Wd = xs.shape[1]
    E, D, F2 = w1.shape
    F = F2 // 2
    nblk = P // bm
    tile = 2 * LANES
    pm = np.zeros((tile, tile), np.float32)
    pm[2 * np.arange(LANES), np.arange(LANES)] = 1.0
    pm[2 * np.arange(LANES) + 1, LANES + np.arange(LANES)] = 1.0
    pm = jnp.asarray(pm, BF16)

    def row_map(b, be, nu, h2):
        return (jnp.minimum(b, nu[0] - 1), 0)

    def exp_map(b, be, nu, h2):
        return (be[b], 0, 0)

    grid_spec = pltpu.PrefetchScalarGridSpec(
        num_scalar_prefetch=3,
        grid=(nblk,),
        in_specs=[pl.BlockSpec((bm, Wd), row_map),
                  pl.BlockSpec((None, D, F2), exp_map),
                  pl.BlockSpec((None, 1, F), exp_map),
                  pl.BlockSpec((None, 1, F), exp_map),
                  pl.BlockSpec((None, F, D), exp_map),
                  pl.BlockSpec((None, 1, D), exp_map),
                  pl.BlockSpec((tile, tile), lambda b, be, nu, h2: (0, 0))],
        out_specs=pl.BlockSpec((bm, D // 2), lambda b, be, nu, h2: (b, 0)),
        scratch_shapes=[pltpu.VMEM((D, F), BF16),
                        pltpu.VMEM((D, F), BF16),
                        pltpu.VMEM((F, D), BF16)],
    )
    return pl.pallas_call(
        _expert_kernel,
        grid_spec=grid_spec,
        out_shape=jax.ShapeDtypeStruct((P, D // 2), jnp.uint32),
        compiler_params=_cparams(("arbitrary",)),
        name="moe_experts",
    )(blk_expert, nused, half2, xs, w1, b1g, b1l, w2, b2, pm)


def _combine_kernel(tcur_ref, tnxt_ref, lp_ref, grow_ref, x1_ref, mod_ref, y_ref, o_ref,
                    buf, sem, *, tm, NR, nt):
    i = pl.program_id(0)
    s = i % 2

    def issue(tref, slot):
        def body(c, carry):
            d = pl.multiple_of(tref[0, 0, c], ROW_CHUNK)
            r = pl.multiple_of(c * ROW_CHUNK, ROW_CHUNK)
            pltpu.make_async_copy(y_ref.at[pl.ds(d, ROW_CHUNK)], buf.at[slot, pl.ds(r, ROW_CHUNK)],
                                  sem.at[slot]).start()
            return carry
        lax.fori_loop(0, NR // ROW_CHUNK, body, 0, unroll=CHUNK_UNROLL)

    @pl.when(i == 0)
    def _():
        issue(tcur_ref, 0)

    for slot in range(2):
        @pl.when((i + 1 < nt) & (s != slot))
        def _():
            issue(tnxt_ref, slot)

    pltpu.make_async_copy(y_ref.at[pl.ds(0, NR)], buf.at[s], sem.at[s]).wait()

    ylo, yhi = _unpack_bf16_pairs(buf[s])
    lpf = jnp.concatenate([lp_ref[...].astype(F32), jnp.zeros((LANES - 8, tm), F32)], axis=0)
    lpr = lpf.T.astype(jnp.int32)
    gr = grow_ref[...]
    cols = lax.broadcasted_iota(jnp.int32, (tm, NR), 1)
    gmat = jnp.zeros((tm, NR), F32)
    for k in range(TOP_K):
        gmat = jnp.where(cols == lpr[:, k:k + 1], gr[:, k:k + 1], gmat)
    gmat = gmat.astype(BF16)
    hw = ylo.shape[1]
    o_ref[:, :hw] = x1_ref[:, :hw] + mod_ref[5:6, :hw] * jnp.dot(gmat, ylo, preferred_element_type=F32)
    o_ref[:, hw:] = x1_ref[:, hw:] + mod_ref[5:6, hw:] * jnp.dot(gmat, yhi, preferred_element_type=F32)


def _combine(tbl3, lp8, grow, x1, mod3, y, S, tm, NR):
    T, D = x1.shape
    nt = T // tm
    tiles_per_batch = S // tm
    nch_tbl = tbl3.shape[2]
    kern = functools.partial(_combine_kernel, tm=tm, NR=NR, nt=nt)
    return pl.pallas_call(
        kern,
        grid=(nt,),
        in_specs=[pl.BlockSpec((1, 1, nch_tbl), lambda i: (i, 0, 0), memory_space=pltpu.SMEM),
                  pl.BlockSpec((1, 1, nch_tbl), lambda i: (jnp.minimum(i + 1, nt - 1), 0, 0),
                               memory_space=pltpu.SMEM),
                  pl.BlockSpec((8, tm), lambda i: (0, i)),
                  pl.BlockSpec((tm, LANES), lambda i: (i, 0)),
                  pl.BlockSpec((tm, D), lambda i: (i, 0)),
                  pl.BlockSpec((None, 6, D), lambda i: (i // tiles_per_batch, 0, 0)),
                  pl.BlockSpec(memory_space=pl.ANY)],
        out_specs=pl.BlockSpec((tm, D), lambda i: (i, 0)),
        out_shape=jax.ShapeDtypeStruct((T, D), F32),
        scratch_shapes=[pltpu.VMEM((2, NR, D // 2), jnp.uint32),
                        pltpu.SemaphoreType.DMA((2,))],
        compiler_params=_cparams(("arbitrary",)),
        name="moe_combine",
    )(tbl3, tbl3, lp8, grow, x1, mod3, y)


def _tiles(S):
    t_attn = min(512, S // 4) if S >= 512 else S
    tm = min(512, t_attn)
    return dict(tm=tm, t_attn=t_attn, bm=1024)


def _moe_layout(cnt, T, tm, bm):
    nt, E = cnt.shape
    c8 = (cnt + ROW_CHUNK - 1) // ROW_CHUNK * ROW_CHUNK
    tile_end = jnp.cumsum(c8, axis=1)
    tile_off = tile_end - c8
    used = tile_end[:, -1]
    off = jnp.cumsum(c8, axis=0) - c8
    tot = jnp.sum(c8, axis=0)
    padded = (tot + bm - 1) // bm * bm
    pends = jnp.cumsum(padded)
    pstarts = pends - padded
    NR = TOP_K * tm + ROW_CHUNK * E
    nch_tbl = NR // ROW_CHUNK
    P = (T * TOP_K + nt * E * (ROW_CHUNK - 1) + E * bm + bm - 1) // bm * bm
    nblk = P // bm

    c_row = jnp.arange(nch_tbl, dtype=jnp.int32) * ROW_CHUNK
    inrun = (tile_off[:, :, None] <= c_row) & (c_row < tile_end[:, :, None])
    base = pstarts[None, :] + off - tile_off
    tbl = jnp.sum(jnp.where(inrun, base[:, :, None], 0), axis=1) + c_row[None, :]
    in_use = c_row[None, :] < used[:, None]
    spare_rows = NR - TOP_K * tm
    spare = P + jnp.arange(nt, dtype=jnp.int32)[:, None] * spare_rows + (c_row[None, :] - used[:, None])
    tbl_out = jnp.where(in_use, tbl, spare).astype(jnp.int32)
    tbl_in = jnp.where(in_use, tbl, 0).astype(jnp.int32)
    n_spare_blk = (nt * spare_rows + bm - 1) // bm

    nused = (pends[-1] // bm).astype(jnp.int32).reshape(1)
    blk_start = jnp.arange(nblk, dtype=jnp.int32) * bm
    blk_expert = jnp.minimum(jnp.sum(pends[None, :] <= blk_start[:, None], axis=1), E - 1).astype(jnp.int32)
    half2 = (blk_start + bm // 2 < (pstarts + tot)[blk_expert]).astype(jnp.int32)
    last_blk = jnp.where(padded > 0, pends // bm - 1, -1)
    n_tail = nblk - (T * TOP_K) // bm
    tail_blk = nused[0] + jnp.arange(n_tail, dtype=jnp.int32)
    tail_blk = jnp.where(tail_blk < nblk, tail_blk, -1)
    spare_blk = nblk + jnp.arange(n_spare_blk, dtype=jnp.int32)
    zero_blk = jnp.concatenate([last_blk, spare_blk, tail_blk]).astype(jnp.int32)
    return dict(tbl_out=tbl_out.reshape(nt, 1, nch_tbl), tbl_in=tbl_in.reshape(nt, 1, nch_tbl),
                nused=nused, blk_expert=blk_expert, half2=half2, zero_blk=zero_blk,
                nz_first=E + n_spare_blk,
                P=P, P_alloc=P + n_spare_blk * bm, NR=NR)


def kernel(x, c, w_ada, b_ada, g_mix, w_in, w_fourier, q_norm_g, k_norm_g, lambda_q1, lambda_k1,
           lambda_q2, lambda_k2, g_subln, w_out, rel_bias, g_ffn, w_router, b_router, w1, b1, w2, b2):
    B, S, D = x.shape
    T = B * S
    L = w_ada.shape[0]
    E = w_router.shape[-1]
    cfg = _tiles(S)
    tm, t_attn, bm = cfg["tm"], cfg["t_attn"], cfg["bm"]
    dqk = q_norm_g.shape[-1]
    scale = dqk ** -0.5
    fw = w_in.shape[-1] // 4

    btiles = _bias_tiles(rel_bias, t_attn)
    x2 = x.reshape(T, D)
    for l in range(L):
        lam_init = 0.8 - 0.6 * math.exp(-0.3 * l)
        mod3 = _ada(c, w_ada[l], b_ada[l]).reshape(B, 6, D)
        gq = (jnp.tile(q_norm_g[l], 4) * (scale * LOG2E)).reshape(1, 2 * LANES)
        gk = jnp.tile(k_norm_g[l], 4).reshape(1, 2 * LANES)
        uf, qx, kx, v = _in_proj(x2, mod3, g_mix[l].reshape(1, D), w_in[l].astype(BF16), gq, gk,
                                 S, tm, t_attn)
        yf = _fourier(uf.reshape(B, S, fw), w_fourier[l])
        lam4 = jnp.stack([lambda_q1[l], lambda_k1[l], lambda_q2[l], lambda_k2[l]]).astype(F32)
        yd = _attention(rel_bias, qx.reshape(B, S, -1), kx.reshape(B, S, -1), v.reshape(B, S, fw),
                        btiles, lam4, g_subln[l].reshape(1, LANES), t_attn, lam_init).reshape(T, fw)
        x1, hp, lp8, grow, cnt3 = _out_proj(
            yf, yd, x2, mod3, w_out[l].astype(BF16), g_ffn[l].reshape(1, D),
            w_router[l].T.astype(BF16), b_router[l].reshape(E, 1), S, tm)
        lay = _moe_layout(cnt3[:, :, 0], T, tm, bm)
        xs = _dispatch(lay["tbl_out"], lay["zero_blk"], lay["nz_first"], lp8, hp, lay["P_alloc"], bm, tm,
                       lay["NR"])
        y = _experts(lay["blk_expert"], lay["nused"], lay["half2"], xs, lay["P"], w1[l], b1[l][:, None, 0::2],
                     b1[l][:, None, 1::2], w2[l], b2[l][:, None, :], bm)
        x2 = _combine(lay["tbl_in"], lp8, grow, x1, mod3, y, S, tm, lay["NR"])
    return x2.reshape(B, S, D)
```

```python
import functools
import math

import numpy as np
import jax
import jax.numpy as jnp
from jax import lax
from jax.experimental import pallas as pl
from jax.experimental.pallas import tpu as pltpu

F32 = jnp.float32
BF16 = jnp.bfloat16

EPS = 1e-6
N_FOURIER_GROUPS = 4
N_DIFF_HEADS = 4
TOP_K = 4
N_BUCKETS = 32
MAX_DISTANCE = 128
SWIGLU_ALPHA = 1.702
SWIGLU_LIMIT = 7.0
LANES = 128
DFT_RADIX = 4
LOG2E = 1.4426950408889634
VMEM_LIMIT = 56 * 1024 * 1024


def _cparams(sem):
    return pltpu.CompilerParams(dimension_semantics=sem, vmem_limit_bytes=VMEM_LIMIT)


def _ada_kernel(c_ref, w_ref, b_ref, o_ref):
    c = c_ref[...]
    cond = c * jax.nn.sigmoid(c)
    o_ref[...] = jnp.dot(cond.astype(BF16), w_ref[...].astype(BF16),
                         preferred_element_type=F32) + b_ref[...]


def _ada(c, w, b):
    B, D = c.shape
    N = w.shape[1]
    tn = 1536 if N % 1536 == 0 else N
    return pl.pallas_call(
        _ada_kernel,
        grid=(N // tn,),
        in_specs=[pl.BlockSpec((B, D), lambda j: (0, 0)),
                  pl.BlockSpec((D, tn), lambda j: (0, j)),
                  pl.BlockSpec((1, tn), lambda j: (0, j))],
        out_specs=pl.BlockSpec((B, tn), lambda j: (0, j)),
        out_shape=jax.ShapeDtypeStruct((B, N), F32),
        compiler_params=_cparams(("arbitrary",)),
        name="ada",
    )(c, w, b.reshape(1, N))


def _group_rms_inv(xh, gmat, group):
    ss = jnp.dot((xh * xh).astype(BF16), gmat, preferred_element_type=F32)
    return lax.rsqrt(ss * (1.0 / group) + EPS)


def _in_proj_kernel(x_ref, mod_ref, g_ref, w_ref, gq_ref, gk_ref, gmat_ref,
                    uf_ref, q_ref, k_ref, v_ref, *, tiles_per_batch, tm, t_attn, n_chunks, fw, qw):
    i = pl.program_id(0)
    x = x_ref[...]
    ms = jnp.mean(x * x, axis=-1, keepdims=True)
    y = x * lax.rsqrt(ms + EPS) * g_ref[...]
    h = y * (1.0 + mod_ref[1:2, :]) + mod_ref[0:1, :]
    proj = jnp.dot(h.astype(BF16), w_ref[...], preferred_element_type=F32)
    uf_ref[...] = proj[:, :fw].astype(BF16)
    v_ref[...] = proj[:, fw + 2 * qw:].astype(BF16)

    lane = lax.broadcasted_iota(jnp.int32, (1, LANES), 1)
    chunk = ((i % tiles_per_batch) * tm) // t_attn
    gmat = gmat_ref[...]
    half = LANES // 2
    pair = 2 * LANES
    for hd in range(N_DIFF_HEADS):
        if hd % 2 == 0:
            q2 = proj[:, fw + hd * LANES: fw + hd * LANES + pair]
            k2 = proj[:, fw + qw + hd * LANES: fw + qw + hd * LANES + pair]
            qn2 = (q2 * _group_rms_inv(q2, gmat, half) * gq_ref[...]).astype(BF16)
            kn2 = (k2 * _group_rms_inv(k2, gmat, half) * gk_ref[...]).astype(BF16)
        qn = qn2[:, (hd % 2) * LANES:(hd % 2 + 1) * LANES]
        kn = kn2[:, (hd % 2) * LANES:(hd % 2 + 1) * LANES]
        for m in range(2):
            fl = lane - half if m == 0 else lane
            is_data = (lane < half) if m == 0 else (lane >= half)
            f = jnp.where(fl < n_chunks, fl, fl - n_chunks)
            onehot = ((fl >= 0) & (fl < 2 * n_chunks) & (f == chunk)).astype(BF16)
            c0 = (2 * hd + m) * LANES
            q_ref[:, c0:c0 + LANES] = jnp.where(is_data, qn, jnp.zeros_like(qn))
            k_ref[:, c0:c0 + LANES] = jnp.where(is_data, kn, onehot)


def _in_proj(x2, mod3, g_mix, w_in_b, gq, gk, S, tm, t_attn):
    T, D = x2.shape
    ncols = w_in_b.shape[1]
    fw = ncols // 4
    qw = fw
    tiles_per_batch = S // tm
    n_chunks = S // t_attn
    half = LANES // 2
    pair = 2 * LANES
    gmat = (np.arange(pair)[:, None] // half == np.arange(pair)[None, :] // half)
    gmat = jnp.asarray(gmat, BF16)
    kern = functools.partial(_in_proj_kernel, tiles_per_batch=tiles_per_batch, tm=tm, t_attn=t_attn,
                             n_chunks=n_chunks, fw=fw, qw=qw)
    ext = 2 * qw
    return pl.pallas_call(
        kern,
        grid=(T // tm,),
        in_specs=[pl.BlockSpec((tm, D), lambda i: (i, 0)),
                  pl.BlockSpec((None, 6, D), lambda i: (i // tiles_per_batch, 0, 0)),
                  pl.BlockSpec((1, D), lambda i: (0, 0)),
                  pl.BlockSpec((D, ncols), lambda i: (0, 0)),
                  pl.BlockSpec((1, pair), lambda i: (0, 0)),
                  pl.BlockSpec((1, pair), lambda i: (0, 0)),
                  pl.BlockSpec((pair, pair), lambda i: (0, 0))],
        out_specs=[pl.BlockSpec((tm, fw), lambda i: (i, 0)),
                   pl.BlockSpec((tm, ext), lambda i: (i, 0)),
                   pl.BlockSpec((tm, ext), lambda i: (i, 0)),
                   pl.BlockSpec((tm, fw), lambda i: (i, 0))],
        out_shape=[jax.ShapeDtypeStruct((T, fw), BF16),
                   jax.ShapeDtypeStruct((T, ext), BF16),
                   jax.ShapeDtypeStruct((T, ext), BF16),
                   jax.ShapeDtypeStruct((T, fw), BF16)],
        compiler_params=_cparams(("arbitrary",)),
        name="in_proj",
    )(x2, mod3, g_mix, w_in_b, gq, gk, gmat)


def _fourier_kernel(uf_ref, wcs_ref, tw_ref, dft_ref, o_ref, z_sc, tt_sc, *, M, gw):
    c = pl.program_id(1)

    def channel_stage():
        for g in range(N_FOURIER_GROUPS):
            for j in range(DFT_RADIX):
                ab = jnp.dot(uf_ref[j * M:(j + 1) * M, g * gw:(g + 1) * gw], wcs_ref[g],
                             preferred_element_type=F32)
                z_sc[0, j, :, g * gw:(g + 1) * gw] = ab[:, :gw]
                z_sc[1, j, :, g * gw:(g + 1) * gw] = -ab[:, gw:]

    def butterfly(k2):
        cs = tw_ref[:, k2:k2 + 1]
        sn = tw_ref[:, DFT_RADIX + k2:DFT_RADIX + k2 + 1]
        for g in range(N_FOURIER_GROUPS):
            cols = slice(g * gw, (g + 1) * gw)
            zr = [z_sc[0, j, :, cols] for j in range(DFT_RADIX)]
            zi = [z_sc[1, j, :, cols] for j in range(DFT_RADIX)]
            if k2 == 0:
                yr = zr[0] + zr[1] + zr[2] + zr[3]
                yi = zi[0] + zi[1] + zi[2] + zi[3]
            elif k2 == 1:
                yr = zr[0] + zi[1] - zr[2] - zi[3]
                yi = zi[0] - zr[1] - zi[2] + zr[3]
            elif k2 == 2:
                yr = zr[0] - zr[1] + zr[2] - zr[3]
                yi = zi[0] - zi[1] + zi[2] - zi[3]
            else:
                yr = zr[0] - zi[1] - zr[2] + zi[3]
                yi = zi[0] + zr[1] - zi[2] - zr[3]
            tt_sc[k2, 0:M, cols] = (cs * yr + sn * yi).astype(BF16)
            tt_sc[k2, M:2 * M, cols] = (cs * yi - sn * yr).astype(BF16)

    for cc in range(DFT_RADIX):
        @pl.when(c == cc)
        def _(cc=cc):
            if cc == 0:
                channel_stage()
                butterfly(0)
            o_ref[...] = jnp.dot(dft_ref[...], tt_sc[cc], preferred_element_type=F32).astype(BF16)
            if cc + 1 < DFT_RADIX:
                butterfly(cc + 1)


def _fourier(uf3, w_f):
    B, S, W = uf3.shape
    G = N_FOURIER_GROUPS
    gw = W // G
    M = S // DFT_RADIX
    norm = 1.0 / math.sqrt(S * gw)
    cidx = (np.arange(gw)[:, None] * np.arange(gw)[None, :]) % gw
    ang_c = cidx * (2.0 * math.pi / gw)
    cc = jnp.asarray(np.cos(ang_c) * norm, F32)
    sc = jnp.asarray(np.sin(ang_c) * norm, F32)
    hp = lax.Precision.HIGHEST
    wcs = jnp.concatenate([jnp.einsum("ab,gbd->gad", cc, w_f, precision=hp),
                           jnp.einsum("ab,gbd->gad", sc, w_f, precision=hp)], axis=-1).astype(BF16)
    n1 = np.arange(M)
    ang_t = ((np.arange(DFT_RADIX)[None, :] * n1[:, None]) % S) * (2.0 * math.pi / S)
    tw = jnp.asarray(np.concatenate([np.cos(ang_t), np.sin(ang_t),
                                     np.zeros((M, LANES - 2 * DFT_RADIX))], axis=1), F32)
    ang_p = ((n1[:, None] * n1[None, :]) % M) * (2.0 * math.pi / M)
    dft = jnp.asarray(np.concatenate([np.cos(ang_p), np.sin(ang_p)], axis=1).astype(np.float32)
                      .astype(jnp.bfloat16))
    kern = functools.partial(_fourier_kernel, M=M, gw=gw)
    return pl.pallas_call(
        kern,
        grid=(B, DFT_RADIX),
        in_specs=[pl.BlockSpec((None, S, W), lambda b, c: (b, 0, 0)),
                  pl.BlockSpec((G, gw, 2 * gw), lambda b, c: (0, 0, 0)),
                  pl.BlockSpec((M, LANES), lambda b, c: (0, 0)),
                  pl.BlockSpec((M, 2 * M), lambda b, c: (0, 0))],
        out_specs=pl.BlockSpec((None, M, W), lambda b, c: (b, 0, c)),
        out_shape=jax.ShapeDtypeStruct((B, M, DFT_RADIX * W), BF16),
        scratch_shapes=[pltpu.VMEM((2, DFT_RADIX, M, W), F32),
                        pltpu.VMEM((DFT_RADIX, 2 * M, W), BF16)],
        compiler_params=_cparams(("arbitrary", "arbitrary")),
        name="fourier",
    )(uf3, wcs, tw, dft)


def _bias_kernel(rb_ref, o_ref, *, t):
    h = pl.program_id(0)
    which = pl.program_id(1)
    nb = N_BUCKETS // 2

    def tile(d, buckets, r0=0, c0=0, nr=t, nc=t):
        row = r0 + lax.broadcasted_iota(jnp.int32, (nr, nc), 0)
        col = c0 + lax.broadcasted_iota(jnp.int32, (nr, nc), 1)
        rel = d * t + col - row
        max_exact = nb // 2
        ret = jnp.where(rel > 0, nb, 0)
        n = jnp.abs(rel)
        nf = jnp.maximum(n, 1).astype(F32)
        large = max_exact + (jnp.log(nf / max_exact) / math.log(MAX_DISTANCE / max_exact)
                             * (nb - max_exact)).astype(jnp.int32)
        large = jnp.minimum(large, nb - 1)
        bucket = ret + jnp.where(n < max_exact, n, large)
        val = jnp.zeros((nr, nc), F32)
        for j in buckets:
            val = jnp.where(bucket == j, rb_ref[j * N_DIFF_HEADS + h], val)
        return val * LOG2E

    left_b, right_b = _far_bucket_consts()
    corner = MAX_DISTANCE

    @pl.when(which == 0)
    def _():
        o_ref[...] = jnp.full((t, t), rb_ref[left_b * N_DIFF_HEADS + h] * LOG2E, F32)
        o_ref[0:corner, t - corner:t] = tile(-1, range(0, nb), 0, t - corner, corner, corner)

    @pl.when(which == 1)
    def _():
        o_ref[...] = tile(0, range(0, 2 * nb))

    @pl.when(which == 2)
    def _():
        o_ref[...] = jnp.full((t, t), rb_ref[right_b * N_DIFF_HEADS + h] * LOG2E, F32)
        o_ref[t - corner:t, 0:corner] = tile(1, range(nb, 2 * nb), t - corner, 0, corner, corner)

    @pl.when(which == 3)
    def _():
        o_ref[...] = jnp.zeros((t, t), F32)


def _bias_tiles(rel_bias, t):
    H = N_DIFF_HEADS
    assert t >= MAX_DISTANCE, "far key tiles must lie entirely in the saturated bucket"
    return pl.pallas_call(
        functools.partial(_bias_kernel, t=t),
        grid=(H, 4),
        in_specs=[pl.BlockSpec(memory_space=pltpu.SMEM)],
        out_specs=pl.BlockSpec((None, None, t, t), lambda h, d: (h, d, 0, 0)),
        out_shape=jax.ShapeDtypeStruct((H, 4, t, t), F32),
        compiler_params=_cparams(("arbitrary", "arbitrary")),
        name="t5_bias",
    )(rel_bias.reshape(-1))


def _far_bucket_consts():
    nb = N_BUCKETS // 2
    return nb - 1, 2 * nb - 1


def _attn_kernel(rb_ref, q_ref, k_ref, v_ref, b_ref, lam_ref, gs_ref, o_ref,
                 s_sc, p_sc, vext_sc, m_sc, raw_sc, *, t, n, n_tiles, lam_init):
    g = pl.program_id(0)
    cur = jnp.minimum(g, n_tiles - 1)
    h = (cur // n) % N_DIFF_HEADS
    i = cur % n
    half = LANES // 2

    @pl.when(g == 0)
    def _():
        raw_sc[...] = jnp.ones(raw_sc.shape, F32)

    @pl.when((i == 0) & (g < n_tiles))
    def _():
        vext_sc[:, :LANES] = v_ref[...]
        vext_sc[:, LANES:] = jnp.ones(v_ref.shape, BF16)

    left_b, right_b = _far_bucket_consts()
    c_left = rb_ref[left_b * N_DIFF_HEADS + h] * LOG2E
    c_right = rb_ref[right_b * N_DIFF_HEADS + h] * LOG2E
    lane = lax.broadcasted_iota(jnp.int32, (1, LANES), 1)

    def all_scores(m):
        fl = lane - half if m == 0 else lane
        is_data = (lane < half) if m == 0 else (lane >= half)
        f = jnp.where(fl < n, fl, fl - n)
        cfar = jnp.where(f < i - 1, c_left, jnp.where(f > i + 1, c_right, 0.0))
        c_hi = cfar.astype(BF16)
        c_lo = (cfar - c_hi.astype(F32)).astype(BF16)
        feat = jnp.where(fl < n, c_hi, c_lo)
        feat = jnp.where((fl >= 0) & (fl < 2 * n), feat, jnp.zeros_like(feat))
        qm = jnp.where(is_data, q_ref[:, m * LANES:(m + 1) * LANES], feat)
        for j in range(n):
            kc = k_ref[j * t:(j + 1) * t, m * LANES:(m + 1) * LANES]
            s = lax.dot_general(qm, kc, (((1,), (1,)), ((), ())), preferred_element_type=F32)
            d = j - i
            s = s + b_ref[jnp.where(jnp.abs(d) <= 1, d + 1, 3)]
            s_sc[m, j] = s
            tile_max = s[:, 0:LANES]
            for c in range(1, t // LANES):
                tile_max = jnp.maximum(tile_max, s[:, c * LANES:(c + 1) * LANES])
            if j == 0:
                m_sc[m] = tile_max
            else:
                m_sc[m] = jnp.maximum(m_sc[m], tile_max)

    def exponentials(m):
        mx = jnp.max(m_sc[m], axis=-1, keepdims=True)
        for j in range(n):
            p_sc[m, :, j * t:(j + 1) * t] = jnp.exp2(s_sc[m, j] - mx).astype(BF16)

    def finish_previous():
        r0 = raw_sc[0]
        r1 = raw_sc[1]
        lam = (jnp.exp(jnp.sum(lam_ref[0:1, :] * lam_ref[1:2, :], axis=-1, keepdims=True))
               - jnp.exp(jnp.sum(lam_ref[2:3, :] * lam_ref[3:4, :], axis=-1, keepdims=True)) + lam_init)
        o = r0[:, :LANES] / r0[:, LANES:LANES + 1] - lam * (r1[:, :LANES] / r1[:, LANES:LANES + 1])
        ms = jnp.mean(o * o, axis=-1, keepdims=True)
        o_ref[...] = (o * lax.rsqrt(ms + EPS) * gs_ref[...] * (1.0 - lam_init)).astype(BF16)

    @pl.when(i < n)
    def _():
        finish_previous()
        all_scores(0)

    @pl.when(g < n_tiles)
    def _():
        all_scores(1)
        exponentials(0)
        raw_sc[0] = jnp.dot(p_sc[0], vext_sc[...], preferred_element_type=F32)
        exponentials(1)
        raw_sc[1] = jnp.dot(p_sc[1], vext_sc[...], preferred_element_type=F32)


def _attention(rel_bias, q3, k3, v3, btiles, lam4, g_subln, t, lam_init):
    B, S, _ = q3.shape
    H = N_DIFF_HEADS
    n = S // t
    n_tiles = B * H * n
    kern = functools.partial(_attn_kernel, t=t, n=n, n_tiles=n_tiles, lam_init=lam_init)

    def tile_of(g):
        return g // (H * n), (g // n) % H, g % n

    def cur_map(fn):
        def index_map(g):
            return fn(*tile_of(jnp.minimum(g, n_tiles - 1)))
        return index_map

    def prev_map(g):
        b, h, i = tile_of(jnp.maximum(g - 1, 0))
        return (b, i, h)

    return pl.pallas_call(
        kern,
        grid=(n_tiles + 1,),
        in_specs=[pl.BlockSpec(memory_space=pltpu.SMEM),
                  pl.BlockSpec((None, t, 2 * LANES), cur_map(lambda b, h, i: (b, i, h))),
                  pl.BlockSpec((None, S, 2 * LANES), cur_map(lambda b, h, i: (b, 0, h))),
                  pl.BlockSpec((None, S, LANES), cur_map(lambda b, h, i: (b, 0, h))),
                  pl.BlockSpec((None, 4, t, t), cur_map(lambda b, h, i: (h, 0, 0, 0))),
                  pl.BlockSpec((4, LANES // 2), lambda g: (0, 0)),
                  pl.BlockSpec((1, LANES), lambda g: (0, 0))],
        out_specs=pl.BlockSpec((None, t, LANES), prev_map),
        out_shape=jax.ShapeDtypeStruct((B, S, H * LANES), BF16),
        scratch_shapes=[pltpu.VMEM((2, n, t, t), F32),
                        pltpu.VMEM((2, t, S), BF16),
                        pltpu.VMEM((S, 2 * LANES), BF16),
                        pltpu.VMEM((2, t, LANES), F32),
                        pltpu.VMEM((2, t, 2 * LANES), F32)],
        compiler_params=_cparams(("arbitrary",)),
        name="diff_attn",
    )(rel_bias.reshape(-1), q3, k3, v3, btiles, lam4, g_subln)


def _pack_bf16_pairs(v):
    bits = lax.bitcast_convert_type(v.astype(BF16).astype(F32), jnp.uint32)
    n = bits.shape[1] // 2
    return (bits[:, :n] >> 16) | (bits[:, n:] & jnp.uint32(0xFFFF0000))


def _unpack_bf16_pairs(p):
    lo = lax.bitcast_convert_type(p << 16, F32).astype(BF16)
    hi = lax.bitcast_convert_type(p & jnp.uint32(0xFFFF0000), F32).astype(BF16)
    return lo, hi


def _out_proj_kernel(yf_ref, perm_ref, yd_ref, x_ref, mod_ref, wo_ref, g_ref, wr_ref, br_ref,
                     x1_ref, hp_ref, lp_ref, grow_ref, cnt_ref, *, tm, fw, E):
    yf_stack = jnp.concatenate([yf_ref[:, r * fw:(r + 1) * fw] for r in range(DFT_RADIX)], axis=0)
    yf = jnp.dot(perm_ref[...], yf_stack, preferred_element_type=F32).astype(BF16)
    mix = (jnp.dot(yf, wo_ref[:fw, :], preferred_element_type=F32)
           + jnp.dot(yd_ref[...], wo_ref[fw:, :], preferred_element_type=F32))
    x1 = x_ref[...] + mod_ref[2:3, :] * mix
    x1_ref[...] = x1
    ms = jnp.mean(x1 * x1, axis=-1, keepdims=True)
    h = x1 * lax.rsqrt(ms + EPS) * g_ref[...]
    h = h * (1.0 + mod_ref[4:5, :]) + mod_ref[3:4, :]
    hb = h.astype(BF16)
    hp_ref[...] = _pack_bf16_pairs(hb)

    logits = lax.dot_general(wr_ref[...], hb, (((1,), (1,)), ((), ())),
                             preferred_element_type=F32) + br_ref[...]
    rowid = lax.broadcasted_iota(jnp.int32, (E, tm), 0)
    vals, idxs = [], []
    comb = jnp.zeros((E, tm), F32)
    l = logits
    for _ in range(TOP_K):
        mv = jnp.max(l, axis=0, keepdims=True)
        ix = jnp.min(jnp.where(l == mv, rowid, E), axis=0, keepdims=True)
        sel = rowid == ix
        vals.append(mv)
        idxs.append(ix)
        comb = comb + sel.astype(F32)
        l = jnp.where(sel, -jnp.inf, l)
    es = [jnp.exp(v - vals[0]) for v in vals]
    den = es[0] + es[1] + es[2] + es[3]
    gates = [e / den for e in es]

    r_i = lax.broadcasted_iota(jnp.int32, (tm, tm), 0)
    c_i = lax.broadcasted_iota(jnp.int32, (tm, tm), 1)
    upper = (r_i < c_i).astype(BF16)
    before = jnp.dot(comb.astype(BF16), upper, preferred_element_type=F32)
    cnt = jnp.sum(comb, axis=1, keepdims=True)
    cnt_ref[...] = jnp.broadcast_to(cnt, cnt_ref.shape).astype(jnp.int32)
    chunks = jnp.floor((cnt + (ROW_CHUNK - 1)) * (1.0 / ROW_CHUNK))
    e_r = lax.broadcasted_iota(jnp.int32, (E, E), 0)
    e_c = lax.broadcasted_iota(jnp.int32, (E, E), 1)
    run_start = ROW_CHUNK * jnp.dot((e_c < e_r).astype(BF16),
                                    jnp.broadcast_to(chunks, (E, LANES)).astype(BF16),
                                    preferred_element_type=F32)[:, 0:1]
    ranks = [jnp.sum(jnp.where(rowid == ix, before + run_start, 0.0), axis=0, keepdims=True) for ix in idxs]

    pad_i = jnp.full((8 - TOP_K, tm), -1, jnp.int32)
    lp_ref[...] = jnp.concatenate([r.astype(jnp.int32) for r in ranks] + [pad_i], axis=0)
    gpad = jnp.concatenate(gates + [jnp.zeros((LANES - TOP_K, tm), F32)], axis=0)
    grow_ref[...] = gpad.T


def _out_proj(yf3, yd2, x2, mod3, w_out_b, g_ffn, wr_t, br_t, S, tm):
    T, D = x2.shape
    fw = yd2.shape[1]
    E = wr_t.shape[0]
    nt = T // tm
    tiles_per_batch = S // tm
    q = tm // DFT_RADIX
    col = np.arange(tm)
    perm = jnp.asarray(np.arange(tm)[:, None] == DFT_RADIX * (col % q) + col // q, BF16)
    kern = functools.partial(_out_proj_kernel, tm=tm, fw=fw, E=E)
    return pl.pallas_call(
        kern,
        grid=(nt,),
        in_specs=[pl.BlockSpec((None, q, DFT_RADIX * fw), lambda i: (i // tiles_per_batch, i % tiles_per_batch, 0)),
                  pl.BlockSpec((tm, tm), lambda i: (0, 0)),
                  pl.BlockSpec((tm, fw), lambda i: (i, 0)),
                  pl.BlockSpec((tm, D), lambda i: (i, 0)),
                  pl.BlockSpec((None, 6, D), lambda i: (i // tiles_per_batch, 0, 0)),
                  pl.BlockSpec((D, D), lambda i: (0, 0)),
                  pl.BlockSpec((1, D), lambda i: (0, 0)),
                  pl.BlockSpec((E, D), lambda i: (0, 0)),
                  pl.BlockSpec((E, 1), lambda i: (0, 0))],
        out_specs=[pl.BlockSpec((tm, D), lambda i: (i, 0)),
                   pl.BlockSpec((tm, D // 2), lambda i: (i, 0)),
                   pl.BlockSpec((8, tm), lambda i: (0, i)),
                   pl.BlockSpec((tm, LANES), lambda i: (i, 0)),
                   pl.BlockSpec((None, E, LANES), lambda i: (i, 0, 0))],
        out_shape=[jax.ShapeDtypeStruct((T, D), F32),
                   jax.ShapeDtypeStruct((T, D // 2), jnp.uint32),
                   jax.ShapeDtypeStruct((8, T), jnp.int32),
                   jax.ShapeDtypeStruct((T, LANES), F32),
                   jax.ShapeDtypeStruct((nt, E, LANES), jnp.int32)],
        compiler_params=_cparams(("arbitrary",)),
        name="out_proj_router",
    )(yf3, perm, yd2, x2, mod3, w_out_b, g_ffn, wr_t, br_t)


ROW_CHUNK = 8
CHUNK_UNROLL = 8


def _dispatch_kernel(tbl_ref, zb_ref, lp_ref, hp_ref, xs_ref, buf, zero_sc, sem, zsem, tsem,
                     *, tm, bm, NR, nz, nz_first, nt):
    i = pl.program_id(0)
    s = i % 2

    def zero_copy(e, zs):
        return pltpu.make_async_copy(zero_sc, xs_ref.at[pl.ds(zb_ref[e] * bm, bm)], zs)

    @pl.when(i == 0)
    def _():
        zero_sc[...] = jnp.zeros(zero_sc.shape, jnp.uint32)
        for e in range(nz):
            @pl.when(zb_ref[e] >= 0)
            def _():
                zero_copy(e, zsem if e < nz_first else tsem).start()
        for e in range(nz_first):
            @pl.when(zb_ref[e] >= 0)
            def _():
                zero_copy(e, zsem).wait()

    @pl.when(i == nt - 1)
    def _():
        for e in range(nz_first, nz):
            @pl.when(zb_ref[e] >= 0)
            def _():
                zero_copy(e, tsem).wait()

    lp = lp_ref[...]
    rows = lax.broadcasted_iota(jnp.int32, (NR, tm), 0)
    hit = rows == lp[0:1, :]
    for k in range(1, TOP_K):
        hit = hit | (rows == lp[k:k + 1, :])
    sel = jnp.where(hit, 1.0, 0.0).astype(BF16)
    lo, hi = _unpack_bf16_pairs(hp_ref[...])
    slo = jnp.dot(sel, lo, preferred_element_type=F32)
    shi = jnp.dot(sel, hi, preferred_element_type=F32)
    buf[s] = ((lax.bitcast_convert_type(slo, jnp.uint32) >> 16)
              | (lax.bitcast_convert_type(shi, jnp.uint32) & jnp.uint32(0xFFFF0000)))

    for slot in range(2):
        def issue(c, carry, slot=slot):
            d = pl.multiple_of(tbl_ref[0, 0, c], ROW_CHUNK)
            r = pl.multiple_of(c * ROW_CHUNK, ROW_CHUNK)
            pltpu.make_async_copy(buf.at[slot, pl.ds(r, ROW_CHUNK)], xs_ref.at[pl.ds(d, ROW_CHUNK)],
                                  sem.at[slot]).start()
            return carry

        @pl.when(s == slot)
        def _():
            lax.fori_loop(0, NR // ROW_CHUNK, issue, 0, unroll=CHUNK_UNROLL)

    def wait_slot(slot):
        pltpu.make_async_copy(buf.at[slot], xs_ref.at[pl.ds(0, NR)], sem.at[slot]).wait()

    @pl.when(i > 0)
    def _():
        wait_slot(1 - s)

    @pl.when(i == nt - 1)
    def _():
        wait_slot(s)


def _dispatch(tbl3, zero_blk, nz_first, lp8, hp, P, bm, tm, NR):
    T, Wd = hp.shape
    nt = T // tm
    nz = zero_blk.shape[0]
    nch_tbl = tbl3.shape[2]
    kern = functools.partial(_dispatch_kernel, tm=tm, bm=bm, NR=NR, nz=nz, nz_first=nz_first, nt=nt)
    return pl.pallas_call(
        kern,
        grid=(nt,),
        in_specs=[pl.BlockSpec((1, 1, nch_tbl), lambda i: (i, 0, 0), memory_space=pltpu.SMEM),
                  pl.BlockSpec(memory_space=pltpu.SMEM),
                  pl.BlockSpec((8, tm), lambda i: (0, i)),
                  pl.BlockSpec((tm, Wd), lambda i: (i, 0))],
        out_specs=pl.BlockSpec(memory_space=pl.ANY),
        out_shape=jax.ShapeDtypeStruct((P, Wd), jnp.uint32),
        scratch_shapes=[pltpu.VMEM((2, NR, Wd), jnp.uint32),
                        pltpu.VMEM((bm, Wd), jnp.uint32),
                        pltpu.SemaphoreType.DMA((2,)),
                        pltpu.SemaphoreType.DMA,
                        pltpu.SemaphoreType.DMA],
        compiler_params=_cparams(("arbitrary",)),
        name="moe_dispatch",
    )(tbl3, zero_blk, lp8, hp)


def _expert_kernel(be_ref, nu_ref, h2_ref, ord_ref, x_ref, w1_ref, b1g_ref, b1l_ref, w2_ref, b2_ref, pm_ref,
                   o_ref, w1g_sc, w1l_sc, w2_sc):
    blk = pl.program_id(0)
    nused = nu_ref[0]
    e = be_ref[blk]
    prev = be_ref[jnp.maximum(blk - 1, 0)]
    active = blk < nused
    second = active & (h2_ref[blk] != 0)
    half = x_ref.shape[0] // 2
    tile = 2 * LANES
    n_tiles = w1_ref.shape[1] // tile

    @pl.when(active & ((blk == 0) | (e != prev)))
    def _():
        for c in range(n_tiles):
            wt = w1_ref[:, c * tile:(c + 1) * tile].astype(BF16)
            pw = jnp.dot(wt, pm_ref[...], preferred_element_type=F32)
            w1g_sc[:, c * LANES:(c + 1) * LANES] = pw[:, :LANES].astype(BF16)
            w1l_sc[:, c * LANES:(c + 1) * LANES] = pw[:, LANES:].astype(BF16)
        w2_sc[...] = w2_ref[...].astype(BF16)

    @pl.when(jnp.logical_not(active))
    def _():
        o_ref[...] = jnp.zeros(o_ref.shape, jnp.uint32)

    def mlp(r0):
        xp = x_ref[r0:r0 + half, :]
        hw = xp.shape[1]
        xlo, xhi = _unpack_bf16_pairs(xp)
        zg = (jnp.dot(xlo, w1g_sc[:hw, :], preferred_element_type=F32)
              + jnp.dot(xhi, w1g_sc[hw:, :], preferred_element_type=F32) + b1g_ref[...])
        zl = (jnp.dot(xlo, w1l_sc[:hw, :], preferred_element_type=F32)
              + jnp.dot(xhi, w1l_sc[hw:, :], preferred_element_type=F32) + b1l_ref[...])
        g = jnp.minimum(zg, SWIGLU_LIMIT)
        lin = jnp.clip(zl, -SWIGLU_LIMIT, SWIGLU_LIMIT)
        act = g * jax.nn.sigmoid(SWIGLU_ALPHA * g) * (lin + 1.0)
        y = jnp.dot(act.astype(BF16), w2_sc[...], preferred_element_type=F32) + b2_ref[...]
        o_ref[r0:r0 + half, :] = _pack_bf16_pairs(y)

    @pl.when(second)
    def _():
        mlp(0)
        mlp(half)

    @pl.when(active & jnp.logical_not(second))
    def _():
        mlp(0)
        o_ref[half:, :] = jnp.zeros((half, o_ref.shape[1]), jnp.uint32)


def _experts(blk_expert, nused, half2, order, xs, P, w1, b1g, b1l, w2, b2, bm):
    Wd = xs.shape[1]
    E, D, F2 = w1.shape
    F = F2 // 2
    nblk = P // bm
    tile = 2 * LANES
    pm = np.zeros((tile, tile), np.float32)
    pm[2 * np.arange(LANES), np.arange(LANES)] = 1.0
    pm[2 * np.arange(LANES) + 1, LANES + np.arange(LANES)] = 1.0
    pm = jnp.asarray(pm, BF16)

    def row_map(b, be, nu, h2, od):
        return (jnp.minimum(od[b], nu[0] - 1), 0)

    def exp_map(b, be, nu, h2, od):
        return (be[b], 0, 0)

    grid_spec = pltpu.PrefetchScalarGridSpec(
        num_scalar_prefetch=4,
        grid=(nblk,),
        in_specs=[pl.BlockSpec((bm, Wd), row_map),
                  pl.BlockSpec((None, D, F2), exp_map),
                  pl.BlockSpec((None, 1, F), exp_map),
                  pl.BlockSpec((None, 1, F), exp_map),
                  pl.BlockSpec((None, F, D), exp_map),
                  pl.BlockSpec((None, 1, D), exp_map),
                  pl.BlockSpec((tile, tile), lambda b, be, nu, h2, od: (0, 0))],
        out_specs=pl.BlockSpec((bm, D // 2), lambda b, be, nu, h2, od: (od[b], 0)),
        scratch_shapes=[pltpu.VMEM((D, F), BF16),
                        pltpu.VMEM((D, F), BF16),
                        pltpu.VMEM((F, D), BF16)],
    )
    return pl.pallas_call(
        _expert_kernel,
        grid_spec=grid_spec,
        out_shape=jax.ShapeDtypeStruct((P, D // 2), jnp.uint32),
        compiler_params=_cparams(("arbitrary",)),
        name="moe_experts",
    )(blk_expert, nused, half2, order, xs, w1, b1g, b1l, w2, b2, pm)


def _combine_kernel(tcur_ref, tnxt_ref, lp_ref, grow_ref, x1_ref, mod_ref, y_ref, o_ref,
                    buf, sem, *, tm, NR, nt):
    i = pl.program_id(0)
    s = i % 2

    def issue(tref, slot):
        def body(c, carry):
            d = pl.multiple_of(tref[0, 0, c], ROW_CHUNK)
            r = pl.multiple_of(c * ROW_CHUNK, ROW_CHUNK)
            pltpu.make_async_copy(y_ref.at[pl.ds(d, ROW_CHUNK)], buf.at[slot, pl.ds(r, ROW_CHUNK)],
                                  sem.at[slot]).start()
            return carry
        lax.fori_loop(0, NR // ROW_CHUNK, body, 0, unroll=CHUNK_UNROLL)

    @pl.when(i == 0)
    def _():
        issue(tcur_ref, 0)

    for slot in range(2):
        @pl.when((i + 1 < nt) & (s != slot))
        def _():
            issue(tnxt_ref, slot)

    pltpu.make_async_copy(y_ref.at[pl.ds(0, NR)], buf.at[s], sem.at[s]).wait()

    ylo, yhi = _unpack_bf16_pairs(buf[s])
    lpf = jnp.concatenate([lp_ref[...].astype(F32), jnp.zeros((LANES - 8, tm), F32)], axis=0)
    lpr = lpf.T.astype(jnp.int32)
    gr = grow_ref[...]
    cols = lax.broadcasted_iota(jnp.int32, (tm, NR), 1)
    gmat = jnp.zeros((tm, NR), F32)
    for k in range(TOP_K):
        gmat = jnp.where(cols == lpr[:, k:k + 1], gr[:, k:k + 1], gmat)
    gmat = gmat.astype(BF16)
    hw = ylo.shape[1]
    o_ref[:, :hw] = x1_ref[:, :hw] + mod_ref[5:6, :hw] * jnp.dot(gmat, ylo, preferred_element_type=F32)
    o_ref[:, hw:] = x1_ref[:, hw:] + mod_ref[5:6, hw:] * jnp.dot(gmat, yhi, preferred_element_type=F32)


def _combine(tbl3, lp8, grow, x1, mod3, y, S, tm, NR):
    T, D = x1.shape
    nt = T // tm
    tiles_per_batch = S // tm
    nch_tbl = tbl3.shape[2]
    kern = functools.partial(_combine_kernel, tm=tm, NR=NR, nt=nt)
    return pl.pallas_call(
        kern,
        grid=(nt,),
        in_specs=[pl.BlockSpec((1, 1, nch_tbl), lambda i: (i, 0, 0), memory_space=pltpu.SMEM),
                  pl.BlockSpec((1, 1, nch_tbl), lambda i: (jnp.minimum(i + 1, nt - 1), 0, 0),
                               memory_space=pltpu.SMEM),
                  pl.BlockSpec((8, tm), lambda i: (0, i)),
                  pl.BlockSpec((tm, LANES), lambda i: (i, 0)),
                  pl.BlockSpec((tm, D), lambda i: (i, 0)),
                  pl.BlockSpec((None, 6, D), lambda i: (i // tiles_per_batch, 0, 0)),
                  pl.BlockSpec(memory_space=pl.ANY)],
        out_specs=pl.BlockSpec((tm, D), lambda i: (i, 0)),
        out_shape=jax.ShapeDtypeStruct((T, D), F32),
        scratch_shapes=[pltpu.VMEM((2, NR, D // 2), jnp.uint32),
                        pltpu.SemaphoreType.DMA((2,))],
        compiler_params=_cparams(("arbitrary",)),
        name="moe_combine",
    )(tbl3, tbl3, lp8, grow, x1, mod3, y)


def _tiles(S):
    t_attn = min(512, S // 4) if S >= 512 else S
    tm = min(512, t_attn)
    return dict(tm=tm, t_attn=t_attn, bm=1024)


def _moe_layout(cnt, T, tm, bm):
    nt, E = cnt.shape
    c8 = (cnt + ROW_CHUNK - 1) // ROW_CHUNK * ROW_CHUNK
    tile_end = jnp.cumsum(c8, axis=1)
    tile_off = tile_end - c8
    used = tile_end[:, -1]
    off = jnp.cumsum(c8, axis=0) - c8
    tot = jnp.sum(c8, axis=0)
    padded = (tot + bm - 1) // bm * bm
    pends = jnp.cumsum(padded)
    pstarts = pends - padded
    NR = TOP_K * tm + ROW_CHUNK * E
    nch_tbl = NR // ROW_CHUNK
    P = (T * TOP_K + nt * E * (ROW_CHUNK - 1) + E * bm + bm - 1) // bm * bm
    nblk = P // bm

    c_row = jnp.arange(nch_tbl, dtype=jnp.int32) * ROW_CHUNK
    inrun = (tile_off[:, :, None] <= c_row) & (c_row < tile_end[:, :, None])
    base = pstarts[None, :] + off - tile_off
    tbl = jnp.sum(jnp.where(inrun, base[:, :, None], 0), axis=1) + c_row[None, :]
    in_use = c_row[None, :] < used[:, None]
    spare_rows = NR - TOP_K * tm
    spare = P + jnp.arange(nt, dtype=jnp.int32)[:, None] * spare_rows + (c_row[None, :] - used[:, None])
    tbl_out = jnp.where(in_use, tbl, spare).astype(jnp.int32)
    tbl_in = jnp.where(in_use, tbl, 0).astype(jnp.int32)
    n_spare_blk = (nt * spare_rows + bm - 1) // bm

    nused = (pends[-1] // bm).astype(jnp.int32).reshape(1)
    blk_start = jnp.arange(nblk, dtype=jnp.int32) * bm
    blk_expert = jnp.minimum(jnp.sum(pends[None, :] <= blk_start[:, None], axis=1), E - 1).astype(jnp.int32)
    half2 = (blk_start + bm // 2 < (pstarts + tot)[blk_expert]).astype(jnp.int32)
    step = jnp.arange(nblk, dtype=jnp.int32)
    first_of = (pstarts // bm)[blk_expert]
    last_of = (pends // bm - 1)[blk_expert]
    order = jnp.where(step >= nused[0], step, jnp.where(step == first_of, last_of, step - 1)).astype(jnp.int32)
    half2 = half2[order]
    last_blk = jnp.where(padded > 0, pends // bm - 1, -1)
    n_tail = nblk - (T * TOP_K) // bm
    tail_blk = nused[0] + jnp.arange(n_tail, dtype=jnp.int32)
    tail_blk = jnp.where(tail_blk < nblk, tail_blk, -1)
    spare_blk = nblk + jnp.arange(n_spare_blk, dtype=jnp.int32)
    zero_blk = jnp.concatenate([last_blk, spare_blk, tail_blk]).astype(jnp.int32)
    return dict(tbl_out=tbl_out.reshape(nt, 1, nch_tbl), tbl_in=tbl_in.reshape(nt, 1, nch_tbl),
                nused=nused, blk_expert=blk_expert, half2=half2, order=order, zero_blk=zero_blk,
                nz_first=E + n_spare_blk,
                P=P, P_alloc=P + n_spare_blk * bm, NR=NR)


def kernel(x, c, w_ada, b_ada, g_mix, w_in, w_fourier, q_norm_g, k_norm_g, lambda_q1, lambda_k1,
           lambda_q2, lambda_k2, g_subln, w_out, rel_bias, g_ffn, w_router, b_router, w1, b1, w2, b2):
    B, S, D = x.shape
    T = B * S
    L = w_ada.shape[0]
    E = w_router.shape[-1]
    cfg = _tiles(S)
    tm, t_attn, bm = cfg["tm"], cfg["t_attn"], cfg["bm"]
    dqk = q_norm_g.shape[-1]
    scale = dqk ** -0.5
    fw = w_in.shape[-1] // 4

    btiles = _bias_tiles(rel_bias, t_attn)
    x2 = x.reshape(T, D)
    for l in range(L):
        lam_init = 0.8 - 0.6 * math.exp(-0.3 * l)
        mod3 = _ada(c, w_ada[l], b_ada[l]).reshape(B, 6, D)
        gq = (jnp.tile(q_norm_g[l], 4) * (scale * LOG2E)).reshape(1, 2 * LANES)
        gk = jnp.tile(k_norm_g[l], 4).reshape(1, 2 * LANES)
        uf, qx, kx, v = _in_proj(x2, mod3, g_mix[l].reshape(1, D), w_in[l].astype(BF16), gq, gk,
                                 S, tm, t_attn)
        yf = _fourier(uf.reshape(B, S, fw), w_fourier[l])
        lam4 = jnp.stack([lambda_q1[l], lambda_k1[l], lambda_q2[l], lambda_k2[l]]).astype(F32)
        yd = _attention(rel_bias, qx.reshape(B, S, -1), kx.reshape(B, S, -1), v.reshape(B, S, fw),
                        btiles, lam4, g_subln[l].reshape(1, LANES), t_attn, lam_init).reshape(T, fw)
        x1, hp, lp8, grow, cnt3 = _out_proj(
            yf, yd, x2, mod3, w_out[l].astype(BF16), g_ffn[l].reshape(1, D),
            w_router[l].T.astype(BF16), b_router[l].reshape(E, 1), S, tm)
        lay = _moe_layout(cnt3[:, :, 0], T, tm, bm)
        xs = _dispatch(lay["tbl_out"], lay["zero_blk"], lay["nz_first"], lp8, hp, lay["P_alloc"], bm, tm,
                       lay["NR"])
        y = _experts(lay["blk_expert"], lay["nused"], lay["half2"], lay["order"], xs, lay["P"], w1[l],
                     b1[l][:, None, 0::2],
                     b1[l][:, None, 1::2], w2[l], b2[l][:, None, :], bm)
        x2 = _combine(lay["tbl_in"], lp8, grow, x1, mod3, y, S, tm, lay["NR"])
    return x2.reshape(B, S, D)
```

```python
import functools
import math

import numpy as np
import jax
import jax.numpy as jnp
from jax import lax
from jax.experimental import pallas as pl
from jax.experimental.pallas import tpu as pltpu

F32 = jnp.float32
BF16 = jnp.bfloat16

EPS = 1e-6
N_FOURIER_GROUPS = 4
N_DIFF_HEADS = 4
TOP_K = 4
N_BUCKETS = 32
MAX_DISTANCE = 128
SWIGLU_ALPHA = 1.702
SWIGLU_LIMIT = 7.0
LANES = 128
DFT_RADIX = 4
LOG2E = 1.4426950408889634
VMEM_LIMIT = 56 * 1024 * 1024


def _cparams(sem):
    return pltpu.CompilerParams(dimension_semantics=sem, vmem_limit_bytes=VMEM_LIMIT)


def _ada_kernel(c_ref, w_ref, b_ref, o_ref):
    c = c_ref[...]
    cond = c * jax.nn.sigmoid(c)
    o_ref[...] = jnp.dot(cond.astype(BF16), w_ref[...].astype(BF16),
                         preferred_element_type=F32) + b_ref[...]


def _ada(c, w, b):
    B, D = c.shape
    N = w.shape[1]
    tn = 1536 if N % 1536 == 0 else N
    return pl.pallas_call(
        _ada_kernel,
        grid=(N // tn,),
        in_specs=[pl.BlockSpec((B, D), lambda j: (0, 0)),
                  pl.BlockSpec((D, tn), lambda j: (0, j)),
                  pl.BlockSpec((1, tn), lambda j: (0, j))],
        out_specs=pl.BlockSpec((B, tn), lambda j: (0, j)),
        out_shape=jax.ShapeDtypeStruct((B, N), F32),
        compiler_params=_cparams(("arbitrary",)),
        name="ada",
    )(c, w, b.reshape(1, N))


def _group_rms_inv(xh, gmat, group):
    ss = jnp.dot((xh * xh).astype(BF16), gmat, preferred_element_type=F32)
    return lax.rsqrt(ss * (1.0 / group) + EPS)


def _in_proj_kernel(x_ref, mod_ref, g_ref, w_ref, gq_ref, gk_ref, gmat_ref,
                    uf_ref, q_ref, k_ref, v_ref, *, tiles_per_batch, tm, t_attn, n_chunks, fw, qw):
    i = pl.program_id(0)
    x = x_ref[...]
    ms = jnp.mean(x * x, axis=-1, keepdims=True)
    y = x * lax.rsqrt(ms + EPS) * g_ref[...]
    h = y * (1.0 + mod_ref[1:2, :]) + mod_ref[0:1, :]
    proj = jnp.dot(h.astype(BF16), w_ref[...], preferred_element_type=F32)
    uf_ref[...] = proj[:, :fw].astype(BF16)
    v_ref[...] = proj[:, fw + 2 * qw:].astype(BF16)

    lane = lax.broadcasted_iota(jnp.int32, (1, LANES), 1)
    chunk = ((i % tiles_per_batch) * tm) // t_attn
    gmat = gmat_ref[...]
    half = LANES // 2
    pair = 2 * LANES
    for hd in range(N_DIFF_HEADS):
        if hd % 2 == 0:
            q2 = proj[:, fw + hd * LANES: fw + hd * LANES + pair]
            k2 = proj[:, fw + qw + hd * LANES: fw + qw + hd * LANES + pair]
            qn2 = (q2 * _group_rms_inv(q2, gmat, half) * gq_ref[...]).astype(BF16)
            kn2 = (k2 * _group_rms_inv(k2, gmat, half) * gk_ref[...]).astype(BF16)
        qn = qn2[:, (hd % 2) * LANES:(hd % 2 + 1) * LANES]
        kn = kn2[:, (hd % 2) * LANES:(hd % 2 + 1) * LANES]
        for m in range(2):
            fl = lane - half if m == 0 else lane
            is_data = (lane < half) if m == 0 else (lane >= half)
            f = jnp.where(fl < n_chunks, fl, fl - n_chunks)
            onehot = ((fl >= 0) & (fl < 2 * n_chunks) & (f == chunk)).astype(BF16)
            c0 = (2 * hd + m) * LANES
            q_ref[:, c0:c0 + LANES] = jnp.where(is_data, qn, jnp.zeros_like(qn))
            k_ref[:, c0:c0 + LANES] = jnp.where(is_data, kn, onehot)


def _in_proj(x2, mod3, g_mix, w_in_b, gq, gk, S, tm, t_attn):
    T, D = x2.shape
    ncols = w_in_b.shape[1]
    fw = ncols // 4
    qw = fw
    tiles_per_batch = S // tm
    n_chunks = S // t_attn
    half = LANES // 2
    pair = 2 * LANES
    gmat = (np.arange(pair)[:, None] // half == np.arange(pair)[None, :] // half)
    gmat = jnp.asarray(gmat, BF16)
    kern = functools.partial(_in_proj_kernel, tiles_per_batch=tiles_per_batch, tm=tm, t_attn=t_attn,
                             n_chunks=n_chunks, fw=fw, qw=qw)
    ext = 2 * qw
    return pl.pallas_call(
        kern,
        grid=(T // tm,),
        in_specs=[pl.BlockSpec((tm, D), lambda i: (i, 0)),
                  pl.BlockSpec((None, 6, D), lambda i: (i // tiles_per_batch, 0, 0)),
                  pl.BlockSpec((1, D), lambda i: (0, 0)),
                  pl.BlockSpec((D, ncols), lambda i: (0, 0)),
                  pl.BlockSpec((1, pair), lambda i: (0, 0)),
                  pl.BlockSpec((1, pair), lambda i: (0, 0)),
                  pl.BlockSpec((pair, pair), lambda i: (0, 0))],
        out_specs=[pl.BlockSpec((tm, fw), lambda i: (i, 0)),
                   pl.BlockSpec((tm, ext), lambda i: (i, 0)),
                   pl.BlockSpec((tm, ext), lambda i: (i, 0)),
                   pl.BlockSpec((tm, fw), lambda i: (i, 0))],
        out_shape=[jax.ShapeDtypeStruct((T, fw), BF16),
                   jax.ShapeDtypeStruct((T, ext), BF16),
                   jax.ShapeDtypeStruct((T, ext), BF16),
                   jax.ShapeDtypeStruct((T, fw), BF16)],
        compiler_params=_cparams(("arbitrary",)),
        name="in_proj",
    )(x2, mod3, g_mix, w_in_b, gq, gk, gmat)


def _fourier_kernel(uf_ref, wcs_ref, tw_ref, dft_ref, o_ref, z_sc, tt_sc, *, M, gw):
    c = pl.program_id(1)

    def channel_stage():
        for g in range(N_FOURIER_GROUPS):
            for j in range(DFT_RADIX):
                ab = jnp.dot(uf_ref[j * M:(j + 1) * M, g * gw:(g + 1) * gw], wcs_ref[g],
                             preferred_element_type=F32)
                z_sc[0, j, :, g * gw:(g + 1) * gw] = ab[:, :gw]
                z_sc[1, j, :, g * gw:(g + 1) * gw] = -ab[:, gw:]

    def butterfly(k2):
        cs = tw_ref[:, k2:k2 + 1]
        sn = tw_ref[:, DFT_RADIX + k2:DFT_RADIX + k2 + 1]
        for g in range(N_FOURIER_GROUPS):
            cols = slice(g * gw, (g + 1) * gw)
            zr = [z_sc[0, j, :, cols] for j in range(DFT_RADIX)]
            zi = [z_sc[1, j, :, cols] for j in range(DFT_RADIX)]
            if k2 == 0:
                yr = zr[0] + zr[1] + zr[2] + zr[3]
                yi = zi[0] + zi[1] + zi[2] + zi[3]
            elif k2 == 1:
                yr = zr[0] + zi[1] - zr[2] - zi[3]
                yi = zi[0] - zr[1] - zi[2] + zr[3]
            elif k2 == 2:
                yr = zr[0] - zr[1] + zr[2] - zr[3]
                yi = zi[0] - zi[1] + zi[2] - zi[3]
            else:
                yr = zr[0] - zi[1] - zr[2] + zi[3]
                yi = zi[0] + zr[1] - zi[2] - zr[3]
            tt_sc[k2, 0:M, cols] = (cs * yr + sn * yi).astype(BF16)
            tt_sc[k2, M:2 * M, cols] = (cs * yi - sn * yr).astype(BF16)

    for cc in range(DFT_RADIX):
        @pl.when(c == cc)
        def _(cc=cc):
            if cc == 0:
                channel_stage()
                butterfly(0)
            o_ref[...] = jnp.dot(dft_ref[...], tt_sc[cc], preferred_element_type=F32).astype(BF16)
            if cc + 1 < DFT_RADIX:
                butterfly(cc + 1)


def _fourier(uf3, w_f):
    B, S, W = uf3.shape
    G = N_FOURIER_GROUPS
    gw = W // G
    M = S // DFT_RADIX
    norm = 1.0 / math.sqrt(S * gw)
    cidx = (np.arange(gw)[:, None] * np.arange(gw)[None, :]) % gw
    ang_c = cidx * (2.0 * math.pi / gw)
    cc = jnp.asarray(np.cos(ang_c) * norm, F32)
    sc = jnp.asarray(np.sin(ang_c) * norm, F32)
    hp = lax.Precision.HIGHEST
    wcs = jnp.concatenate([jnp.einsum("ab,gbd->gad", cc, w_f, precision=hp),
                           jnp.einsum("ab,gbd->gad", sc, w_f, precision=hp)], axis=-1).astype(BF16)
    n1 = np.arange(M)
    ang_t = ((np.arange(DFT_RADIX)[None, :] * n1[:, None]) % S) * (2.0 * math.pi / S)
    tw = jnp.asarray(np.concatenate([np.cos(ang_t), np.sin(ang_t),
                                     np.zeros((M, LANES - 2 * DFT_RADIX))], axis=1), F32)
    ang_p = ((n1[:, None] * n1[None, :]) % M) * (2.0 * math.pi / M)
    dft = jnp.asarray(np.concatenate([np.cos(ang_p), np.sin(ang_p)], axis=1).astype(np.float32)
                      .astype(jnp.bfloat16))
    kern = functools.partial(_fourier_kernel, M=M, gw=gw)
    return pl.pallas_call(
        kern,
        grid=(B, DFT_RADIX),
        in_specs=[pl.BlockSpec((None, S, W), lambda b, c: (b, 0, 0)),
                  pl.BlockSpec((G, gw, 2 * gw), lambda b, c: (0, 0, 0)),
                  pl.BlockSpec((M, LANES), lambda b, c: (0, 0)),
                  pl.BlockSpec((M, 2 * M), lambda b, c: (0, 0))],
        out_specs=pl.BlockSpec((None, M, W), lambda b, c: (b, 0, c)),
        out_shape=jax.ShapeDtypeStruct((B, M, DFT_RADIX * W), BF16),
        scratch_shapes=[pltpu.VMEM((2, DFT_RADIX, M, W), F32),
                        pltpu.VMEM((DFT_RADIX, 2 * M, W), BF16)],
        compiler_params=_cparams(("arbitrary", "arbitrary")),
        name="fourier",
    )(uf3, wcs, tw, dft)


def _bias_kernel(rb_ref, o_ref, *, t):
    h = pl.program_id(0)
    which = pl.program_id(1)
    nb = N_BUCKETS // 2

    def tile(d, buckets, r0=0, c0=0, nr=t, nc=t):
        row = r0 + lax.broadcasted_iota(jnp.int32, (nr, nc), 0)
        col = c0 + lax.broadcasted_iota(jnp.int32, (nr, nc), 1)
        rel = d * t + col - row
        max_exact = nb // 2
        ret = jnp.where(rel > 0, nb, 0)
        n = jnp.abs(rel)
        nf = jnp.maximum(n, 1).astype(F32)
        large = max_exact + (jnp.log(nf / max_exact) / math.log(MAX_DISTANCE / max_exact)
                             * (nb - max_exact)).astype(jnp.int32)
        large = jnp.minimum(large, nb - 1)
        bucket = ret + jnp.where(n < max_exact, n, large)
        val = jnp.zeros((nr, nc), F32)
        for j in buckets:
            val = jnp.where(bucket == j, rb_ref[j * N_DIFF_HEADS + h], val)
        return val * LOG2E

    left_b, right_b = _far_bucket_consts()
    corner = MAX_DISTANCE

    @pl.when(which == 0)
    def _():
        o_ref[...] = jnp.full((t, t), rb_ref[left_b * N_DIFF_HEADS + h] * LOG2E, F32)
        o_ref[0:corner, t - corner:t] = tile(-1, range(0, nb), 0, t - corner, corner, corner)

    @pl.when(which == 1)
    def _():
        o_ref[...] = tile(0, range(0, 2 * nb))

    @pl.when(which == 2)
    def _():
        o_ref[...] = jnp.full((t, t), rb_ref[right_b * N_DIFF_HEADS + h] * LOG2E, F32)
        o_ref[t - corner:t, 0:corner] = tile(1, range(nb, 2 * nb), t - corner, 0, corner, corner)

    @pl.when(which == 3)
    def _():
        o_ref[...] = jnp.zeros((t, t), F32)


def _bias_tiles(rel_bias, t):
    H = N_DIFF_HEADS
    assert t >= MAX_DISTANCE, "far key tiles must lie entirely in the saturated bucket"
    return pl.pallas_call(
        functools.partial(_bias_kernel, t=t),
        grid=(H, 4),
        in_specs=[pl.BlockSpec(memory_space=pltpu.SMEM)],
        out_specs=pl.BlockSpec((None, None, t, t), lambda h, d: (h, d, 0, 0)),
        out_shape=jax.ShapeDtypeStruct((H, 4, t, t), F32),
        compiler_params=_cparams(("arbitrary", "arbitrary")),
        name="t5_bias",
    )(rel_bias.reshape(-1))


def _far_bucket_consts():
    nb = N_BUCKETS // 2
    return nb - 1, 2 * nb - 1


def _attn_kernel(rb_ref, q_ref, k_ref, v_ref, b_ref, lam_ref, gs_ref, o_ref,
                 s_sc, p_sc, vext_sc, m_sc, raw_sc, *, t, n, n_tiles, lam_init):
    g = pl.program_id(0)
    cur = jnp.minimum(g, n_tiles - 1)
    h = (cur // n) % N_DIFF_HEADS
    i = cur % n
    half = LANES // 2

    @pl.when(g == 0)
    def _():
        raw_sc[...] = jnp.ones(raw_sc.shape, F32)

    @pl.when((i == 0) & (g < n_tiles))
    def _():
        vext_sc[:, :LANES] = v_ref[...]
        vext_sc[:, LANES:] = jnp.ones(v_ref.shape, BF16)

    left_b, right_b = _far_bucket_consts()
    c_left = rb_ref[left_b * N_DIFF_HEADS + h] * LOG2E
    c_right = rb_ref[right_b * N_DIFF_HEADS + h] * LOG2E
    lane = lax.broadcasted_iota(jnp.int32, (1, LANES), 1)

    def all_scores(m):
        fl = lane - half if m == 0 else lane
        is_data = (lane < half) if m == 0 else (lane >= half)
        f = jnp.where(fl < n, fl, fl - n)
        cfar = jnp.where(f < i - 1, c_left, jnp.where(f > i + 1, c_right, 0.0))
        c_hi = cfar.astype(BF16)
        c_lo = (cfar - c_hi.astype(F32)).astype(BF16)
        feat = jnp.where(fl < n, c_hi, c_lo)
        feat = jnp.where((fl >= 0) & (fl < 2 * n), feat, jnp.zeros_like(feat))
        qm = jnp.where(is_data, q_ref[:, m * LANES:(m + 1) * LANES], feat)
        for j in range(n):
            kc = k_ref[j * t:(j + 1) * t, m * LANES:(m + 1) * LANES]
            s = lax.dot_general(qm, kc, (((1,), (1,)), ((), ())), preferred_element_type=F32)
            d = j - i
            s = s + b_ref[jnp.where(jnp.abs(d) <= 1, d + 1, 3)]
            s_sc[m, j] = s
            tile_max = s[:, 0:LANES]
            for c in range(1, t // LANES):
                tile_max = jnp.maximum(tile_max, s[:, c * LANES:(c + 1) * LANES])
            if j == 0:
                m_sc[m] = tile_max
            else:
                m_sc[m] = jnp.maximum(m_sc[m], tile_max)

    def exponentials(m):
        mx = jnp.max(m_sc[m], axis=-1, keepdims=True)
        for j in range(n):
            p_sc[m, :, j * t:(j + 1) * t] = jnp.exp2(s_sc[m, j] - mx).astype(BF16)

    def finish_previous():
        r0 = raw_sc[0]
        r1 = raw_sc[1]
        lam = (jnp.exp(jnp.sum(lam_ref[0:1, :] * lam_ref[1:2, :], axis=-1, keepdims=True))
               - jnp.exp(jnp.sum(lam_ref[2:3, :] * lam_ref[3:4, :], axis=-1, keepdims=True)) + lam_init)
        o = r0[:, :LANES] / r0[:, LANES:LANES + 1] - lam * (r1[:, :LANES] / r1[:, LANES:LANES + 1])
        ms = jnp.mean(o * o, axis=-1, keepdims=True)
        o_ref[...] = (o * lax.rsqrt(ms + EPS) * gs_ref[...] * (1.0 - lam_init)).astype(BF16)

    @pl.when(i < n)
    def _():
        finish_previous()
        all_scores(0)

    @pl.when(g < n_tiles)
    def _():
        all_scores(1)
        exponentials(0)
        raw_sc[0] = jnp.dot(p_sc[0], vext_sc[...], preferred_element_type=F32)
        exponentials(1)
        raw_sc[1] = jnp.dot(p_sc[1], vext_sc[...], preferred_element_type=F32)


def _attention(rel_bias, q3, k3, v3, btiles, lam4, g_subln, t, lam_init):
    B, S, _ = q3.shape
    H = N_DIFF_HEADS
    n = S // t
    n_tiles = B * H * n
    kern = functools.partial(_attn_kernel, t=t, n=n, n_tiles=n_tiles, lam_init=lam_init)

    def tile_of(g):
        return g // (H * n), (g // n) % H, g % n

    def cur_map(fn):
        def index_map(g):
            return fn(*tile_of(jnp.minimum(g, n_tiles - 1)))
        return index_map

    def prev_map(g):
        b, h, i = tile_of(jnp.maximum(g - 1, 0))
        return (b, i, h)

    return pl.pallas_call(
        kern,
        grid=(n_tiles + 1,),
        in_specs=[pl.BlockSpec(memory_space=pltpu.SMEM),
                  pl.BlockSpec((None, t, 2 * LANES), cur_map(lambda b, h, i: (b, i, h))),
                  pl.BlockSpec((None, S, 2 * LANES), cur_map(lambda b, h, i: (b, 0, h))),
                  pl.BlockSpec((None, S, LANES), cur_map(lambda b, h, i: (b, 0, h))),
                  pl.BlockSpec((None, 4, t, t), cur_map(lambda b, h, i: (h, 0, 0, 0))),
                  pl.BlockSpec((4, LANES // 2), lambda g: (0, 0)),
                  pl.BlockSpec((1, LANES), lambda g: (0, 0))],
        out_specs=pl.BlockSpec((None, t, LANES), prev_map),
        out_shape=jax.ShapeDtypeStruct((B, S, H * LANES), BF16),
        scratch_shapes=[pltpu.VMEM((2, n, t, t), F32),
                        pltpu.VMEM((2, t, S), BF16),
                        pltpu.VMEM((S, 2 * LANES), BF16),
                        pltpu.VMEM((2, t, LANES), F32),
                        pltpu.VMEM((2, t, 2 * LANES), F32)],
        compiler_params=_cparams(("arbitrary",)),
        name="diff_attn",
    )(rel_bias.reshape(-1), q3, k3, v3, btiles, lam4, g_subln)


def _pack_bf16_pairs(v):
    bits = lax.bitcast_convert_type(v.astype(BF16).astype(F32), jnp.uint32)
    n = bits.shape[1] // 2
    return (bits[:, :n] >> 16) | (bits[:, n:] & jnp.uint32(0xFFFF0000))


def _unpack_bf16_pairs(p):
    lo = lax.bitcast_convert_type(p << 16, F32).astype(BF16)
    hi = lax.bitcast_convert_type(p & jnp.uint32(0xFFFF0000), F32).astype(BF16)
    return lo, hi


def _out_proj_kernel(yf_ref, perm_ref, yd_ref, x_ref, mod_ref, wo_ref, g_ref, wr_ref, br_ref,
                     x1_ref, hp_ref, lp_ref, grow_ref, cnt_ref, *, tm, fw, E):
    yf_stack = jnp.concatenate([yf_ref[:, r * fw:(r + 1) * fw] for r in range(DFT_RADIX)], axis=0)
    yf = jnp.dot(perm_ref[...], yf_stack, preferred_element_type=F32).astype(BF16)
    mix = (jnp.dot(yf, wo_ref[:fw, :], preferred_element_type=F32)
           + jnp.dot(yd_ref[...], wo_ref[fw:, :], preferred_element_type=F32))
    x1 = x_ref[...] + mod_ref[2:3, :] * mix
    x1_ref[...] = x1
    ms = jnp.mean(x1 * x1, axis=-1, keepdims=True)
    h = x1 * lax.rsqrt(ms + EPS) * g_ref[...]
    h = h * (1.0 + mod_ref[4:5, :]) + mod_ref[3:4, :]
    hb = h.astype(BF16)
    hp_ref[...] = _pack_bf16_pairs(hb)

    logits = lax.dot_general(wr_ref[...], hb, (((1,), (1,)), ((), ())),
                             preferred_element_type=F32) + br_ref[...]
    rowid = lax.broadcasted_iota(jnp.int32, (E, tm), 0)
    vals, idxs = [], []
    comb = jnp.zeros((E, tm), F32)
    l = logits
    for _ in range(TOP_K):
        mv = jnp.max(l, axis=0, keepdims=True)
        ix = jnp.min(jnp.where(l == mv, rowid, E), axis=0, keepdims=True)
        sel = rowid == ix
        vals.append(mv)
        idxs.append(ix)
        comb = comb + sel.astype(F32)
        l = jnp.where(sel, -jnp.inf, l)
    es = [jnp.exp(v - vals[0]) for v in vals]
    den = es[0] + es[1] + es[2] + es[3]
    gates = [e / den for e in es]

    r_i = lax.broadcasted_iota(jnp.int32, (tm, tm), 0)
    c_i = lax.broadcasted_iota(jnp.int32, (tm, tm), 1)
    upper = (r_i < c_i).astype(BF16)
    before = jnp.dot(comb.astype(BF16), upper, preferred_element_type=F32)
    cnt = jnp.sum(comb, axis=1, keepdims=True)
    cnt_ref[...] = jnp.broadcast_to(cnt, cnt_ref.shape).astype(jnp.int32)
    chunks = jnp.floor((cnt + (ROW_CHUNK - 1)) * (1.0 / ROW_CHUNK))
    e_r = lax.broadcasted_iota(jnp.int32, (E, E), 0)
    e_c = lax.broadcasted_iota(jnp.int32, (E, E), 1)
    run_start = ROW_CHUNK * jnp.dot((e_c < e_r).astype(BF16),
                                    jnp.broadcast_to(chunks, (E, LANES)).astype(BF16),
                                    preferred_element_type=F32)[:, 0:1]
    ranks = [jnp.sum(jnp.where(rowid == ix, before + run_start, 0.0), axis=0, keepdims=True) for ix in idxs]

    pad_i = jnp.full((8 - TOP_K, tm), -1, jnp.int32)
    lp_ref[...] = jnp.concatenate([r.astype(jnp.int32) for r in ranks] + [pad_i], axis=0)
    gpad = jnp.concatenate(gates + [jnp.zeros((LANES - TOP_K, tm), F32)], axis=0)
    grow_ref[...] = gpad.T


def _out_proj(yf3, yd2, x2, mod3, w_out_b, g_ffn, wr_t, br_t, S, tm):
    T, D = x2.shape
    fw = yd2.shape[1]
    E = wr_t.shape[0]
    nt = T // tm
    tiles_per_batch = S // tm
    q = tm // DFT_RADIX
    col = np.arange(tm)
    perm = jnp.asarray(np.arange(tm)[:, None] == DFT_RADIX * (col % q) + col // q, BF16)
    kern = functools.partial(_out_proj_kernel, tm=tm, fw=fw, E=E)
    return pl.pallas_call(
        kern,
        grid=(nt,),
        in_specs=[pl.BlockSpec((None, q, DFT_RADIX * fw), lambda i: (i // tiles_per_batch, i % tiles_per_batch, 0)),
                  pl.BlockSpec((tm, tm), lambda i: (0, 0)),
                  pl.BlockSpec((tm, fw), lambda i: (i, 0)),
                  pl.BlockSpec((tm, D), lambda i: (i, 0)),
                  pl.BlockSpec((None, 6, D), lambda i: (i // tiles_per_batch, 0, 0)),
                  pl.BlockSpec((D, D), lambda i: (0, 0)),
                  pl.BlockSpec((1, D), lambda i: (0, 0)),
                  pl.BlockSpec((E, D), lambda i: (0, 0)),
                  pl.BlockSpec((E, 1), lambda i: (0, 0))],
        out_specs=[pl.BlockSpec((tm, D), lambda i: (i, 0)),
                   pl.BlockSpec((tm, D // 2), lambda i: (i, 0)),
                   pl.BlockSpec((8, tm), lambda i: (0, i)),
                   pl.BlockSpec((tm, LANES), lambda i: (i, 0)),
                   pl.BlockSpec((None, E, LANES), lambda i: (i, 0, 0))],
        out_shape=[jax.ShapeDtypeStruct((T, D), F32),
                   jax.ShapeDtypeStruct((T, D // 2), jnp.uint32),
                   jax.ShapeDtypeStruct((8, T), jnp.int32),
                   jax.ShapeDtypeStruct((T, LANES), F32),
                   jax.ShapeDtypeStruct((nt, E, LANES), jnp.int32)],
        compiler_params=_cparams(("arbitrary",)),
        name="out_proj_router",
    )(yf3, perm, yd2, x2, mod3, w_out_b, g_ffn, wr_t, br_t)


ROW_CHUNK = 8
CHUNK_UNROLL = 8


def _dispatch_kernel(tbl_ref, zb_ref, lp_ref, hp_ref, xs_ref, buf, zero_sc, sem, zsem, tsem,
                     *, tm, bm, NR, nz, nz_first, nt):
    i = pl.program_id(0)
    s = i % 2

    def zero_copy(e, zs):
        return pltpu.make_async_copy(zero_sc, xs_ref.at[pl.ds(zb_ref[e] * bm, bm)], zs)

    @pl.when(i == 0)
    def _():
        zero_sc[...] = jnp.zeros(zero_sc.shape, jnp.uint32)
        for e in range(nz):
            @pl.when(zb_ref[e] >= 0)
            def _():
                zero_copy(e, zsem if e < nz_first else tsem).start()
        for e in range(nz_first):
            @pl.when(zb_ref[e] >= 0)
            def _():
                zero_copy(e, zsem).wait()

    @pl.when(i == nt - 1)
    def _():
        for e in range(nz_first, nz):
            @pl.when(zb_ref[e] >= 0)
            def _():
                zero_copy(e, tsem).wait()

    lp = lp_ref[...]
    rows = lax.broadcasted_iota(jnp.int32, (NR, tm), 0)
    hit = rows == lp[0:1, :]
    for k in range(1, TOP_K):
        hit = hit | (rows == lp[k:k + 1, :])
    sel = jnp.where(hit, 1.0, 0.0).astype(BF16)
    lo, hi = _unpack_bf16_pairs(hp_ref[...])
    slo = jnp.dot(sel, lo, preferred_element_type=F32)
    shi = jnp.dot(sel, hi, preferred_element_type=F32)
    buf[s] = ((lax.bitcast_convert_type(slo, jnp.uint32) >> 16)
              | (lax.bitcast_convert_type(shi, jnp.uint32) & jnp.uint32(0xFFFF0000)))

    for slot in range(2):
        def issue(c, carry, slot=slot):
            d = pl.multiple_of(tbl_ref[0, 0, c], ROW_CHUNK)
            r = pl.multiple_of(c * ROW_CHUNK, ROW_CHUNK)
            pltpu.make_async_copy(buf.at[slot, pl.ds(r, ROW_CHUNK)], xs_ref.at[pl.ds(d, ROW_CHUNK)],
                                  sem.at[slot]).start()
            return carry

        @pl.when(s == slot)
        def _():
            lax.fori_loop(0, NR // ROW_CHUNK, issue, 0, unroll=CHUNK_UNROLL)

    def wait_slot(slot):
        pltpu.make_async_copy(buf.at[slot], xs_ref.at[pl.ds(0, NR)], sem.at[slot]).wait()

    @pl.when(i > 0)
    def _():
        wait_slot(1 - s)

    @pl.when(i == nt - 1)
    def _():
        wait_slot(s)


def _dispatch(tbl3, zero_blk, nz_first, lp8, hp, P, bm, tm, NR):
    T, Wd = hp.shape
    nt = T // tm
    nz = zero_blk.shape[0]
    nch_tbl = tbl3.shape[2]
    kern = functools.partial(_dispatch_kernel, tm=tm, bm=bm, NR=NR, nz=nz, nz_first=nz_first, nt=nt)
    return pl.pallas_call(
        kern,
        grid=(nt,),
        in_specs=[pl.BlockSpec((1, 1, nch_tbl), lambda i: (i, 0, 0), memory_space=pltpu.SMEM),
                  pl.BlockSpec(memory_space=pltpu.SMEM),
                  pl.BlockSpec((8, tm), lambda i: (0, i)),
                  pl.BlockSpec((tm, Wd), lambda i: (i, 0))],
        out_specs=pl.BlockSpec(memory_space=pl.ANY),
        out_shape=jax.ShapeDtypeStruct((P, Wd), jnp.uint32),
        scratch_shapes=[pltpu.VMEM((2, NR, Wd), jnp.uint32),
                        pltpu.VMEM((bm, Wd), jnp.uint32),
                        pltpu.SemaphoreType.DMA((2,)),
                        pltpu.SemaphoreType.DMA,
                        pltpu.SemaphoreType.DMA],
        compiler_params=_cparams(("arbitrary",)),
        name="moe_dispatch",
    )(tbl3, zero_blk, lp8, hp)


def _expert_kernel(be_ref, nu_ref, h2_ref, ord_ref, x_ref, w1_ref, b1g_ref, b1l_ref, w2_ref, b2_ref, pm_ref,
                   o_ref, w1g_sc, w1l_sc, w2_sc):
    blk = pl.program_id(0)
    nused = nu_ref[0]
    e = be_ref[blk]
    prev = be_ref[jnp.maximum(blk - 1, 0)]
    active = blk < nused
    second = active & (h2_ref[blk] != 0)
    half = x_ref.shape[0] // 2
    tile = 2 * LANES
    n_tiles = w1_ref.shape[1] // tile

    @pl.when(active & ((blk == 0) | (e != prev)))
    def _():
        for c in range(n_tiles):
            wt = w1_ref[:, c * tile:(c + 1) * tile].astype(BF16)
            pw = jnp.dot(wt, pm_ref[...], preferred_element_type=F32)
            w1g_sc[:, c * LANES:(c + 1) * LANES] = pw[:, :LANES].astype(BF16)
            w1l_sc[:, c * LANES:(c + 1) * LANES] = pw[:, LANES:].astype(BF16)
        w2_sc[...] = w2_ref[...].astype(BF16)

    @pl.when(jnp.logical_not(active))
    def _():
        o_ref[...] = jnp.zeros(o_ref.shape, jnp.uint32)

    def mlp(r0):
        xp = x_ref[r0:r0 + half, :]
        hw = xp.shape[1]
        xlo, xhi = _unpack_bf16_pairs(xp)
        zg = (jnp.dot(xlo, w1g_sc[:hw, :], preferred_element_type=F32)
              + jnp.dot(xhi, w1g_sc[hw:, :], preferred_element_type=F32) + b1g_ref[...])
        zl = (jnp.dot(xlo, w1l_sc[:hw, :], preferred_element_type=F32)
              + jnp.dot(xhi, w1l_sc[hw:, :], preferred_element_type=F32) + b1l_ref[...])
        g = jnp.minimum(zg, SWIGLU_LIMIT)
        lin = jnp.clip(zl, -SWIGLU_LIMIT, SWIGLU_LIMIT)
        act = g * jax.nn.sigmoid(SWIGLU_ALPHA * g) * (lin + 1.0)
        y = jnp.dot(act.astype(BF16), w2_sc[...], preferred_element_type=F32) + b2_ref[...]
        o_ref[r0:r0 + half, :] = _pack_bf16_pairs(y)

    @pl.when(second)
    def _():
        mlp(0)
        mlp(half)

    @pl.when(active & jnp.logical_not(second))
    def _():
        mlp(0)
        o_ref[half:, :] = jnp.zeros((half, o_ref.shape[1]), jnp.uint32)


def _experts(blk_expert, nused, half2, order, xs, P, w1, b1g, b1l, w2, b2, bm):
    Wd = xs.shape[1]
    E, D, F2 = w1.shape
    F = F2 // 2
    nblk = P // bm
    tile = 2 * LANES
    pm = np.zeros((tile, tile), np.float32)
    pm[2 * np.arange(LANES), np.arange(LANES)] = 1.0
    pm[2 * np.arange(LANES) + 1, LANES + np.arange(LANES)] = 1.0
    pm = jnp.asarray(pm, BF16)

    def row_map(b, be, nu, h2, od):
        return (jnp.minimum(od[b], nu[0] - 1), 0)

    def exp_map(b, be, nu, h2, od):
        return (be[b], 0, 0)

    grid_spec = pltpu.PrefetchScalarGridSpec(
        num_scalar_prefetch=4,
        grid=(nblk,),
        in_specs=[pl.BlockSpec((bm, Wd), row_map),
                  pl.BlockSpec((None, D, F2), exp_map),
                  pl.BlockSpec((None, 1, F), exp_map),
                  pl.BlockSpec((None, 1, F), exp_map),
                  pl.BlockSpec((None, F, D), exp_map),
                  pl.BlockSpec((None, 1, D), exp_map),
                  pl.BlockSpec((tile, tile), lambda b, be, nu, h2, od: (0, 0))],
        out_specs=pl.BlockSpec((bm, D // 2), lambda b, be, nu, h2, od: (od[b], 0)),
        scratch_shapes=[pltpu.VMEM((D, F), BF16),
                        pltpu.VMEM((D, F), BF16),
                        pltpu.VMEM((F, D), BF16)],
    )
    return pl.pallas_call(
        _expert_kernel,
        grid_spec=grid_spec,
        out_shape=jax.ShapeDtypeStruct((P, D // 2), jnp.uint32),
        compiler_params=_cparams(("arbitrary",)),
        name="moe_experts",
    )(blk_expert, nused, half2, order, xs, w1, b1g, b1l, w2, b2, pm)


def _combine_kernel(tcur_ref, tnxt_ref, lp_ref, grow_ref, x1_ref, mod_ref, y_ref, o_ref,
                    buf, sem, *, tm, NR, nt):
    i = pl.program_id(0)
    s = i % 2

    def issue(tref, slot):
        def body(c, carry):
            d = pl.multiple_of(tref[0, 0, c], ROW_CHUNK)
            r = pl.multiple_of(c * ROW_CHUNK, ROW_CHUNK)
            pltpu.make_async_copy(y_ref.at[pl.ds(d, ROW_CHUNK)], buf.at[slot, pl.ds(r, ROW_CHUNK)],
                                  sem.at[slot]).start()
            return carry
        lax.fori_loop(0, NR // ROW_CHUNK, body, 0, unroll=CHUNK_UNROLL)

    @pl.when(i == 0)
    def _():
        issue(tcur_ref, 0)

    for slot in range(2):
        @pl.when((i + 1 < nt) & (s != slot))
        def _():
            issue(tnxt_ref, slot)

    pltpu.make_async_copy(y_ref.at[pl.ds(0, NR)], buf.at[s], sem.at[s]).wait()

    ylo, yhi = _unpack_bf16_pairs(buf[s])
    lpf = jnp.concatenate([lp_ref[...].astype(F32), jnp.zeros((LANES - 8, tm), F32)], axis=0)
    lpr = lpf.T.astype(jnp.int32)
    gr = grow_ref[...]
    cols = lax.broadcasted_iota(jnp.int32, (tm, NR), 1)
    gmat = jnp.zeros((tm, NR), F32)
    for k in range(TOP_K):
        gmat = jnp.where(cols == lpr[:, k:k + 1], gr[:, k:k + 1], gmat)
    gmat = gmat.astype(BF16)
    hw = ylo.shape[1]
    o_ref[:, :hw] = x1_ref[:, :hw] + mod_ref[5:6, :hw] * jnp.dot(gmat, ylo, preferred_element_type=F32)
    o_ref[:, hw:] = x1_ref[:, hw:] + mod_ref[5:6, hw:] * jnp.dot(gmat, yhi, preferred_element_type=F32)


def _combine(tbl3, lp8, grow, x1, mod3, y, S, tm, NR):
    T, D = x1.shape
    nt = T // tm
    tiles_per_batch = S // tm
    nch_tbl = tbl3.shape[2]
    kern = functools.partial(_combine_kernel, tm=tm, NR=NR, nt=nt)
    return pl.pallas_call(
        kern,
        grid=(nt,),
        in_specs=[pl.BlockSpec((1, 1, nch_tbl), lambda i: (i, 0, 0), memory_space=pltpu.SMEM),
                  pl.BlockSpec((1, 1, nch_tbl), lambda i: (jnp.minimum(i + 1, nt - 1), 0, 0),
                               memory_space=pltpu.SMEM),
                  pl.BlockSpec((8, tm), lambda i: (0, i)),
                  pl.BlockSpec((tm, LANES), lambda i: (i, 0)),
                  pl.BlockSpec((tm, D), lambda i: (i, 0)),
                  pl.BlockSpec((None, 6, D), lambda i: (i // tiles_per_batch, 0, 0)),
                  pl.BlockSpec(memory_space=pl.ANY)],
        out_specs=pl.BlockSpec((tm, D), lambda i: (i, 0)),
        out_shape=jax.ShapeDtypeStruct((T, D), F32),
        scratch_shapes=[pltpu.VMEM((2, NR, D // 2), jnp.uint32),
                        pltpu.SemaphoreType.DMA((2,))],
        compiler_params=_cparams(("arbitrary",)),
        name="moe_combine",
    )(tbl3, tbl3, lp8, grow, x1, mod3, y)


def _tiles(S):
    t_attn = min(512, S // 4) if S >= 512 else S
    tm = min(512, t_attn)
    return dict(tm=tm, t_attn=t_attn, bm=1024)


def _moe_layout(cnt, T, tm, bm):
    nt, E = cnt.shape
    c8 = (cnt + ROW_CHUNK - 1) // ROW_CHUNK * ROW_CHUNK
    tile_end = jnp.cumsum(c8, axis=1)
    tile_off = tile_end - c8
    used = tile_end[:, -1]
    off = jnp.cumsum(c8, axis=0) - c8
    tot = jnp.sum(c8, axis=0)
    padded = (tot + bm - 1) // bm * bm
    pends = jnp.cumsum(padded)
    pstarts = pends - padded
    NR = TOP_K * tm + ROW_CHUNK * E
    nch_tbl = NR // ROW_CHUNK
    P = (T * TOP_K + nt * E * (ROW_CHUNK - 1) + E * bm + bm - 1) // bm * bm
    nblk = P // bm

    c_row = jnp.arange(nch_tbl, dtype=jnp.int32) * ROW_CHUNK
    inrun = (tile_off[:, :, None] <= c_row) & (c_row < tile_end[:, :, None])
    base = pstarts[None, :] + off - tile_off
    tbl = jnp.sum(jnp.where(inrun, base[:, :, None], 0), axis=1) + c_row[None, :]
    in_use = c_row[None, :] < used[:, None]
    spare_rows = NR - TOP_K * tm
    spare = P + jnp.arange(nt, dtype=jnp.int32)[:, None] * spare_rows + (c_row[None, :] - used[:, None])
    tbl_out = jnp.where(in_use, tbl, spare).astype(jnp.int32)
    tbl_in = jnp.where(in_use, tbl, 0).astype(jnp.int32)
    n_spare_blk = (nt * spare_rows + bm - 1) // bm

    nused = (pends[-1] // bm).astype(jnp.int32).reshape(1)
    blk_start = jnp.arange(nblk, dtype=jnp.int32) * bm
    blk_expert = jnp.minimum(jnp.sum(pends[None, :] <= blk_start[:, None], axis=1), E - 1).astype(jnp.int32)
    own = blk_expert[:, None] == jnp.arange(E, dtype=jnp.int32)[None, :]

    def of_expert(table):
        return jnp.sum(jnp.where(own, table[None, :], 0), axis=1)

    step = jnp.arange(nblk, dtype=jnp.int32)
    order = jnp.where(step >= nused[0], step,
                      jnp.where(step == of_expert(pstarts // bm), of_expert(pends // bm - 1), step - 1))
    order = order.astype(jnp.int32)
    half2 = (order * bm + bm // 2 < of_expert(pstarts + tot)).astype(jnp.int32)
    last_blk = jnp.where(padded > 0, pends // bm - 1, -1)
    n_tail = nblk - (T * TOP_K) // bm
    tail_blk = nused[0] + jnp.arange(n_tail, dtype=jnp.int32)
    tail_blk = jnp.where(tail_blk < nblk, tail_blk, -1)
    spare_blk = nblk + jnp.arange(n_spare_blk, dtype=jnp.int32)
    zero_blk = jnp.concatenate([last_blk, spare_blk, tail_blk]).astype(jnp.int32)
    return dict(tbl_out=tbl_out.reshape(nt, 1, nch_tbl), tbl_in=tbl_in.reshape(nt, 1, nch_tbl),
                nused=nused, blk_expert=blk_expert, half2=half2, order=order, zero_blk=zero_blk,
                nz_first=E + n_spare_blk,
                P=P, P_alloc=P + n_spare_blk * bm, NR=NR)


def kernel(x, c, w_ada, b_ada, g_mix, w_in, w_fourier, q_norm_g, k_norm_g, lambda_q1, lambda_k1,
           lambda_q2, lambda_k2, g_subln, w_out, rel_bias, g_ffn, w_router, b_router, w1, b1, w2, b2):
    B, S, D = x.shape
    T = B * S
    L = w_ada.shape[0]
    E = w_router.shape[-1]
    cfg = _tiles(S)
    tm, t_attn, bm = cfg["tm"], cfg["t_attn"], cfg["bm"]
    dqk = q_norm_g.shape[-1]
    scale = dqk ** -0.5
    fw = w_in.shape[-1] // 4

    btiles = _bias_tiles(rel_bias, t_attn)
    x2 = x.reshape(T, D)
    for l in range(L):
        lam_init = 0.8 - 0.6 * math.exp(-0.3 * l)
        mod3 = _ada(c, w_ada[l], b_ada[l]).reshape(B, 6, D)
        gq = (jnp.tile(q_norm_g[l], 4) * (scale * LOG2E)).reshape(1, 2 * LANES)
        gk = jnp.tile(k_norm_g[l], 4).reshape(1, 2 * LANES)
        uf, qx, kx, v = _in_proj(x2, mod3, g_mix[l].reshape(1, D), w_in[l].astype(BF16), gq, gk,
                                 S, tm, t_attn)
        yf = _fourier(uf.reshape(B, S, fw), w_fourier[l])
        lam4 = jnp.stack([lambda_q1[l], lambda_k1[l], lambda_q2[l], lambda_k2[l]]).astype(F32)
        yd = _attention(rel_bias, qx.reshape(B, S, -1), kx.reshape(B, S, -1), v.reshape(B, S, fw),
                        btiles, lam4, g_subln[l].reshape(1, LANES), t_attn, lam_init).reshape(T, fw)
        x1, hp, lp8, grow, cnt3 = _out_proj(
            yf, yd, x2, mod3, w_out[l].astype(BF16), g_ffn[l].reshape(1, D),
            w_router[l].T.astype(BF16), b_router[l].reshape(E, 1), S, tm)
        lay = _moe_layout(cnt3[:, :, 0], T, tm, bm)
        xs = _dispatch(lay["tbl_out"], lay["zero_blk"], lay["nz_first"], lp8, hp, lay["P_alloc"], bm, tm,
                       lay["NR"])
        y = _experts(lay["blk_expert"], lay["nused"], lay["half2"], lay["order"], xs, lay["P"], w1[l],
                     b1[l][:, None, 0::2],
                     b1[l][:, None, 1::2], w2[l], b2[l][:, None, :], bm)
        x2 = _combine(lay["tbl_in"], lp8, grow, x1, mod3, y, S, tm, lay["NR"])
    return x2.reshape(B, S, D)
```

```python
import functools
import math

import numpy as np
import jax
import jax.numpy as jnp
from jax import lax
from jax.experimental import pallas as pl
from jax.experimental.pallas import tpu as pltpu

F32 = jnp.float32
BF16 = jnp.bfloat16

EPS = 1e-6
N_FOURIER_GROUPS = 4
N_DIFF_HEADS = 4
TOP_K = 4
N_BUCKETS = 32
MAX_DISTANCE = 128
SWIGLU_ALPHA = 1.702
SWIGLU_LIMIT = 7.0
LANES = 128
DFT_RADIX = 4
LOG2E = 1.4426950408889634
VMEM_LIMIT = 56 * 1024 * 1024


def _cparams(sem):
    return pltpu.CompilerParams(dimension_semantics=sem, vmem_limit_bytes=VMEM_LIMIT)


def _ada_kernel(c_ref, w_ref, b_ref, o_ref):
    c = c_ref[...]
    cond = c * jax.nn.sigmoid(c)
    o_ref[...] = jnp.dot(cond.astype(BF16), w_ref[...].astype(BF16),
                         preferred_element_type=F32) + b_ref[...]


def _ada(c, w, b):
    B, D = c.shape
    N = w.shape[1]
    tn = 1536 if N % 1536 == 0 else N
    return pl.pallas_call(
        _ada_kernel,
        grid=(N // tn,),
        in_specs=[pl.BlockSpec((B, D), lambda j: (0, 0)),
                  pl.BlockSpec((D, tn), lambda j: (0, j)),
                  pl.BlockSpec((1, tn), lambda j: (0, j))],
        out_specs=pl.BlockSpec((B, tn), lambda j: (0, j)),
        out_shape=jax.ShapeDtypeStruct((B, N), F32),
        compiler_params=_cparams(("arbitrary",)),
        name="ada",
    )(c, w, b.reshape(1, N))


def _group_rms_inv(xh, gmat, group):
    ss = jnp.dot((xh * xh).astype(BF16), gmat, preferred_element_type=F32)
    return lax.rsqrt(ss * (1.0 / group) + EPS)


def _in_proj_kernel(x_ref, mod_ref, g_ref, w_ref, gq_ref, gk_ref, gmat_ref,
                    uf_ref, q_ref, k_ref, v_ref, *, tiles_per_batch, tm, t_attn, n_chunks, fw, qw):
    i = pl.program_id(0)
    x = x_ref[...]
    ms = jnp.mean(x * x, axis=-1, keepdims=True)
    y = x * lax.rsqrt(ms + EPS) * g_ref[...]
    h = y * (1.0 + mod_ref[1:2, :]) + mod_ref[0:1, :]
    proj = jnp.dot(h.astype(BF16), w_ref[...], preferred_element_type=F32)
    uf_ref[...] = proj[:, :fw].astype(BF16)
    v_ref[...] = proj[:, fw + 2 * qw:].astype(BF16)

    lane = lax.broadcasted_iota(jnp.int32, (1, LANES), 1)
    chunk = ((i % tiles_per_batch) * tm) // t_attn
    gmat = gmat_ref[...]
    half = LANES // 2
    pair = 2 * LANES
    for hd in range(N_DIFF_HEADS):
        if hd % 2 == 0:
            q2 = proj[:, fw + hd * LANES: fw + hd * LANES + pair]
            k2 = proj[:, fw + qw + hd * LANES: fw + qw + hd * LANES + pair]
            qn2 = (q2 * _group_rms_inv(q2, gmat, half) * gq_ref[...]).astype(BF16)
            kn2 = (k2 * _group_rms_inv(k2, gmat, half) * gk_ref[...]).astype(BF16)
        qn = qn2[:, (hd % 2) * LANES:(hd % 2 + 1) * LANES]
        kn = kn2[:, (hd % 2) * LANES:(hd % 2 + 1) * LANES]
        for m in range(2):
            fl = lane - half if m == 0 else lane
            is_data = (lane < half) if m == 0 else (lane >= half)
            f = jnp.where(fl < n_chunks, fl, fl - n_chunks)
            onehot = ((fl >= 0) & (fl < 2 * n_chunks) & (f == chunk)).astype(BF16)
            c0 = (2 * hd + m) * LANES
            q_ref[:, c0:c0 + LANES] = jnp.where(is_data, qn, jnp.zeros_like(qn))
            k_ref[:, c0:c0 + LANES] = jnp.where(is_data, kn, onehot)


def _in_proj(x2, mod3, g_mix, w_in_b, gq, gk, S, tm, t_attn):
    T, D = x2.shape
    ncols = w_in_b.shape[1]
    fw = ncols // 4
    qw = fw
    tiles_per_batch = S // tm
    n_chunks = S // t_attn
    half = LANES // 2
    pair = 2 * LANES
    gmat = (np.arange(pair)[:, None] // half == np.arange(pair)[None, :] // half)
    gmat = jnp.asarray(gmat, BF16)
    kern = functools.partial(_in_proj_kernel, tiles_per_batch=tiles_per_batch, tm=tm, t_attn=t_attn,
                             n_chunks=n_chunks, fw=fw, qw=qw)
    ext = 2 * qw
    return pl.pallas_call(
        kern,
        grid=(T // tm,),
        in_specs=[pl.BlockSpec((tm, D), lambda i: (i, 0)),
                  pl.BlockSpec((None, 6, D), lambda i: (i // tiles_per_batch, 0, 0)),
                  pl.BlockSpec((1, D), lambda i: (0, 0)),
                  pl.BlockSpec((D, ncols), lambda i: (0, 0)),
                  pl.BlockSpec((1, pair), lambda i: (0, 0)),
                  pl.BlockSpec((1, pair), lambda i: (0, 0)),
                  pl.BlockSpec((pair, pair), lambda i: (0, 0))],
        out_specs=[pl.BlockSpec((tm, fw), lambda i: (i, 0)),
                   pl.BlockSpec((tm, ext), lambda i: (i, 0)),
                   pl.BlockSpec((tm, ext), lambda i: (i, 0)),
                   pl.BlockSpec((tm, fw), lambda i: (i, 0))],
        out_shape=[jax.ShapeDtypeStruct((T, fw), BF16),
                   jax.ShapeDtypeStruct((T, ext), BF16),
                   jax.ShapeDtypeStruct((T, ext), BF16),
                   jax.ShapeDtypeStruct((T, fw), BF16)],
        compiler_params=_cparams(("arbitrary",)),
        name="in_proj",
    )(x2, mod3, g_mix, w_in_b, gq, gk, gmat)


def _fourier_kernel(uf_ref, wcs_ref, tw_ref, dft_ref, o_ref, z_sc, tt_sc, *, M, gw):
    c = pl.program_id(1)

    def channel_stage():
        for g in range(N_FOURIER_GROUPS):
            for j in range(DFT_RADIX):
                ab = jnp.dot(uf_ref[j * M:(j + 1) * M, g * gw:(g + 1) * gw], wcs_ref[g],
                             preferred_element_type=F32)
                z_sc[0, j, :, g * gw:(g + 1) * gw] = ab[:, :gw]
                z_sc[1, j, :, g * gw:(g + 1) * gw] = -ab[:, gw:]

    def butterfly(k2):
        cs = tw_ref[:, k2:k2 + 1]
        sn = tw_ref[:, DFT_RADIX + k2:DFT_RADIX + k2 + 1]
        for g in range(N_FOURIER_GROUPS):
            cols = slice(g * gw, (g + 1) * gw)
            zr = [z_sc[0, j, :, cols] for j in range(DFT_RADIX)]
            zi = [z_sc[1, j, :, cols] for j in range(DFT_RADIX)]
            if k2 == 0:
                yr = zr[0] + zr[1] + zr[2] + zr[3]
                yi = zi[0] + zi[1] + zi[2] + zi[3]
            elif k2 == 1:
                yr = zr[0] + zi[1] - zr[2] - zi[3]
                yi = zi[0] - zr[1] - zi[2] + zr[3]
            elif k2 == 2:
                yr = zr[0] - zr[1] + zr[2] - zr[3]
                yi = zi[0] - zi[1] + zi[2] - zi[3]
            else:
                yr = zr[0] - zi[1] - zr[2] + zi[3]
                yi = zi[0] + zr[1] - zi[2] - zr[3]
            tt_sc[k2, 0:M, cols] = (cs * yr + sn * yi).astype(BF16)
            tt_sc[k2, M:2 * M, cols] = (cs * yi - sn * yr).astype(BF16)

    for cc in range(DFT_RADIX):
        @pl.when(c == cc)
        def _(cc=cc):
            if cc == 0:
                channel_stage()
                butterfly(0)
            o_ref[...] = jnp.dot(dft_ref[...], tt_sc[cc], preferred_element_type=F32).astype(BF16)
            if cc + 1 < DFT_RADIX:
                butterfly(cc + 1)


def _fourier(uf3, w_f):
    B, S, W = uf3.shape
    G = N_FOURIER_GROUPS
    gw = W // G
    M = S // DFT_RADIX
    norm = 1.0 / math.sqrt(S * gw)
    cidx = (np.arange(gw)[:, None] * np.arange(gw)[None, :]) % gw
    ang_c = cidx * (2.0 * math.pi / gw)
    cc = jnp.asarray(np.cos(ang_c) * norm, F32)
    sc = jnp.asarray(np.sin(ang_c) * norm, F32)
    hp = lax.Precision.HIGHEST
    wcs = jnp.concatenate([jnp.einsum("ab,gbd->gad", cc, w_f, precision=hp),
                           jnp.einsum("ab,gbd->gad", sc, w_f, precision=hp)], axis=-1).astype(BF16)
    n1 = np.arange(M)
    ang_t = ((np.arange(DFT_RADIX)[None, :] * n1[:, None]) % S) * (2.0 * math.pi / S)
    tw = jnp.asarray(np.concatenate([np.cos(ang_t), np.sin(ang_t),
                                     np.zeros((M, LANES - 2 * DFT_RADIX))], axis=1), F32)
    ang_p = ((n1[:, None] * n1[None, :]) % M) * (2.0 * math.pi / M)
    dft = jnp.asarray(np.concatenate([np.cos(ang_p), np.sin(ang_p)], axis=1).astype(np.float32)
                      .astype(jnp.bfloat16))
    kern = functools.partial(_fourier_kernel, M=M, gw=gw)
    return pl.pallas_call(
        kern,
        grid=(B, DFT_RADIX),
        in_specs=[pl.BlockSpec((None, S, W), lambda b, c: (b, 0, 0)),
                  pl.BlockSpec((G, gw, 2 * gw), lambda b, c: (0, 0, 0)),
                  pl.BlockSpec((M, LANES), lambda b, c: (0, 0)),
                  pl.BlockSpec((M, 2 * M), lambda b, c: (0, 0))],
        out_specs=pl.BlockSpec((None, M, W), lambda b, c: (b, 0, c)),
        out_shape=jax.ShapeDtypeStruct((B, M, DFT_RADIX * W), BF16),
        scratch_shapes=[pltpu.VMEM((2, DFT_RADIX, M, W), F32),
                        pltpu.VMEM((DFT_RADIX, 2 * M, W), BF16)],
        compiler_params=_cparams(("arbitrary", "arbitrary")),
        name="fourier",
    )(uf3, wcs, tw, dft)


def _bias_kernel(rb_ref, o_ref, *, t):
    h = pl.program_id(0)
    which = pl.program_id(1)
    nb = N_BUCKETS // 2

    def tile(d, buckets, r0=0, c0=0, nr=t, nc=t):
        row = r0 + lax.broadcasted_iota(jnp.int32, (nr, nc), 0)
        col = c0 + lax.broadcasted_iota(jnp.int32, (nr, nc), 1)
        rel = d * t + col - row
        max_exact = nb // 2
        ret = jnp.where(rel > 0, nb, 0)
        n = jnp.abs(rel)
        nf = jnp.maximum(n, 1).astype(F32)
        large = max_exact + (jnp.log(nf / max_exact) / math.log(MAX_DISTANCE / max_exact)
                             * (nb - max_exact)).astype(jnp.int32)
        large = jnp.minimum(large, nb - 1)
        bucket = ret + jnp.where(n < max_exact, n, large)
        val = jnp.zeros((nr, nc), F32)
        for j in buckets:
            val = jnp.where(bucket == j, rb_ref[j * N_DIFF_HEADS + h], val)
        return val * LOG2E

    left_b, right_b = _far_bucket_consts()
    corner = MAX_DISTANCE

    @pl.when(which == 0)
    def _():
        o_ref[...] = jnp.full((t, t), rb_ref[left_b * N_DIFF_HEADS + h] * LOG2E, F32)
        o_ref[0:corner, t - corner:t] = tile(-1, range(0, nb), 0, t - corner, corner, corner)

    @pl.when(which == 1)
    def _():
        o_ref[...] = tile(0, range(0, 2 * nb))

    @pl.when(which == 2)
    def _():
        o_ref[...] = jnp.full((t, t), rb_ref[right_b * N_DIFF_HEADS + h] * LOG2E, F32)
        o_ref[t - corner:t, 0:corner] = tile(1, range(nb, 2 * nb), t - corner, 0, corner, corner)

    @pl.when(which == 3)
    def _():
        o_ref[...] = jnp.zeros((t, t), F32)


def _bias_tiles(rel_bias, t):
    H = N_DIFF_HEADS
    assert t >= MAX_DISTANCE, "far key tiles must lie entirely in the saturated bucket"
    return pl.pallas_call(
        functools.partial(_bias_kernel, t=t),
        grid=(H, 4),
        in_specs=[pl.BlockSpec(memory_space=pltpu.SMEM)],
        out_specs=pl.BlockSpec((None, None, t, t), lambda h, d: (h, d, 0, 0)),
        out_shape=jax.ShapeDtypeStruct((H, 4, t, t), F32),
        compiler_params=_cparams(("arbitrary", "arbitrary")),
        name="t5_bias",
    )(rel_bias.reshape(-1))


def _far_bucket_consts():
    nb = N_BUCKETS // 2
    return nb - 1, 2 * nb - 1


def _attn_kernel(rb_ref, q_ref, k_ref, v_ref, b_ref, lam_ref, gs_ref, o_ref,
                 s_sc, p_sc, vext_sc, m_sc, raw_sc, *, t, n, n_tiles, lam_init):
    g = pl.program_id(0)
    cur = jnp.minimum(g, n_tiles - 1)
    h = (cur // n) % N_DIFF_HEADS
    i = cur % n
    half = LANES // 2

    @pl.when(g == 0)
    def _():
        raw_sc[...] = jnp.ones(raw_sc.shape, F32)

    @pl.when((i == 0) & (g < n_tiles))
    def _():
        vext_sc[:, :LANES] = v_ref[...]
        vext_sc[:, LANES:] = jnp.ones(v_ref.shape, BF16)

    left_b, right_b = _far_bucket_consts()
    c_left = rb_ref[left_b * N_DIFF_HEADS + h] * LOG2E
    c_right = rb_ref[right_b * N_DIFF_HEADS + h] * LOG2E
    lane = lax.broadcasted_iota(jnp.int32, (1, LANES), 1)

    def all_scores(m):
        fl = lane - half if m == 0 else lane
        is_data = (lane < half) if m == 0 else (lane >= half)
        f = jnp.where(fl < n, fl, fl - n)
        cfar = jnp.where(f < i - 1, c_left, jnp.where(f > i + 1, c_right, 0.0))
        c_hi = cfar.astype(BF16)
        c_lo = (cfar - c_hi.astype(F32)).astype(BF16)
        feat = jnp.where(fl < n, c_hi, c_lo)
        feat = jnp.where((fl >= 0) & (fl < 2 * n), feat, jnp.zeros_like(feat))
        qm = jnp.where(is_data, q_ref[:, m * LANES:(m + 1) * LANES], feat)
        for j in range(n):
            kc = k_ref[j * t:(j + 1) * t, m * LANES:(m + 1) * LANES]
            s = lax.dot_general(qm, kc, (((1,), (1,)), ((), ())), preferred_element_type=F32)
            d = j - i
            s = s + b_ref[jnp.where(jnp.abs(d) <= 1, d + 1, 3)]
            s_sc[m, j] = s
            tile_max = s[:, 0:LANES]
            for c in range(1, t // LANES):
                tile_max = jnp.maximum(tile_max, s[:, c * LANES:(c + 1) * LANES])
            if j == 0:
                m_sc[m] = tile_max
            else:
                m_sc[m] = jnp.maximum(m_sc[m], tile_max)

    def exponentials(m):
        mx = jnp.max(m_sc[m], axis=-1, keepdims=True)
        for j in range(n):
            p_sc[m, :, j * t:(j + 1) * t] = jnp.exp2(s_sc[m, j] - mx).astype(BF16)

    def finish_previous():
        r0 = raw_sc[0]
        r1 = raw_sc[1]
        lam = (jnp.exp(jnp.sum(lam_ref[0:1, :] * lam_ref[1:2, :], axis=-1, keepdims=True))
               - jnp.exp(jnp.sum(lam_ref[2:3, :] * lam_ref[3:4, :], axis=-1, keepdims=True)) + lam_init)
        o = r0[:, :LANES] / r0[:, LANES:LANES + 1] - lam * (r1[:, :LANES] / r1[:, LANES:LANES + 1])
        ms = jnp.mean(o * o, axis=-1, keepdims=True)
        o_ref[...] = (o * lax.rsqrt(ms + EPS) * gs_ref[...] * (1.0 - lam_init)).astype(BF16)

    @pl.when(i < n)
    def _():
        finish_previous()
        all_scores(0)

    @pl.when(g < n_tiles)
    def _():
        all_scores(1)
        exponentials(0)
        raw_sc[0] = jnp.dot(p_sc[0], vext_sc[...], preferred_element_type=F32)
        exponentials(1)
        raw_sc[1] = jnp.dot(p_sc[1], vext_sc[...], preferred_element_type=F32)


def _attention(rel_bias, q3, k3, v3, btiles, lam4, g_subln, t, lam_init):
    B, S, _ = q3.shape
    H = N_DIFF_HEADS
    n = S // t
    n_tiles = B * H * n
    kern = functools.partial(_attn_kernel, t=t, n=n, n_tiles=n_tiles, lam_init=lam_init)

    def tile_of(g):
        return g // (H * n), (g // n) % H, g % n

    def cur_map(fn):
        def index_map(g):
            return fn(*tile_of(jnp.minimum(g, n_tiles - 1)))
        return index_map

    def prev_map(g):
        b, h, i = tile_of(jnp.maximum(g - 1, 0))
        return (b, i, h)

    return pl.pallas_call(
        kern,
        grid=(n_tiles + 1,),
        in_specs=[pl.BlockSpec(memory_space=pltpu.SMEM),
                  pl.BlockSpec((None, t, 2 * LANES), cur_map(lambda b, h, i: (b, i, h))),
                  pl.BlockSpec((None, S, 2 * LANES), cur_map(lambda b, h, i: (b, 0, h))),
                  pl.BlockSpec((None, S, LANES), cur_map(lambda b, h, i: (b, 0, h))),
                  pl.BlockSpec((None, 4, t, t), cur_map(lambda b, h, i: (h, 0, 0, 0))),
                  pl.BlockSpec((4, LANES // 2), lambda g: (0, 0)),
                  pl.BlockSpec((1, LANES), lambda g: (0, 0))],
        out_specs=pl.BlockSpec((None, t, LANES), prev_map),
        out_shape=jax.ShapeDtypeStruct((B, S, H * LANES), BF16),
        scratch_shapes=[pltpu.VMEM((2, n, t, t), F32),
                        pltpu.VMEM((2, t, S), BF16),
                        pltpu.VMEM((S, 2 * LANES), BF16),
                        pltpu.VMEM((2, t, LANES), F32),
                        pltpu.VMEM((2, t, 2 * LANES), F32)],
        compiler_params=_cparams(("arbitrary",)),
        name="diff_attn",
    )(rel_bias.reshape(-1), q3, k3, v3, btiles, lam4, g_subln)


def _pack_bf16_pairs(v):
    bits = lax.bitcast_convert_type(v.astype(BF16).astype(F32), jnp.uint32)
    n = bits.shape[1] // 2
    return (bits[:, :n] >> 16) | (bits[:, n:] & jnp.uint32(0xFFFF0000))


def _unpack_bf16_pairs(p):
    lo = lax.bitcast_convert_type(p << 16, F32).astype(BF16)
    hi = lax.bitcast_convert_type(p & jnp.uint32(0xFFFF0000), F32).astype(BF16)
    return lo, hi


def _out_proj_kernel(yf_ref, perm_ref, yd_ref, x_ref, mod_ref, wo_ref, g_ref, wr_ref, br_ref,
                     x1_ref, hp_ref, lp_ref, grow_ref, cnt_ref, *, tm, fw, E):
    yf_stack = jnp.concatenate([yf_ref[:, r * fw:(r + 1) * fw] for r in range(DFT_RADIX)], axis=0)
    yf = jnp.dot(perm_ref[...], yf_stack, preferred_element_type=F32).astype(BF16)
    mix = (jnp.dot(yf, wo_ref[:fw, :], preferred_element_type=F32)
           + jnp.dot(yd_ref[...], wo_ref[fw:, :], preferred_element_type=F32))
    x1 = x_ref[...] + mod_ref[2:3, :] * mix
    x1_ref[...] = x1
    ms = jnp.mean(x1 * x1, axis=-1, keepdims=True)
    h = x1 * lax.rsqrt(ms + EPS) * g_ref[...]
    h = h * (1.0 + mod_ref[4:5, :]) + mod_ref[3:4, :]
    hb = h.astype(BF16)
    hp_ref[...] = _pack_bf16_pairs(hb)

    logits = lax.dot_general(wr_ref[...], hb, (((1,), (1,)), ((), ())),
                             preferred_element_type=F32) + br_ref[...]
    rowid = lax.broadcasted_iota(jnp.int32, (E, tm), 0)
    vals, idxs = [], []
    comb = jnp.zeros((E, tm), F32)
    l = logits
    for _ in range(TOP_K):
        mv = jnp.max(l, axis=0, keepdims=True)
        ix = jnp.min(jnp.where(l == mv, rowid, E), axis=0, keepdims=True)
        sel = rowid == ix
        vals.append(mv)
        idxs.append(ix)
        comb = comb + sel.astype(F32)
        l = jnp.where(sel, -jnp.inf, l)
    es = [jnp.exp(v - vals[0]) for v in vals]
    den = es[0] + es[1] + es[2] + es[3]
    gates = [e / den for e in es]

    r_i = lax.broadcasted_iota(jnp.int32, (tm, tm), 0)
    c_i = lax.broadcasted_iota(jnp.int32, (tm, tm), 1)
    upper = (r_i < c_i).astype(BF16)
    before = jnp.dot(comb.astype(BF16), upper, preferred_element_type=F32)
    cnt = jnp.sum(comb, axis=1, keepdims=True)
    cnt_ref[...] = jnp.broadcast_to(cnt, cnt_ref.shape).astype(jnp.int32)
    chunks = jnp.floor((cnt + (ROW_CHUNK - 1)) * (1.0 / ROW_CHUNK))
    e_r = lax.broadcasted_iota(jnp.int32, (E, E), 0)
    e_c = lax.broadcasted_iota(jnp.int32, (E, E), 1)
    run_start = ROW_CHUNK * jnp.dot((e_c < e_r).astype(BF16),
                                    jnp.broadcast_to(chunks, (E, LANES)).astype(BF16),
                                    preferred_element_type=F32)[:, 0:1]
    ranks = [jnp.sum(jnp.where(rowid == ix, before + run_start, 0.0), axis=0, keepdims=True) for ix in idxs]

    pad_i = jnp.full((8 - TOP_K, tm), -1, jnp.int32)
    lp_ref[...] = jnp.concatenate([r.astype(jnp.int32) for r in ranks] + [pad_i], axis=0)
    gpad = jnp.concatenate(gates + [jnp.zeros((LANES - TOP_K, tm), F32)], axis=0)
    grow_ref[...] = gpad.T


def _out_proj(yf3, yd2, x2, mod3, w_out_b, g_ffn, wr_t, br_t, S, tm):
    T, D = x2.shape
    fw = yd2.shape[1]
    E = wr_t.shape[0]
    nt = T // tm
    tiles_per_batch = S // tm
    q = tm // DFT_RADIX
    col = np.arange(tm)
    perm = jnp.asarray(np.arange(tm)[:, None] == DFT_RADIX * (col % q) + col // q, BF16)
    kern = functools.partial(_out_proj_kernel, tm=tm, fw=fw, E=E)
    return pl.pallas_call(
        kern,
        grid=(nt,),
        in_specs=[pl.BlockSpec((None, q, DFT_RADIX * fw), lambda i: (i // tiles_per_batch, i % tiles_per_batch, 0)),
                  pl.BlockSpec((tm, tm), lambda i: (0, 0)),
                  pl.BlockSpec((tm, fw), lambda i: (i, 0)),
                  pl.BlockSpec((tm, D), lambda i: (i, 0)),
                  pl.BlockSpec((None, 6, D), lambda i: (i // tiles_per_batch, 0, 0)),
                  pl.BlockSpec((D, D), lambda i: (0, 0)),
                  pl.BlockSpec((1, D), lambda i: (0, 0)),
                  pl.BlockSpec((E, D), lambda i: (0, 0)),
                  pl.BlockSpec((E, 1), lambda i: (0, 0))],
        out_specs=[pl.BlockSpec((tm, D), lambda i: (i, 0)),
                   pl.BlockSpec((tm, D // 2), lambda i: (i, 0)),
                   pl.BlockSpec((8, tm), lambda i: (0, i)),
                   pl.BlockSpec((tm, LANES), lambda i: (i, 0)),
                   pl.BlockSpec((None, E, LANES), lambda i: (i, 0, 0))],
        out_shape=[jax.ShapeDtypeStruct((T, D), F32),
                   jax.ShapeDtypeStruct((T, D // 2), jnp.uint32),
                   jax.ShapeDtypeStruct((8, T), jnp.int32),
                   jax.ShapeDtypeStruct((T, LANES), F32),
                   jax.ShapeDtypeStruct((nt, E, LANES), jnp.int32)],
        compiler_params=_cparams(("arbitrary",)),
        name="out_proj_router",
    )(yf3, perm, yd2, x2, mod3, w_out_b, g_ffn, wr_t, br_t)


ROW_CHUNK = 8
CHUNK_UNROLL = 8


def _dispatch_kernel(tbl_ref, zb_ref, lp_ref, hp_ref, xs_ref, buf, zero_sc, sem, zsem, tsem,
                     *, tm, bm, NR, nz, nz_first, nt):
    i = pl.program_id(0)
    s = i % 2

    def zero_copy(e, zs):
        return pltpu.make_async_copy(zero_sc, xs_ref.at[pl.ds(zb_ref[e] * bm, bm)], zs)

    @pl.when(i == 0)
    def _():
        zero_sc[...] = jnp.zeros(zero_sc.shape, jnp.uint32)
        for e in range(nz):
            @pl.when(zb_ref[e] >= 0)
            def _():
                zero_copy(e, zsem if e < nz_first else tsem).start()
        for e in range(nz_first):
            @pl.when(zb_ref[e] >= 0)
            def _():
                zero_copy(e, zsem).wait()

    @pl.when(i == nt - 1)
    def _():
        for e in range(nz_first, nz):
            @pl.when(zb_ref[e] >= 0)
            def _():
                zero_copy(e, tsem).wait()

    lp = lp_ref[...].astype(jnp.int16)
    rows = lax.broadcasted_iota(jnp.int16, (NR, tm), 0)
    hit = rows == lp[0:1, :]
    for k in range(1, TOP_K):
        hit = hit | (rows == lp[k:k + 1, :])
    sel = jnp.where(hit, jnp.ones((), BF16), jnp.zeros((), BF16))
    lo, hi = _unpack_bf16_pairs(hp_ref[...])
    slo = jnp.dot(sel, lo, preferred_element_type=F32)
    shi = jnp.dot(sel, hi, preferred_element_type=F32)
    buf[s] = ((lax.bitcast_convert_type(slo, jnp.uint32) >> 16)
              | (lax.bitcast_convert_type(shi, jnp.uint32) & jnp.uint32(0xFFFF0000)))

    for slot in range(2):
        def issue(c, carry, slot=slot):
            d = pl.multiple_of(tbl_ref[0, 0, c], ROW_CHUNK)
            r = pl.multiple_of(c * ROW_CHUNK, ROW_CHUNK)
            pltpu.make_async_copy(buf.at[slot, pl.ds(r, ROW_CHUNK)], xs_ref.at[pl.ds(d, ROW_CHUNK)],
                                  sem.at[slot]).start()
            return carry

        @pl.when(s == slot)
        def _():
            lax.fori_loop(0, NR // ROW_CHUNK, issue, 0, unroll=CHUNK_UNROLL)

    def wait_slot(slot):
        pltpu.make_async_copy(buf.at[slot], xs_ref.at[pl.ds(0, NR)], sem.at[slot]).wait()

    @pl.when(i > 0)
    def _():
        wait_slot(1 - s)

    @pl.when(i == nt - 1)
    def _():
        wait_slot(s)


def _dispatch(tbl3, zero_blk, nz_first, lp8, hp, P, bm, tm, NR):
    T, Wd = hp.shape
    nt = T // tm
    nz = zero_blk.shape[0]
    nch_tbl = tbl3.shape[2]
    kern = functools.partial(_dispatch_kernel, tm=tm, bm=bm, NR=NR, nz=nz, nz_first=nz_first, nt=nt)
    return pl.pallas_call(
        kern,
        grid=(nt,),
        in_specs=[pl.BlockSpec((1, 1, nch_tbl), lambda i: (i, 0, 0), memory_space=pltpu.SMEM),
                  pl.BlockSpec(memory_space=pltpu.SMEM),
                  pl.BlockSpec((8, tm), lambda i: (0, i)),
                  pl.BlockSpec((tm, Wd), lambda i: (i, 0))],
        out_specs=pl.BlockSpec(memory_space=pl.ANY),
        out_shape=jax.ShapeDtypeStruct((P, Wd), jnp.uint32),
        scratch_shapes=[pltpu.VMEM((2, NR, Wd), jnp.uint32),
                        pltpu.VMEM((bm, Wd), jnp.uint32),
                        pltpu.SemaphoreType.DMA((2,)),
                        pltpu.SemaphoreType.DMA,
                        pltpu.SemaphoreType.DMA],
        compiler_params=_cparams(("arbitrary",)),
        name="moe_dispatch",
    )(tbl3, zero_blk, lp8, hp)


def _expert_kernel(be_ref, nu_ref, h2_ref, ord_ref, x_ref, w1_ref, b1g_ref, b1l_ref, w2_ref, b2_ref, pm_ref,
                   o_ref, w1g_sc, w1l_sc, w2_sc):
    blk = pl.program_id(0)
    nused = nu_ref[0]
    e = be_ref[blk]
    prev = be_ref[jnp.maximum(blk - 1, 0)]
    active = blk < nused
    second = active & (h2_ref[blk] != 0)
    half = x_ref.shape[0] // 2
    tile = 2 * LANES
    n_tiles = w1_ref.shape[1] // tile

    @pl.when(active & ((blk == 0) | (e != prev)))
    def _():
        for c in range(n_tiles):
            wt = w1_ref[:, c * tile:(c + 1) * tile].astype(BF16)
            pw = jnp.dot(wt, pm_ref[...], preferred_element_type=F32)
            w1g_sc[:, c * LANES:(c + 1) * LANES] = pw[:, :LANES].astype(BF16)
            w1l_sc[:, c * LANES:(c + 1) * LANES] = pw[:, LANES:].astype(BF16)
        w2_sc[...] = w2_ref[...].astype(BF16)

    @pl.when(jnp.logical_not(active))
    def _():
        o_ref[...] = jnp.zeros(o_ref.shape, jnp.uint32)

    def mlp(r0):
        xp = x_ref[r0:r0 + half, :]
        hw = xp.shape[1]
        xlo, xhi = _unpack_bf16_pairs(xp)
        zg = (jnp.dot(xlo, w1g_sc[:hw, :], preferred_element_type=F32)
              + jnp.dot(xhi, w1g_sc[hw:, :], preferred_element_type=F32) + b1g_ref[...])
        zl = (jnp.dot(xlo, w1l_sc[:hw, :], preferred_element_type=F32)
              + jnp.dot(xhi, w1l_sc[hw:, :], preferred_element_type=F32) + b1l_ref[...])
        g = jnp.minimum(zg, SWIGLU_LIMIT)
        lin = jnp.clip(zl, -SWIGLU_LIMIT, SWIGLU_LIMIT)
        act = g * jax.nn.sigmoid(SWIGLU_ALPHA * g) * (lin + 1.0)
        y = jnp.dot(act.astype(BF16), w2_sc[...], preferred_element_type=F32) + b2_ref[...]
        o_ref[r0:r0 + half, :] = _pack_bf16_pairs(y)

    @pl.when(second)
    def _():
        mlp(0)
        mlp(half)

    @pl.when(active & jnp.logical_not(second))
    def _():
        mlp(0)
        o_ref[half:, :] = jnp.zeros((half, o_ref.shape[1]), jnp.uint32)


def _experts(blk_expert, nused, half2, order, xs, P, w1, b1g, b1l, w2, b2, bm):
    Wd = xs.shape[1]
    E, D, F2 = w1.shape
    F = F2 // 2
    nblk = P // bm
    tile = 2 * LANES
    pm = np.zeros((tile, tile), np.float32)
    pm[2 * np.arange(LANES), np.arange(LANES)] = 1.0
    pm[2 * np.arange(LANES) + 1, LANES + np.arange(LANES)] = 1.0
    pm = jnp.asarray(pm, BF16)

    def row_map(b, be, nu, h2, od):
        return (jnp.minimum(od[b], nu[0] - 1), 0)

    def exp_map(b, be, nu, h2, od):
        return (be[b], 0, 0)

    grid_spec = pltpu.PrefetchScalarGridSpec(
        num_scalar_prefetch=4,
        grid=(nblk,),
        in_specs=[pl.BlockSpec((bm, Wd), row_map),
                  pl.BlockSpec((None, D, F2), exp_map),
                  pl.BlockSpec((None, 1, F), exp_map),
                  pl.BlockSpec((None, 1, F), exp_map),
                  pl.BlockSpec((None, F, D), exp_map),
                  pl.BlockSpec((None, 1, D), exp_map),
                  pl.BlockSpec((tile, tile), lambda b, be, nu, h2, od: (0, 0))],
        out_specs=pl.BlockSpec((bm, D // 2), lambda b, be, nu, h2, od: (od[b], 0)),
        scratch_shapes=[pltpu.VMEM((D, F), BF16),
                        pltpu.VMEM((D, F), BF16),
                        pltpu.VMEM((F, D), BF16)],
    )
    return pl.pallas_call(
        _expert_kernel,
        grid_spec=grid_spec,
        out_shape=jax.ShapeDtypeStruct((P, D // 2), jnp.uint32),
        compiler_params=_cparams(("arbitrary",)),
        name="moe_experts",
    )(blk_expert, nused, half2, order, xs, w1, b1g, b1l, w2, b2, pm)


def _combine_kernel(tcur_ref, tnxt_ref, lp_ref, grow_ref, x1_ref, mod_ref, y_ref, o_ref,
                    buf, sem, *, tm, NR, nt):
    i = pl.program_id(0)
    s = i % 2

    def issue(tref, slot):
        def body(c, carry):
            d = pl.multiple_of(tref[0, 0, c], ROW_CHUNK)
            r = pl.multiple_of(c * ROW_CHUNK, ROW_CHUNK)
            pltpu.make_async_copy(y_ref.at[pl.ds(d, ROW_CHUNK)], buf.at[slot, pl.ds(r, ROW_CHUNK)],
                                  sem.at[slot]).start()
            return carry
        lax.fori_loop(0, NR // ROW_CHUNK, body, 0, unroll=CHUNK_UNROLL)

    @pl.when(i == 0)
    def _():
        issue(tcur_ref, 0)

    for slot in range(2):
        @pl.when((i + 1 < nt) & (s != slot))
        def _():
            issue(tnxt_ref, slot)

    pltpu.make_async_copy(y_ref.at[pl.ds(0, NR)], buf.at[s], sem.at[s]).wait()

    ylo, yhi = _unpack_bf16_pairs(buf[s])
    lpf = jnp.concatenate([lp_ref[...].astype(F32), jnp.zeros((LANES - 8, tm), F32)], axis=0)
    lpr = lpf.T.astype(jnp.int16)
    gr = grow_ref[...].astype(BF16)
    cols = lax.broadcasted_iota(jnp.int16, (tm, NR), 1)
    gmat = jnp.zeros((tm, NR), BF16)
    for k in range(TOP_K):
        gmat = jnp.where(cols == lpr[:, k:k + 1], gr[:, k:k + 1], gmat)
    hw = ylo.shape[1]
    o_ref[:, :hw] = x1_ref[:, :hw] + mod_ref[5:6, :hw] * jnp.dot(gmat, ylo, preferred_element_type=F32)
    o_ref[:, hw:] = x1_ref[:, hw:] + mod_ref[5:6, hw:] * jnp.dot(gmat, yhi, preferred_element_type=F32)


def _combine(tbl3, lp8, grow, x1, mod3, y, S, tm, NR):
    T, D = x1.shape
    nt = T // tm
    tiles_per_batch = S // tm
    nch_tbl = tbl3.shape[2]
    kern = functools.partial(_combine_kernel, tm=tm, NR=NR, nt=nt)
    return pl.pallas_call(
        kern,
        grid=(nt,),
        in_specs=[pl.BlockSpec((1, 1, nch_tbl), lambda i: (i, 0, 0), memory_space=pltpu.SMEM),
                  pl.BlockSpec((1, 1, nch_tbl), lambda i: (jnp.minimum(i + 1, nt - 1), 0, 0),
                               memory_space=pltpu.SMEM),
                  pl.BlockSpec((8, tm), lambda i: (0, i)),
                  pl.BlockSpec((tm, LANES), lambda i: (i, 0)),
                  pl.BlockSpec((tm, D), lambda i: (i, 0)),
                  pl.BlockSpec((None, 6, D), lambda i: (i // tiles_per_batch, 0, 0)),
                  pl.BlockSpec(memory_space=pl.ANY)],
        out_specs=pl.BlockSpec((tm, D), lambda i: (i, 0)),
        out_shape=jax.ShapeDtypeStruct((T, D), F32),
        scratch_shapes=[pltpu.VMEM((2, NR, D // 2), jnp.uint32),
                        pltpu.SemaphoreType.DMA((2,))],
        compiler_params=_cparams(("arbitrary",)),
        name="moe_combine",
    )(tbl3, tbl3, lp8, grow, x1, mod3, y)


def _tiles(S):
    t_attn = min(512, S // 4) if S >= 512 else S
    tm = min(512, t_attn)
    return dict(tm=tm, t_attn=t_attn, bm=1024)


def _moe_layout(cnt, T, tm, bm):
    nt, E = cnt.shape
    c8 = (cnt + ROW_CHUNK - 1) // ROW_CHUNK * ROW_CHUNK
    tile_end = jnp.cumsum(c8, axis=1)
    tile_off = tile_end - c8
    used = tile_end[:, -1]
    off = jnp.cumsum(c8, axis=0) - c8
    tot = jnp.sum(c8, axis=0)
    padded = (tot + bm - 1) // bm * bm
    pends = jnp.cumsum(padded)
    pstarts = pends - padded
    NR = TOP_K * tm + ROW_CHUNK * E
    nch_tbl = NR // ROW_CHUNK
    P = (T * TOP_K + nt * E * (ROW_CHUNK - 1) + E * bm + bm - 1) // bm * bm
    nblk = P // bm

    c_row = jnp.arange(nch_tbl, dtype=jnp.int32) * ROW_CHUNK
    inrun = (tile_off[:, :, None] <= c_row) & (c_row < tile_end[:, :, None])
    base = pstarts[None, :] + off - tile_off
    tbl = jnp.sum(jnp.where(inrun, base[:, :, None], 0), axis=1) + c_row[None, :]
    in_use = c_row[None, :] < used[:, None]
    spare_rows = NR - TOP_K * tm
    spare = P + jnp.arange(nt, dtype=jnp.int32)[:, None] * spare_rows + (c_row[None, :] - used[:, None])
    tbl_out = jnp.where(in_use, tbl, spare).astype(jnp.int32)
    tbl_in = jnp.where(in_use, tbl, 0).astype(jnp.int32)
    n_spare_blk = (nt * spare_rows + bm - 1) // bm

    nused = (pends[-1] // bm).astype(jnp.int32).reshape(1)
    blk_start = jnp.arange(nblk, dtype=jnp.int32) * bm
    blk_expert = jnp.minimum(jnp.sum(pends[None, :] <= blk_start[:, None], axis=1), E - 1).astype(jnp.int32)
    own = blk_expert[:, None] == jnp.arange(E, dtype=jnp.int32)[None, :]

    def of_expert(table):
        return jnp.sum(jnp.where(own, table[None, :], 0), axis=1)

    step = jnp.arange(nblk, dtype=jnp.int32)
    order = jnp.where(step >= nused[0], step,
                      jnp.where(step == of_expert(pstarts // bm), of_expert(pends // bm - 1), step - 1))
    order = order.astype(jnp.int32)
    half2 = (order * bm + bm // 2 < of_expert(pstarts + tot)).astype(jnp.int32)
    last_blk = jnp.where(padded > 0, pends // bm - 1, -1)
    n_tail = nblk - (T * TOP_K) // bm
    tail_blk = nused[0] + jnp.arange(n_tail, dtype=jnp.int32)
    tail_blk = jnp.where(tail_blk < nblk, tail_blk, -1)
    spare_blk = nblk + jnp.arange(n_spare_blk, dtype=jnp.int32)
    zero_blk = jnp.concatenate([last_blk, spare_blk, tail_blk]).astype(jnp.int32)
    return dict(tbl_out=tbl_out.reshape(nt, 1, nch_tbl), tbl_in=tbl_in.reshape(nt, 1, nch_tbl),
                nused=nused, blk_expert=blk_expert, half2=half2, order=order, zero_blk=zero_blk,
                nz_first=E + n_spare_blk,
                P=P, P_alloc=P + n_spare_blk * bm, NR=NR)


def kernel(x, c, w_ada, b_ada, g_mix, w_in, w_fourier, q_norm_g, k_norm_g, lambda_q1, lambda_k1,
           lambda_q2, lambda_k2, g_subln, w_out, rel_bias, g_ffn, w_router, b_router, w1, b1, w2, b2):
    B, S, D = x.shape
    T = B * S
    L = w_ada.shape[0]
    E = w_router.shape[-1]
    cfg = _tiles(S)
    tm, t_attn, bm = cfg["tm"], cfg["t_attn"], cfg["bm"]
    dqk = q_norm_g.shape[-1]
    scale = dqk ** -0.5
    fw = w_in.shape[-1] // 4

    btiles = _bias_tiles(rel_bias, t_attn)
    x2 = x.reshape(T, D)
    for l in range(L):
        lam_init = 0.8 - 0.6 * math.exp(-0.3 * l)
        mod3 = _ada(c, w_ada[l], b_ada[l]).reshape(B, 6, D)
        gq = (jnp.tile(q_norm_g[l], 4) * (scale * LOG2E)).reshape(1, 2 * LANES)
        gk = jnp.tile(k_norm_g[l], 4).reshape(1, 2 * LANES)
        uf, qx, kx, v = _in_proj(x2, mod3, g_mix[l].reshape(1, D), w_in[l].astype(BF16), gq, gk,
                                 S, tm, t_attn)
        yf = _fourier(uf.reshape(B, S, fw), w_fourier[l])
        lam4 = jnp.stack([lambda_q1[l], lambda_k1[l], lambda_q2[l], lambda_k2[l]]).astype(F32)
        yd = _attention(rel_bias, qx.reshape(B, S, -1), kx.reshape(B, S, -1), v.reshape(B, S, fw),
                        btiles, lam4, g_subln[l].reshape(1, LANES), t_attn, lam_init).reshape(T, fw)
        x1, hp, lp8, grow, cnt3 = _out_proj(
            yf, yd, x2, mod3, w_out[l].astype(BF16), g_ffn[l].reshape(1, D),
            w_router[l].T.astype(BF16), b_router[l].reshape(E, 1), S, tm)
        lay = _moe_layout(cnt3[:, :, 0], T, tm, bm)
        xs = _dispatch(lay["tbl_out"], lay["zero_blk"], lay["nz_first"], lp8, hp, lay["P_alloc"], bm, tm,
                       lay["NR"])
        y = _experts(lay["blk_expert"], lay["nused"], lay["half2"], lay["order"], xs, lay["P"], w1[l],
                     b1[l][:, None, 0::2],
                     b1[l][:, None, 1::2], w2[l], b2[l][:, None, :], bm)
        x2 = _combine(lay["tbl_in"], lp8, grow, x1, mod3, y, S, tm, lay["NR"])
    return x2.reshape(B, S, D)
```

```python
import functools
import math

import numpy as np
import jax
import jax.numpy as jnp
from jax import lax
from jax.experimental import pallas as pl
from jax.experimental.pallas import tpu as pltpu

F32 = jnp.float32
BF16 = jnp.bfloat16

EPS = 1e-6
N_FOURIER_GROUPS = 4
N_DIFF_HEADS = 4
TOP_K = 4
N_BUCKETS = 32
MAX_DISTANCE = 128
SWIGLU_ALPHA = 1.702
SWIGLU_LIMIT = 7.0
LANES = 128
DFT_RADIX = 4
LOG2E = 1.4426950408889634
VMEM_LIMIT = 56 * 1024 * 1024


def _cparams(sem):
    return pltpu.CompilerParams(dimension_semantics=sem, vmem_limit_bytes=VMEM_LIMIT)


def _ada_kernel(c_ref, w_ref, b_ref, o_ref):
    c = c_ref[...]
    cond = c * jax.nn.sigmoid(c)
    o_ref[...] = jnp.dot(cond.astype(BF16), w_ref[...].astype(BF16),
                         preferred_element_type=F32) + b_ref[...]


def _ada(c, w, b):
    B, D = c.shape
    N = w.shape[1]
    tn = 1536 if N % 1536 == 0 else N
    return pl.pallas_call(
        _ada_kernel,
        grid=(N // tn,),
        in_specs=[pl.BlockSpec((B, D), lambda j: (0, 0)),
                  pl.BlockSpec((D, tn), lambda j: (0, j)),
                  pl.BlockSpec((1, tn), lambda j: (0, j))],
        out_specs=pl.BlockSpec((B, tn), lambda j: (0, j)),
        out_shape=jax.ShapeDtypeStruct((B, N), F32),
        compiler_params=_cparams(("arbitrary",)),
        name="ada",
    )(c, w, b.reshape(1, N))


def _group_rms_inv(xh, gmat, group):
    ss = jnp.dot((xh * xh).astype(BF16), gmat, preferred_element_type=F32)
    return lax.rsqrt(ss * (1.0 / group) + EPS)


def _in_proj_kernel(x_ref, mod_ref, g_ref, w_ref, gq_ref, gk_ref, gmat_ref,
                    uf_ref, q_ref, k_ref, v_ref, *, tiles_per_batch, tm, t_attn, n_chunks, fw, qw):
    i = pl.program_id(0)
    x = x_ref[...]
    ms = jnp.mean(x * x, axis=-1, keepdims=True)
    y = x * lax.rsqrt(ms + EPS) * g_ref[...]
    h = y * (1.0 + mod_ref[1:2, :]) + mod_ref[0:1, :]
    proj = jnp.dot(h.astype(BF16), w_ref[...], preferred_element_type=F32)
    uf_ref[...] = proj[:, :fw].astype(BF16)
    v_ref[...] = proj[:, fw + 2 * qw:].astype(BF16)

    lane = lax.broadcasted_iota(jnp.int32, (1, LANES), 1)
    chunk = ((i % tiles_per_batch) * tm) // t_attn
    gmat = gmat_ref[...]
    half = LANES // 2
    pair = 2 * LANES
    for hd in range(N_DIFF_HEADS):
        if hd % 2 == 0:
            q2 = proj[:, fw + hd * LANES: fw + hd * LANES + pair]
            k2 = proj[:, fw + qw + hd * LANES: fw + qw + hd * LANES + pair]
            qn2 = (q2 * _group_rms_inv(q2, gmat, half) * gq_ref[...]).astype(BF16)
            kn2 = (k2 * _group_rms_inv(k2, gmat, half) * gk_ref[...]).astype(BF16)
        qn = qn2[:, (hd % 2) * LANES:(hd % 2 + 1) * LANES]
        kn = kn2[:, (hd % 2) * LANES:(hd % 2 + 1) * LANES]
        for m in range(2):
            fl = lane - half if m == 0 else lane
            is_data = (lane < half) if m == 0 else (lane >= half)
            f = jnp.where(fl < n_chunks, fl, fl - n_chunks)
            onehot = ((fl >= 0) & (fl < 2 * n_chunks) & (f == chunk)).astype(BF16)
            c0 = (2 * hd + m) * LANES
            q_ref[:, c0:c0 + LANES] = jnp.where(is_data, qn, jnp.zeros_like(qn))
            k_ref[:, c0:c0 + LANES] = jnp.where(is_data, kn, onehot)


def _in_proj(x2, mod3, g_mix, w_in_b, gq, gk, S, tm, t_attn):
    T, D = x2.shape
    ncols = w_in_b.shape[1]
    fw = ncols // 4
    qw = fw
    tiles_per_batch = S // tm
    n_chunks = S // t_attn
    half = LANES // 2
    pair = 2 * LANES
    gmat = (np.arange(pair)[:, None] // half == np.arange(pair)[None, :] // half)
    gmat = jnp.asarray(gmat, BF16)
    kern = functools.partial(_in_proj_kernel, tiles_per_batch=tiles_per_batch, tm=tm, t_attn=t_attn,
                             n_chunks=n_chunks, fw=fw, qw=qw)
    ext = 2 * qw
    return pl.pallas_call(
        kern,
        grid=(T // tm,),
        in_specs=[pl.BlockSpec((tm, D), lambda i: (i, 0)),
                  pl.BlockSpec((None, 6, D), lambda i: (i // tiles_per_batch, 0, 0)),
                  pl.BlockSpec((1, D), lambda i: (0, 0)),
                  pl.BlockSpec((D, ncols), lambda i: (0, 0)),
                  pl.BlockSpec((1, pair), lambda i: (0, 0)),
                  pl.BlockSpec((1, pair), lambda i: (0, 0)),
                  pl.BlockSpec((pair, pair), lambda i: (0, 0))],
        out_specs=[pl.BlockSpec((tm, fw), lambda i: (i, 0)),
                   pl.BlockSpec((tm, ext), lambda i: (i, 0)),
                   pl.BlockSpec((tm, ext), lambda i: (i, 0)),
                   pl.BlockSpec((tm, fw), lambda i: (i, 0))],
        out_shape=[jax.ShapeDtypeStruct((T, fw), BF16),
                   jax.ShapeDtypeStruct((T, ext), BF16),
                   jax.ShapeDtypeStruct((T, ext), BF16),
                   jax.ShapeDtypeStruct((T, fw), BF16)],
        compiler_params=_cparams(("arbitrary",)),
        name="in_proj",
    )(x2, mod3, g_mix, w_in_b, gq, gk, gmat)


def _fourier_kernel(uf_ref, wcs_ref, tw_ref, dft_ref, o_ref, z_sc, tt_sc, *, M, gw):
    c = pl.program_id(1)

    def channel_stage():
        for g in range(N_FOURIER_GROUPS):
            for j in range(DFT_RADIX):
                ab = jnp.dot(uf_ref[j * M:(j + 1) * M, g * gw:(g + 1) * gw], wcs_ref[g],
                             preferred_element_type=F32)
                z_sc[0, j, :, g * gw:(g + 1) * gw] = ab[:, :gw]
                z_sc[1, j, :, g * gw:(g + 1) * gw] = -ab[:, gw:]

    def butterfly(k2):
        cs = tw_ref[:, k2:k2 + 1]
        sn = tw_ref[:, DFT_RADIX + k2:DFT_RADIX + k2 + 1]
        for g in range(N_FOURIER_GROUPS):
            cols = slice(g * gw, (g + 1) * gw)
            zr = [z_sc[0, j, :, cols] for j in range(DFT_RADIX)]
            zi = [z_sc[1, j, :, cols] for j in range(DFT_RADIX)]
            if k2 == 0:
                yr = zr[0] + zr[1] + zr[2] + zr[3]
                yi = zi[0] + zi[1] + zi[2] + zi[3]
            elif k2 == 1:
                yr = zr[0] + zi[1] - zr[2] - zi[3]
                yi = zi[0] - zr[1] - zi[2] + zr[3]
            elif k2 == 2:
                yr = zr[0] - zr[1] + zr[2] - zr[3]
                yi = zi[0] - zi[1] + zi[2] - zi[3]
            else:
                yr = zr[0] - zi[1] - zr[2] + zi[3]
                yi = zi[0] + zr[1] - zi[2] - zr[3]
            tt_sc[k2, 0:M, cols] = (cs * yr + sn * yi).astype(BF16)
            tt_sc[k2, M:2 * M, cols] = (cs * yi - sn * yr).astype(BF16)

    for cc in range(DFT_RADIX):
        @pl.when(c == cc)
        def _(cc=cc):
            if cc == 0:
                channel_stage()
                butterfly(0)
            o_ref[...] = jnp.dot(dft_ref[...], tt_sc[cc], preferred_element_type=F32).astype(BF16)
            if cc + 1 < DFT_RADIX:
                butterfly(cc + 1)


def _fourier(uf3, w_f):
    B, S, W = uf3.shape
    G = N_FOURIER_GROUPS
    gw = W // G
    M = S // DFT_RADIX
    norm = 1.0 / math.sqrt(S * gw)
    cidx = (np.arange(gw)[:, None] * np.arange(gw)[None, :]) % gw
    ang_c = cidx * (2.0 * math.pi / gw)
    cc = jnp.asarray(np.cos(ang_c) * norm, F32)
    sc = jnp.asarray(np.sin(ang_c) * norm, F32)
    hp = lax.Precision.HIGHEST
    wcs = jnp.concatenate([jnp.einsum("ab,gbd->gad", cc, w_f, precision=hp),
                           jnp.einsum("ab,gbd->gad", sc, w_f, precision=hp)], axis=-1).astype(BF16)
    n1 = np.arange(M)
    ang_t = ((np.arange(DFT_RADIX)[None, :] * n1[:, None]) % S) * (2.0 * math.pi / S)
    tw = jnp.asarray(np.concatenate([np.cos(ang_t), np.sin(ang_t),
                                     np.zeros((M, LANES - 2 * DFT_RADIX))], axis=1), F32)
    ang_p = ((n1[:, None] * n1[None, :]) % M) * (2.0 * math.pi / M)
    dft = jnp.asarray(np.concatenate([np.cos(ang_p), np.sin(ang_p)], axis=1).astype(np.float32)
                      .astype(jnp.bfloat16))
    kern = functools.partial(_fourier_kernel, M=M, gw=gw)
    return pl.pallas_call(
        kern,
        grid=(B, DFT_RADIX),
        in_specs=[pl.BlockSpec((None, S, W), lambda b, c: (b, 0, 0)),
                  pl.BlockSpec((G, gw, 2 * gw), lambda b, c: (0, 0, 0)),
                  pl.BlockSpec((M, LANES), lambda b, c: (0, 0)),
                  pl.BlockSpec((M, 2 * M), lambda b, c: (0, 0))],
        out_specs=pl.BlockSpec((None, M, W), lambda b, c: (b, 0, c)),
        out_shape=jax.ShapeDtypeStruct((B, M, DFT_RADIX * W), BF16),
        scratch_shapes=[pltpu.VMEM((2, DFT_RADIX, M, W), F32),
                        pltpu.VMEM((DFT_RADIX, 2 * M, W), BF16)],
        compiler_params=_cparams(("arbitrary", "arbitrary")),
        name="fourier",
    )(uf3, wcs, tw, dft)


def _bias_kernel(rb_ref, o_ref, *, t):
    h = pl.program_id(0)
    which = pl.program_id(1)
    nb = N_BUCKETS // 2

    def tile(d, buckets, r0=0, c0=0, nr=t, nc=t):
        row = r0 + lax.broadcasted_iota(jnp.int32, (nr, nc), 0)
        col = c0 + lax.broadcasted_iota(jnp.int32, (nr, nc), 1)
        rel = d * t + col - row
        max_exact = nb // 2
        ret = jnp.where(rel > 0, nb, 0)
        n = jnp.abs(rel)
        nf = jnp.maximum(n, 1).astype(F32)
        large = max_exact + (jnp.log(nf / max_exact) / math.log(MAX_DISTANCE / max_exact)
                             * (nb - max_exact)).astype(jnp.int32)
        large = jnp.minimum(large, nb - 1)
        bucket = ret + jnp.where(n < max_exact, n, large)
        val = jnp.zeros((nr, nc), F32)
        for j in buckets:
            val = jnp.where(bucket == j, rb_ref[j * N_DIFF_HEADS + h], val)
        return val * LOG2E

    left_b, right_b = _far_bucket_consts()
    corner = MAX_DISTANCE

    @pl.when(which == 0)
    def _():
        o_ref[...] = jnp.full((t, t), rb_ref[left_b * N_DIFF_HEADS + h] * LOG2E, F32)
        o_ref[0:corner, t - corner:t] = tile(-1, range(0, nb), 0, t - corner, corner, corner)

    @pl.when(which == 1)
    def _():
        o_ref[...] = tile(0, range(0, 2 * nb))

    @pl.when(which == 2)
    def _():
        o_ref[...] = jnp.full((t, t), rb_ref[right_b * N_DIFF_HEADS + h] * LOG2E, F32)
        o_ref[t - corner:t, 0:corner] = tile(1, range(nb, 2 * nb), t - corner, 0, corner, corner)

    @pl.when(which == 3)
    def _():
        o_ref[...] = jnp.zeros((t, t), F32)


def _bias_tiles(rel_bias, t):
    H = N_DIFF_HEADS
    assert t >= MAX_DISTANCE, "far key tiles must lie entirely in the saturated bucket"
    return pl.pallas_call(
        functools.partial(_bias_kernel, t=t),
        grid=(H, 4),
        in_specs=[pl.BlockSpec(memory_space=pltpu.SMEM)],
        out_specs=pl.BlockSpec((None, None, t, t), lambda h, d: (h, d, 0, 0)),
        out_shape=jax.ShapeDtypeStruct((H, 4, t, t), F32),
        compiler_params=_cparams(("arbitrary", "arbitrary")),
        name="t5_bias",
    )(rel_bias.reshape(-1))


def _far_bucket_consts():
    nb = N_BUCKETS // 2
    return nb - 1, 2 * nb - 1


def _attn_kernel(rb_ref, q_ref, k_ref, v_ref, b_ref, lam_ref, gs_ref, o_ref,
                 s_sc, p_sc, vext_sc, m_sc, raw_sc, *, t, n, n_tiles, lam_init):
    g = pl.program_id(0)
    cur = jnp.minimum(g, n_tiles - 1)
    h = (cur // n) % N_DIFF_HEADS
    i = cur % n
    half = LANES // 2

    @pl.when(g == 0)
    def _():
        raw_sc[...] = jnp.ones(raw_sc.shape, F32)

    @pl.when((i == 0) & (g < n_tiles))
    def _():
        vext_sc[:, :LANES] = v_ref[...]
        vext_sc[:, LANES:] = jnp.ones(v_ref.shape, BF16)

    left_b, right_b = _far_bucket_consts()
    c_left = rb_ref[left_b * N_DIFF_HEADS + h] * LOG2E
    c_right = rb_ref[right_b * N_DIFF_HEADS + h] * LOG2E
    lane = lax.broadcasted_iota(jnp.int32, (1, LANES), 1)

    def all_scores(m):
        fl = lane - half if m == 0 else lane
        is_data = (lane < half) if m == 0 else (lane >= half)
        f = jnp.where(fl < n, fl, fl - n)
        cfar = jnp.where(f < i - 1, c_left, jnp.where(f > i + 1, c_right, 0.0))
        c_hi = cfar.astype(BF16)
        c_lo = (cfar - c_hi.astype(F32)).astype(BF16)
        feat = jnp.where(fl < n, c_hi, c_lo)
        feat = jnp.where((fl >= 0) & (fl < 2 * n), feat, jnp.zeros_like(feat))
        qm = jnp.where(is_data, q_ref[:, m * LANES:(m + 1) * LANES], feat)
        for j in range(n):
            kc = k_ref[j * t:(j + 1) * t, m * LANES:(m + 1) * LANES]
            s = lax.dot_general(qm, kc, (((1,), (1,)), ((), ())), preferred_element_type=F32)
            d = j - i
            s = s + b_ref[jnp.where(jnp.abs(d) <= 1, d + 1, 3)]
            s_sc[m, j] = s
            tile_max = s[:, 0:LANES]
            for c in range(1, t // LANES):
                tile_max = jnp.maximum(tile_max, s[:, c * LANES:(c + 1) * LANES])
            if j == 0:
                m_sc[m] = tile_max
            else:
                m_sc[m] = jnp.maximum(m_sc[m], tile_max)

    def exponentials(m):
        mx = jnp.max(m_sc[m], axis=-1, keepdims=True)
        for j in range(n):
            p_sc[m, :, j * t:(j + 1) * t] = jnp.exp2(s_sc[m, j] - mx).astype(BF16)

    def finish_previous():
        r0 = raw_sc[0]
        r1 = raw_sc[1]
        lam = (jnp.exp(jnp.sum(lam_ref[0:1, :] * lam_ref[1:2, :], axis=-1, keepdims=True))
               - jnp.exp(jnp.sum(lam_ref[2:3, :] * lam_ref[3:4, :], axis=-1, keepdims=True)) + lam_init)
        o = r0[:, :LANES] / r0[:, LANES:LANES + 1] - lam * (r1[:, :LANES] / r1[:, LANES:LANES + 1])
        ms = jnp.mean(o * o, axis=-1, keepdims=True)
        o_ref[...] = (o * lax.rsqrt(ms + EPS) * gs_ref[...] * (1.0 - lam_init)).astype(BF16)

    @pl.when(i < n)
    def _():
        finish_previous()
        all_scores(0)

    @pl.when(g < n_tiles)
    def _():
        all_scores(1)
        exponentials(0)
        raw_sc[0] = jnp.dot(p_sc[0], vext_sc[...], preferred_element_type=F32)
        exponentials(1)
        raw_sc[1] = jnp.dot(p_sc[1], vext_sc[...], preferred_element_type=F32)


def _attention(rel_bias, q3, k3, v3, btiles, lam4, g_subln, t, lam_init):
    B, S, _ = q3.shape
    H = N_DIFF_HEADS
    n = S // t
    n_tiles = B * H * n
    kern = functools.partial(_attn_kernel, t=t, n=n, n_tiles=n_tiles, lam_init=lam_init)

    def tile_of(g):
        return g // (H * n), (g // n) % H, g % n

    def cur_map(fn):
        def index_map(g):
            return fn(*tile_of(jnp.minimum(g, n_tiles - 1)))
        return index_map

    def prev_map(g):
        b, h, i = tile_of(jnp.maximum(g - 1, 0))
        return (b, i, h)

    return pl.pallas_call(
        kern,
        grid=(n_tiles + 1,),
        in_specs=[pl.BlockSpec(memory_space=pltpu.SMEM),
                  pl.BlockSpec((None, t, 2 * LANES), cur_map(lambda b, h, i: (b, i, h))),
                  pl.BlockSpec((None, S, 2 * LANES), cur_map(lambda b, h, i: (b, 0, h))),
                  pl.BlockSpec((None, S, LANES), cur_map(lambda b, h, i: (b, 0, h))),
                  pl.BlockSpec((None, 4, t, t), cur_map(lambda b, h, i: (h, 0, 0, 0))),
                  pl.BlockSpec((4, LANES // 2), lambda g: (0, 0)),
                  pl.BlockSpec((1, LANES), lambda g: (0, 0))],
        out_specs=pl.BlockSpec((None, t, LANES), prev_map),
        out_shape=jax.ShapeDtypeStruct((B, S, H * LANES), BF16),
        scratch_shapes=[pltpu.VMEM((2, n, t, t), F32),
                        pltpu.VMEM((2, t, S), BF16),
                        pltpu.VMEM((S, 2 * LANES), BF16),
                        pltpu.VMEM((2, t, LANES), F32),
                        pltpu.VMEM((2, t, 2 * LANES), F32)],
        compiler_params=_cparams(("arbitrary",)),
        name="diff_attn",
    )(rel_bias.reshape(-1), q3, k3, v3, btiles, lam4, g_subln)


def _pack_bf16_pairs(v):
    bits = lax.bitcast_convert_type(v.astype(BF16).astype(F32), jnp.uint32)
    n = bits.shape[1] // 2
    return (bits[:, :n] >> 16) | (bits[:, n:] & jnp.uint32(0xFFFF0000))


def _unpack_bf16_pairs(p):
    lo = lax.bitcast_convert_type(p << 16, F32).astype(BF16)
    hi = lax.bitcast_convert_type(p & jnp.uint32(0xFFFF0000), F32).astype(BF16)
    return lo, hi


def _out_proj_kernel(yf_ref, perm_ref, yd_ref, x_ref, mod_ref, wo_ref, g_ref, wr_ref, br_ref,
                     x1_ref, hp_ref, lp_ref, grow_ref, cnt_ref, *, tm, fw, E):
    yf_stack = jnp.concatenate([yf_ref[:, r * fw:(r + 1) * fw] for r in range(DFT_RADIX)], axis=0)
    yf = jnp.dot(perm_ref[...], yf_stack, preferred_element_type=F32).astype(BF16)
    mix = (jnp.dot(yf, wo_ref[:fw, :], preferred_element_type=F32)
           + jnp.dot(yd_ref[...], wo_ref[fw:, :], preferred_element_type=F32))
    x1 = x_ref[...] + mod_ref[2:3, :] * mix
    x1_ref[...] = x1
    ms = jnp.mean(x1 * x1, axis=-1, keepdims=True)
    h = x1 * lax.rsqrt(ms + EPS) * g_ref[...]
    h = h * (1.0 + mod_ref[4:5, :]) + mod_ref[3:4, :]
    hb = h.astype(BF16)
    hp_ref[...] = hb

    logits = lax.dot_general(wr_ref[...], hb, (((1,), (1,)), ((), ())),
                             preferred_element_type=F32) + br_ref[...]
    rowid = lax.broadcasted_iota(jnp.int32, (E, tm), 0)
    vals, idxs = [], []
    comb = jnp.zeros((E, tm), F32)
    l = logits
    for _ in range(TOP_K):
        mv = jnp.max(l, axis=0, keepdims=True)
        ix = jnp.min(jnp.where(l == mv, rowid, E), axis=0, keepdims=True)
        sel = rowid == ix
        vals.append(mv)
        idxs.append(ix)
        comb = comb + sel.astype(F32)
        l = jnp.where(sel, -jnp.inf, l)
    es = [jnp.exp(v - vals[0]) for v in vals]
    den = es[0] + es[1] + es[2] + es[3]
    gates = [e / den for e in es]

    r_i = lax.broadcasted_iota(jnp.int32, (tm, tm), 0)
    c_i = lax.broadcasted_iota(jnp.int32, (tm, tm), 1)
    upper = (r_i < c_i).astype(BF16)
    before = jnp.dot(comb.astype(BF16), upper, preferred_element_type=F32)
    cnt = jnp.sum(comb, axis=1, keepdims=True)
    cnt_ref[...] = jnp.broadcast_to(cnt, cnt_ref.shape).astype(jnp.int32)
    chunks = jnp.floor((cnt + (ROW_CHUNK - 1)) * (1.0 / ROW_CHUNK))
    e_r = lax.broadcasted_iota(jnp.int32, (E, E), 0)
    e_c = lax.broadcasted_iota(jnp.int32, (E, E), 1)
    run_start = ROW_CHUNK * jnp.dot((e_c < e_r).astype(BF16),
                                    jnp.broadcast_to(chunks, (E, LANES)).astype(BF16),
                                    preferred_element_type=F32)[:, 0:1]
    ranks = [jnp.sum(jnp.where(rowid == ix, before + run_start, 0.0), axis=0, keepdims=True) for ix in idxs]

    pad_i = jnp.full((8 - TOP_K, tm), -1, jnp.int32)
    lp_ref[...] = jnp.concatenate([r.astype(jnp.int32) for r in ranks] + [pad_i], axis=0)
    gpad = jnp.concatenate(gates + [jnp.zeros((LANES - TOP_K, tm), F32)], axis=0)
    grow_ref[...] = gpad.T


def _out_proj(yf3, yd2, x2, mod3, w_out_b, g_ffn, wr_t, br_t, S, tm):
    T, D = x2.shape
    fw = yd2.shape[1]
    E = wr_t.shape[0]
    nt = T // tm
    tiles_per_batch = S // tm
    q = tm // DFT_RADIX
    col = np.arange(tm)
    perm = jnp.asarray(np.arange(tm)[:, None] == DFT_RADIX * (col % q) + col // q, BF16)
    kern = functools.partial(_out_proj_kernel, tm=tm, fw=fw, E=E)
    return pl.pallas_call(
        kern,
        grid=(nt,),
        in_specs=[pl.BlockSpec((None, q, DFT_RADIX * fw), lambda i: (i // tiles_per_batch, i % tiles_per_batch, 0)),
                  pl.BlockSpec((tm, tm), lambda i: (0, 0)),
                  pl.BlockSpec((tm, fw), lambda i: (i, 0)),
                  pl.BlockSpec((tm, D), lambda i: (i, 0)),
                  pl.BlockSpec((None, 6, D), lambda i: (i // tiles_per_batch, 0, 0)),
                  pl.BlockSpec((D, D), lambda i: (0, 0)),
                  pl.BlockSpec((1, D), lambda i: (0, 0)),
                  pl.BlockSpec((E, D), lambda i: (0, 0)),
                  pl.BlockSpec((E, 1), lambda i: (0, 0))],
        out_specs=[pl.BlockSpec((tm, D), lambda i: (i, 0)),
                   pl.BlockSpec((tm, D), lambda i: (i, 0)),
                   pl.BlockSpec((8, tm), lambda i: (0, i)),
                   pl.BlockSpec((tm, LANES), lambda i: (i, 0)),
                   pl.BlockSpec((None, E, LANES), lambda i: (i, 0, 0))],
        out_shape=[jax.ShapeDtypeStruct((T, D), F32),
                   jax.ShapeDtypeStruct((T, D), BF16),
                   jax.ShapeDtypeStruct((8, T), jnp.int32),
                   jax.ShapeDtypeStruct((T, LANES), F32),
                   jax.ShapeDtypeStruct((nt, E, LANES), jnp.int32)],
        compiler_params=_cparams(("arbitrary",)),
        name="out_proj_router",
    )(yf3, perm, yd2, x2, mod3, w_out_b, g_ffn, wr_t, br_t)


ROW_CHUNK = 8
CHUNK_UNROLL = 8


def _dispatch_kernel(tbl_ref, zb_ref, lp_ref, hp_ref, xs_ref, buf, zero_sc, sem, zsem, tsem,
                     *, tm, bm, NR, nz, nz_first, nt):
    i = pl.program_id(0)
    s = i % 2

    def zero_copy(e, zs):
        return pltpu.make_async_copy(zero_sc, xs_ref.at[pl.ds(zb_ref[e] * bm, bm)], zs)

    @pl.when(i == 0)
    def _():
        zero_sc[...] = jnp.zeros(zero_sc.shape, jnp.uint32)
        for e in range(nz):
            @pl.when(zb_ref[e] >= 0)
            def _():
                zero_copy(e, zsem if e < nz_first else tsem).start()
        for e in range(nz_first):
            @pl.when(zb_ref[e] >= 0)
            def _():
                zero_copy(e, zsem).wait()

    @pl.when(i == nt - 1)
    def _():
        for e in range(nz_first, nz):
            @pl.when(zb_ref[e] >= 0)
            def _():
                zero_copy(e, tsem).wait()

    lp = lp_ref[...].astype(jnp.int16)
    rows = lax.broadcasted_iota(jnp.int16, (NR, tm), 0)
    hit = rows == lp[0:1, :]
    for k in range(1, TOP_K):
        hit = hit | (rows == lp[k:k + 1, :])
    sel = jnp.where(hit, jnp.ones((), BF16), jnp.zeros((), BF16))
    hw = hp_ref.shape[1] // 2
    lo = hp_ref[:, :hw]
    hi = hp_ref[:, hw:]
    slo = jnp.dot(sel, lo, preferred_element_type=F32)
    shi = jnp.dot(sel, hi, preferred_element_type=F32)
    buf[s] = ((lax.bitcast_convert_type(slo, jnp.uint32) >> 16)
              | (lax.bitcast_convert_type(shi, jnp.uint32) & jnp.uint32(0xFFFF0000)))

    for slot in range(2):
        def issue(c, carry, slot=slot):
            d = pl.multiple_of(tbl_ref[0, 0, c], ROW_CHUNK)
            r = pl.multiple_of(c * ROW_CHUNK, ROW_CHUNK)
            pltpu.make_async_copy(buf.at[slot, pl.ds(r, ROW_CHUNK)], xs_ref.at[pl.ds(d, ROW_CHUNK)],
                                  sem.at[slot]).start()
            return carry

        @pl.when(s == slot)
        def _():
            lax.fori_loop(0, NR // ROW_CHUNK, issue, 0, unroll=CHUNK_UNROLL)

    def wait_slot(slot):
        pltpu.make_async_copy(buf.at[slot], xs_ref.at[pl.ds(0, NR)], sem.at[slot]).wait()

    @pl.when(i > 0)
    def _():
        wait_slot(1 - s)

    @pl.when(i == nt - 1)
    def _():
        wait_slot(s)


def _dispatch(tbl3, zero_blk, nz_first, lp8, hp, P, bm, tm, NR):
    T, D = hp.shape
    Wd = D // 2
    nt = T // tm
    nz = zero_blk.shape[0]
    nch_tbl = tbl3.shape[2]
    kern = functools.partial(_dispatch_kernel, tm=tm, bm=bm, NR=NR, nz=nz, nz_first=nz_first, nt=nt)
    return pl.pallas_call(
        kern,
        grid=(nt,),
        in_specs=[pl.BlockSpec((1, 1, nch_tbl), lambda i: (i, 0, 0), memory_space=pltpu.SMEM),
                  pl.BlockSpec(memory_space=pltpu.SMEM),
                  pl.BlockSpec((8, tm), lambda i: (0, i)),
                  pl.BlockSpec((tm, D), lambda i: (i, 0))],
        out_specs=pl.BlockSpec(memory_space=pl.ANY),
        out_shape=jax.ShapeDtypeStruct((P, Wd), jnp.uint32),
        scratch_shapes=[pltpu.VMEM((2, NR, Wd), jnp.uint32),
                        pltpu.VMEM((bm, Wd), jnp.uint32),
                        pltpu.SemaphoreType.DMA((2,)),
                        pltpu.SemaphoreType.DMA,
                        pltpu.SemaphoreType.DMA],
        compiler_params=_cparams(("arbitrary",)),
        name="moe_dispatch",
    )(tbl3, zero_blk, lp8, hp)


def _expert_kernel(be_ref, nu_ref, h2_ref, ord_ref, x_ref, w1_ref, b1g_ref, b1l_ref, w2_ref, b2_ref, pm_ref,
                   o_ref, w1g_sc, w1l_sc, w2_sc):
    blk = pl.program_id(0)
    nused = nu_ref[0]
    e = be_ref[blk]
    prev = be_ref[jnp.maximum(blk - 1, 0)]
    active = blk < nused
    second = active & (h2_ref[blk] != 0)
    half = x_ref.shape[0] // 2
    tile = 2 * LANES
    n_tiles = w1_ref.shape[1] // tile

    @pl.when(active & ((blk == 0) | (e != prev)))
    def _():
        for c in range(n_tiles):
            wt = w1_ref[:, c * tile:(c + 1) * tile].astype(BF16)
            pw = jnp.dot(wt, pm_ref[...], preferred_element_type=F32)
            w1g_sc[:, c * LANES:(c + 1) * LANES] = pw[:, :LANES].astype(BF16)
            w1l_sc[:, c * LANES:(c + 1) * LANES] = pw[:, LANES:].astype(BF16)
        w2_sc[...] = w2_ref[...].astype(BF16)

    @pl.when(jnp.logical_not(active))
    def _():
        o_ref[...] = jnp.zeros(o_ref.shape, jnp.uint32)

    def mlp(r0):
        xp = x_ref[r0:r0 + half, :]
        hw = xp.shape[1]
        xlo, xhi = _unpack_bf16_pairs(xp)
        zg = (jnp.dot(xlo, w1g_sc[:hw, :], preferred_element_type=F32)
              + jnp.dot(xhi, w1g_sc[hw:, :], preferred_element_type=F32) + b1g_ref[...])
        zl = (jnp.dot(xlo, w1l_sc[:hw, :], preferred_element_type=F32)
              + jnp.dot(xhi, w1l_sc[hw:, :], preferred_element_type=F32) + b1l_ref[...])
        g = jnp.minimum(zg, SWIGLU_LIMIT)
        lin = jnp.clip(zl, -SWIGLU_LIMIT, SWIGLU_LIMIT)
        act = g * jax.nn.sigmoid(SWIGLU_ALPHA * g) * (lin + 1.0)
        y = jnp.dot(act.astype(BF16), w2_sc[...], preferred_element_type=F32) + b2_ref[...]
        o_ref[r0:r0 + half, :] = _pack_bf16_pairs(y)

    @pl.when(second)
    def _():
        mlp(0)
        mlp(half)

    @pl.when(active & jnp.logical_not(second))
    def _():
        mlp(0)
        o_ref[half:, :] = jnp.zeros((half, o_ref.shape[1]), jnp.uint32)


def _experts(blk_expert, nused, half2, order, xs, P, w1, b1g, b1l, w2, b2, bm):
    Wd = xs.shape[1]
    E, D, F2 = w1.shape
    F = F2 // 2
    nblk = P // bm
    tile = 2 * LANES
    pm = np.zeros((tile, tile), np.float32)
    pm[2 * np.arange(LANES), np.arange(LANES)] = 1.0
    pm[2 * np.arange(LANES) + 1, LANES + np.arange(LANES)] = 1.0
    pm = jnp.asarray(pm, BF16)

    def row_map(b, be, nu, h2, od):
        return (jnp.minimum(od[b], nu[0] - 1), 0)

    def exp_map(b, be, nu, h2, od):
        return (be[b], 0, 0)

    grid_spec = pltpu.PrefetchScalarGridSpec(
        num_scalar_prefetch=4,
        grid=(nblk,),
        in_specs=[pl.BlockSpec((bm, Wd), row_map),
                  pl.BlockSpec((None, D, F2), exp_map),
                  pl.BlockSpec((None, 1, F), exp_map),
                  pl.BlockSpec((None, 1, F), exp_map),
                  pl.BlockSpec((None, F, D), exp_map),
                  pl.BlockSpec((None, 1, D), exp_map),
                  pl.BlockSpec((tile, tile), lambda b, be, nu, h2, od: (0, 0))],
        out_specs=pl.BlockSpec((bm, D // 2), lambda b, be, nu, h2, od: (od[b], 0)),
        scratch_shapes=[pltpu.VMEM((D, F), BF16),
                        pltpu.VMEM((D, F), BF16),
                        pltpu.VMEM((F, D), BF16)],
    )
    return pl.pallas_call(
        _expert_kernel,
        grid_spec=grid_spec,
        out_shape=jax.ShapeDtypeStruct((P, D // 2), jnp.uint32),
        compiler_params=_cparams(("arbitrary",)),
        name="moe_experts",
    )(blk_expert, nused, half2, order, xs, w1, b1g, b1l, w2, b2, pm)


def _combine_kernel(tcur_ref, tnxt_ref, lp_ref, grow_ref, x1_ref, mod_ref, y_ref, o_ref,
                    buf, sem, *, tm, NR, nt):
    i = pl.program_id(0)
    s = i % 2

    def issue(tref, slot):
        def body(c, carry):
            d = pl.multiple_of(tref[0, 0, c], ROW_CHUNK)
            r = pl.multiple_of(c * ROW_CHUNK, ROW_CHUNK)
            pltpu.make_async_copy(y_ref.at[pl.ds(d, ROW_CHUNK)], buf.at[slot, pl.ds(r, ROW_CHUNK)],
                                  sem.at[slot]).start()
            return carry
        lax.fori_loop(0, NR // ROW_CHUNK, body, 0, unroll=CHUNK_UNROLL)

    @pl.when(i == 0)
    def _():
        issue(tcur_ref, 0)

    for slot in range(2):
        @pl.when((i + 1 < nt) & (s != slot))
        def _():
            issue(tnxt_ref, slot)

    pltpu.make_async_copy(y_ref.at[pl.ds(0, NR)], buf.at[s], sem.at[s]).wait()

    ylo, yhi = _unpack_bf16_pairs(buf[s])
    lpf = jnp.concatenate([lp_ref[...].astype(F32), jnp.zeros((LANES - 8, tm), F32)], axis=0)
    lpr = lpf.T.astype(jnp.int16)
    gr = grow_ref[...].astype(BF16)
    cols = lax.broadcasted_iota(jnp.int16, (tm, NR), 1)
    gmat = jnp.zeros((tm, NR), BF16)
    for k in range(TOP_K):
        gmat = jnp.where(cols == lpr[:, k:k + 1], gr[:, k:k + 1], gmat)
    hw = ylo.shape[1]
    o_ref[:, :hw] = x1_ref[:, :hw] + mod_ref[5:6, :hw] * jnp.dot(gmat, ylo, preferred_element_type=F32)
    o_ref[:, hw:] = x1_ref[:, hw:] + mod_ref[5:6, hw:] * jnp.dot(gmat, yhi, preferred_element_type=F32)


def _combine(tbl3, lp8, grow, x1, mod3, y, S, tm, NR):
    T, D = x1.shape
    nt = T // tm
    tiles_per_batch = S // tm
    nch_tbl = tbl3.shape[2]
    kern = functools.partial(_combine_kernel, tm=tm, NR=NR, nt=nt)
    return pl.pallas_call(
        kern,
        grid=(nt,),
        in_specs=[pl.BlockSpec((1, 1, nch_tbl), lambda i: (i, 0, 0), memory_space=pltpu.SMEM),
                  pl.BlockSpec((1, 1, nch_tbl), lambda i: (jnp.minimum(i + 1, nt - 1), 0, 0),
                               memory_space=pltpu.SMEM),
                  pl.BlockSpec((8, tm), lambda i: (0, i)),
                  pl.BlockSpec((tm, LANES), lambda i: (i, 0)),
                  pl.BlockSpec((tm, D), lambda i: (i, 0)),
                  pl.BlockSpec((None, 6, D), lambda i: (i // tiles_per_batch, 0, 0)),
                  pl.BlockSpec(memory_space=pl.ANY)],
        out_specs=pl.BlockSpec((tm, D), lambda i: (i, 0)),
        out_shape=jax.ShapeDtypeStruct((T, D), F32),
        scratch_shapes=[pltpu.VMEM((2, NR, D // 2), jnp.uint32),
                        pltpu.SemaphoreType.DMA((2,))],
        compiler_params=_cparams(("arbitrary",)),
        name="moe_combine",
    )(tbl3, tbl3, lp8, grow, x1, mod3, y)


def _tiles(S):
    t_attn = min(512, S // 4) if S >= 512 else S
    tm = min(512, t_attn)
    return dict(tm=tm, t_attn=t_attn, bm=1024)


def _moe_layout(cnt, T, tm, bm):
    nt, E = cnt.shape
    c8 = (cnt + ROW_CHUNK - 1) // ROW_CHUNK * ROW_CHUNK
    tile_end = jnp.cumsum(c8, axis=1)
    tile_off = tile_end - c8
    used = tile_end[:, -1]
    off = jnp.cumsum(c8, axis=0) - c8
    tot = jnp.sum(c8, axis=0)
    padded = (tot + bm - 1) // bm * bm
    pends = jnp.cumsum(padded)
    pstarts = pends - padded
    NR = TOP_K * tm + ROW_CHUNK * E
    nch_tbl = NR // ROW_CHUNK
    P = (T * TOP_K + nt * E * (ROW_CHUNK - 1) + E * bm + bm - 1) // bm * bm
    nblk = P // bm

    c_row = jnp.arange(nch_tbl, dtype=jnp.int32) * ROW_CHUNK
    inrun = (tile_off[:, :, None] <= c_row) & (c_row < tile_end[:, :, None])
    base = pstarts[None, :] + off - tile_off
    tbl = jnp.sum(jnp.where(inrun, base[:, :, None], 0), axis=1) + c_row[None, :]
    in_use = c_row[None, :] < used[:, None]
    spare_rows = NR - TOP_K * tm
    spare = P + jnp.arange(nt, dtype=jnp.int32)[:, None] * spare_rows + (c_row[None, :] - used[:, None])
    tbl_out = jnp.where(in_use, tbl, spare).astype(jnp.int32)
    tbl_in = jnp.where(in_use, tbl, 0).astype(jnp.int32)
    n_spare_blk = (nt * spare_rows + bm - 1) // bm

    nused = (pends[-1] // bm).astype(jnp.int32).reshape(1)
    blk_start = jnp.arange(nblk, dtype=jnp.int32) * bm
    blk_expert = jnp.minimum(jnp.sum(pends[None, :] <= blk_start[:, None], axis=1), E - 1).astype(jnp.int32)
    own = blk_expert[:, None] == jnp.arange(E, dtype=jnp.int32)[None, :]

    def of_expert(table):
        return jnp.sum(jnp.where(own, table[None, :], 0), axis=1)

    step = jnp.arange(nblk, dtype=jnp.int32)
    order = jnp.where(step >= nused[0], step,
                      jnp.where(step == of_expert(pstarts // bm), of_expert(pends // bm - 1), step - 1))
    order = order.astype(jnp.int32)
    half2 = (order * bm + bm // 2 < of_expert(pstarts + tot)).astype(jnp.int32)
    last_blk = jnp.where(padded > 0, pends // bm - 1, -1)
    n_tail = nblk - (T * TOP_K) // bm
    tail_blk = nused[0] + jnp.arange(n_tail, dtype=jnp.int32)
    tail_blk = jnp.where(tail_blk < nblk, tail_blk, -1)
    spare_blk = nblk + jnp.arange(n_spare_blk, dtype=jnp.int32)
    zero_blk = jnp.concatenate([last_blk, spare_blk, tail_blk]).astype(jnp.int32)
    return dict(tbl_out=tbl_out.reshape(nt, 1, nch_tbl), tbl_in=tbl_in.reshape(nt, 1, nch_tbl),
                nused=nused, blk_expert=blk_expert, half2=half2, order=order, zero_blk=zero_blk,
                nz_first=E + n_spare_blk,
                P=P, P_alloc=P + n_spare_blk * bm, NR=NR)


def kernel(x, c, w_ada, b_ada, g_mix, w_in, w_fourier, q_norm_g, k_norm_g, lambda_q1, lambda_k1,
           lambda_q2, lambda_k2, g_subln, w_out, rel_bias, g_ffn, w_router, b_router, w1, b1, w2, b2):
    B, S, D = x.shape
    T = B * S
    L = w_ada.shape[0]
    E = w_router.shape[-1]
    cfg = _tiles(S)
    tm, t_attn, bm = cfg["tm"], cfg["t_attn"], cfg["bm"]
    dqk = q_norm_g.shape[-1]
    scale = dqk ** -0.5
    fw = w_in.shape[-1] // 4

    btiles = _bias_tiles(rel_bias, t_attn)
    x2 = x.reshape(T, D)
    for l in range(L):
        lam_init = 0.8 - 0.6 * math.exp(-0.3 * l)
        mod3 = _ada(c, w_ada[l], b_ada[l]).reshape(B, 6, D)
        gq = (jnp.tile(q_norm_g[l], 4) * (scale * LOG2E)).reshape(1, 2 * LANES)
        gk = jnp.tile(k_norm_g[l], 4).reshape(1, 2 * LANES)
        uf, qx, kx, v = _in_proj(x2, mod3, g_mix[l].reshape(1, D), w_in[l].astype(BF16), gq, gk,
                                 S, tm, t_attn)
        yf = _fourier(uf.reshape(B, S, fw), w_fourier[l])
        lam4 = jnp.stack([lambda_q1[l], lambda_k1[l], lambda_q2[l], lambda_k2[l]]).astype(F32)
        yd = _attention(rel_bias, qx.reshape(B, S, -1), kx.reshape(B, S, -1), v.reshape(B, S, fw),
                        btiles, lam4, g_subln[l].reshape(1, LANES), t_attn, lam_init).reshape(T, fw)
        x1, hp, lp8, grow, cnt3 = _out_proj(
            yf, yd, x2, mod3, w_out[l].astype(BF16), g_ffn[l].reshape(1, D),
            w_router[l].T.astype(BF16), b_router[l].reshape(E, 1), S, tm)
        lay = _moe_layout(cnt3[:, :, 0], T, tm, bm)
        xs = _dispatch(lay["tbl_out"], lay["zero_blk"], lay["nz_first"], lp8, hp, lay["P_alloc"], bm, tm,
                       lay["NR"])
        y = _experts(lay["blk_expert"], lay["nused"], lay["half2"], lay["order"], xs, lay["P"], w1[l],
                     b1[l][:, None, 0::2],
                     b1[l][:, None, 1::2], w2[l], b2[l][:, None, :], bm)
        x2 = _combine(lay["tbl_in"], lp8, grow, x1, mod3, y, S, tm, lay["NR"])
    return x2.reshape(B, S, D)
```

```python
import functools
import math

import numpy as np
import jax
import jax.numpy as jnp
from jax import lax
from jax.experimental import pallas as pl
from jax.experimental.pallas import tpu as pltpu

F32 = jnp.float32
BF16 = jnp.bfloat16

EPS = 1e-6
N_FOURIER_GROUPS = 4
N_DIFF_HEADS = 4
TOP_K = 4
N_BUCKETS = 32
MAX_DISTANCE = 128
SWIGLU_ALPHA = 1.702
SWIGLU_LIMIT = 7.0
LANES = 128
DFT_RADIX = 4
LOG2E = 1.4426950408889634
VMEM_LIMIT = 56 * 1024 * 1024


def _cparams(sem):
    return pltpu.CompilerParams(dimension_semantics=sem, vmem_limit_bytes=VMEM_LIMIT)


def _ada_kernel(c_ref, w_ref, b_ref, o_ref):
    c = c_ref[...]
    cond = c * jax.nn.sigmoid(c)
    o_ref[...] = jnp.dot(cond.astype(BF16), w_ref[...].astype(BF16),
                         preferred_element_type=F32) + b_ref[...]


def _ada(c, w, b):
    B, D = c.shape
    N = w.shape[1]
    tn = 1536 if N % 1536 == 0 else N
    return pl.pallas_call(
        _ada_kernel,
        grid=(N // tn,),
        in_specs=[pl.BlockSpec((B, D), lambda j: (0, 0)),
                  pl.BlockSpec((D, tn), lambda j: (0, j)),
                  pl.BlockSpec((1, tn), lambda j: (0, j))],
        out_specs=pl.BlockSpec((B, tn), lambda j: (0, j)),
        out_shape=jax.ShapeDtypeStruct((B, N), F32),
        compiler_params=_cparams(("arbitrary",)),
        name="ada",
    )(c, w, b.reshape(1, N))


def _group_rms_inv(xh, gmat, group):
    ss = jnp.dot((xh * xh).astype(BF16), gmat, preferred_element_type=F32)
    return lax.rsqrt(ss * (1.0 / group) + EPS)


def _in_proj_kernel(x_ref, mod_ref, g_ref, w_ref, gq_ref, gk_ref, gmat_ref,
                    uf_ref, q_ref, k_ref, v_ref, *, tiles_per_batch, tm, t_attn, n_chunks, fw, qw):
    i = pl.program_id(0)
    x = x_ref[...]
    ms = jnp.mean(x * x, axis=-1, keepdims=True)
    y = x * lax.rsqrt(ms + EPS) * g_ref[...]
    h = y * (1.0 + mod_ref[1:2, :]) + mod_ref[0:1, :]
    proj = jnp.dot(h.astype(BF16), w_ref[...], preferred_element_type=F32)
    uf_ref[...] = proj[:, :fw].astype(BF16)
    v_ref[...] = proj[:, fw + 2 * qw:].astype(BF16)

    lane = lax.broadcasted_iota(jnp.int32, (1, LANES), 1)
    chunk = ((i % tiles_per_batch) * tm) // t_attn
    gmat = gmat_ref[...]
    half = LANES // 2
    pair = 2 * LANES
    for hd in range(N_DIFF_HEADS):
        if hd % 2 == 0:
            q2 = proj[:, fw + hd * LANES: fw + hd * LANES + pair]
            k2 = proj[:, fw + qw + hd * LANES: fw + qw + hd * LANES + pair]
            qn2 = (q2 * _group_rms_inv(q2, gmat, half) * gq_ref[...]).astype(BF16)
            kn2 = (k2 * _group_rms_inv(k2, gmat, half) * gk_ref[...]).astype(BF16)
        qn = qn2[:, (hd % 2) * LANES:(hd % 2 + 1) * LANES]
        kn = kn2[:, (hd % 2) * LANES:(hd % 2 + 1) * LANES]
        for m in range(2):
            fl = lane - half if m == 0 else lane
            is_data = (lane < half) if m == 0 else (lane >= half)
            f = jnp.where(fl < n_chunks, fl, fl - n_chunks)
            onehot = ((fl >= 0) & (fl < 2 * n_chunks) & (f == chunk)).astype(BF16)
            c0 = (2 * hd + m) * LANES
            q_ref[:, c0:c0 + LANES] = jnp.where(is_data, qn, jnp.zeros_like(qn))
            k_ref[:, c0:c0 + LANES] = jnp.where(is_data, kn, onehot)


def _in_proj(x2, mod3, g_mix, w_in_b, gq, gk, S, tm, t_attn):
    T, D = x2.shape
    ncols = w_in_b.shape[1]
    fw = ncols // 4
    qw = fw
    tiles_per_batch = S // tm
    n_chunks = S // t_attn
    half = LANES // 2
    pair = 2 * LANES
    gmat = (np.arange(pair)[:, None] // half == np.arange(pair)[None, :] // half)
    gmat = jnp.asarray(gmat, BF16)
    kern = functools.partial(_in_proj_kernel, tiles_per_batch=tiles_per_batch, tm=tm, t_attn=t_attn,
                             n_chunks=n_chunks, fw=fw, qw=qw)
    ext = 2 * qw
    return pl.pallas_call(
        kern,
        grid=(T // tm,),
        in_specs=[pl.BlockSpec((tm, D), lambda i: (i, 0)),
                  pl.BlockSpec((None, 6, D), lambda i: (i // tiles_per_batch, 0, 0)),
                  pl.BlockSpec((1, D), lambda i: (0, 0)),
                  pl.BlockSpec((D, ncols), lambda i: (0, 0)),
                  pl.BlockSpec((1, pair), lambda i: (0, 0)),
                  pl.BlockSpec((1, pair), lambda i: (0, 0)),
                  pl.BlockSpec((pair, pair), lambda i: (0, 0))],
        out_specs=[pl.BlockSpec((tm, fw), lambda i: (i, 0)),
                   pl.BlockSpec((tm, ext), lambda i: (i, 0)),
                   pl.BlockSpec((tm, ext), lambda i: (i, 0)),
                   pl.BlockSpec((tm, fw), lambda i: (i, 0))],
        out_shape=[jax.ShapeDtypeStruct((T, fw), BF16),
                   jax.ShapeDtypeStruct((T, ext), BF16),
                   jax.ShapeDtypeStruct((T, ext), BF16),
                   jax.ShapeDtypeStruct((T, fw), BF16)],
        compiler_params=_cparams(("arbitrary",)),
        name="in_proj",
    )(x2, mod3, g_mix, w_in_b, gq, gk, gmat)


def _fourier_kernel(uf_ref, wcs_ref, tw_ref, dft_ref, o_ref, z_sc, tt_sc, *, M, gw):
    c = pl.program_id(1)

    def channel_stage():
        for g in range(N_FOURIER_GROUPS):
            for j in range(DFT_RADIX):
                ab = jnp.dot(uf_ref[j * M:(j + 1) * M, g * gw:(g + 1) * gw], wcs_ref[g],
                             preferred_element_type=F32)
                z_sc[0, j, :, g * gw:(g + 1) * gw] = ab[:, :gw]
                z_sc[1, j, :, g * gw:(g + 1) * gw] = -ab[:, gw:]

    def butterfly(k2):
        cs = tw_ref[:, k2:k2 + 1]
        sn = tw_ref[:, DFT_RADIX + k2:DFT_RADIX + k2 + 1]
        for g in range(N_FOURIER_GROUPS):
            cols = slice(g * gw, (g + 1) * gw)
            zr = [z_sc[0, j, :, cols] for j in range(DFT_RADIX)]
            zi = [z_sc[1, j, :, cols] for j in range(DFT_RADIX)]
            if k2 == 0:
                yr = zr[0] + zr[1] + zr[2] + zr[3]
                yi = zi[0] + zi[1] + zi[2] + zi[3]
            elif k2 == 1:
                yr = zr[0] + zi[1] - zr[2] - zi[3]
                yi = zi[0] - zr[1] - zi[2] + zr[3]
            elif k2 == 2:
                yr = zr[0] - zr[1] + zr[2] - zr[3]
                yi = zi[0] - zi[1] + zi[2] - zi[3]
            else:
                yr = zr[0] - zi[1] - zr[2] + zi[3]
                yi = zi[0] + zr[1] - zi[2] - zr[3]
            tt_sc[k2, 0:M, cols] = (cs * yr + sn * yi).astype(BF16)
            tt_sc[k2, M:2 * M, cols] = (cs * yi - sn * yr).astype(BF16)

    for cc in range(DFT_RADIX):
        @pl.when(c == cc)
        def _(cc=cc):
            if cc == 0:
                channel_stage()
                butterfly(0)
            o_ref[...] = jnp.dot(dft_ref[...], tt_sc[cc], preferred_element_type=F32).astype(BF16)
            if cc + 1 < DFT_RADIX:
                butterfly(cc + 1)


def _fourier(uf3, w_f):
    B, S, W = uf3.shape
    G = N_FOURIER_GROUPS
    gw = W // G
    M = S // DFT_RADIX
    norm = 1.0 / math.sqrt(S * gw)
    cidx = (np.arange(gw)[:, None] * np.arange(gw)[None, :]) % gw
    ang_c = cidx * (2.0 * math.pi / gw)
    cc = jnp.asarray(np.cos(ang_c) * norm, F32)
    sc = jnp.asarray(np.sin(ang_c) * norm, F32)
    hp = lax.Precision.HIGHEST
    wcs = jnp.concatenate([jnp.einsum("ab,gbd->gad", cc, w_f, precision=hp),
                           jnp.einsum("ab,gbd->gad", sc, w_f, precision=hp)], axis=-1).astype(BF16)
    n1 = np.arange(M)
    ang_t = ((np.arange(DFT_RADIX)[None, :] * n1[:, None]) % S) * (2.0 * math.pi / S)
    tw = jnp.asarray(np.concatenate([np.cos(ang_t), np.sin(ang_t),
                                     np.zeros((M, LANES - 2 * DFT_RADIX))], axis=1), F32)
    ang_p = ((n1[:, None] * n1[None, :]) % M) * (2.0 * math.pi / M)
    dft = jnp.asarray(np.concatenate([np.cos(ang_p), np.sin(ang_p)], axis=1).astype(np.float32)
                      .astype(jnp.bfloat16))
    kern = functools.partial(_fourier_kernel, M=M, gw=gw)
    return pl.pallas_call(
        kern,
        grid=(B, DFT_RADIX),
        in_specs=[pl.BlockSpec((None, S, W), lambda b, c: (b, 0, 0)),
                  pl.BlockSpec((G, gw, 2 * gw), lambda b, c: (0, 0, 0)),
                  pl.BlockSpec((M, LANES), lambda b, c: (0, 0)),
                  pl.BlockSpec((M, 2 * M), lambda b, c: (0, 0))],
        out_specs=pl.BlockSpec((None, M, W), lambda b, c: (b, 0, c)),
        out_shape=jax.ShapeDtypeStruct((B, M, DFT_RADIX * W), BF16),
        scratch_shapes=[pltpu.VMEM((2, DFT_RADIX, M, W), F32),
                        pltpu.VMEM((DFT_RADIX, 2 * M, W), BF16)],
        compiler_params=_cparams(("arbitrary", "arbitrary")),
        name="fourier",
    )(uf3, wcs, tw, dft)


def _bias_kernel(rb_ref, o_ref, *, t):
    h = pl.program_id(0)
    which = pl.program_id(1)
    nb = N_BUCKETS // 2

    def tile(d, buckets, r0=0, c0=0, nr=t, nc=t):
        row = r0 + lax.broadcasted_iota(jnp.int32, (nr, nc), 0)
        col = c0 + lax.broadcasted_iota(jnp.int32, (nr, nc), 1)
        rel = d * t + col - row
        max_exact = nb // 2
        ret = jnp.where(rel > 0, nb, 0)
        n = jnp.abs(rel)
        nf = jnp.maximum(n, 1).astype(F32)
        large = max_exact + (jnp.log(nf / max_exact) / math.log(MAX_DISTANCE / max_exact)
                             * (nb - max_exact)).astype(jnp.int32)
        large = jnp.minimum(large, nb - 1)
        bucket = ret + jnp.where(n < max_exact, n, large)
        val = jnp.zeros((nr, nc), F32)
        for j in buckets:
            val = jnp.where(bucket == j, rb_ref[j * N_DIFF_HEADS + h], val)
        return val * LOG2E

    left_b, right_b = _far_bucket_consts()
    corner = MAX_DISTANCE

    @pl.when(which == 0)
    def _():
        o_ref[...] = jnp.full((t, t), rb_ref[left_b * N_DIFF_HEADS + h] * LOG2E, F32)
        o_ref[0:corner, t - corner:t] = tile(-1, range(0, nb), 0, t - corner, corner, corner)

    @pl.when(which == 1)
    def _():
        o_ref[...] = tile(0, range(0, 2 * nb))

    @pl.when(which == 2)
    def _():
        o_ref[...] = jnp.full((t, t), rb_ref[right_b * N_DIFF_HEADS + h] * LOG2E, F32)
        o_ref[t - corner:t, 0:corner] = tile(1, range(nb, 2 * nb), t - corner, 0, corner, corner)

    @pl.when(which == 3)
    def _():
        o_ref[...] = jnp.zeros((t, t), F32)


def _bias_tiles(rel_bias, t):
    H = N_DIFF_HEADS
    assert t >= MAX_DISTANCE, "far key tiles must lie entirely in the saturated bucket"
    return pl.pallas_call(
        functools.partial(_bias_kernel, t=t),
        grid=(H, 4),
        in_specs=[pl.BlockSpec(memory_space=pltpu.SMEM)],
        out_specs=pl.BlockSpec((None, None, t, t), lambda h, d: (h, d, 0, 0)),
        out_shape=jax.ShapeDtypeStruct((H, 4, t, t), F32),
        compiler_params=_cparams(("arbitrary", "arbitrary")),
        name="t5_bias",
    )(rel_bias.reshape(-1))


def _far_bucket_consts():
    nb = N_BUCKETS // 2
    return nb - 1, 2 * nb - 1


def _attn_kernel(rb_ref, q_ref, k_ref, v_ref, b_ref, lam_ref, gs_ref, o_ref,
                 s_sc, p_sc, vext_sc, m_sc, raw_sc, *, t, n, n_tiles, lam_init):
    g = pl.program_id(0)
    cur = jnp.minimum(g, n_tiles - 1)
    h = (cur // n) % N_DIFF_HEADS
    i = cur % n
    half = LANES // 2

    @pl.when(g == 0)
    def _():
        raw_sc[...] = jnp.ones(raw_sc.shape, F32)

    @pl.when((i == 0) & (g < n_tiles))
    def _():
        vext_sc[:, :LANES] = v_ref[...]
        vext_sc[:, LANES:] = jnp.ones(v_ref.shape, BF16)

    left_b, right_b = _far_bucket_consts()
    c_left = rb_ref[left_b * N_DIFF_HEADS + h] * LOG2E
    c_right = rb_ref[right_b * N_DIFF_HEADS + h] * LOG2E
    lane = lax.broadcasted_iota(jnp.int32, (1, LANES), 1)

    def all_scores(m):
        fl = lane - half if m == 0 else lane
        is_data = (lane < half) if m == 0 else (lane >= half)
        f = jnp.where(fl < n, fl, fl - n)
        cfar = jnp.where(f < i - 1, c_left, jnp.where(f > i + 1, c_right, 0.0))
        c_hi = cfar.astype(BF16)
        c_lo = (cfar - c_hi.astype(F32)).astype(BF16)
        feat = jnp.where(fl < n, c_hi, c_lo)
        feat = jnp.where((fl >= 0) & (fl < 2 * n), feat, jnp.zeros_like(feat))
        qm = jnp.where(is_data, q_ref[:, m * LANES:(m + 1) * LANES], feat)
        for j in range(n):
            kc = k_ref[j * t:(j + 1) * t, m * LANES:(m + 1) * LANES]
            s = lax.dot_general(qm, kc, (((1,), (1,)), ((), ())), preferred_element_type=F32)
            d = j - i
            s = s + b_ref[jnp.where(jnp.abs(d) <= 1, d + 1, 3)]
            s_sc[m, j] = s
            tile_max = s[:, 0:LANES]
            for c in range(1, t // LANES):
                tile_max = jnp.maximum(tile_max, s[:, c * LANES:(c + 1) * LANES])
            if j == 0:
                m_sc[m] = tile_max
            else:
                m_sc[m] = jnp.maximum(m_sc[m], tile_max)

    def exponentials(m):
        mx = jnp.max(m_sc[m], axis=-1, keepdims=True)
        for j in range(n):
            p_sc[m, :, j * t:(j + 1) * t] = jnp.exp2(s_sc[m, j] - mx).astype(BF16)

    def finish_previous():
        r0 = raw_sc[0]
        r1 = raw_sc[1]
        lam = (jnp.exp(jnp.sum(lam_ref[0:1, :] * lam_ref[1:2, :], axis=-1, keepdims=True))
               - jnp.exp(jnp.sum(lam_ref[2:3, :] * lam_ref[3:4, :], axis=-1, keepdims=True)) + lam_init)
        o = r0[:, :LANES] / r0[:, LANES:LANES + 1] - lam * (r1[:, :LANES] / r1[:, LANES:LANES + 1])
        ms = jnp.mean(o * o, axis=-1, keepdims=True)
        o_ref[...] = (o * lax.rsqrt(ms + EPS) * gs_ref[...] * (1.0 - lam_init)).astype(BF16)

    @pl.when(i < n)
    def _():
        finish_previous()
        all_scores(0)

    @pl.when(g < n_tiles)
    def _():
        all_scores(1)
        exponentials(0)
        raw_sc[0] = jnp.dot(p_sc[0], vext_sc[...], preferred_element_type=F32)
        exponentials(1)
        raw_sc[1] = jnp.dot(p_sc[1], vext_sc[...], preferred_element_type=F32)


def _attention(rel_bias, q3, k3, v3, btiles, lam4, g_subln, t, lam_init):
    B, S, _ = q3.shape
    H = N_DIFF_HEADS
    n = S // t
    n_tiles = B * H * n
    kern = functools.partial(_attn_kernel, t=t, n=n, n_tiles=n_tiles, lam_init=lam_init)

    def tile_of(g):
        return g // (H * n), (g // n) % H, g % n

    def cur_map(fn):
        def index_map(g):
            return fn(*tile_of(jnp.minimum(g, n_tiles - 1)))
        return index_map

    def prev_map(g):
        b, h, i = tile_of(jnp.maximum(g - 1, 0))
        return (b, i, h)

    return pl.pallas_call(
        kern,
        grid=(n_tiles + 1,),
        in_specs=[pl.BlockSpec(memory_space=pltpu.SMEM),
                  pl.BlockSpec((None, t, 2 * LANES), cur_map(lambda b, h, i: (b, i, h))),
                  pl.BlockSpec((None, S, 2 * LANES), cur_map(lambda b, h, i: (b, 0, h))),
                  pl.BlockSpec((None, S, LANES), cur_map(lambda b, h, i: (b, 0, h))),
                  pl.BlockSpec((None, 4, t, t), cur_map(lambda b, h, i: (h, 0, 0, 0))),
                  pl.BlockSpec((4, LANES // 2), lambda g: (0, 0)),
                  pl.BlockSpec((1, LANES), lambda g: (0, 0))],
        out_specs=pl.BlockSpec((None, t, LANES), prev_map),
        out_shape=jax.ShapeDtypeStruct((B, S, H * LANES), BF16),
        scratch_shapes=[pltpu.VMEM((2, n, t, t), F32),
                        pltpu.VMEM((2, t, S), BF16),
                        pltpu.VMEM((S, 2 * LANES), BF16),
                        pltpu.VMEM((2, t, LANES), F32),
                        pltpu.VMEM((2, t, 2 * LANES), F32)],
        compiler_params=_cparams(("arbitrary",)),
        name="diff_attn",
    )(rel_bias.reshape(-1), q3, k3, v3, btiles, lam4, g_subln)


def _pack_bf16_pairs(v):
    bits = lax.bitcast_convert_type(v.astype(BF16).astype(F32), jnp.uint32)
    n = bits.shape[1] // 2
    return (bits[:, :n] >> 16) | (bits[:, n:] & jnp.uint32(0xFFFF0000))


def _unpack_bf16_pairs(p):
    lo = lax.bitcast_convert_type(p << 16, F32).astype(BF16)
    hi = lax.bitcast_convert_type(p & jnp.uint32(0xFFFF0000), F32).astype(BF16)
    return lo, hi


def _out_proj_kernel(yf_ref, perm_ref, yd_ref, x_ref, mod_ref, wo_ref, g_ref, wr_ref, br_ref,
                     x1_ref, hp_ref, lp_ref, grow_ref, cnt_ref, *, tm, fw, E):
    yf_stack = jnp.concatenate([yf_ref[:, r * fw:(r + 1) * fw] for r in range(DFT_RADIX)], axis=0)
    yf = jnp.dot(perm_ref[...], yf_stack, preferred_element_type=F32).astype(BF16)
    mix = (jnp.dot(yf, wo_ref[:fw, :], preferred_element_type=F32)
           + jnp.dot(yd_ref[...], wo_ref[fw:, :], preferred_element_type=F32))
    x1 = x_ref[...] + mod_ref[2:3, :] * mix
    x1_ref[...] = x1
    ms = jnp.mean(x1 * x1, axis=-1, keepdims=True)
    h = x1 * lax.rsqrt(ms + EPS) * g_ref[...]
    h = h * (1.0 + mod_ref[4:5, :]) + mod_ref[3:4, :]
    hb = h.astype(BF16)
    hp_ref[...] = hb

    logits = lax.dot_general(wr_ref[...], hb, (((1,), (1,)), ((), ())),
                             preferred_element_type=F32) + br_ref[...]
    rowid = lax.broadcasted_iota(jnp.int32, (E, tm), 0)
    vals, idxs = [], []
    comb = jnp.zeros((E, tm), F32)
    l = logits
    for _ in range(TOP_K):
        mv = jnp.max(l, axis=0, keepdims=True)
        ix = jnp.min(jnp.where(l == mv, rowid, E), axis=0, keepdims=True)
        sel = rowid == ix
        vals.append(mv)
        idxs.append(ix)
        comb = comb + sel.astype(F32)
        l = jnp.where(sel, -jnp.inf, l)
    es = [jnp.exp(v - vals[0]) for v in vals]
    den = es[0] + es[1] + es[2] + es[3]
    gates = [e / den for e in es]

    r_i = lax.broadcasted_iota(jnp.int32, (tm, tm), 0)
    c_i = lax.broadcasted_iota(jnp.int32, (tm, tm), 1)
    upper = (r_i < c_i).astype(BF16)
    before = jnp.dot(comb.astype(BF16), upper, preferred_element_type=F32)
    cnt = jnp.sum(comb, axis=1, keepdims=True)
    cnt_ref[...] = jnp.broadcast_to(cnt, cnt_ref.shape).astype(jnp.int32)
    chunks = jnp.floor((cnt + (ROW_CHUNK - 1)) * (1.0 / ROW_CHUNK))
    e_r = lax.broadcasted_iota(jnp.int32, (E, E), 0)
    e_c = lax.broadcasted_iota(jnp.int32, (E, E), 1)
    run_start = ROW_CHUNK * jnp.dot((e_c < e_r).astype(BF16),
                                    jnp.broadcast_to(chunks, (E, LANES)).astype(BF16),
                                    preferred_element_type=F32)[:, 0:1]
    ranks = [jnp.sum(jnp.where(rowid == ix, before + run_start, 0.0), axis=0, keepdims=True) for ix in idxs]

    pad_i = jnp.full((8 - TOP_K, tm), -1, jnp.int32)
    lp_ref[...] = jnp.concatenate([r.astype(jnp.int32) for r in ranks] + [pad_i], axis=0)
    gpad = jnp.concatenate(gates + [jnp.zeros((LANES - TOP_K, tm), F32)], axis=0)
    grow_ref[...] = gpad.T


def _out_proj(yf3, yd2, x2, mod3, w_out_b, g_ffn, wr_t, br_t, S, tm):
    T, D = x2.shape
    fw = yd2.shape[1]
    E = wr_t.shape[0]
    nt = T // tm
    tiles_per_batch = S // tm
    q = tm // DFT_RADIX
    col = np.arange(tm)
    perm = jnp.asarray(np.arange(tm)[:, None] == DFT_RADIX * (col % q) + col // q, BF16)
    kern = functools.partial(_out_proj_kernel, tm=tm, fw=fw, E=E)
    return pl.pallas_call(
        kern,
        grid=(nt,),
        in_specs=[pl.BlockSpec((None, q, DFT_RADIX * fw), lambda i: (i // tiles_per_batch, i % tiles_per_batch, 0)),
                  pl.BlockSpec((tm, tm), lambda i: (0, 0)),
                  pl.BlockSpec((tm, fw), lambda i: (i, 0)),
                  pl.BlockSpec((tm, D), lambda i: (i, 0)),
                  pl.BlockSpec((None, 6, D), lambda i: (i // tiles_per_batch, 0, 0)),
                  pl.BlockSpec((D, D), lambda i: (0, 0)),
                  pl.BlockSpec((1, D), lambda i: (0, 0)),
                  pl.BlockSpec((E, D), lambda i: (0, 0)),
                  pl.BlockSpec((E, 1), lambda i: (0, 0))],
        out_specs=[pl.BlockSpec((tm, D), lambda i: (i, 0)),
                   pl.BlockSpec((tm, D), lambda i: (i, 0)),
                   pl.BlockSpec((8, tm), lambda i: (0, i)),
                   pl.BlockSpec((tm, LANES), lambda i: (i, 0)),
                   pl.BlockSpec((None, E, LANES), lambda i: (i, 0, 0))],
        out_shape=[jax.ShapeDtypeStruct((T, D), F32),
                   jax.ShapeDtypeStruct((T, D), BF16),
                   jax.ShapeDtypeStruct((8, T), jnp.int32),
                   jax.ShapeDtypeStruct((T, LANES), F32),
                   jax.ShapeDtypeStruct((nt, E, LANES), jnp.int32)],
        compiler_params=_cparams(("arbitrary",)),
        name="out_proj_router",
    )(yf3, perm, yd2, x2, mod3, w_out_b, g_ffn, wr_t, br_t)


ROW_CHUNK = 8
CHUNK_UNROLL = 8


def _dispatch_kernel(tbl_ref, zb_ref, lp_ref, hp_ref, xs_ref, buf, zero_sc, sem, zsem, tsem,
                     *, tm, bm, NR, nz, nz_first, nt):
    i = pl.program_id(0)
    s = i % 2

    def zero_copy(e, zs):
        return pltpu.make_async_copy(zero_sc, xs_ref.at[pl.ds(zb_ref[e] * bm, bm)], zs)

    @pl.when(i == 0)
    def _():
        zero_sc[...] = jnp.zeros(zero_sc.shape, jnp.uint32)
        for e in range(nz):
            @pl.when(zb_ref[e] >= 0)
            def _():
                zero_copy(e, zsem if e < nz_first else tsem).start()
        for e in range(nz_first):
            @pl.when(zb_ref[e] >= 0)
            def _():
                zero_copy(e, zsem).wait()

    @pl.when(i == nt - 1)
    def _():
        for e in range(nz_first, nz):
            @pl.when(zb_ref[e] >= 0)
            def _():
                zero_copy(e, tsem).wait()

    lp = lp_ref[...].astype(jnp.int16)
    rows = lax.broadcasted_iota(jnp.int16, (NR, tm), 0)
    hit = rows == lp[0:1, :]
    for k in range(1, TOP_K):
        hit = hit | (rows == lp[k:k + 1, :])
    sel = jnp.where(hit, jnp.ones((), BF16), jnp.zeros((), BF16))
    hw = hp_ref.shape[1] // 2
    lo = hp_ref[:, :hw]
    hi = hp_ref[:, hw:]
    slo = jnp.dot(sel, lo, preferred_element_type=F32)
    shi = jnp.dot(sel, hi, preferred_element_type=F32)
    buf[s] = ((lax.bitcast_convert_type(slo, jnp.uint32) >> 16)
              | (lax.bitcast_convert_type(shi, jnp.uint32) & jnp.uint32(0xFFFF0000)))

    for slot in range(2):
        def issue(grp, carry, slot=slot):
            for j in range(CHUNK_UNROLL):
                c = grp * CHUNK_UNROLL + j
                d = pl.multiple_of(tbl_ref[0, 0, c], ROW_CHUNK)
                r = pl.multiple_of(c * ROW_CHUNK, ROW_CHUNK)
                pltpu.make_async_copy(buf.at[slot, pl.ds(r, ROW_CHUNK)], xs_ref.at[pl.ds(d, ROW_CHUNK)],
                                      sem.at[slot]).start(priority=j % 2)
            return carry

        @pl.when(s == slot)
        def _():
            lax.fori_loop(0, NR // (ROW_CHUNK * CHUNK_UNROLL), issue, 0)

    def wait_slot(slot):
        pltpu.make_async_copy(buf.at[slot], xs_ref.at[pl.ds(0, NR)], sem.at[slot]).wait()

    @pl.when(i > 0)
    def _():
        wait_slot(1 - s)

    @pl.when(i == nt - 1)
    def _():
        wait_slot(s)


def _dispatch(tbl3, zero_blk, nz_first, lp8, hp, P, bm, tm, NR):
    T, D = hp.shape
    Wd = D // 2
    nt = T // tm
    nz = zero_blk.shape[0]
    nch_tbl = tbl3.shape[2]
    kern = functools.partial(_dispatch_kernel, tm=tm, bm=bm, NR=NR, nz=nz, nz_first=nz_first, nt=nt)
    return pl.pallas_call(
        kern,
        grid=(nt,),
        in_specs=[pl.BlockSpec((1, 1, nch_tbl), lambda i: (i, 0, 0), memory_space=pltpu.SMEM),
                  pl.BlockSpec(memory_space=pltpu.SMEM),
                  pl.BlockSpec((8, tm), lambda i: (0, i)),
                  pl.BlockSpec((tm, D), lambda i: (i, 0))],
        out_specs=pl.BlockSpec(memory_space=pl.ANY),
        out_shape=jax.ShapeDtypeStruct((P, Wd), jnp.uint32),
        scratch_shapes=[pltpu.VMEM((2, NR, Wd), jnp.uint32),
                        pltpu.VMEM((bm, Wd), jnp.uint32),
                        pltpu.SemaphoreType.DMA((2,)),
                        pltpu.SemaphoreType.DMA,
                        pltpu.SemaphoreType.DMA],
        compiler_params=_cparams(("arbitrary",)),
        name="moe_dispatch",
    )(tbl3, zero_blk, lp8, hp)


def _expert_kernel(be_ref, nu_ref, h2_ref, ord_ref, x_ref, w1_ref, b1g_ref, b1l_ref, w2_ref, b2_ref, pm_ref,
                   o_ref, w1g_sc, w1l_sc, w2_sc):
    blk = pl.program_id(0)
    nused = nu_ref[0]
    e = be_ref[blk]
    prev = be_ref[jnp.maximum(blk - 1, 0)]
    active = blk < nused
    second = active & (h2_ref[blk] != 0)
    half = x_ref.shape[0] // 2
    tile = 2 * LANES
    n_tiles = w1_ref.shape[1] // tile

    @pl.when(active & ((blk == 0) | (e != prev)))
    def _():
        for c in range(n_tiles):
            wt = w1_ref[:, c * tile:(c + 1) * tile].astype(BF16)
            pw = jnp.dot(wt, pm_ref[...], preferred_element_type=F32)
            w1g_sc[:, c * LANES:(c + 1) * LANES] = pw[:, :LANES].astype(BF16)
            w1l_sc[:, c * LANES:(c + 1) * LANES] = pw[:, LANES:].astype(BF16)
        w2_sc[...] = w2_ref[...].astype(BF16)

    @pl.when(jnp.logical_not(active))
    def _():
        o_ref[...] = jnp.zeros(o_ref.shape, jnp.uint32)

    def mlp(r0):
        xp = x_ref[r0:r0 + half, :]
        hw = xp.shape[1]
        xlo, xhi = _unpack_bf16_pairs(xp)
        zg = (jnp.dot(xlo, w1g_sc[:hw, :], preferred_element_type=F32)
              + jnp.dot(xhi, w1g_sc[hw:, :], preferred_element_type=F32) + b1g_ref[...])
        zl = (jnp.dot(xlo, w1l_sc[:hw, :], preferred_element_type=F32)
              + jnp.dot(xhi, w1l_sc[hw:, :], preferred_element_type=F32) + b1l_ref[...])
        g = jnp.minimum(zg, SWIGLU_LIMIT)
        lin = jnp.clip(zl, -SWIGLU_LIMIT, SWIGLU_LIMIT)
        act = g * jax.nn.sigmoid(SWIGLU_ALPHA * g) * (lin + 1.0)
        y = jnp.dot(act.astype(BF16), w2_sc[...], preferred_element_type=F32) + b2_ref[...]
        o_ref[r0:r0 + half, :] = _pack_bf16_pairs(y)

    @pl.when(second)
    def _():
        mlp(0)
        mlp(half)

    @pl.when(active & jnp.logical_not(second))
    def _():
        mlp(0)
        o_ref[half:, :] = jnp.zeros((half, o_ref.shape[1]), jnp.uint32)


def _experts(blk_expert, nused, half2, order, xs, P, w1, b1g, b1l, w2, b2, bm):
    Wd = xs.shape[1]
    E, D, F2 = w1.shape
    F = F2 // 2
    nblk = P // bm
    tile = 2 * LANES
    pm = np.zeros((tile, tile), np.float32)
    pm[2 * np.arange(LANES), np.arange(LANES)] = 1.0
    pm[2 * np.arange(LANES) + 1, LANES + np.arange(LANES)] = 1.0
    pm = jnp.asarray(pm, BF16)

    def row_map(b, be, nu, h2, od):
        return (jnp.minimum(od[b], nu[0] - 1), 0)

    def exp_map(b, be, nu, h2, od):
        return (be[b], 0, 0)

    grid_spec = pltpu.PrefetchScalarGridSpec(
        num_scalar_prefetch=4,
        grid=(nblk,),
        in_specs=[pl.BlockSpec((bm, Wd), row_map),
                  pl.BlockSpec((None, D, F2), exp_map),
                  pl.BlockSpec((None, 1, F), exp_map),
                  pl.BlockSpec((None, 1, F), exp_map),
                  pl.BlockSpec((None, F, D), exp_map),
                  pl.BlockSpec((None, 1, D), exp_map),
                  pl.BlockSpec((tile, tile), lambda b, be, nu, h2, od: (0, 0))],
        out_specs=pl.BlockSpec((bm, D // 2), lambda b, be, nu, h2, od: (od[b], 0)),
        scratch_shapes=[pltpu.VMEM((D, F), BF16),
                        pltpu.VMEM((D, F), BF16),
                        pltpu.VMEM((F, D), BF16)],
    )
    return pl.pallas_call(
        _expert_kernel,
        grid_spec=grid_spec,
        out_shape=jax.ShapeDtypeStruct((P, D // 2), jnp.uint32),
        compiler_params=_cparams(("arbitrary",)),
        name="moe_experts",
    )(blk_expert, nused, half2, order, xs, w1, b1g, b1l, w2, b2, pm)


def _combine_kernel(tcur_ref, tnxt_ref, lp_ref, grow_ref, x1_ref, mod_ref, y_ref, o_ref,
                    buf, sem, *, tm, NR, nt):
    i = pl.program_id(0)
    s = i % 2

    def issue(tref, slot):
        def body(grp, carry):
            for j in range(CHUNK_UNROLL):
                c = grp * CHUNK_UNROLL + j
                d = pl.multiple_of(tref[0, 0, c], ROW_CHUNK)
                r = pl.multiple_of(c * ROW_CHUNK, ROW_CHUNK)
                pltpu.make_async_copy(y_ref.at[pl.ds(d, ROW_CHUNK)], buf.at[slot, pl.ds(r, ROW_CHUNK)],
                                      sem.at[slot]).start(priority=j % 2)
            return carry
        lax.fori_loop(0, NR // (ROW_CHUNK * CHUNK_UNROLL), body, 0)

    @pl.when(i == 0)
    def _():
        issue(tcur_ref, 0)

    for slot in range(2):
        @pl.when((i + 1 < nt) & (s != slot))
        def _():
            issue(tnxt_ref, slot)

    pltpu.make_async_copy(y_ref.at[pl.ds(0, NR)], buf.at[s], sem.at[s]).wait()

    ylo, yhi = _unpack_bf16_pairs(buf[s])
    lpf = jnp.concatenate([lp_ref[...].astype(F32), jnp.zeros((LANES - 8, tm), F32)], axis=0)
    lpr = lpf.T.astype(jnp.int16)
    gr = grow_ref[...].astype(BF16)
    cols = lax.broadcasted_iota(jnp.int16, (tm, NR), 1)
    gmat = jnp.zeros((tm, NR), BF16)
    for k in range(TOP_K):
        gmat = jnp.where(cols == lpr[:, k:k + 1], gr[:, k:k + 1], gmat)
    hw = ylo.shape[1]
    o_ref[:, :hw] = x1_ref[:, :hw] + mod_ref[5:6, :hw] * jnp.dot(gmat, ylo, preferred_element_type=F32)
    o_ref[:, hw:] = x1_ref[:, hw:] + mod_ref[5:6, hw:] * jnp.dot(gmat, yhi, preferred_element_type=F32)


def _combine(tbl3, lp8, grow, x1, mod3, y, S, tm, NR):
    T, D = x1.shape
    nt = T // tm
    tiles_per_batch = S // tm
    nch_tbl = tbl3.shape[2]
    kern = functools.partial(_combine_kernel, tm=tm, NR=NR, nt=nt)
    return pl.pallas_call(
        kern,
        grid=(nt,),
        in_specs=[pl.BlockSpec((1, 1, nch_tbl), lambda i: (i, 0, 0), memory_space=pltpu.SMEM),
                  pl.BlockSpec((1, 1, nch_tbl), lambda i: (jnp.minimum(i + 1, nt - 1), 0, 0),
                               memory_space=pltpu.SMEM),
                  pl.BlockSpec((8, tm), lambda i: (0, i)),
                  pl.BlockSpec((tm, LANES), lambda i: (i, 0)),
                  pl.BlockSpec((tm, D), lambda i: (i, 0)),
                  pl.BlockSpec((None, 6, D), lambda i: (i // tiles_per_batch, 0, 0)),
                  pl.BlockSpec(memory_space=pl.ANY)],
        out_specs=pl.BlockSpec((tm, D), lambda i: (i, 0)),
        out_shape=jax.ShapeDtypeStruct((T, D), F32),
        scratch_shapes=[pltpu.VMEM((2, NR, D // 2), jnp.uint32),
                        pltpu.SemaphoreType.DMA((2,))],
        compiler_params=_cparams(("arbitrary",)),
        name="moe_combine",
    )(tbl3, tbl3, lp8, grow, x1, mod3, y)


def _tiles(S):
    t_attn = min(512, S // 4) if S >= 512 else S
    tm = min(512, t_attn)
    return dict(tm=tm, t_attn=t_attn, bm=1024)


def _moe_layout(cnt, T, tm, bm):
    nt, E = cnt.shape
    c8 = (cnt + ROW_CHUNK - 1) // ROW_CHUNK * ROW_CHUNK
    tile_end = jnp.cumsum(c8, axis=1)
    tile_off = tile_end - c8
    used = tile_end[:, -1]
    off = jnp.cumsum(c8, axis=0) - c8
    tot = jnp.sum(c8, axis=0)
    padded = (tot + bm - 1) // bm * bm
    pends = jnp.cumsum(padded)
    pstarts = pends - padded
    NR = TOP_K * tm + ROW_CHUNK * E
    nch_tbl = NR // ROW_CHUNK
    P = (T * TOP_K + nt * E * (ROW_CHUNK - 1) + E * bm + bm - 1) // bm * bm
    nblk = P // bm

    c_row = jnp.arange(nch_tbl, dtype=jnp.int32) * ROW_CHUNK
    inrun = (tile_off[:, :, None] <= c_row) & (c_row < tile_end[:, :, None])
    base = pstarts[None, :] + off - tile_off
    tbl = jnp.sum(jnp.where(inrun, base[:, :, None], 0), axis=1) + c_row[None, :]
    in_use = c_row[None, :] < used[:, None]
    spare_rows = NR - TOP_K * tm
    spare = P + jnp.arange(nt, dtype=jnp.int32)[:, None] * spare_rows + (c_row[None, :] - used[:, None])
    tbl_out = jnp.where(in_use, tbl, spare).astype(jnp.int32)
    tbl_in = jnp.where(in_use, tbl, 0).astype(jnp.int32)
    n_spare_blk = (nt * spare_rows + bm - 1) // bm

    nused = (pends[-1] // bm).astype(jnp.int32).reshape(1)
    blk_start = jnp.arange(nblk, dtype=jnp.int32) * bm
    blk_expert = jnp.minimum(jnp.sum(pends[None, :] <= blk_start[:, None], axis=1), E - 1).astype(jnp.int32)
    own = blk_expert[:, None] == jnp.arange(E, dtype=jnp.int32)[None, :]

    def of_expert(table):
        return jnp.sum(jnp.where(own, table[None, :], 0), axis=1)

    step = jnp.arange(nblk, dtype=jnp.int32)
    order = jnp.where(step >= nused[0], step,
                      jnp.where(step == of_expert(pstarts // bm), of_expert(pends // bm - 1), step - 1))
    order = order.astype(jnp.int32)
    half2 = (order * bm + bm // 2 < of_expert(pstarts + tot)).astype(jnp.int32)
    last_blk = jnp.where(padded > 0, pends // bm - 1, -1)
    n_tail = nblk - (T * TOP_K) // bm
    tail_blk = nused[0] + jnp.arange(n_tail, dtype=jnp.int32)
    tail_blk = jnp.where(tail_blk < nblk, tail_blk, -1)
    spare_blk = nblk + jnp.arange(n_spare_blk, dtype=jnp.int32)
    zero_blk = jnp.concatenate([last_blk, spare_blk, tail_blk]).astype(jnp.int32)
    return dict(tbl_out=tbl_out.reshape(nt, 1, nch_tbl), tbl_in=tbl_in.reshape(nt, 1, nch_tbl),
                nused=nused, blk_expert=blk_expert, half2=half2, order=order, zero_blk=zero_blk,
                nz_first=E + n_spare_blk,
                P=P, P_alloc=P + n_spare_blk * bm, NR=NR)


def kernel(x, c, w_ada, b_ada, g_mix, w_in, w_fourier, q_norm_g, k_norm_g, lambda_q1, lambda_k1,
           lambda_q2, lambda_k2, g_subln, w_out, rel_bias, g_ffn, w_router, b_router, w1, b1, w2, b2):
    B, S, D = x.shape
    T = B * S
    L = w_ada.shape[0]
    E = w_router.shape[-1]
    cfg = _tiles(S)
    tm, t_attn, bm = cfg["tm"], cfg["t_attn"], cfg["bm"]
    dqk = q_norm_g.shape[-1]
    scale = dqk ** -0.5
    fw = w_in.shape[-1] // 4

    btiles = _bias_tiles(rel_bias, t_attn)
    x2 = x.reshape(T, D)
    for l in range(L):
        lam_init = 0.8 - 0.6 * math.exp(-0.3 * l)
        mod3 = _ada(c, w_ada[l], b_ada[l]).reshape(B, 6, D)
        gq = (jnp.tile(q_norm_g[l], 4) * (scale * LOG2E)).reshape(1, 2 * LANES)
        gk = jnp.tile(k_norm_g[l], 4).reshape(1, 2 * LANES)
        uf, qx, kx, v = _in_proj(x2, mod3, g_mix[l].reshape(1, D), w_in[l].astype(BF16), gq, gk,
                                 S, tm, t_attn)
        yf = _fourier(uf.reshape(B, S, fw), w_fourier[l])
        lam4 = jnp.stack([lambda_q1[l], lambda_k1[l], lambda_q2[l], lambda_k2[l]]).astype(F32)
        yd = _attention(rel_bias, qx.reshape(B, S, -1), kx.reshape(B, S, -1), v.reshape(B, S, fw),
                        btiles, lam4, g_subln[l].reshape(1, LANES), t_attn, lam_init).reshape(T, fw)
        x1, hp, lp8, grow, cnt3 = _out_proj(
            yf, yd, x2, mod3, w_out[l].astype(BF16), g_ffn[l].reshape(1, D),
            w_router[l].T.astype(BF16), b_router[l].reshape(E, 1), S, tm)
        lay = _moe_layout(cnt3[:, :, 0], T, tm, bm)
        xs = _dispatch(lay["tbl_out"], lay["zero_blk"], lay["nz_first"], lp8, hp, lay["P_alloc"], bm, tm,
                       lay["NR"])
        y = _experts(lay["blk_expert"], lay["nused"], lay["half2"], lay["order"], xs, lay["P"], w1[l],
                     b1[l][:, None, 0::2],
                     b1[l][:, None, 1::2], w2[l], b2[l][:, None, :], bm)
        x2 = _combine(lay["tbl_in"], lp8, grow, x1, mod3, y, S, tm, lay["NR"])
    return x2.reshape(B, S, D)
```
